```python
import jax, jax.numpy as jnp
from jax import lax
import numpy as np

D_MODEL = 1024
BATCH = 4
SEQ = 4096
DEPTH = 2

H_DN = 4
DK_DN = 128
DV_DN = 128
SHORT_CONV = 4
CHUNK = 64
H_FOX = 4
D_FOX = 128
Q_BLOCK = 128
C_CONV = 512
CONV_WIDTH = 31
N_GROUPS = 4
EXPERTS_PER_GROUP = 8
N_EXPERTS = N_GROUPS * EXPERTS_PER_GROUP
TOP_K = 2
D_EXPERT = 256
EXPERT_BLOCK = 128

EPS = 1e-6

IN_SIZES = (H_DN * DK_DN, H_DN * DK_DN, H_DN * DV_DN, H_DN * DV_DN, H_DN, H_DN,
            H_FOX * D_FOX, H_FOX * D_FOX, H_FOX * D_FOX, H_FOX,
            2 * C_CONV, D_MODEL, D_MODEL, D_MODEL)
D_IN = sum(IN_SIZES)

kernel_name = "hybrid_deltanet_fox_conformer_hmoe"


def rms_norm(x, g):
    xf = x.astype(jnp.float32)
    y = xf * lax.rsqrt(jnp.mean(xf * xf, axis=-1, keepdims=True) + EPS)
    return (y * g.astype(jnp.float32)).astype(x.dtype)


def l2_normalize(x):
    xf = x.astype(jnp.float32)
    return (xf * lax.rsqrt(jnp.sum(xf * xf, axis=-1, keepdims=True) + EPS)).astype(x.dtype)


def causal_depthwise_conv(x, w):
    k = w.shape[0]
    return lax.conv_general_dilated(
        x, w[:, None, :].astype(x.dtype), window_strides=(1,), padding=[(k - 1, 0)],
        dimension_numbers=("NWC", "WIO", "NWC"), feature_group_count=x.shape[-1])


def _to_chunks(a):
    b, t, hh = a.shape[:3]
    a = a.reshape(b, t // CHUNK, CHUNK, hh, *a.shape[3:])
    return jnp.moveaxis(a, 3, 1)


def gated_delta_rule(q, k, v, beta, g):
    out_dtype = v.dtype
    f32 = jnp.float32
    b, t, hh, dk = q.shape
    dv = v.shape[-1]
    q, k, v = (_to_chunks(a.astype(f32)) for a in (q, k, v))
    beta = _to_chunks(beta.astype(f32))
    g = _to_chunks(g.astype(f32))
    q = q * dk ** -0.5
    kb = k * beta[..., None]
    vb = v * beta[..., None]
    gc = jnp.cumsum(g, axis=-1)
    causal = jnp.tril(jnp.ones((CHUNK, CHUNK), bool))
    strict = jnp.tril(jnp.ones((CHUNK, CHUNK), bool), -1)
    diff = gc[..., :, None] - gc[..., None, :]
    decay = jnp.where(causal, jnp.exp(jnp.where(causal, diff, 0.0)), 0.0)
    a_mat = jnp.where(strict, jnp.einsum("bhncd,bhnsd->bhncs", kb, k) * decay, 0.0)
    lhs = a_mat + jnp.eye(CHUNK, dtype=f32)
    rhs = jnp.concatenate([vb, kb * jnp.exp(gc)[..., None]], axis=-1)
    sol = lax.linalg.triangular_solve(lhs, rhs, left_side=True, lower=True, unit_diagonal=True)
    u, w = sol[..., :dv], sol[..., dv:]
    qk = jnp.where(causal, jnp.einsum("bhncd,bhnsd->bhncs", q, k) * decay, 0.0)

    def step(state, xs):
        qi, ki, ui, wi, gi, ai = xs
        v_new = ui - jnp.einsum("bhck,bhkv->bhcv", wi, state)
        o = (jnp.einsum("bhck,bhkv->bhcv", qi * jnp.exp(gi)[..., None], state)
             + jnp.einsum("bhcs,bhsv->bhcv", ai, v_new))
        g_last = gi[..., -1]
        state = (state * jnp.exp(g_last)[..., None, None]
                 + jnp.einsum("bhck,bhcv->bhkv", ki * jnp.exp(g_last[..., None] - gi)[..., None], v_new))
        return state, o

    xs = tuple(jnp.moveaxis(a, 2, 0) for a in (q, k, u, w, gc, qk))
    s0 = jnp.zeros((b, hh, dk, dv), f32)
    _, o = lax.scan(step, s0, xs)
    o = jnp.moveaxis(o, 0, 2).reshape(b, hh, t, dv)
    return o.transpose(0, 2, 1, 3).astype(out_dtype)


def forgetting_attention(q, k, v, log_f):
    out_dtype = v.dtype
    f32 = jnp.float32
    b, t, hh, d = q.shape
    nb = t // Q_BLOCK
    c = jnp.cumsum(log_f.astype(f32), axis=1)
    qb = q.astype(f32).reshape(b, nb, Q_BLOCK, hh, d).transpose(1, 0, 3, 2, 4)
    cq = c.reshape(b, nb, Q_BLOCK, hh).transpose(1, 0, 3, 2)
    kt = k.astype(f32).transpose(0, 2, 1, 3)
    vt = v.astype(f32).transpose(0, 2, 1, 3)
    ck = c.transpose(0, 2, 1)
    kpos = jnp.arange(t)
    scale = d ** -0.5

    def block(args):
        qi, ci, start = args
        s = jnp.einsum("bhqd,bhkd->bhqk", qi, kt) * scale + (ci[..., :, None] - ck[:, :, None, :])
        qpos = start + jnp.arange(Q_BLOCK)
        s = jnp.where(kpos[None, :] <= qpos[:, None], s, -jnp.inf)
        return jnp.einsum("bhqk,bhkd->bhqd", jax.nn.softmax(s, axis=-1), vt)

    o = lax.map(block, (qb, cq, jnp.arange(nb) * Q_BLOCK))
    return o.transpose(1, 0, 3, 2, 4).reshape(b, t, hh, d).astype(out_dtype)


def conformer_conv_module(u, dw_w, dw_b, ln_g, ln_b):
    a, gate = jnp.split(u, 2, axis=-1)
    z = a * jax.nn.sigmoid(gate)
    z = causal_depthwise_conv(z, dw_w) + dw_b
    zf = z.astype(jnp.float32)
    mu = jnp.mean(zf, axis=-1, keepdims=True)
    var = jnp.mean(jnp.square(zf - mu), axis=-1, keepdims=True)
    zf = (zf - mu) * lax.rsqrt(var + EPS) * ln_g.astype(jnp.float32) + ln_b.astype(jnp.float32)
    return jax.nn.silu(zf).astype(u.dtype)


def hierarchical_moe(h, wg, bg, we, be, w1, w3, w2):
    bsz, t, d = h.shape
    n = bsz * t
    xf = h.reshape(n, d)
    glog = (xf @ wg + bg).astype(jnp.float32)
    gprob = jax.nn.softmax(glog, axis=-1)
    grp = jnp.argmax(glog, axis=-1)
    p_grp = jnp.take_along_axis(gprob, grp[:, None], axis=-1)
    elog = (xf @ we + be).astype(jnp.float32).reshape(n, N_GROUPS, EXPERTS_PER_GROUP)
    elog_g = jnp.take_along_axis(elog, grp[:, None, None], axis=1)[:, 0]
    top_v, top_i = lax.top_k(elog_g, TOP_K)
    gate = p_grp * jax.nn.softmax(top_v, axis=-1)
    eid = (grp[:, None] * EXPERTS_PER_GROUP + top_i).astype(jnp.int32)

    a = n * TOP_K
    eid_f = eid.reshape(a)
    tok_f = jnp.repeat(jnp.arange(n, dtype=jnp.int32), TOP_K)
    gate_f = gate.reshape(a)
    order = jnp.argsort(eid_f)
    s_eid, s_tok, s_gate = eid_f[order], tok_f[order], gate_f[order]
    counts = jnp.bincount(eid_f, length=N_EXPERTS).astype(jnp.int32)
    start = jnp.cumsum(counts) - counts
    padded = ((counts + EXPERT_BLOCK - 1) // EXPERT_BLOCK) * EXPERT_BLOCK
    pend = jnp.cumsum(padded)
    pstart = pend - padded
    dest = pstart[s_eid] + (jnp.arange(a, dtype=jnp.int32) - start[s_eid])
    nb = -(-a // EXPERT_BLOCK) + N_EXPERTS
    p = nb * EXPERT_BLOCK
    buf_tok = jnp.zeros((p,), jnp.int32).at[dest].set(s_tok)
    buf_gate = jnp.zeros((p,), jnp.float32).at[dest].set(s_gate)
    blk_eid = jnp.minimum(jnp.searchsorted(pend, jnp.arange(nb, dtype=jnp.int32) * EXPERT_BLOCK, side="right"),
                          N_EXPERTS - 1)

    def expert_block(args):
        toks, e = args
        xb = xf[toks]
        return (jax.nn.silu(xb @ w1[e]) * (xb @ w3[e])) @ w2[e]

    yb = lax.map(expert_block, (buf_tok.reshape(nb, EXPERT_BLOCK), blk_eid))
    y = jnp.zeros((n, d), h.dtype).at[buf_tok].add(yb.reshape(p, d) * buf_gate[:, None].astype(h.dtype))
    return y.reshape(bsz, t, d)


def setup_inputs(seed: int = 0) -> dict:
    key = jax.random.key(seed)
    ks = jax.random.split(key, 24)
    L, D = DEPTH, D_MODEL
    f32 = jnp.float32
    nrm = lambda k, shape, scale: jax.random.normal(k, shape, f32) * scale
    qkv_dn = 2 * H_DN * DK_DN + H_DN * DV_DN
    dt = jnp.exp(jax.random.uniform(ks[5], (L, H_DN), f32, np.log(1e-3), np.log(1e-1)))
    return {
        "x": jax.random.normal(ks[0], (BATCH, SEQ, D), f32),
        "norm_mix": 1.0 + nrm(ks[1], (L, D), 0.02),
        "w_in": nrm(ks[2], (L, D, D_IN), D ** -0.5),
        "conv_qkv": nrm(ks[3], (L, SHORT_CONV, qkv_dn), SHORT_CONV ** -0.5),
        "dn_a_log": jnp.log(jax.random.uniform(ks[4], (L, H_DN), f32, 1.0, 16.0)),
        "dn_dt_bias": dt + jnp.log(-jnp.expm1(-dt)),
        "dn_norm": 1.0 + nrm(ks[6], (L, DV_DN), 0.02),
        "fox_bias": jax.random.uniform(ks[7], (L, H_FOX), f32, 1.0, 6.0),
        "conv_dw": nrm(ks[8], (L, CONV_WIDTH, C_CONV), CONV_WIDTH ** -0.5),
        "conv_dw_b": nrm(ks[9], (L, C_CONV), 0.02),
        "conv_ln_g": 1.0 + nrm(ks[10], (L, C_CONV), 0.02),
        "conv_ln_b": nrm(ks[11], (L, C_CONV), 0.02),
        "w_a": nrm(ks[12], (L, H_DN * DV_DN, D), (H_DN * DV_DN) ** -0.5),
        "w_b": nrm(ks[13], (L, H_FOX * D_FOX, D), (H_FOX * D_FOX) ** -0.5),
        "w_c": nrm(ks[14], (L, C_CONV, D), C_CONV ** -0.5),
        "w_out": nrm(ks[15], (L, D, D), D ** -0.5),
        "norm_ffn": 1.0 + nrm(ks[16], (L, D), 0.02),
        "router_group_w": nrm(ks[17], (L, D, N_GROUPS), D ** -0.5),
        "router_group_b": nrm(ks[18], (L, N_GROUPS), 0.01),
        "router_expert_w": nrm(ks[19], (L, D, N_EXPERTS), D ** -0.5),
        "router_expert_b": nrm(ks[20], (L, N_EXPERTS), 0.01),
        "expert_w1": nrm(ks[21], (L, N_EXPERTS, D, D_EXPERT), D ** -0.5),
        "expert_w3": nrm(ks[22], (L, N_EXPERTS, D, D_EXPERT), D ** -0.5),
        "expert_w2": nrm(ks[23], (L, N_EXPERTS, D_EXPERT, D), D_EXPERT ** -0.5),
        "norm_final": 1.0 + nrm(jax.random.fold_in(key, 99), (D,), 0.02),
    }


def reference(x, norm_mix, w_in, conv_qkv, dn_a_log, dn_dt_bias, dn_norm, fox_bias,
              conv_dw, conv_dw_b, conv_ln_g, conv_ln_b, w_a, w_b, w_c, w_out,
              norm_ffn, router_group_w, router_group_b, router_expert_w, router_expert_b,
              expert_w1, expert_w3, expert_w2, norm_final):
    b, t, _ = x.shape
    split_idx = np.cumsum(IN_SIZES)[:-1].tolist()
    qk_w = H_DN * DK_DN
    for l in range(DEPTH):
        h = rms_norm(x, norm_mix[l])
        proj = h @ w_in[l]
        (qa, ka, va, za, ba, aa, qb, kb, vb, fb, uc, ga, gb, gc) = jnp.split(proj, split_idx, axis=-1)

        qkv = jax.nn.silu(causal_depthwise_conv(jnp.concatenate([qa, ka, va], axis=-1), conv_qkv[l]))
        qa, ka, va = qkv[..., :qk_w], qkv[..., qk_w:2 * qk_w], qkv[..., 2 * qk_w:]
        qa = l2_normalize(qa.reshape(b, t, H_DN, DK_DN))
        ka = l2_normalize(ka.reshape(b, t, H_DN, DK_DN))
        va = va.reshape(b, t, H_DN, DV_DN)
        beta = jax.nn.sigmoid(ba.astype(jnp.float32))
        g_log = -jnp.exp(dn_a_log[l].astype(jnp.float32)) * jax.nn.softplus(
            aa.astype(jnp.float32) + dn_dt_bias[l].astype(jnp.float32))
        oa = gated_delta_rule(qa, ka, va, beta, g_log)
        oa = rms_norm(oa, dn_norm[l]) * jax.nn.silu(za.reshape(b, t, H_DN, DV_DN))
        ya = oa.reshape(b, t, H_DN * DV_DN) @ w_a[l]

        log_f = jax.nn.log_sigmoid(fb.astype(jnp.float32) + fox_bias[l].astype(jnp.float32))
        ob = forgetting_attention(qb.reshape(b, t, H_FOX, D_FOX), kb.reshape(b, t, H_FOX, D_FOX),
                                  vb.reshape(b, t, H_FOX, D_FOX), log_f)
        yb = ob.reshape(b, t, H_FOX * D_FOX) @ w_b[l]

        yc = conformer_conv_module(uc, conv_dw[l], conv_dw_b[l], conv_ln_g[l], conv_ln_b[l]) @ w_c[l]

        merged = jax.nn.sigmoid(ga) * ya + jax.nn.sigmoid(gb) * yb + jax.nn.sigmoid(gc) * yc
        x = x + merged @ w_out[l]

        h2 = rms_norm(x, norm_ffn[l])
        x = x + hierarchical_moe(h2, router_group_w[l], router_group_b[l], router_expert_w[l],
                                 router_expert_b[l], expert_w1[l], expert_w3[l], expert_w2[l])
    return rms_norm(x, norm_final)
```

```python
import functools

import jax
import jax.numpy as jnp
import numpy as np
from jax import lax
from jax.experimental import pallas as pl
from jax.experimental.pallas import tpu as pltpu

F32 = jnp.float32
BF16 = jnp.bfloat16
U32 = jnp.uint32
I32 = jnp.int32
HIGHEST = lax.Precision.HIGHEST

EPS = 1e-6
LANES = 128
H_DN, DK_DN, DV_DN = 4, 128, 128
SHORT_CONV = 4
CHUNK = 64
H_FOX, D_FOX = 4, 128
C_CONV = 512
CONV_WIDTH = 31
N_GROUPS, EXPERTS_PER_GROUP = 4, 8
N_EXPERTS = N_GROUPS * EXPERTS_PER_GROUP
D_EXPERT = 256

W_DN = 3 * H_DN * DK_DN
W_FOX = 3 * H_FOX * D_FOX
VMEM_LIMIT = 56 * 1024 * 1024

LANE_BETA, LANE_G, LANE_F = 0, 4, 8
LANE_GRP, LANE_EXP = 0, 32
R_GATE0, R_GATE1, R_EID0, R_EID1, R_RANK0, R_RANK1 = 0, 1, 2, 3, 4, 5

NN = (((1,), (0,)), ((), ()))
NT = (((1,), (1,)), ((), ()))
TN = (((0,), (0,)), ((), ()))


def _mm(a, b, dims=NN):
    return lax.dot_general(a.astype(BF16), b.astype(BF16), dims, preferred_element_type=F32)


def _mm_f32(a, b, dims=NN):
    return lax.dot_general(a.astype(F32), b.astype(F32), dims, precision=HIGHEST,
                           preferred_element_type=F32)


def _sigmoid(x):
    return 1.0 / (1.0 + jnp.exp(-x))


def _silu(x):
    return x * _sigmoid(x)


def _params(*sem):
    return pltpu.CompilerParams(dimension_semantics=sem, vmem_limit_bytes=VMEM_LIMIT)


def _in_proj_body(x_ref, g_ref, w_ref, ws_ref, dn_ref, za_ref, fox_ref, uc_ref, gates_ref, small_ref,
                  *, col_chunk):
    x = x_ref[...]
    h = x * lax.rsqrt(jnp.mean(x * x, axis=-1, keepdims=True) + EPS) * g_ref[...]
    hb = h.astype(BF16)
    col = 0
    for ref in (dn_ref, za_ref, fox_ref, uc_ref, gates_ref):
        width = ref.shape[-1]
        for c in range(0, width, col_chunk):
            ref[:, c:c + col_chunk] = jnp.dot(
                hb, w_ref[:, col + c:col + c + col_chunk], preferred_element_type=F32).astype(ref.dtype)
        col += width
    small_ref[...] = _mm_f32(h, ws_ref[...])


def _in_proj(x2, g, w, ws, *, tm=256, col_chunk=512):
    n, d = x2.shape
    d_gate = 3 * d
    widths = (W_DN, H_DN * DV_DN, W_FOX, 2 * C_CONV, d_gate)
    assert w.shape == (d, sum(widths)) and n % tm == 0
    out_shape = [jax.ShapeDtypeStruct((n, wd), BF16) for wd in widths]
    out_shape.append(jax.ShapeDtypeStruct((n, LANES), F32))
    row = lambda i: (i, 0)
    fixed = lambda i: (0, 0)
    return pl.pallas_call(
        functools.partial(_in_proj_body, col_chunk=col_chunk),
        grid=(n // tm,),
        in_specs=[pl.BlockSpec((tm, d), row), pl.BlockSpec((1, d), fixed),
                  pl.BlockSpec(w.shape, fixed), pl.BlockSpec(ws.shape, fixed)],
        out_specs=[pl.BlockSpec((tm, wd), row) for wd in widths] + [pl.BlockSpec((tm, LANES), row)],
        out_shape=out_shape,
        compiler_params=_params("parallel"),
        name="in_proj",
    )(x2, g, w, ws)


def _softplus_parts(z):
    t = jnp.log1p(jnp.exp(-jnp.abs(z)))
    return jnp.maximum(z, 0.0) + t, -(jnp.maximum(-z, 0.0) + t)


def _delta_body(qkv_ref, za_ref, sm_ref, cw_ref, par_ref, oa_ref, c_ref, ct_ref,
                xs_ref, s_ref, carry_ref, *, ts):
    j = pl.program_id(1)
    halo = 8

    @pl.when(j == 0)
    def _():
        xs_ref[0:halo, :] = jnp.zeros((halo, W_DN), F32)
        s_ref[...] = jnp.zeros_like(s_ref)
        carry_ref[...] = jnp.zeros_like(carry_ref)

    @pl.when(j > 0)
    def _():
        xs_ref[0:halo, :] = xs_ref[ts:ts + halo, :]

    xs_ref[halo:halo + ts, :] = qkv_ref[0].astype(F32)

    def conv_silu(lane0):
        acc = None
        for k in range(SHORT_CONV):
            off = halo - (SHORT_CONV - 1) + k
            term = cw_ref[k:k + 1, lane0:lane0 + LANES] * xs_ref[off:off + ts, lane0:lane0 + LANES]
            acc = term if acc is None else acc + term
        return _silu(acc)

    def l2n(a):
        return a * lax.rsqrt(jnp.sum(a * a, axis=-1, keepdims=True) + EPS)

    sm = sm_ref[0]
    lane = lax.broadcasted_iota(I32, sm.shape, 1)
    sp, logsig = _softplus_parts(sm + par_ref[1:2, :])
    vals = jnp.where(lane < LANE_G, _sigmoid(sm),
                     jnp.where(lane < LANE_F, par_ref[0:1, :] * sp,
                               jnp.where(lane < LANE_F + H_FOX, logsig, 0.0)))
    row = lax.broadcasted_iota(I32, (ts, ts), 0)
    colm = lax.broadcasted_iota(I32, (ts, ts), 1)
    tril = row >= colm
    same = (row // CHUNK) == (colm // CHUNK)
    ccum = _mm_f32(jnp.where(tril, 1.0, 0.0), vals) + carry_ref[...]
    gcum = _mm_f32(jnp.where(tril & same, 1.0, 0.0), vals)
    carry_ref[...] = ccum[ts - 1:ts, :]
    c_ref[0] = ccum
    ct_ref[0] = ccum.T[LANE_F:LANE_F + 8, :]
    gcum_t = gcum.T

    ci = lax.broadcasted_iota(I32, (CHUNK, CHUNK), 0)
    cj = lax.broadcasted_iota(I32, (CHUNK, CHUNK), 1)
    causal = ci >= cj
    strict = ci > cj
    scale = DK_DN ** -0.5
    dn_norm = par_ref[2:3, :]

    for h in range(H_DN):
        q = l2n(conv_silu(h * DK_DN))
        k = l2n(conv_silu(H_DN * DK_DN + h * DK_DN))
        v = conv_silu(2 * H_DN * DK_DN + h * DV_DN)
        for c in range(ts // CHUNK):
            r0 = c * CHUNK
            qc, kc, vc = q[r0:r0 + CHUNK], k[r0:r0 + CHUNK], v[r0:r0 + CHUNK]
            gcol = gcum[r0:r0 + CHUNK, LANE_G + h:LANE_G + h + 1]
            grow = gcum_t[LANE_G + h:LANE_G + h + 1, r0:r0 + CHUNK]
            beta = vals[r0:r0 + CHUNK, LANE_BETA + h:LANE_BETA + h + 1]
            decay = jnp.where(causal, jnp.exp(jnp.where(causal, gcol - grow, 0.0)), 0.0)
            kb = kc * beta
            a_mat = jnp.where(strict, _mm(kb, kc, NT) * decay, 0.0)
            qk = jnp.where(causal, _mm(qc * scale, kc, NT) * decay, 0.0)
            egc = jnp.exp(gcol)
            sol = jnp.concatenate([vc * beta, kb * egc], axis=-1)
            sol = sol - _mm_f32(a_mat, sol)
            pw = a_mat
            for _ in range(5):
                pw = _mm_f32(pw, pw)
                sol = sol + _mm_f32(pw, sol)
            u, w = sol[:, :DV_DN], sol[:, DV_DN:]

            state = s_ref[h]
            v_new = u - _mm(w, state)
            o = _mm(qc * (scale * egc), state) + _mm(qk, v_new)
            glast = gcum[r0 + CHUNK - 1:r0 + CHUNK, LANE_G + h:LANE_G + h + 1]
            s_ref[h] = state * jnp.exp(glast) + _mm(kc * jnp.exp(glast - gcol), v_new, TN)

            o = o * lax.rsqrt(jnp.mean(o * o, axis=-1, keepdims=True) + EPS) * dn_norm
            za = za_ref[0, r0:r0 + CHUNK, h * DV_DN:(h + 1) * DV_DN].astype(F32)
            oa_ref[0, r0:r0 + CHUNK, h * DV_DN:(h + 1) * DV_DN] = (o * _silu(za)).astype(oa_ref.dtype)


def _delta_rule(qkv, za, small, conv_w, par, *, ts=256):
    b, t, _ = qkv.shape
    assert t % ts == 0 and ts % CHUNK == 0
    blk = lambda width: pl.BlockSpec((1, ts, width), lambda bi, j: (bi, j, 0))
    fixed = lambda bi, j: (0, 0)
    return pl.pallas_call(
        functools.partial(_delta_body, ts=ts),
        grid=(b, t // ts),
        in_specs=[blk(W_DN), blk(H_DN * DV_DN), blk(LANES),
                  pl.BlockSpec(conv_w.shape, fixed), pl.BlockSpec(par.shape, fixed)],
        out_specs=[blk(H_DN * DV_DN), blk(LANES), pl.BlockSpec((1, 8, ts), lambda bi, j: (bi, 0, j))],
        out_shape=[jax.ShapeDtypeStruct((b, t, H_DN * DV_DN), BF16),
                   jax.ShapeDtypeStruct((b, t, LANES), F32),
                   jax.ShapeDtypeStruct((b, 8, t), F32)],
        scratch_shapes=[pltpu.VMEM((ts + 8, W_DN), F32),
                        pltpu.VMEM((H_DN, DK_DN, DV_DN), F32),
                        pltpu.VMEM((1, LANES), F32)],
        compiler_params=_params("parallel", "arbitrary"),
        name="delta_rule",
    )(qkv, za, small, conv_w, par)


def _fox_body(q_ref, k_ref, v_ref, c_ref, ct_ref, o_ref, *, tq):
    i = pl.program_id(1)
    ri = lax.broadcasted_iota(I32, (tq, tq), 0)
    cj = lax.broadcasted_iota(I32, (tq, tq), 1)
    keep = cj <= ri

    for h in range(H_FOX):
        lanes = slice(h * D_FOX, (h + 1) * D_FOX)
        q = q_ref[0, :, lanes]
        cq = c_ref[0, :, LANE_F + h:LANE_F + h + 1]

        def block(jb, carry, masked):
            m, l, acc = carry
            start = pl.multiple_of(jb * tq, tq)
            kj = k_ref[0, pl.ds(start, tq), lanes]
            vj = v_ref[0, pl.ds(start, tq), lanes]
            ck = ct_ref[0, h:h + 1, pl.ds(start, tq)]
            s = lax.dot_general(q, kj, NT, preferred_element_type=F32) + cq - ck
            if masked:
                s = jnp.where(keep, s, -jnp.inf)
            m_new = jnp.maximum(m, jnp.max(s, axis=-1, keepdims=True))
            alpha = jnp.exp(m - m_new)
            p = jnp.exp(s - m_new)
            l = alpha * l + jnp.sum(p, axis=-1, keepdims=True)
            acc = alpha * acc + jnp.dot(p.astype(BF16), vj, preferred_element_type=F32)
            return m_new, l, acc

        init = (jnp.full((tq, 1), -jnp.inf, F32), jnp.zeros((tq, 1), F32), jnp.zeros((tq, D_FOX), F32))
        carry = lax.fori_loop(0, i, functools.partial(block, masked=False), init)
        _, l, acc = block(i, carry, True)
        o_ref[0, :, lanes] = (acc / l).astype(o_ref.dtype)


def _fox_attention(qkv, c, ct, *, tq=256):
    b, t, _ = qkv.shape
    hd = H_FOX * D_FOX
    assert t % tq == 0
    return pl.pallas_call(
        functools.partial(_fox_body, tq=tq),
        grid=(b, t // tq),
        in_specs=[pl.BlockSpec((1, tq, hd), lambda bi, i: (bi, i, 0)),
                  pl.BlockSpec((1, t, hd), lambda bi, i: (bi, 0, 1)),
                  pl.BlockSpec((1, t, hd), lambda bi, i: (bi, 0, 2)),
                  pl.BlockSpec((1, tq, LANES), lambda bi, i: (bi, i, 0)),
                  pl.BlockSpec((1, 8, t), lambda bi, i: (bi, 0, 0))],
        out_specs=pl.BlockSpec((1, tq, hd), lambda bi, i: (bi, i, 0)),
        out_shape=jax.ShapeDtypeStruct((b, t, hd), BF16),
        compiler_params=_params("parallel", "arbitrary"),
        name="fox_attention",
    )(qkv, qkv, qkv, c, ct)


def _conv_body(cur_ref, halo_ref, dw_ref, vec_ref, o_ref, z_ref, y_ref, *, tt, hrows, rchunk):
    i = pl.program_id(1)
    cur = cur_ref[0].astype(F32)
    z_ref[hrows:hrows + tt, :] = cur[:, :C_CONV] * _sigmoid(cur[:, C_CONV:])
    hal = halo_ref[0].astype(F32)
    zh = hal[:, :C_CONV] * _sigmoid(hal[:, C_CONV:])
    z_ref[0:hrows, :] = jnp.where(i > 0, zh, 0.0)

    for lg in range(C_CONV // LANES):
        lanes = slice(lg * LANES, (lg + 1) * LANES)
        for r0 in range(0, tt, rchunk):
            acc = jnp.broadcast_to(vec_ref[0:1, lanes], (rchunk, LANES))
            for k in range(CONV_WIDTH):
                off = hrows - (CONV_WIDTH - 1) + k + r0
                acc = acc + dw_ref[k:k + 1, lanes] * z_ref[off:off + rchunk, lanes]
            y_ref[r0:r0 + rchunk, lanes] = acc

    y = y_ref[...]
    mu = jnp.mean(y, axis=-1, keepdims=True)
    yc = y - mu
    var = jnp.mean(yc * yc, axis=-1, keepdims=True)
    yn = yc * lax.rsqrt(var + EPS) * vec_ref[1:2, :] + vec_ref[2:3, :]
    o_ref[0] = _silu(yn).astype(o_ref.dtype)


def _conv_module(uc, dw, vec, *, tt=256, hrows=32, rchunk=64):
    b, t, _ = uc.shape
    assert t % tt == 0 and tt % hrows == 0 and hrows >= CONV_WIDTH - 1
    ratio = tt // hrows
    return pl.pallas_call(
        functools.partial(_conv_body, tt=tt, hrows=hrows, rchunk=rchunk),
        grid=(b, t // tt),
        in_specs=[pl.BlockSpec((1, tt, 2 * C_CONV), lambda bi, i: (bi, i, 0)),
                  pl.BlockSpec((1, hrows, 2 * C_CONV), lambda bi, i: (bi, jnp.maximum(i * ratio - 1, 0), 0)),
                  pl.BlockSpec(dw.shape, lambda bi, i: (0, 0)),
                  pl.BlockSpec(vec.shape, lambda bi, i: (0, 0))],
        out_specs=pl.BlockSpec((1, tt, C_CONV), lambda bi, i: (bi, i, 0)),
        out_shape=jax.ShapeDtypeStruct((b, t, C_CONV), BF16),
        scratch_shapes=[pltpu.VMEM((tt + hrows, C_CONV), F32), pltpu.VMEM((tt, C_CONV), F32)],
        compiler_params=_params("parallel", "parallel"),
        name="conv_module",
    )(uc, uc, dw, vec)


def _merge_body(oa_ref, ob_ref, zc_ref, gates_ref, x_ref, wa_ref, wb_ref, wc_ref, wo_ref, g_ref,
                wr_ref, br_ref, xn_ref, hp_ref, route_ref, cnt_ref, run_ref, *, tm):
    step = pl.program_id(0)
    d = x_ref.shape[-1]

    @pl.when(step == 0)
    def _():
        run_ref[...] = jnp.zeros_like(run_ref)

    merged = None
    for idx, (m_ref, w_ref) in enumerate(((oa_ref, wa_ref), (ob_ref, wb_ref), (zc_ref, wc_ref))):
        y = jnp.dot(m_ref[...], w_ref[...], preferred_element_type=F32)
        term = _sigmoid(gates_ref[:, idx * d:(idx + 1) * d].astype(F32)) * y
        merged = term if merged is None else merged + term
    xn = x_ref[...] + jnp.dot(merged.astype(BF16), wo_ref[...], preferred_element_type=F32)
    xn_ref[...] = xn
    h2 = xn * lax.rsqrt(jnp.mean(xn * xn, axis=-1, keepdims=True) + EPS) * g_ref[...]

    lo = pltpu.bitcast(h2[:, :d // 2].astype(BF16).astype(F32), U32) >> 16
    hi = pltpu.bitcast(h2[:, d // 2:].astype(BF16).astype(F32), U32) & jnp.uint32(0xFFFF0000)
    hp_ref[...] = lo | hi

    logits = _mm_f32(h2, wr_ref[...]) + br_ref[...]
    lane = lax.broadcasted_iota(I32, logits.shape, 1)
    big = jnp.int32(4 * LANES)
    in_grp = lane < N_GROUPS
    gl = jnp.where(in_grp, logits, -jnp.inf)
    gmax = jnp.max(gl, axis=-1, keepdims=True)
    grp = jnp.min(jnp.where(gl == gmax, lane, big), axis=-1, keepdims=True)
    p_grp = 1.0 / jnp.sum(jnp.where(in_grp, jnp.exp(gl - gmax), 0.0), axis=-1, keepdims=True)
    e_lo = LANE_EXP + grp * EXPERTS_PER_GROUP
    el = jnp.where((lane >= e_lo) & (lane < e_lo + EXPERTS_PER_GROUP), logits, -jnp.inf)
    v0 = jnp.max(el, axis=-1, keepdims=True)
    i0 = jnp.min(jnp.where(el == v0, lane, big), axis=-1, keepdims=True)
    el1 = jnp.where(lane == i0, -jnp.inf, el)
    v1 = jnp.max(el1, axis=-1, keepdims=True)
    i1 = jnp.min(jnp.where(el1 == v1, lane, big), axis=-1, keepdims=True)
    e1 = jnp.exp(v1 - v0)
    gate0 = p_grp / (1.0 + e1)
    gate1 = p_grp * e1 / (1.0 + e1)

    hot0 = lane == i0
    hot1 = lane == i1
    onehot = jnp.where(hot0 | hot1, 1.0, 0.0)
    ri = lax.broadcasted_iota(I32, (tm, tm), 0)
    cj = lax.broadcasted_iota(I32, (tm, tm), 1)
    prefix = _mm(jnp.where(cj < ri, 1.0, 0.0), onehot) + run_ref[...]
    rank0 = jnp.sum(jnp.where(hot0, prefix, 0.0), axis=-1, keepdims=True)
    rank1 = jnp.sum(jnp.where(hot1, prefix, 0.0), axis=-1, keepdims=True)
    run_ref[...] = run_ref[...] + jnp.sum(onehot, axis=0, keepdims=True)
    cnt_ref[...] = jnp.broadcast_to(run_ref[...], cnt_ref.shape)

    route = jnp.where(lane == R_GATE0, gate0, 0.0)
    route = jnp.where(lane == R_GATE1, gate1, route)
    route = jnp.where(lane == R_EID0, (i0 - LANE_EXP).astype(F32), route)
    route = jnp.where(lane == R_EID1, (i1 - LANE_EXP).astype(F32), route)
    route = jnp.where(lane == R_RANK0, rank0, route)
    route = jnp.where(lane == R_RANK1, rank1, route)
    route_ref[...] = route


def _merge(oa, ob, zc, gates, x2, wa, wb, wc, wo, g, wr, br, *, tm=256):
    n, d = x2.shape
    assert n % tm == 0
    row = lambda i: (i, 0)
    fixed = lambda i: (0, 0)
    full = lambda a: pl.BlockSpec(a.shape, fixed)
    return pl.pallas_call(
        functools.partial(_merge_body, tm=tm),
        grid=(n // tm,),
        in_specs=[pl.BlockSpec((tm, oa.shape[1]), row), pl.BlockSpec((tm, ob.shape[1]), row),
                  pl.BlockSpec((tm, zc.shape[1]), row), pl.BlockSpec((tm, 3 * d), row),
                  pl.BlockSpec((tm, d), row),
                  full(wa), full(wb), full(wc), full(wo), full(g), full(wr), full(br)],
        out_specs=[pl.BlockSpec((tm, d), row), pl.BlockSpec((tm, d // 2), row),
                   pl.BlockSpec((tm, LANES), row), pl.BlockSpec((8, LANES), fixed)],
        out_shape=[jax.ShapeDtypeStruct((n, d), F32), jax.ShapeDtypeStruct((n, d // 2), U32),
                   jax.ShapeDtypeStruct((n, LANES), F32), jax.ShapeDtypeStruct((8, LANES), F32)],
        scratch_shapes=[pltpu.VMEM((1, LANES), F32)],
        compiler_params=_params("arbitrary"),
        name="merge_router",
    )(oa, ob, zc, gates, x2, wa, wb, wc, wo, g, wr, br)


def _row_copy(src_ref, src_row, dst_ref, dst_row, sem):
    return pltpu.make_async_copy(src_ref.at[pl.ds(src_row, 1)], dst_ref.at[pl.ds(dst_row, 1)], sem)


def _dispatch_body(dest_ref, hp_ref, xs_in_ref, xs_ref, sem, *, tm):
    del xs_in_ref
    base = pl.program_id(0) * tm

    def issue(n, carry):
        _row_copy(hp_ref, base + n, xs_ref, dest_ref[0, 0, n], sem).start()
        _row_copy(hp_ref, base + n, xs_ref, dest_ref[0, 0, tm + n], sem).start()
        return carry

    lax.fori_loop(0, tm, issue, 0)

    def drain(n, carry):
        _row_copy(hp_ref, 0, xs_ref, 0, sem).wait()
        _row_copy(hp_ref, 0, xs_ref, 0, sem).wait()
        return carry

    lax.fori_loop(0, tm, drain, 0)


def _dispatch(dest3, hp, xs_zero, *, tm):
    n = hp.shape[0]
    return pl.pallas_call(
        functools.partial(_dispatch_body, tm=tm),
        grid=(n // tm,),
        in_specs=[pl.BlockSpec((1, 1, 2 * tm), lambda i: (i, 0, 0), memory_space=pltpu.SMEM),
                  pl.BlockSpec(memory_space=pl.ANY), pl.BlockSpec(memory_space=pl.ANY)],
        out_specs=pl.BlockSpec(memory_space=pl.ANY),
        out_shape=jax.ShapeDtypeStruct(xs_zero.shape, xs_zero.dtype),
        scratch_shapes=[pltpu.SemaphoreType.DMA(())],
        input_output_aliases={2: 0},
        compiler_params=pltpu.CompilerParams(dimension_semantics=("arbitrary",), has_side_effects=True),
        name="moe_dispatch",
    )(dest3, hp, xs_zero)


def _expert_body(be_ref, nu_ref, xs_ref, w1_ref, w3_ref, w2_ref, ys_ref):
    del be_ref
    live = pl.program_id(0) < nu_ref[0]

    @pl.when(jnp.logical_not(live))
    def _():
        ys_ref[...] = jnp.zeros_like(ys_ref)

    @pl.when(live)
    def _():
        xp = xs_ref[...]
        half = xp.shape[-1]
        lo = pltpu.bitcast(xp << 16, F32).astype(BF16)
        hi = pltpu.bitcast(xp & jnp.uint32(0xFFFF0000), F32).astype(BF16)

        def up(w_ref):
            return (jnp.dot(lo, w_ref[0, :half, :], preferred_element_type=F32)
                    + jnp.dot(hi, w_ref[0, half:, :], preferred_element_type=F32))

        act = (_silu(up(w1_ref)) * up(w3_ref)).astype(BF16)
        ys_ref[...] = jnp.dot(act, w2_ref[0], preferred_element_type=F32)


def _experts(blk_eid, n_used, xs, w1, w3, w2, *, rb):
    p, half = xs.shape
    d = 2 * half
    nb = p // rb
    used = lambda i, be, nu: jnp.minimum(i, nu[0] - 1)
    wmap = lambda i, be, nu: (be[used(i, be, nu)], 0, 0)
    grid_spec = pltpu.PrefetchScalarGridSpec(
        num_scalar_prefetch=2,
        grid=(nb,),
        in_specs=[pl.BlockSpec((rb, half), lambda i, be, nu: (used(i, be, nu), 0)),
                  pl.BlockSpec((1, d, D_EXPERT), wmap), pl.BlockSpec((1, d, D_EXPERT), wmap),
                  pl.BlockSpec((1, D_EXPERT, d), wmap)],
        out_specs=pl.BlockSpec((rb, d), lambda i, be, nu: (i, 0)),
    )
    return pl.pallas_call(
        _expert_body,
        grid_spec=grid_spec,
        out_shape=jax.ShapeDtypeStruct((p, d), F32),
        compiler_params=_params("arbitrary"),
        name="moe_experts",
    )(blk_eid, n_used, xs, w1, w3, w2)


def _combine_body(dest_ref, ys_ref, x_ref, route_ref, g_ref, o_ref, buf_ref, sem, *, tm, final_norm):
    def issue(n, carry):
        _row_copy(ys_ref, dest_ref[0, 0, n], buf_ref.at[0], n, sem).start()
        _row_copy(ys_ref, dest_ref[0, 0, tm + n], buf_ref.at[1], n, sem).start()
        return carry

    lax.fori_loop(0, tm, issue, 0)

    def drain(n, carry):
        _row_copy(ys_ref, 0, buf_ref.at[0], 0, sem).wait()
        _row_copy(ys_ref, 0, buf_ref.at[1], 0, sem).wait()
        return carry

    lax.fori_loop(0, tm, drain, 0)

    route = route_ref[...]
    out = (x_ref[...] + route[:, R_GATE0:R_GATE0 + 1] * buf_ref[0]
           + route[:, R_GATE1:R_GATE1 + 1] * buf_ref[1])
    if final_norm:
        out = out * lax.rsqrt(jnp.mean(out * out, axis=-1, keepdims=True) + EPS) * g_ref[...]
    o_ref[...] = out


def _combine(dest3, ys, xn, route, g, *, tm, final_norm):
    n, d = xn.shape
    row = lambda i: (i, 0)
    return pl.pallas_call(
        functools.partial(_combine_body, tm=tm, final_norm=final_norm),
        grid=(n // tm,),
        in_specs=[pl.BlockSpec((1, 1, 2 * tm), lambda i: (i, 0, 0), memory_space=pltpu.SMEM),
                  pl.BlockSpec(memory_space=pl.ANY),
                  pl.BlockSpec((tm, d), row), pl.BlockSpec((tm, LANES), row),
                  pl.BlockSpec((1, d), lambda i: (0, 0))],
        out_specs=pl.BlockSpec((tm, d), row),
        out_shape=jax.ShapeDtypeStruct((n, d), F32),
        scratch_shapes=[pltpu.VMEM((2, tm, d), F32), pltpu.SemaphoreType.DMA(())],
        compiler_params=_params("arbitrary"),
        name="moe_combine",
    )(dest3, ys, xn, route, g)


def _moe(hp, route, counts, xn, w1, w3, w2, g_final, *, tm, rb, final_norm):
    n = hp.shape[0]
    cnt = counts[0, LANE_EXP:LANE_EXP + N_EXPERTS].astype(I32)
    nblk = (cnt + rb - 1) // rb
    bend = jnp.cumsum(nblk)
    pstart = (bend - nblk) * rb
    nb = (2 * n) // rb + N_EXPERTS
    n_used = bend[-1:].astype(I32)
    blk_eid = jnp.minimum(jnp.searchsorted(bend, jnp.arange(nb, dtype=I32), side="right"),
                          N_EXPERTS - 1).astype(I32)
    eid = route[:, R_EID0:R_EID1 + 1].astype(I32)
    rank = route[:, R_RANK0:R_RANK1 + 1].astype(I32)
    dest = pstart[eid] + rank
    dest3 = dest.reshape(n // tm, tm, 2).transpose(0, 2, 1).reshape(n // tm, 1, 2 * tm)

    xs = _dispatch(dest3, hp, jnp.zeros((nb * rb, hp.shape[1]), hp.dtype), tm=tm)
    ys = _experts(blk_eid, n_used, xs, w1, w3, w2, rb=rb)
    return _combine(dest3, ys, xn, route, g_final, tm=tm, final_norm=final_norm)


def _lane_row(pairs):
    row = jnp.zeros((LANES,), F32)
    for off, vec in pairs:
        row = row.at[off:off + vec.shape[0]].set(vec.astype(F32))
    return row


def kernel(x, norm_mix, w_in, conv_qkv, dn_a_log, dn_dt_bias, dn_norm, fox_bias, conv_dw, conv_dw_b,
           conv_ln_g, conv_ln_b, w_a, w_b, w_c, w_out, norm_ffn, router_group_w, router_group_b,
           router_expert_w, router_expert_b, expert_w1, expert_w3, expert_w2, norm_final):
    b, t, d = x.shape
    n = b * t
    depth = w_in.shape[0]
    qk_dn = H_DN * DK_DN
    in_sizes = (qk_dn, qk_dn, H_DN * DV_DN, H_DN * DV_DN, H_DN, H_DN,
                H_FOX * D_FOX, H_FOX * D_FOX, H_FOX * D_FOX, H_FOX, 2 * C_CONV, d, d, d)
    splits = np.cumsum(in_sizes)[:-1].tolist()
    tm_moe, rb = 256, 256

    x2 = x.reshape(n, d)
    for l in range(depth):
        (qa, ka, va, za, ba, aa, qb, kb, vb, fb, uc, ga, gb, gc) = jnp.split(w_in[l], splits, axis=1)
        w_main = jnp.concatenate([qa, ka, va, za, qb * (D_FOX ** -0.5), kb, vb, uc, ga, gb, gc],
                                 axis=1).astype(BF16)
        w_small = jnp.concatenate([ba, aa, fb, jnp.zeros((d, LANES - 3 * H_DN), F32)], axis=1)
        dn_qkv, za_p, fox_qkv, uc_p, gates, small = _in_proj(x2, norm_mix[l][None, :], w_main, w_small)

        par = jnp.stack([_lane_row([(LANE_G, -jnp.exp(dn_a_log[l]))]),
                         _lane_row([(LANE_G, dn_dt_bias[l]), (LANE_F, fox_bias[l])]),
                         _lane_row([(0, dn_norm[l])])] + [jnp.zeros((LANES,), F32)] * 5)
        oa, c, ct = _delta_rule(dn_qkv.reshape(b, t, -1), za_p.reshape(b, t, -1),
                                small.reshape(b, t, LANES), conv_qkv[l], par)
        ob = _fox_attention(fox_qkv.reshape(b, t, -1), c, ct)
        vec = jnp.stack([conv_dw_b[l], conv_ln_g[l], conv_ln_b[l]] + [jnp.zeros((C_CONV,), F32)] * 5)
        zc = _conv_module(uc_p.reshape(b, t, -1), conv_dw[l], vec)

        w_r = jnp.concatenate([router_group_w[l], jnp.zeros((d, LANE_EXP - N_GROUPS), F32),
                               router_expert_w[l], jnp.zeros((d, LANES - LANE_EXP - N_EXPERTS), F32)], axis=1)
        b_r = _lane_row([(LANE_GRP, router_group_b[l]), (LANE_EXP, router_expert_b[l])])[None, :]
        xn, hp, route, counts = _merge(
            oa.reshape(n, -1), ob.reshape(n, -1), zc.reshape(n, -1), gates, x2,
            w_a[l].astype(BF16), w_b[l].astype(BF16), w_c[l].astype(BF16), w_out[l].astype(BF16),
            norm_ffn[l][None, :], w_r, b_r)

        x2 = _moe(hp, route, counts, xn, expert_w1[l].astype(BF16), expert_w3[l].astype(BF16),
                  expert_w2[l].astype(BF16), norm_final[None, :], tm=tm_moe, rb=rb,
                  final_norm=(l == depth - 1))
    return x2.reshape(b, t, d)
```

```python
import functools

import jax
import jax.numpy as jnp
import numpy as np
from jax import lax
from jax.experimental import pallas as pl
from jax.experimental.pallas import tpu as pltpu

F32 = jnp.float32
BF16 = jnp.bfloat16
U32 = jnp.uint32
I32 = jnp.int32
HIGHEST = lax.Precision.HIGHEST

EPS = 1e-6
LOG2E = 1.4426950408889634
LANES = 128
H_DN, DK_DN, DV_DN = 4, 128, 128
SHORT_CONV = 4
CHUNK = 64
H_FOX, D_FOX = 4, 128
C_CONV = 512
CONV_WIDTH = 31
N_GROUPS, EXPERTS_PER_GROUP = 4, 8
N_EXPERTS = N_GROUPS * EXPERTS_PER_GROUP
D_EXPERT = 256

W_DN = 3 * H_DN * DK_DN
W_FOX = 3 * H_FOX * D_FOX
VMEM_LIMIT = 56 * 1024 * 1024

LANE_BETA, LANE_G, LANE_F = 0, 4, 8
LANE_GRP, LANE_EXP = 0, 32
R_GATE0, R_GATE1, R_EID0, R_EID1, R_RANK0, R_RANK1 = 0, 1, 2, 3, 4, 5

NN = (((1,), (0,)), ((), ()))
NT = (((1,), (1,)), ((), ()))
TN = (((0,), (0,)), ((), ()))


def _mm(a, b, dims=NN):
    return lax.dot_general(a.astype(BF16), b.astype(BF16), dims, preferred_element_type=F32)


def _sigmoid(x):
    return 0.5 * jnp.tanh(0.5 * x) + 0.5


def _silu(x):
    return x * _sigmoid(x)


def _params(*sem):
    return pltpu.CompilerParams(dimension_semantics=sem, vmem_limit_bytes=VMEM_LIMIT)


def _in_proj_body(x_ref, g_ref, w_ref, ws_ref, dn_ref, za_ref, fox_ref, uc_ref, gates_ref, small_ref,
                  *, col_chunk):
    x = x_ref[...]
    h = x * lax.rsqrt(jnp.mean(x * x, axis=-1, keepdims=True) + EPS) * g_ref[...]
    hb = h.astype(BF16)
    col = 0
    for ref in (dn_ref, za_ref, fox_ref, uc_ref, gates_ref):
        width = ref.shape[-1]
        for c in range(0, width, col_chunk):
            ref[:, c:c + col_chunk] = jnp.dot(
                hb, w_ref[:, col + c:col + c + col_chunk], preferred_element_type=F32).astype(ref.dtype)
        col += width
    small_ref[...] = jnp.dot(hb, ws_ref[...], preferred_element_type=F32)


def _in_proj(x2, g, w, ws, *, tm=256, col_chunk=512):
    n, d = x2.shape
    d_gate = 3 * d
    widths = (W_DN, H_DN * DV_DN, W_FOX, 2 * C_CONV, d_gate)
    assert w.shape == (d, sum(widths)) and n % tm == 0
    out_shape = [jax.ShapeDtypeStruct((n, wd), BF16) for wd in widths]
    out_shape.append(jax.ShapeDtypeStruct((n, LANES), F32))
    row = lambda i: (i, 0)
    fixed = lambda i: (0, 0)
    return pl.pallas_call(
        functools.partial(_in_proj_body, col_chunk=col_chunk),
        grid=(n // tm,),
        in_specs=[pl.BlockSpec((tm, d), row), pl.BlockSpec((1, d), fixed),
                  pl.BlockSpec(w.shape, fixed), pl.BlockSpec(ws.shape, fixed)],
        out_specs=[pl.BlockSpec((tm, wd), row) for wd in widths] + [pl.BlockSpec((tm, LANES), row)],
        out_shape=out_shape,
        compiler_params=_params("parallel"),
        name="in_proj",
    )(x2, g, w, ws)


def _softplus_parts(z):
    t = jnp.log1p(jnp.exp(-jnp.abs(z)))
    return jnp.maximum(z, 0.0) + t, -(jnp.maximum(-z, 0.0) + t)


def _delta_body(qkv_ref, za_ref, sm_ref, cw_ref, par_ref, oa_ref, qaug_ref, kaug_ref,
                xs_ref, s_ref, carry_ref, *, ts):
    j = pl.program_id(1)
    halo = 8

    @pl.when(j == 0)
    def _():
        xs_ref[0:halo, :] = jnp.zeros((halo, W_DN), F32)
        s_ref[...] = jnp.zeros_like(s_ref)
        carry_ref[...] = jnp.zeros_like(carry_ref)

    @pl.when(j > 0)
    def _():
        xs_ref[0:halo, :] = xs_ref[ts:ts + halo, :]

    xs_ref[halo:halo + ts, :] = qkv_ref[0].astype(F32)

    def conv_silu(lane0):
        acc = None
        for k in range(SHORT_CONV):
            off = halo - (SHORT_CONV - 1) + k
            term = cw_ref[k:k + 1, lane0:lane0 + LANES] * xs_ref[off:off + ts, lane0:lane0 + LANES]
            acc = term if acc is None else acc + term
        return _silu(acc)

    def l2n(a):
        return a * lax.rsqrt(jnp.sum(a * a, axis=-1, keepdims=True) + EPS)

    sm = sm_ref[0]
    lane = lax.broadcasted_iota(I32, sm.shape, 1)
    sp, logsig = _softplus_parts(sm + par_ref[1:2, :])
    vals = jnp.where(lane < LANE_G, _sigmoid(sm),
                     jnp.where(lane < LANE_F, par_ref[0:1, :] * sp,
                               jnp.where(lane < LANE_F + H_FOX, logsig, 0.0)))
    row = lax.broadcasted_iota(I32, (ts, ts), 0)
    colm = lax.broadcasted_iota(I32, (ts, ts), 1)
    log_chunk = CHUNK.bit_length() - 1
    same = (row >> log_chunk) == (colm >> log_chunk)
    causal = (row >= colm) & same
    strict = (row > colm) & same

    hi = vals.astype(BF16)
    rem = vals - hi.astype(F32)
    mid = rem.astype(BF16)
    lo = (rem - mid.astype(F32)).astype(BF16)
    pieces = jnp.concatenate([hi, mid, lo], axis=-1)

    def cumsum(mask):
        y = jnp.dot(jnp.where(mask, 1.0, 0.0).astype(BF16), pieces, preferred_element_type=F32)
        return (y[:, :LANES] + y[:, LANES:2 * LANES]) + y[:, 2 * LANES:]

    ccum = cumsum(row >= colm) + carry_ref[...]
    gcum = cumsum(causal)
    carry_ref[...] = ccum[ts - 1:ts, :]
    gcum_t = gcum.T

    cl = ccum * LOG2E
    c_hi = cl.astype(BF16)
    c_rem = cl - c_hi.astype(F32)
    c_mid = c_rem.astype(BF16)
    c_lo = (c_rem - c_mid.astype(F32)).astype(BF16)
    c_pieces = jnp.concatenate([c_hi, c_mid, c_lo], axis=-1)
    pr = lax.broadcasted_iota(I32, (3 * LANES, LANES), 0)
    pc = lax.broadcasted_iota(I32, (3 * LANES, LANES), 1)
    src_lane, piece = pr & (LANES - 1), pr >> (LANES.bit_length() - 1)
    owned = (src_lane >= LANE_F) & (src_lane < LANE_F + H_FOX) & ((pc >> 3) == src_lane - LANE_F)
    place_q = jnp.where(owned & ((pc & 7) == piece), 1.0, 0.0).astype(BF16)
    place_k = jnp.where(owned & ((pc & 7) == piece + 3), -1.0, 0.0).astype(BF16)
    slot = lane & 7
    in_heads = lane < 8 * H_FOX
    ones_q = jnp.where(in_heads & (slot >= 3) & (slot < 6), 1.0, 0.0)
    ones_k = jnp.where(in_heads & (slot < 3), 1.0, 0.0)
    qaug_ref[0] = (jnp.dot(c_pieces, place_q, preferred_element_type=F32) + ones_q).astype(BF16)
    kaug_ref[0] = (jnp.dot(c_pieces, place_k, preferred_element_type=F32) + ones_k).astype(BF16)

    scale = DK_DN ** -0.5
    dn_norm = par_ref[2:3, :]

    for h in range(H_DN):
        q = l2n(conv_silu(h * DK_DN))
        k = l2n(conv_silu(H_DN * DK_DN + h * DK_DN))
        v = conv_silu(2 * H_DN * DK_DN + h * DV_DN)
        gcol = gcum[:, LANE_G + h:LANE_G + h + 1]
        grow = gcum_t[LANE_G + h:LANE_G + h + 1, :]
        beta = vals[:, LANE_BETA + h:LANE_BETA + h + 1]
        decay = jnp.where(causal, jnp.exp(jnp.where(causal, gcol - grow, 0.0)), 0.0)
        kb = k * beta
        a_mat = jnp.where(strict, _mm(kb, k, NT) * decay, 0.0)
        qk = jnp.where(causal, _mm(q * scale, k, NT) * decay, 0.0)
        egc = jnp.exp(gcol)

        inv_n = None
        s = 1
        while s < CHUNK:
            pair = (row >> s.bit_length()) == (colm >> s.bit_length())
            m = jnp.where(pair & ((row & s) != 0) & ((colm & s) == 0), a_mat, 0.0)
            if inv_n is None:
                inv_n = -m
            else:
                y = m + _mm(inv_n, m)
                inv_n = inv_n - (y + _mm(y, inv_n))
            s *= 2
        rhs = jnp.concatenate([v * beta, kb * egc], axis=-1)
        sol = rhs + _mm(inv_n, rhs)
        u, w = sol[:, :DV_DN], sol[:, DV_DN:]
        qg = q * (scale * egc)

        state = s_ref[h]
        v_prev = None
        for c in range(ts // CHUNK):
            rows = slice(c * CHUNK, (c + 1) * CHUNK)
            v_new = u[rows] - _mm(w[rows], state)
            if c % 2 == 0:
                v_pair = jnp.concatenate([v_new, jnp.zeros_like(v_new)], axis=0)
            else:
                v_pair = jnp.concatenate([v_prev, v_new], axis=0)
            pair_cols = slice((c // 2) * 2 * CHUNK, (c // 2 + 1) * 2 * CHUNK)
            o = _mm(qg[rows], state) + _mm(qk[rows, pair_cols], v_pair)
            glast = gcum[(c + 1) * CHUNK - 1:(c + 1) * CHUNK, LANE_G + h:LANE_G + h + 1]
            state = state * jnp.exp(glast) + _mm(k[rows] * jnp.exp(glast - gcol[rows]), v_new, TN)
            v_prev = v_new

            o = o * lax.rsqrt(jnp.mean(o * o, axis=-1, keepdims=True) + EPS) * dn_norm
            za = za_ref[0, rows, h * DV_DN:(h + 1) * DV_DN].astype(F32)
            oa_ref[0, rows, h * DV_DN:(h + 1) * DV_DN] = (o * _silu(za)).astype(oa_ref.dtype)
        s_ref[h] = state


def _delta_rule(qkv, za, small, conv_w, par, *, ts=256):
    b, t, _ = qkv.shape
    assert t % ts == 0 and ts % (2 * CHUNK) == 0
    blk = lambda width: pl.BlockSpec((1, ts, width), lambda bi, j: (bi, j, 0))
    fixed = lambda bi, j: (0, 0)
    return pl.pallas_call(
        functools.partial(_delta_body, ts=ts),
        grid=(b, t // ts),
        in_specs=[blk(W_DN), blk(H_DN * DV_DN), blk(LANES),
                  pl.BlockSpec(conv_w.shape, fixed), pl.BlockSpec(par.shape, fixed)],
        out_specs=[blk(H_DN * DV_DN), blk(LANES), blk(LANES)],
        out_shape=[jax.ShapeDtypeStruct((b, t, H_DN * DV_DN), BF16),
                   jax.ShapeDtypeStruct((b, t, LANES), BF16),
                   jax.ShapeDtypeStruct((b, t, LANES), BF16)],
        scratch_shapes=[pltpu.VMEM((ts + 8, W_DN), F32),
                        pltpu.VMEM((H_DN, DK_DN, DV_DN), F32),
                        pltpu.VMEM((1, LANES), F32)],
        compiler_params=_params("parallel", "arbitrary"),
        name="delta_rule",
    )(qkv, za, small, conv_w, par)


def _fox_body(q_ref, k_ref, v_ref, qa_ref, ka_ref, o_ref, m_ref, acc_ref, *, tq, tk):
    i = pl.program_id(1)
    m_ref[...] = jnp.full(m_ref.shape, -jnp.inf, F32)
    acc_ref[...] = jnp.zeros_like(acc_ref)
    head_lanes = [slice(h * D_FOX, (h + 1) * D_FOX) for h in range(H_FOX)]
    lane = lax.broadcasted_iota(I32, (tk, LANES), 1)
    own = [jnp.where((lane >> 3) == h, 1.0, 0.0).astype(BF16) for h in range(H_FOX)]
    ones = jnp.ones((tk, D_FOX), BF16)
    keep = lax.broadcasted_iota(I32, (tq, tk), 1) <= lax.broadcasted_iota(I32, (tq, tk), 0)

    def block(start, diag_offset):
        rows = slice(0 if diag_offset is None else diag_offset, tq)
        ka = ka_ref[0, pl.ds(start, tk), :]
        for h, lanes in enumerate(head_lanes):
            q_aug = jnp.concatenate([q_ref[0, rows, lanes], qa_ref[0, rows, :]], axis=1)
            k_aug = jnp.concatenate([k_ref[0, pl.ds(start, tk), lanes], ka * own[h]], axis=1)
            s = lax.dot_general(q_aug, k_aug, NT, preferred_element_type=F32)
            if diag_offset is not None:
                s = jnp.where(keep[:tq - diag_offset], s, -jnp.inf)
            m_prev = m_ref[h, rows]
            m_next = jnp.maximum(m_prev, jnp.max(s, axis=-1, keepdims=True))
            p = jnp.exp2(s - jnp.concatenate([m_next] * (tk // LANES), axis=1))
            alpha = jnp.exp2(m_prev - m_next)
            v_aug = jnp.concatenate([v_ref[0, pl.ds(start, tk), lanes], ones], axis=1)
            acc_ref[h, rows] = (jnp.concatenate([alpha, alpha], axis=1) * acc_ref[h, rows]
                                + jnp.dot(p.astype(BF16), v_aug, preferred_element_type=F32))
            m_ref[h, rows] = m_next

    def full_block(jb, carry):
        block(pl.multiple_of(jb * tk, tk), None)
        return carry

    lax.fori_loop(0, i * (tq // tk), full_block, 0)
    for d in range(tq // tk):
        block(pl.multiple_of(i * tq + d * tk, tk), d * tk)
    for h, lanes in enumerate(head_lanes):
        acc = acc_ref[h]
        o_ref[0, :, lanes] = (acc[:, :D_FOX] / acc[:, D_FOX:]).astype(o_ref.dtype)


def _fox_attention(qkv, qaug, kaug, *, tq=1024, tk=256):
    b, t, _ = qkv.shape
    hd = H_FOX * D_FOX
    assert t % tq == 0 and tq % tk == 0 and tk % LANES == 0
    return pl.pallas_call(
        functools.partial(_fox_body, tq=tq, tk=tk),
        grid=(b, t // tq),
        in_specs=[pl.BlockSpec((1, tq, hd), lambda bi, i: (bi, i, 0)),
                  pl.BlockSpec((1, t, hd), lambda bi, i: (bi, 0, 1)),
                  pl.BlockSpec((1, t, hd), lambda bi, i: (bi, 0, 2)),
                  pl.BlockSpec((1, tq, LANES), lambda bi, i: (bi, i, 0)),
                  pl.BlockSpec((1, t, LANES), lambda bi, i: (bi, 0, 0))],
        out_specs=pl.BlockSpec((1, tq, hd), lambda bi, i: (bi, i, 0)),
        out_shape=jax.ShapeDtypeStruct((b, t, hd), BF16),
        scratch_shapes=[pltpu.VMEM((H_FOX, tq, LANES), F32), pltpu.VMEM((H_FOX, tq, 2 * D_FOX), F32)],
        compiler_params=_params("parallel", "arbitrary"),
        name="fox_attention",
    )(qkv, qkv, qkv, qaug, kaug)


def _conv_body(cur_ref, halo_ref, dw_ref, vec_ref, o_ref, z_ref, y_ref, *, tt, hrows, rchunk):
    i = pl.program_id(1)
    cur = cur_ref[0].astype(F32)
    z_ref[hrows:hrows + tt, :] = cur[:, :C_CONV] * _sigmoid(cur[:, C_CONV:])
    hal = halo_ref[0].astype(F32)
    zh = hal[:, :C_CONV] * _sigmoid(hal[:, C_CONV:])
    z_ref[0:hrows, :] = jnp.where(i > 0, zh, 0.0)

    for lg in range(C_CONV // LANES):
        lanes = slice(lg * LANES, (lg + 1) * LANES)
        for r0 in range(0, tt, rchunk):
            acc = jnp.broadcast_to(vec_ref[0:1, lanes], (rchunk, LANES))
            for k in range(CONV_WIDTH):
                off = hrows - (CONV_WIDTH - 1) + k + r0
                acc = acc + dw_ref[k:k + 1, lanes] * z_ref[off:off + rchunk, lanes]
            y_ref[r0:r0 + rchunk, lanes] = acc

    y = y_ref[...]
    mu = jnp.mean(y, axis=-1, keepdims=True)
    yc = y - mu
    var = jnp.mean(yc * yc, axis=-1, keepdims=True)
    yn = yc * lax.rsqrt(var + EPS) * vec_ref[1:2, :] + vec_ref[2:3, :]
    o_ref[0] = _silu(yn).astype(o_ref.dtype)


def _conv_module(uc, dw, vec, *, tt=256, hrows=32, rchunk=64):
    b, t, _ = uc.shape
    assert t % tt == 0 and tt % hrows == 0 and hrows >= CONV_WIDTH - 1
    ratio = tt // hrows
    return pl.pallas_call(
        functools.partial(_conv_body, tt=tt, hrows=hrows, rchunk=rchunk),
        grid=(b, t // tt),
        in_specs=[pl.BlockSpec((1, tt, 2 * C_CONV), lambda bi, i: (bi, i, 0)),
                  pl.BlockSpec((1, hrows, 2 * C_CONV), lambda bi, i: (bi, jnp.maximum(i * ratio - 1, 0), 0)),
                  pl.BlockSpec(dw.shape, lambda bi, i: (0, 0)),
                  pl.BlockSpec(vec.shape, lambda bi, i: (0, 0))],
        out_specs=pl.BlockSpec((1, tt, C_CONV), lambda bi, i: (bi, i, 0)),
        out_shape=jax.ShapeDtypeStruct((b, t, C_CONV), BF16),
        scratch_shapes=[pltpu.VMEM((tt + hrows, C_CONV), F32), pltpu.VMEM((tt, C_CONV), F32)],
        compiler_params=_params("parallel", "parallel"),
        name="conv_module",
    )(uc, uc, dw, vec)


def _merge_body(oa_ref, ob_ref, zc_ref, gates_ref, x_ref, wa_ref, wb_ref, wc_ref, wo_ref, g_ref,
                wr_ref, br_ref, xn_ref, hp_ref, route_ref, cnt_ref, run_ref, *, tm):
    step = pl.program_id(0)
    d = x_ref.shape[-1]

    @pl.when(step == 0)
    def _():
        run_ref[...] = jnp.zeros_like(run_ref)

    merged = None
    for idx, (m_ref, w_ref) in enumerate(((oa_ref, wa_ref), (ob_ref, wb_ref), (zc_ref, wc_ref))):
        y = jnp.dot(m_ref[...], w_ref[...], preferred_element_type=F32)
        term = _sigmoid(gates_ref[:, idx * d:(idx + 1) * d].astype(F32)) * y
        merged = term if merged is None else merged + term
    xn = x_ref[...] + jnp.dot(merged.astype(BF16), wo_ref[...], preferred_element_type=F32)
    xn_ref[...] = xn
    h2 = xn * lax.rsqrt(jnp.mean(xn * xn, axis=-1, keepdims=True) + EPS) * g_ref[...]

    lo = pltpu.bitcast(h2[:, :d // 2].astype(BF16).astype(F32), U32) >> 16
    hi = pltpu.bitcast(h2[:, d // 2:].astype(BF16).astype(F32), U32) & jnp.uint32(0xFFFF0000)
    hp_ref[...] = lo | hi

    logits = _mm(h2, wr_ref[...]) + br_ref[...]
    lane = lax.broadcasted_iota(I32, logits.shape, 1)
    big = jnp.int32(4 * LANES)
    in_grp = lane < N_GROUPS
    gl = jnp.where(in_grp, logits, -jnp.inf)
    gmax = jnp.max(gl, axis=-1, keepdims=True)
    grp = jnp.min(jnp.where(gl == gmax, lane, big), axis=-1, keepdims=True)
    p_grp = 1.0 / jnp.sum(jnp.where(in_grp, jnp.exp(gl - gmax), 0.0), axis=-1, keepdims=True)
    e_lo = LANE_EXP + grp * EXPERTS_PER_GROUP
    el = jnp.where((lane >= e_lo) & (lane < e_lo + EXPERTS_PER_GROUP), logits, -jnp.inf)
    v0 = jnp.max(el, axis=-1, keepdims=True)
    i0 = jnp.min(jnp.where(el == v0, lane, big), axis=-1, keepdims=True)
    el1 = jnp.where(lane == i0, -jnp.inf, el)
    v1 = jnp.max(el1, axis=-1, keepdims=True)
    i1 = jnp.min(jnp.where(el1 == v1, lane, big), axis=-1, keepdims=True)
    e1 = jnp.exp(v1 - v0)
    gate0 = p_grp / (1.0 + e1)
    gate1 = p_grp * e1 / (1.0 + e1)

    hot0 = lane == i0
    hot1 = lane == i1
    onehot = jnp.where(hot0 | hot1, 1.0, 0.0)
    ri = lax.broadcasted_iota(I32, (tm, tm), 0)
    cj = lax.broadcasted_iota(I32, (tm, tm), 1)
    prefix = _mm(jnp.where(cj < ri, 1.0, 0.0), onehot) + run_ref[...]
    rank0 = jnp.sum(jnp.where(hot0, prefix, 0.0), axis=-1, keepdims=True)
    rank1 = jnp.sum(jnp.where(hot1, prefix, 0.0), axis=-1, keepdims=True)
    run_ref[...] = run_ref[...] + jnp.sum(onehot, axis=0, keepdims=True)
    cnt_ref[...] = jnp.broadcast_to(run_ref[...], cnt_ref.shape)

    route = jnp.where(lane == R_GATE0, gate0, 0.0)
    route = jnp.where(lane == R_GATE1, gate1, route)
    route = jnp.where(lane == R_EID0, (i0 - LANE_EXP).astype(F32), route)
    route = jnp.where(lane == R_EID1, (i1 - LANE_EXP).astype(F32), route)
    route = jnp.where(lane == R_RANK0, rank0, route)
    route = jnp.where(lane == R_RANK1, rank1, route)
    route_ref[...] = route


def _merge(oa, ob, zc, gates, x2, wa, wb, wc, wo, g, wr, br, *, tm=256):
    n, d = x2.shape
    assert n % tm == 0
    row = lambda i: (i, 0)
    fixed = lambda i: (0, 0)
    full = lambda a: pl.BlockSpec(a.shape, fixed)
    return pl.pallas_call(
        functools.partial(_merge_body, tm=tm),
        grid=(n // tm,),
        in_specs=[pl.BlockSpec((tm, oa.shape[1]), row), pl.BlockSpec((tm, ob.shape[1]), row),
                  pl.BlockSpec((tm, zc.shape[1]), row), pl.BlockSpec((tm, 3 * d), row),
                  pl.BlockSpec((tm, d), row),
                  full(wa), full(wb), full(wc), full(wo), full(g), full(wr), full(br)],
        out_specs=[pl.BlockSpec((tm, d), row), pl.BlockSpec((tm, d // 2), row),
                   pl.BlockSpec((tm, LANES), row), pl.BlockSpec((8, LANES), fixed)],
        out_shape=[jax.ShapeDtypeStruct((n, d), F32), jax.ShapeDtypeStruct((n, d // 2), U32),
                   jax.ShapeDtypeStruct((n, LANES), F32), jax.ShapeDtypeStruct((8, LANES), F32)],
        scratch_shapes=[pltpu.VMEM((1, LANES), F32)],
        compiler_params=_params("arbitrary"),
        name="merge_router",
    )(oa, ob, zc, gates, x2, wa, wb, wc, wo, g, wr, br)


def _row_copy(src_ref, src_row, dst_ref, dst_row, sem):
    return pltpu.make_async_copy(src_ref.at[pl.ds(src_row, 1)], dst_ref.at[pl.ds(dst_row, 1)], sem)


def _dispatch_body(dest_ref, hp_ref, xs_in_ref, xs_ref, sem, *, tm):
    del xs_in_ref

    def issue(n, carry):
        _row_copy(hp_ref, n, xs_ref, dest_ref[0, 0, n], sem).start()
        _row_copy(hp_ref, n, xs_ref, dest_ref[0, 0, tm + n], sem).start()
        return carry

    lax.fori_loop(0, tm, issue, 0)

    def drain(n, carry):
        _row_copy(hp_ref, 0, xs_ref, 0, sem).wait()
        _row_copy(hp_ref, 0, xs_ref, 0, sem).wait()
        return carry

    lax.fori_loop(0, tm, drain, 0)


def _dispatch(dest3, hp, xs_zero, *, tm):
    n = hp.shape[0]
    return pl.pallas_call(
        functools.partial(_dispatch_body, tm=tm),
        grid=(n // tm,),
        in_specs=[pl.BlockSpec((1, 1, 2 * tm), lambda i: (i, 0, 0), memory_space=pltpu.SMEM),
                  pl.BlockSpec((tm, hp.shape[1]), lambda i: (i, 0)), pl.BlockSpec(memory_space=pl.ANY)],
        out_specs=pl.BlockSpec(memory_space=pl.ANY),
        out_shape=jax.ShapeDtypeStruct(xs_zero.shape, xs_zero.dtype),
        scratch_shapes=[pltpu.SemaphoreType.DMA(())],
        input_output_aliases={2: 0},
        compiler_params=pltpu.CompilerParams(dimension_semantics=("arbitrary",), has_side_effects=True),
        name="moe_dispatch",
    )(dest3, hp, xs_zero)


def _expert_body(be_ref, nu_ref, xs_ref, w1_ref, w3_ref, w2_ref, ys_ref):
    del be_ref
    live = pl.program_id(0) < nu_ref[0]

    @pl.when(jnp.logical_not(live))
    def _():
        ys_ref[...] = jnp.zeros_like(ys_ref)

    @pl.when(live)
    def _():
        xp = xs_ref[...]
        half = xp.shape[-1]
        lo = pltpu.bitcast(xp << 16, F32).astype(BF16)
        hi = pltpu.bitcast(xp & jnp.uint32(0xFFFF0000), F32).astype(BF16)

        def up(w_ref):
            return (jnp.dot(lo, w_ref[0, :half, :], preferred_element_type=F32)
                    + jnp.dot(hi, w_ref[0, half:, :], preferred_element_type=F32))

        act = (_silu(up(w1_ref)) * up(w3_ref)).astype(BF16)
        ys_ref[...] = jnp.dot(act, w2_ref[0], preferred_element_type=F32)


def _experts(blk_eid, n_used, xs, w1, w3, w2, *, rb):
    p, half = xs.shape
    d = 2 * half
    nb = p // rb
    used = lambda i, be, nu: jnp.minimum(i, nu[0] - 1)
    wmap = lambda i, be, nu: (be[used(i, be, nu)], 0, 0)
    grid_spec = pltpu.PrefetchScalarGridSpec(
        num_scalar_prefetch=2,
        grid=(nb,),
        in_specs=[pl.BlockSpec((rb, half), lambda i, be, nu: (used(i, be, nu), 0)),
                  pl.BlockSpec((1, d, D_EXPERT), wmap), pl.BlockSpec((1, d, D_EXPERT), wmap),
                  pl.BlockSpec((1, D_EXPERT, d), wmap)],
        out_specs=pl.BlockSpec((rb, d), lambda i, be, nu: (i, 0)),
    )
    return pl.pallas_call(
        _expert_body,
        grid_spec=grid_spec,
        out_shape=jax.ShapeDtypeStruct((p, d), F32),
        compiler_params=_params("arbitrary"),
        name="moe_experts",
    )(blk_eid, n_used, xs, w1, w3, w2)


def _combine_body(dest_ref, ys_ref, x_ref, route_ref, g_ref, o_ref, buf_ref, sem, *, tm, final_norm):
    def issue(n, carry):
        _row_copy(ys_ref, dest_ref[0, 0, n], buf_ref.at[0], n, sem).start()
        _row_copy(ys_ref, dest_ref[0, 0, tm + n], buf_ref.at[1], n, sem).start()
        return carry

    lax.fori_loop(0, tm, issue, 0)

    def drain(n, carry):
        _row_copy(ys_ref, 0, buf_ref.at[0], 0, sem).wait()
        _row_copy(ys_ref, 0, buf_ref.at[1], 0, sem).wait()
        return carry

    lax.fori_loop(0, tm, drain, 0)

    route = route_ref[...]
    out = (x_ref[...] + route[:, R_GATE0:R_GATE0 + 1] * buf_ref[0]
           + route[:, R_GATE1:R_GATE1 + 1] * buf_ref[1])
    if final_norm:
        out = out * lax.rsqrt(jnp.mean(out * out, axis=-1, keepdims=True) + EPS) * g_ref[...]
    o_ref[...] = out


def _combine(dest3, ys, xn, route, g, *, tm, final_norm):
    n, d = xn.shape
    row = lambda i: (i, 0)
    return pl.pallas_call(
        functools.partial(_combine_body, tm=tm, final_norm=final_norm),
        grid=(n // tm,),
        in_specs=[pl.BlockSpec((1, 1, 2 * tm), lambda i: (i, 0, 0), memory_space=pltpu.SMEM),
                  pl.BlockSpec(memory_space=pl.ANY),
                  pl.BlockSpec((tm, d), row), pl.BlockSpec((tm, LANES), row),
                  pl.BlockSpec((1, d), lambda i: (0, 0))],
        out_specs=pl.BlockSpec((tm, d), row),
        out_shape=jax.ShapeDtypeStruct((n, d), F32),
        scratch_shapes=[pltpu.VMEM((2, tm, d), F32), pltpu.SemaphoreType.DMA(())],
        compiler_params=_params("arbitrary"),
        name="moe_combine",
    )(dest3, ys, xn, route, g)


def _moe(hp, route, counts, xn, w1, w3, w2, g_final, *, tm, rb, final_norm):
    n = hp.shape[0]
    cnt = counts[0, LANE_EXP:LANE_EXP + N_EXPERTS].astype(I32)
    nblk = (cnt + rb - 1) // rb
    bend = jnp.cumsum(nblk)
    pstart = (bend - nblk) * rb
    nb = (2 * n) // rb + N_EXPERTS
    n_used = bend[-1:].astype(I32)
    blk_eid = jnp.minimum(jnp.sum(bend[None, :] <= jnp.arange(nb, dtype=I32)[:, None], axis=1),
                          N_EXPERTS - 1).astype(I32)
    eid = route[:, R_EID0:R_EID1 + 1].astype(I32)
    rank = route[:, R_RANK0:R_RANK1 + 1].astype(I32)
    dest = pstart[eid] + rank
    dest3 = dest.reshape(n // tm, tm, 2).transpose(0, 2, 1).reshape(n // tm, 1, 2 * tm)

    xs = _dispatch(dest3, hp, jnp.zeros((nb * rb, hp.shape[1]), hp.dtype), tm=tm)
    ys = _experts(blk_eid, n_used, xs, w1, w3, w2, rb=rb)
    return _combine(dest3, ys, xn, route, g_final, tm=tm, final_norm=final_norm)


def _lane_row(pairs):
    row = jnp.zeros((LANES,), F32)
    for off, vec in pairs:
        row = row.at[off:off + vec.shape[0]].set(vec.astype(F32))
    return row


def kernel(x, norm_mix, w_in, conv_qkv, dn_a_log, dn_dt_bias, dn_norm, fox_bias, conv_dw, conv_dw_b,
           conv_ln_g, conv_ln_b, w_a, w_b, w_c, w_out, norm_ffn, router_group_w, router_group_b,
           router_expert_w, router_expert_b, expert_w1, expert_w3, expert_w2, norm_final):
    b, t, d = x.shape
    n = b * t
    depth = w_in.shape[0]
    qk_dn = H_DN * DK_DN
    in_sizes = (qk_dn, qk_dn, H_DN * DV_DN, H_DN * DV_DN, H_DN, H_DN,
                H_FOX * D_FOX, H_FOX * D_FOX, H_FOX * D_FOX, H_FOX, 2 * C_CONV, d, d, d)
    splits = np.cumsum(in_sizes)[:-1].tolist()
    tm_moe, rb = 256, 256

    x2 = x.reshape(n, d)
    for l in range(depth):
        (qa, ka, va, za, ba, aa, qb, kb, vb, fb, uc, ga, gb, gc) = jnp.split(w_in[l], splits, axis=1)
        w_main = jnp.concatenate([qa, ka, va, za, qb * (D_FOX ** -0.5 * LOG2E), kb, vb, uc, ga, gb, gc],
                                 axis=1).astype(BF16)
        w_small = jnp.concatenate([ba, aa, fb, jnp.zeros((d, LANES - 3 * H_DN), F32)], axis=1).astype(BF16)
        dn_qkv, za_p, fox_qkv, uc_p, gates, small = _in_proj(x2, norm_mix[l][None, :], w_main, w_small)

        par = jnp.stack([_lane_row([(LANE_G, -jnp.exp(dn_a_log[l]))]),
                         _lane_row([(LANE_G, dn_dt_bias[l]), (LANE_F, fox_bias[l])]),
                         _lane_row([(0, dn_norm[l])])] + [jnp.zeros((LANES,), F32)] * 5)
        oa, qaug, kaug = _delta_rule(dn_qkv.reshape(b, t, -1), za_p.reshape(b, t, -1),
                                     small.reshape(b, t, LANES), conv_qkv[l], par)
        ob = _fox_attention(fox_qkv.reshape(b, t, -1), qaug, kaug)
        vec = jnp.stack([conv_dw_b[l], conv_ln_g[l], conv_ln_b[l]] + [jnp.zeros((C_CONV,), F32)] * 5)
        zc = _conv_module(uc_p.reshape(b, t, -1), conv_dw[l], vec)

        w_r = jnp.concatenate([router_group_w[l], jnp.zeros((d, LANE_EXP - N_GROUPS), F32),
                               router_expert_w[l], jnp.zeros((d, LANES - LANE_EXP - N_EXPERTS), F32)], axis=1)
        b_r = _lane_row([(LANE_GRP, router_group_b[l]), (LANE_EXP, router_expert_b[l])])[None, :]
        xn, hp, route, counts = _merge(
            oa.reshape(n, -1), ob.reshape(n, -1), zc.reshape(n, -1), gates, x2,
            w_a[l].astype(BF16), w_b[l].astype(BF16), w_c[l].astype(BF16), w_out[l].astype(BF16),
            norm_ffn[l][None, :], w_r.astype(BF16), b_r)

        x2 = _moe(hp, route, counts, xn, expert_w1[l].astype(BF16), expert_w3[l].astype(BF16),
                  expert_w2[l].astype(BF16), norm_final[None, :], tm=tm_moe, rb=rb,
                  final_norm=(l == depth - 1))
    return x2.reshape(b, t, d)
```

```python
import functools

import jax
import jax.numpy as jnp
import numpy as np
from jax import lax
from jax.experimental import pallas as pl
from jax.experimental.pallas import tpu as pltpu

F32 = jnp.float32
BF16 = jnp.bfloat16
U32 = jnp.uint32
I32 = jnp.int32
HIGHEST = lax.Precision.HIGHEST

EPS = 1e-6
LOG2E = 1.4426950408889634
LANES = 128
H_DN, DK_DN, DV_DN = 4, 128, 128
SHORT_CONV = 4
CHUNK = 64
H_FOX, D_FOX = 4, 128
C_CONV = 512
CONV_WIDTH = 31
N_GROUPS, EXPERTS_PER_GROUP = 4, 8
N_EXPERTS = N_GROUPS * EXPERTS_PER_GROUP
D_EXPERT = 256

W_DN = 3 * H_DN * DK_DN
W_FOX = 3 * H_FOX * D_FOX
VMEM_LIMIT = 56 * 1024 * 1024

LANE_BETA, LANE_G, LANE_F = 0, 4, 8
LANE_GRP, LANE_EXP = 0, 32
R_GATE0, R_GATE1, R_EID0, R_EID1, R_RANK0, R_RANK1 = 0, 1, 2, 3, 4, 5

NN = (((1,), (0,)), ((), ()))
NT = (((1,), (1,)), ((), ()))
TN = (((0,), (0,)), ((), ()))


def _mm(a, b, dims=NN):
    return lax.dot_general(a.astype(BF16), b.astype(BF16), dims, preferred_element_type=F32)


def _sigmoid(x):
    return 0.5 * jnp.tanh(0.5 * x) + 0.5


def _silu(x):
    return x * _sigmoid(x)


def _params(*sem):
    return pltpu.CompilerParams(dimension_semantics=sem, vmem_limit_bytes=VMEM_LIMIT)


def _in_proj_body(x_ref, g_ref, w_ref, ws_ref, dn_ref, za_ref, fox_ref, uc_ref, gates_ref, small_ref,
                  *, col_chunk):
    x = x_ref[...]
    h = x * lax.rsqrt(jnp.mean(x * x, axis=-1, keepdims=True) + EPS) * g_ref[...]
    hb = h.astype(BF16)
    col = 0
    for ref in (dn_ref, za_ref, fox_ref, uc_ref, gates_ref):
        width = ref.shape[-1]
        for c in range(0, width, col_chunk):
            ref[:, c:c + col_chunk] = jnp.dot(
                hb, w_ref[:, col + c:col + c + col_chunk], preferred_element_type=F32).astype(ref.dtype)
        col += width
    small_ref[...] = jnp.dot(hb, ws_ref[...], preferred_element_type=F32)


def _in_proj(x2, g, w, ws, *, tm=512, col_chunk=512):
    n, d = x2.shape
    d_gate = 3 * d
    widths = (W_DN, H_DN * DV_DN, W_FOX, 2 * C_CONV, d_gate)
    assert w.shape == (d, sum(widths)) and n % tm == 0
    out_shape = [jax.ShapeDtypeStruct((n, wd), BF16) for wd in widths]
    out_shape.append(jax.ShapeDtypeStruct((n, LANES), F32))
    row = lambda i: (i, 0)
    fixed = lambda i: (0, 0)
    return pl.pallas_call(
        functools.partial(_in_proj_body, col_chunk=col_chunk),
        grid=(n // tm,),
        in_specs=[pl.BlockSpec((tm, d), row), pl.BlockSpec((1, d), fixed),
                  pl.BlockSpec(w.shape, fixed, pipeline_mode=pl.Buffered(1)),
                  pl.BlockSpec(ws.shape, fixed, pipeline_mode=pl.Buffered(1))],
        out_specs=[pl.BlockSpec((tm, wd), row) for wd in widths] + [pl.BlockSpec((tm, LANES), row)],
        out_shape=out_shape,
        compiler_params=_params("parallel"),
        name="in_proj",
    )(x2, g, w, ws)


def _softplus_parts(z):
    t = jnp.log1p(jnp.exp(-jnp.abs(z)))
    return jnp.maximum(z, 0.0) + t, -(jnp.maximum(-z, 0.0) + t)


def _delta_body(qkv_ref, za_ref, sm_ref, cw_ref, par_ref, oa_ref, qaug_ref, kaug_ref,
                xs_ref, s_ref, carry_ref, *, ts):
    j = pl.program_id(1)
    halo = 8

    @pl.when(j == 0)
    def _():
        xs_ref[0:halo, :] = jnp.zeros((halo, W_DN), F32)
        s_ref[...] = jnp.zeros_like(s_ref)
        carry_ref[...] = jnp.zeros_like(carry_ref)

    @pl.when(j > 0)
    def _():
        xs_ref[0:halo, :] = xs_ref[ts:ts + halo, :]

    xs_ref[halo:halo + ts, :] = qkv_ref[0].astype(F32)

    def conv_silu(lane0):
        acc = None
        for k in range(SHORT_CONV):
            off = halo - (SHORT_CONV - 1) + k
            term = cw_ref[k:k + 1, lane0:lane0 + LANES] * xs_ref[off:off + ts, lane0:lane0 + LANES]
            acc = term if acc is None else acc + term
        return _silu(acc)

    def l2n(a):
        return a * lax.rsqrt(jnp.sum(a * a, axis=-1, keepdims=True) + EPS)

    sm = sm_ref[0]
    lane = lax.broadcasted_iota(I32, sm.shape, 1)
    sp, logsig = _softplus_parts(sm + par_ref[1:2, :])
    vals = jnp.where(lane < LANE_G, _sigmoid(sm),
                     jnp.where(lane < LANE_F, par_ref[0:1, :] * sp,
                               jnp.where(lane < LANE_F + H_FOX, logsig, 0.0)))
    row = lax.broadcasted_iota(I32, (ts, ts), 0)
    colm = lax.broadcasted_iota(I32, (ts, ts), 1)
    log_chunk = CHUNK.bit_length() - 1
    causal = (row >= colm) & ((row >> log_chunk) == (colm >> log_chunk))

    hi = vals.astype(BF16)
    rem = vals - hi.astype(F32)
    mid = rem.astype(BF16)
    lo = (rem - mid.astype(F32)).astype(BF16)
    pieces = jnp.concatenate([hi, mid, lo], axis=-1)

    def cumsum(mask):
        y = jnp.dot(jnp.where(mask, 1.0, 0.0).astype(BF16), pieces, preferred_element_type=F32)
        return (y[:, :LANES] + y[:, LANES:2 * LANES]) + y[:, 2 * LANES:]

    ccum = cumsum(row >= colm) + carry_ref[...]
    gcum = cumsum(causal)
    carry_ref[...] = ccum[ts - 1:ts, :]
    gcum_t = gcum.T

    cl = ccum * LOG2E
    c_hi = cl.astype(BF16)
    c_rem = cl - c_hi.astype(F32)
    c_mid = c_rem.astype(BF16)
    c_lo = (c_rem - c_mid.astype(F32)).astype(BF16)
    c_pieces = jnp.concatenate([c_hi, c_mid, c_lo], axis=-1)
    pr = lax.broadcasted_iota(I32, (3 * LANES, LANES), 0)
    pc = lax.broadcasted_iota(I32, (3 * LANES, LANES), 1)
    src_lane, piece = pr & (LANES - 1), pr >> (LANES.bit_length() - 1)
    owned = (src_lane >= LANE_F) & (src_lane < LANE_F + H_FOX) & ((pc >> 3) == src_lane - LANE_F)
    place_q = jnp.where(owned & ((pc & 7) == piece), 1.0, 0.0).astype(BF16)
    place_k = jnp.where(owned & ((pc & 7) == piece + 3), -1.0, 0.0).astype(BF16)
    slot = lane & 7
    in_heads = lane < 8 * H_FOX
    ones_q = jnp.where(in_heads & (slot >= 3) & (slot < 6), 1.0, 0.0)
    ones_k = jnp.where(in_heads & (slot < 3), 1.0, 0.0)
    qaug_ref[0] = (jnp.dot(c_pieces, place_q, preferred_element_type=F32) + ones_q).astype(BF16)
    kaug_ref[0] = (jnp.dot(c_pieces, place_k, preferred_element_type=F32) + ones_k).astype(BF16)

    scale = DK_DN ** -0.5
    dn_norm = par_ref[2:3, :]

    pw = 2 * CHUNK
    prow = lax.broadcasted_iota(I32, (pw, pw), 0)
    pcol = lax.broadcasted_iota(I32, (pw, pw), 1)
    same = (prow >> log_chunk) == (pcol >> log_chunk)
    causal_p = (prow >= pcol) & same
    strict_p = (prow > pcol) & same
    levels = []
    s = 1
    while s < CHUNK:
        levels.append(((prow >> s.bit_length()) == (pcol >> s.bit_length()))
                      & ((prow & s) != 0) & ((pcol & s) == 0))
        s *= 2

    heads = range(H_DN)
    pairs = range(ts // pw)
    q, k, v = [], [], []
    for h in heads:
        q.append(l2n(conv_silu(h * DK_DN)))
        k.append(l2n(conv_silu(H_DN * DK_DN + h * DK_DN)))
        v.append(conv_silu(2 * H_DN * DK_DN + h * DV_DN))

    ctx = []
    for h in heads:
        for p in pairs:
            pr = slice(p * pw, (p + 1) * pw)
            gcol = gcum[pr, LANE_G + h:LANE_G + h + 1]
            grow = gcum_t[LANE_G + h:LANE_G + h + 1, pr]
            beta = vals[pr, LANE_BETA + h:LANE_BETA + h + 1]
            decay = jnp.where(causal_p, jnp.exp(jnp.where(causal_p, gcol - grow, 0.0)), 0.0)
            egc = jnp.exp(gcol)
            kp = k[h][pr]
            kb = kp * beta
            ctx.append(dict(
                h=h, p=p, gcol=gcol, kp=kp,
                a=jnp.where(strict_p, _mm(kb, kp, NT) * decay, 0.0),
                qk=jnp.where(causal_p, _mm(q[h][pr] * scale, kp, NT) * decay, 0.0),
                rhs=jnp.concatenate([v[h][pr] * beta, kb * egc], axis=-1),
                qg=q[h][pr] * (scale * egc)))

    for c in ctx:
        c["n"] = -jnp.where(levels[0], c["a"], 0.0)
    for level in levels[1:]:
        for c in ctx:
            m = jnp.where(level, c["a"], 0.0)
            c["y"] = m + _mm(c["n"], m)
        for c in ctx:
            c["n"] = c["n"] - (c["y"] + _mm(c["y"], c["n"]))
    for c in ctx:
        c["sol"] = c["rhs"] + _mm(c["n"], c["rhs"])

    state = [s_ref[h] for h in heads]
    for p in pairs:
        group = [c for c in ctx if c["p"] == p]
        v_prev = [None] * H_DN
        for ch in range(2):
            rows = slice(ch * CHUNK, (ch + 1) * CHUNK)
            out_rows = slice(p * pw + ch * CHUNK, p * pw + (ch + 1) * CHUNK)
            v_new = [c["sol"][rows, :DV_DN] - _mm(c["sol"][rows, DV_DN:], state[c["h"]]) for c in group]
            for c, vn in zip(group, v_new):
                h = c["h"]
                v_pair = jnp.concatenate([vn, jnp.zeros_like(vn)] if ch == 0 else [v_prev[h], vn], axis=0)
                o = _mm(c["qg"][rows], state[h]) + _mm(c["qk"][rows], v_pair)
                glast = c["gcol"][(ch + 1) * CHUNK - 1:(ch + 1) * CHUNK]
                state[h] = (state[h] * jnp.exp(glast)
                            + _mm(c["kp"][rows] * jnp.exp(glast - c["gcol"][rows]), vn, TN))
                v_prev[h] = vn
                o = o * lax.rsqrt(jnp.mean(o * o, axis=-1, keepdims=True) + EPS) * dn_norm
                za = za_ref[0, out_rows, h * DV_DN:(h + 1) * DV_DN].astype(F32)
                oa_ref[0, out_rows, h * DV_DN:(h + 1) * DV_DN] = (o * _silu(za)).astype(oa_ref.dtype)
    for h in heads:
        s_ref[h] = state[h]


def _delta_rule(qkv, za, small, conv_w, par, *, ts=256):
    b, t, _ = qkv.shape
    assert t % ts == 0 and ts % (2 * CHUNK) == 0
    blk = lambda width: pl.BlockSpec((1, ts, width), lambda bi, j: (bi, j, 0))
    fixed = lambda bi, j: (0, 0)
    return pl.pallas_call(
        functools.partial(_delta_body, ts=ts),
        grid=(b, t // ts),
        in_specs=[blk(W_DN), blk(H_DN * DV_DN), blk(LANES),
                  pl.BlockSpec(conv_w.shape, fixed), pl.BlockSpec(par.shape, fixed)],
        out_specs=[blk(H_DN * DV_DN), blk(LANES), blk(LANES)],
        out_shape=[jax.ShapeDtypeStruct((b, t, H_DN * DV_DN), BF16),
                   jax.ShapeDtypeStruct((b, t, LANES), BF16),
                   jax.ShapeDtypeStruct((b, t, LANES), BF16)],
        scratch_shapes=[pltpu.VMEM((ts + 8, W_DN), F32),
                        pltpu.VMEM((H_DN, DK_DN, DV_DN), F32),
                        pltpu.VMEM((1, LANES), F32)],
        compiler_params=_params("parallel", "arbitrary"),
        name="delta_rule",
    )(qkv, za, small, conv_w, par)


def _fox_body(q_ref, k_ref, v_ref, qa_ref, ka_ref, o_ref, m_ref, acc_ref, *, tq, tk):
    i = pl.program_id(1)
    m_ref[...] = jnp.full(m_ref.shape, -jnp.inf, F32)
    acc_ref[...] = jnp.zeros_like(acc_ref)
    head_lanes = [slice(h * D_FOX, (h + 1) * D_FOX) for h in range(H_FOX)]
    lane = lax.broadcasted_iota(I32, (tk, LANES), 1)
    own = [jnp.where((lane >> 3) == h, 1.0, 0.0).astype(BF16) for h in range(H_FOX)]
    ones = jnp.ones((tk, D_FOX), BF16)
    keep = lax.broadcasted_iota(I32, (tq, tk), 1) <= lax.broadcasted_iota(I32, (tq, tk), 0)

    def block(start, diag_offset):
        rows = slice(0 if diag_offset is None else diag_offset, tq)
        ka = ka_ref[0, pl.ds(start, tk), :]
        for h, lanes in enumerate(head_lanes):
            q_aug = jnp.concatenate([q_ref[0, rows, lanes], qa_ref[0, rows, :]], axis=1)
            k_aug = jnp.concatenate([k_ref[0, pl.ds(start, tk), lanes], ka * own[h]], axis=1)
            s = lax.dot_general(q_aug, k_aug, NT, preferred_element_type=F32)
            if diag_offset is not None:
                s = jnp.where(keep[:tq - diag_offset], s, -jnp.inf)
            m_prev = m_ref[h, rows]
            m_next = jnp.maximum(m_prev, jnp.max(s, axis=-1, keepdims=True))
            p = jnp.exp2(s - jnp.concatenate([m_next] * (tk // LANES), axis=1))
            alpha = jnp.exp2(m_prev - m_next)
            v_aug = jnp.concatenate([v_ref[0, pl.ds(start, tk), lanes], ones], axis=1)
            acc_ref[h, rows] = (jnp.concatenate([alpha, alpha], axis=1) * acc_ref[h, rows]
                                + jnp.dot(p.astype(BF16), v_aug, preferred_element_type=F32))
            m_ref[h, rows] = m_next

    def full_block(jb, carry):
        block(pl.multiple_of(jb * tk, tk), None)
        return carry

    lax.fori_loop(0, i * (tq // tk), full_block, 0)
    for d in range(tq // tk):
        block(pl.multiple_of(i * tq + d * tk, tk), d * tk)
    for h, lanes in enumerate(head_lanes):
        acc = acc_ref[h]
        o_ref[0, :, lanes] = (acc[:, :D_FOX] / acc[:, D_FOX:]).astype(o_ref.dtype)


def _fox_attention(qkv, qaug, kaug, *, tq=1024, tk=256):
    b, t, _ = qkv.shape
    hd = H_FOX * D_FOX
    assert t % tq == 0 and tq % tk == 0 and tk % LANES == 0
    return pl.pallas_call(
        functools.partial(_fox_body, tq=tq, tk=tk),
        grid=(b, t // tq),
        in_specs=[pl.BlockSpec((1, tq, hd), lambda bi, i: (bi, i, 0)),
                  pl.BlockSpec((1, t, hd), lambda bi, i: (bi, 0, 1)),
                  pl.BlockSpec((1, t, hd), lambda bi, i: (bi, 0, 2)),
                  pl.BlockSpec((1, tq, LANES), lambda bi, i: (bi, i, 0)),
                  pl.BlockSpec((1, t, LANES), lambda bi, i: (bi, 0, 0))],
        out_specs=pl.BlockSpec((1, tq, hd), lambda bi, i: (bi, i, 0)),
        out_shape=jax.ShapeDtypeStruct((b, t, hd), BF16),
        scratch_shapes=[pltpu.VMEM((H_FOX, tq, LANES), F32), pltpu.VMEM((H_FOX, tq, 2 * D_FOX), F32)],
        compiler_params=_params("parallel", "arbitrary"),
        name="fox_attention",
    )(qkv, qkv, qkv, qaug, kaug)


def _conv_body(cur_ref, halo_ref, dw_ref, vec_ref, o_ref, z_ref, y_ref, *, tt, hrows, rchunk):
    i = pl.program_id(1)
    cur = cur_ref[0].astype(F32)
    z_ref[hrows:hrows + tt, :] = cur[:, :C_CONV] * _sigmoid(cur[:, C_CONV:])
    hal = halo_ref[0].astype(F32)
    zh = hal[:, :C_CONV] * _sigmoid(hal[:, C_CONV:])
    z_ref[0:hrows, :] = jnp.where(i > 0, zh, 0.0)

    for lg in range(C_CONV // LANES):
        lanes = slice(lg * LANES, (lg + 1) * LANES)
        for r0 in range(0, tt, rchunk):
            acc = jnp.broadcast_to(vec_ref[0:1, lanes], (rchunk, LANES))
            for k in range(CONV_WIDTH):
                off = hrows - (CONV_WIDTH - 1) + k + r0
                acc = acc + dw_ref[k:k + 1, lanes] * z_ref[off:off + rchunk, lanes]
            y_ref[r0:r0 + rchunk, lanes] = acc

    y = y_ref[...]
    mu = jnp.mean(y, axis=-1, keepdims=True)
    yc = y - mu
    var = jnp.mean(yc * yc, axis=-1, keepdims=True)
    yn = yc * lax.rsqrt(var + EPS) * vec_ref[1:2, :] + vec_ref[2:3, :]
    o_ref[0] = _silu(yn).astype(o_ref.dtype)


def _conv_module(uc, dw, vec, *, tt=256, hrows=32, rchunk=64):
    b, t, _ = uc.shape
    assert t % tt == 0 and tt % hrows == 0 and hrows >= CONV_WIDTH - 1
    ratio = tt // hrows
    return pl.pallas_call(
        functools.partial(_conv_body, tt=tt, hrows=hrows, rchunk=rchunk),
        grid=(b, t // tt),
        in_specs=[pl.BlockSpec((1, tt, 2 * C_CONV), lambda bi, i: (bi, i, 0)),
                  pl.BlockSpec((1, hrows, 2 * C_CONV), lambda bi, i: (bi, jnp.maximum(i * ratio - 1, 0), 0)),
                  pl.BlockSpec(dw.shape, lambda bi, i: (0, 0)),
                  pl.BlockSpec(vec.shape, lambda bi, i: (0, 0))],
        out_specs=pl.BlockSpec((1, tt, C_CONV), lambda bi, i: (bi, i, 0)),
        out_shape=jax.ShapeDtypeStruct((b, t, C_CONV), BF16),
        scratch_shapes=[pltpu.VMEM((tt + hrows, C_CONV), F32), pltpu.VMEM((tt, C_CONV), F32)],
        compiler_params=_params("parallel", "parallel"),
        name="conv_module",
    )(uc, uc, dw, vec)


def _merge_body(oa_ref, ob_ref, zc_ref, gates_ref, x_ref, wa_ref, wb_ref, wc_ref, wo_ref, g_ref,
                wr_ref, br_ref, xn_ref, hp_ref, route_ref, cnt_ref, run_ref, *, tm):
    step = pl.program_id(0)
    d = x_ref.shape[-1]

    @pl.when(step == 0)
    def _():
        run_ref[...] = jnp.zeros_like(run_ref)

    merged = None
    for idx, (m_ref, w_ref) in enumerate(((oa_ref, wa_ref), (ob_ref, wb_ref), (zc_ref, wc_ref))):
        y = jnp.dot(m_ref[...], w_ref[...], preferred_element_type=F32)
        term = _sigmoid(gates_ref[:, idx * d:(idx + 1) * d].astype(F32)) * y
        merged = term if merged is None else merged + term
    xn = x_ref[...] + jnp.dot(merged.astype(BF16), wo_ref[...], preferred_element_type=F32)
    xn_ref[...] = xn
    h2 = xn * lax.rsqrt(jnp.mean(xn * xn, axis=-1, keepdims=True) + EPS) * g_ref[...]

    lo = pltpu.bitcast(h2[:, :d // 2].astype(BF16).astype(F32), U32) >> 16
    hi = pltpu.bitcast(h2[:, d // 2:].astype(BF16).astype(F32), U32) & jnp.uint32(0xFFFF0000)
    hp_ref[...] = lo | hi

    logits = _mm(h2, wr_ref[...]) + br_ref[...]
    lane = lax.broadcasted_iota(I32, logits.shape, 1)
    big = jnp.int32(4 * LANES)
    in_grp = lane < N_GROUPS
    gl = jnp.where(in_grp, logits, -jnp.inf)
    gmax = jnp.max(gl, axis=-1, keepdims=True)
    grp = jnp.min(jnp.where(gl == gmax, lane, big), axis=-1, keepdims=True)
    p_grp = 1.0 / jnp.sum(jnp.where(in_grp, jnp.exp(gl - gmax), 0.0), axis=-1, keepdims=True)
    e_lo = LANE_EXP + grp * EXPERTS_PER_GROUP
    el = jnp.where((lane >= e_lo) & (lane < e_lo + EXPERTS_PER_GROUP), logits, -jnp.inf)
    v0 = jnp.max(el, axis=-1, keepdims=True)
    i0 = jnp.min(jnp.where(el == v0, lane, big), axis=-1, keepdims=True)
    el1 = jnp.where(lane == i0, -jnp.inf, el)
    v1 = jnp.max(el1, axis=-1, keepdims=True)
    i1 = jnp.min(jnp.where(el1 == v1, lane, big), axis=-1, keepdims=True)
    e1 = jnp.exp(v1 - v0)
    gate0 = p_grp / (1.0 + e1)
    gate1 = p_grp * e1 / (1.0 + e1)

    hot0 = lane == i0
    hot1 = lane == i1
    onehot = jnp.where(hot0 | hot1, 1.0, 0.0)
    ri = lax.broadcasted_iota(I32, (tm, tm), 0)
    cj = lax.broadcasted_iota(I32, (tm, tm), 1)
    prefix = _mm(jnp.where(cj < ri, 1.0, 0.0), onehot) + run_ref[...]
    rank0 = jnp.sum(jnp.where(hot0, prefix, 0.0), axis=-1, keepdims=True)
    rank1 = jnp.sum(jnp.where(hot1, prefix, 0.0), axis=-1, keepdims=True)
    run_ref[...] = run_ref[...] + jnp.sum(onehot, axis=0, keepdims=True)
    cnt_ref[...] = jnp.broadcast_to(run_ref[...], cnt_ref.shape)

    route = jnp.where(lane == R_GATE0, gate0, 0.0)
    route = jnp.where(lane == R_GATE1, gate1, route)
    route = jnp.where(lane == R_EID0, (i0 - LANE_EXP).astype(F32), route)
    route = jnp.where(lane == R_EID1, (i1 - LANE_EXP).astype(F32), route)
    route = jnp.where(lane == R_RANK0, rank0, route)
    route = jnp.where(lane == R_RANK1, rank1, route)
    route_ref[...] = route


def _merge(oa, ob, zc, gates, x2, wa, wb, wc, wo, g, wr, br, *, tm=256):
    n, d = x2.shape
    assert n % tm == 0
    row = lambda i: (i, 0)
    fixed = lambda i: (0, 0)
    full = lambda a: pl.BlockSpec(a.shape, fixed)
    return pl.pallas_call(
        functools.partial(_merge_body, tm=tm),
        grid=(n // tm,),
        in_specs=[pl.BlockSpec((tm, oa.shape[1]), row), pl.BlockSpec((tm, ob.shape[1]), row),
                  pl.BlockSpec((tm, zc.shape[1]), row), pl.BlockSpec((tm, 3 * d), row),
                  pl.BlockSpec((tm, d), row),
                  full(wa), full(wb), full(wc), full(wo), full(g), full(wr), full(br)],
        out_specs=[pl.BlockSpec((tm, d), row), pl.BlockSpec((tm, d // 2), row),
                   pl.BlockSpec((tm, LANES), row), pl.BlockSpec((8, LANES), fixed)],
        out_shape=[jax.ShapeDtypeStruct((n, d), F32), jax.ShapeDtypeStruct((n, d // 2), U32),
                   jax.ShapeDtypeStruct((n, LANES), F32), jax.ShapeDtypeStruct((8, LANES), F32)],
        scratch_shapes=[pltpu.VMEM((1, LANES), F32)],
        compiler_params=_params("arbitrary"),
        name="merge_router",
    )(oa, ob, zc, gates, x2, wa, wb, wc, wo, g, wr, br)


def _row_copy(src_ref, src_row, dst_ref, dst_row, sem):
    return pltpu.make_async_copy(src_ref.at[pl.ds(src_row, 1)], dst_ref.at[pl.ds(dst_row, 1)], sem)


def _dispatch_body(dest_ref, hp_ref, xs_in_ref, xs_ref, sem, *, tm):
    del xs_in_ref

    def issue(n, carry):
        _row_copy(hp_ref, n, xs_ref, dest_ref[0, 0, n], sem.at[0]).start(priority=0)
        _row_copy(hp_ref, n, xs_ref, dest_ref[0, 0, tm + n], sem.at[1]).start(priority=1)
        return carry

    lax.fori_loop(0, tm, issue, 0)
    for slot in range(2):
        pltpu.make_async_copy(hp_ref, xs_ref.at[pl.ds(0, tm)], sem.at[slot]).wait()


def _dispatch(dest3, hp, xs_zero, *, tm):
    n = hp.shape[0]
    return pl.pallas_call(
        functools.partial(_dispatch_body, tm=tm),
        grid=(n // tm,),
        in_specs=[pl.BlockSpec((1, 1, 2 * tm), lambda i: (i, 0, 0), memory_space=pltpu.SMEM),
                  pl.BlockSpec((tm, hp.shape[1]), lambda i: (i, 0)), pl.BlockSpec(memory_space=pl.ANY)],
        out_specs=pl.BlockSpec(memory_space=pl.ANY),
        out_shape=jax.ShapeDtypeStruct(xs_zero.shape, xs_zero.dtype),
        scratch_shapes=[pltpu.SemaphoreType.DMA((2,))],
        input_output_aliases={2: 0},
        compiler_params=pltpu.CompilerParams(dimension_semantics=("arbitrary",), has_side_effects=True),
        name="moe_dispatch",
    )(dest3, hp, xs_zero)


def _expert_body(be_ref, nu_ref, xs_ref, w1_ref, w3_ref, w2_ref, ys_ref):
    del be_ref
    live = pl.program_id(0) < nu_ref[0]

    @pl.when(jnp.logical_not(live))
    def _():
        ys_ref[...] = jnp.zeros_like(ys_ref)

    @pl.when(live)
    def _():
        xp = xs_ref[...]
        half = xp.shape[-1]
        lo = pltpu.bitcast(xp << 16, F32).astype(BF16)
        hi = pltpu.bitcast(xp & jnp.uint32(0xFFFF0000), F32).astype(BF16)

        def up(w_ref):
            return (jnp.dot(lo, w_ref[0, :half, :].astype(BF16), preferred_element_type=F32)
                    + jnp.dot(hi, w_ref[0, half:, :].astype(BF16), preferred_element_type=F32))

        act = (_silu(up(w1_ref)) * up(w3_ref)).astype(BF16)
        ys_ref[...] = jnp.dot(act, w2_ref[0].astype(BF16), preferred_element_type=F32)


def _experts(blk_eid, n_used, xs, w1, w3, w2, *, rb):
    p, half = xs.shape
    d = 2 * half
    nb = p // rb
    used = lambda i, be, nu: jnp.maximum(jnp.minimum(i, nu[0] - 1), 0)
    wmap = lambda i, be, nu: (be[used(i, be, nu)], 0, 0)
    grid_spec = pltpu.PrefetchScalarGridSpec(
        num_scalar_prefetch=2,
        grid=(nb,),
        in_specs=[pl.BlockSpec((rb, half), lambda i, be, nu: (used(i, be, nu), 0)),
                  pl.BlockSpec((1, d, D_EXPERT), wmap), pl.BlockSpec((1, d, D_EXPERT), wmap),
                  pl.BlockSpec((1, D_EXPERT, d), wmap)],
        out_specs=pl.BlockSpec((rb, d), lambda i, be, nu: (i, 0)),
    )
    return pl.pallas_call(
        _expert_body,
        grid_spec=grid_spec,
        out_shape=jax.ShapeDtypeStruct((p, d), F32),
        compiler_params=_params("arbitrary"),
        name="moe_experts",
    )(blk_eid, n_used, xs, w1, w3, w2)


def _combine_body(dest_ref, ys_ref, x_ref, route_ref, g_ref, o_ref, buf_ref, sem, *, tm, final_norm):
    def issue(n, carry):
        _row_copy(ys_ref, dest_ref[0, 0, n], buf_ref.at[0], n, sem.at[0]).start(priority=0)
        _row_copy(ys_ref, dest_ref[0, 0, tm + n], buf_ref.at[1], n, sem.at[1]).start(priority=1)
        return carry

    lax.fori_loop(0, tm, issue, 0)
    for slot in range(2):
        pltpu.make_async_copy(ys_ref.at[pl.ds(0, tm)], buf_ref.at[slot], sem.at[slot]).wait()

    route = route_ref[...]
    out = (x_ref[...] + route[:, R_GATE0:R_GATE0 + 1] * buf_ref[0]
           + route[:, R_GATE1:R_GATE1 + 1] * buf_ref[1])
    if final_norm:
        out = out * lax.rsqrt(jnp.mean(out * out, axis=-1, keepdims=True) + EPS) * g_ref[...]
    o_ref[...] = out


def _combine(dest3, ys, xn, route, g, *, tm, final_norm):
    n, d = xn.shape
    row = lambda i: (i, 0)
    return pl.pallas_call(
        functools.partial(_combine_body, tm=tm, final_norm=final_norm),
        grid=(n // tm,),
        in_specs=[pl.BlockSpec((1, 1, 2 * tm), lambda i: (i, 0, 0), memory_space=pltpu.SMEM),
                  pl.BlockSpec(memory_space=pl.ANY),
                  pl.BlockSpec((tm, d), row), pl.BlockSpec((tm, LANES), row),
                  pl.BlockSpec((1, d), lambda i: (0, 0))],
        out_specs=pl.BlockSpec((tm, d), row),
        out_shape=jax.ShapeDtypeStruct((n, d), F32),
        scratch_shapes=[pltpu.VMEM((2, tm, d), F32), pltpu.SemaphoreType.DMA((2,))],
        compiler_params=_params("arbitrary"),
        name="moe_combine",
    )(dest3, ys, xn, route, g)


def _moe(hp, route, counts, xn, w1, w3, w2, g_final, *, tm, rb, final_norm):
    n = hp.shape[0]
    cnt = counts[0, LANE_EXP:LANE_EXP + N_EXPERTS].astype(I32)
    nblk = (cnt + rb - 1) // rb
    bend = jnp.cumsum(nblk)
    pstart = (bend - nblk) * rb
    nb = (2 * n) // rb + N_EXPERTS
    n_used = bend[-1:].astype(I32)
    blk_eid = jnp.minimum(jnp.sum(bend[None, :] <= jnp.arange(nb, dtype=I32)[:, None], axis=1),
                          N_EXPERTS - 1).astype(I32)
    eid = route[:, R_EID0:R_EID1 + 1].astype(I32)
    rank = route[:, R_RANK0:R_RANK1 + 1].astype(I32)
    dest = pstart[eid] + rank
    dest3 = dest.reshape(n // tm, tm, 2).transpose(0, 2, 1).reshape(n // tm, 1, 2 * tm)

    xs = _dispatch(dest3, hp, jnp.zeros((nb * rb, hp.shape[1]), hp.dtype), tm=tm)
    ys = _experts(blk_eid, n_used, xs, w1, w3, w2, rb=rb)
    return _combine(dest3, ys, xn, route, g_final, tm=tm, final_norm=final_norm)


def _lane_row(pairs):
    row = jnp.zeros((LANES,), F32)
    for off, vec in pairs:
        row = row.at[off:off + vec.shape[0]].set(vec.astype(F32))
    return row


def kernel(x, norm_mix, w_in, conv_qkv, dn_a_log, dn_dt_bias, dn_norm, fox_bias, conv_dw, conv_dw_b,
           conv_ln_g, conv_ln_b, w_a, w_b, w_c, w_out, norm_ffn, router_group_w, router_group_b,
           router_expert_w, router_expert_b, expert_w1, expert_w3, expert_w2, norm_final):
    b, t, d = x.shape
    n = b * t
    depth = w_in.shape[0]
    qk_dn = H_DN * DK_DN
    in_sizes = (qk_dn, qk_dn, H_DN * DV_DN, H_DN * DV_DN, H_DN, H_DN,
                H_FOX * D_FOX, H_FOX * D_FOX, H_FOX * D_FOX, H_FOX, 2 * C_CONV, d, d, d)
    splits = np.cumsum(in_sizes)[:-1].tolist()
    tm_moe, rb = 256, 256

    x2 = x.reshape(n, d)
    for l in range(depth):
        (qa, ka, va, za, ba, aa, qb, kb, vb, fb, uc, ga, gb, gc) = jnp.split(w_in[l], splits, axis=1)
        w_main = jnp.concatenate([qa, ka, va, za, qb * (D_FOX ** -0.5 * LOG2E), kb, vb, uc, ga, gb, gc],
                                 axis=1).astype(BF16)
        w_small = jnp.concatenate([ba, aa, fb, jnp.zeros((d, LANES - 3 * H_DN), F32)], axis=1).astype(BF16)
        dn_qkv, za_p, fox_qkv, uc_p, gates, small = _in_proj(x2, norm_mix[l][None, :], w_main, w_small)

        par = jnp.stack([_lane_row([(LANE_G, -jnp.exp(dn_a_log[l]))]),
                         _lane_row([(LANE_G, dn_dt_bias[l]), (LANE_F, fox_bias[l])]),
                         _lane_row([(0, dn_norm[l])])] + [jnp.zeros((LANES,), F32)] * 5)
        oa, qaug, kaug = _delta_rule(dn_qkv.reshape(b, t, -1), za_p.reshape(b, t, -1),
                                     small.reshape(b, t, LANES), conv_qkv[l], par)
        ob = _fox_attention(fox_qkv.reshape(b, t, -1), qaug, kaug)
        vec = jnp.stack([conv_dw_b[l], conv_ln_g[l], conv_ln_b[l]] + [jnp.zeros((C_CONV,), F32)] * 5)
        zc = _conv_module(uc_p.reshape(b, t, -1), conv_dw[l], vec)

        w_r = jnp.concatenate([router_group_w[l], jnp.zeros((d, LANE_EXP - N_GROUPS), F32),
                               router_expert_w[l], jnp.zeros((d, LANES - LANE_EXP - N_EXPERTS), F32)], axis=1)
        b_r = _lane_row([(LANE_GRP, router_group_b[l]), (LANE_EXP, router_expert_b[l])])[None, :]
        xn, hp, route, counts = _merge(
            oa.reshape(n, -1), ob.reshape(n, -1), zc.reshape(n, -1), gates, x2,
            w_a[l].astype(BF16), w_b[l].astype(BF16), w_c[l].astype(BF16), w_out[l].astype(BF16),
            norm_ffn[l][None, :], w_r.astype(BF16), b_r)

        x2 = _moe(hp, route, counts, xn, expert_w1[l], expert_w3[l], expert_w2[l], norm_final[None, :],
                  tm=tm_moe, rb=rb, final_norm=(l == depth - 1))
    return x2.reshape(b, t, d)
```

```python
import functools

import jax
import jax.numpy as jnp
import numpy as np
from jax import lax
from jax.experimental import pallas as pl
from jax.experimental.pallas import tpu as pltpu

F32 = jnp.float32
BF16 = jnp.bfloat16
U32 = jnp.uint32
I32 = jnp.int32
HIGHEST = lax.Precision.HIGHEST

EPS = 1e-6
LOG2E = 1.4426950408889634
LANES = 128
SUBLANES = 8
H_DN, DK_DN, DV_DN = 4, 128, 128
SHORT_CONV = 4
CHUNK = 64
H_FOX, D_FOX = 4, 128
C_CONV = 512
CONV_WIDTH = 31
N_GROUPS, EXPERTS_PER_GROUP = 4, 8
N_EXPERTS = N_GROUPS * EXPERTS_PER_GROUP
D_EXPERT = 256

W_DN = 3 * H_DN * DK_DN
W_FOX = 3 * H_FOX * D_FOX
VMEM_LIMIT = 56 * 1024 * 1024

LANE_BETA, LANE_G, LANE_F = 0, 4, 8
LANE_GRP, LANE_EXP = 0, 32
R_GATE0, R_GATE1, R_EID0, R_EID1, R_RANK0, R_RANK1 = 0, 1, 2, 3, 4, 5

NN = (((1,), (0,)), ((), ()))
NT = (((1,), (1,)), ((), ()))
TN = (((0,), (0,)), ((), ()))


def _mm(a, b, dims=NN):
    return lax.dot_general(a.astype(BF16), b.astype(BF16), dims, preferred_element_type=F32)


def _sigmoid(x):
    return 0.5 * jnp.tanh(0.5 * x) + 0.5


def _silu(x):
    return x * _sigmoid(x)


def _params(*sem):
    return pltpu.CompilerParams(dimension_semantics=sem, vmem_limit_bytes=VMEM_LIMIT)


def _in_proj_body(x_ref, g_ref, w_ref, ws_ref, dn_ref, za_ref, fox_ref, uc_ref, gates_ref, small_ref,
                  *, col_chunk):
    x = x_ref[...]
    h = x * lax.rsqrt(jnp.mean(x * x, axis=-1, keepdims=True) + EPS) * g_ref[...]
    hb = h.astype(BF16)
    col = 0
    for ref in (dn_ref, za_ref, fox_ref, uc_ref, gates_ref):
        width = ref.shape[-1]
        for c in range(0, width, col_chunk):
            ref[:, c:c + col_chunk] = jnp.dot(
                hb, w_ref[:, col + c:col + c + col_chunk], preferred_element_type=F32).astype(ref.dtype)
        col += width
    small_ref[...] = jnp.dot(hb, ws_ref[...], preferred_element_type=F32)


def _in_proj(x2, g, w, ws, *, tm=512, col_chunk=512):
    n, d = x2.shape
    d_gate = 3 * d
    widths = (W_DN, H_DN * DV_DN, W_FOX, 2 * C_CONV, d_gate)
    assert w.shape == (d, sum(widths)) and n % tm == 0
    out_shape = [jax.ShapeDtypeStruct((n, wd), BF16) for wd in widths]
    out_shape.append(jax.ShapeDtypeStruct((n, LANES), F32))
    row = lambda i: (i, 0)
    fixed = lambda i: (0, 0)
    return pl.pallas_call(
        functools.partial(_in_proj_body, col_chunk=col_chunk),
        grid=(n // tm,),
        in_specs=[pl.BlockSpec((tm, d), row), pl.BlockSpec((1, d), fixed),
                  pl.BlockSpec(w.shape, fixed, pipeline_mode=pl.Buffered(1)),
                  pl.BlockSpec(ws.shape, fixed, pipeline_mode=pl.Buffered(1))],
        out_specs=[pl.BlockSpec((tm, wd), row) for wd in widths] + [pl.BlockSpec((tm, LANES), row)],
        out_shape=out_shape,
        compiler_params=_params("parallel"),
        name="in_proj",
    )(x2, g, w, ws)


def _softplus_parts(z):
    t = jnp.log1p(jnp.exp(-jnp.abs(z)))
    return jnp.maximum(z, 0.0) + t, -(jnp.maximum(-z, 0.0) + t)


def _delta_body(qkv_ref, za_ref, sm_ref, cw_ref, par_ref, oa_ref, qaug_ref, kaug_ref,
                xs_ref, s_ref, carry_ref, *, ts):
    j = pl.program_id(1)
    halo = 8

    @pl.when(j == 0)
    def _():
        xs_ref[0:halo, :] = jnp.zeros((halo, W_DN), F32)
        s_ref[...] = jnp.zeros_like(s_ref)
        carry_ref[...] = jnp.zeros_like(carry_ref)

    @pl.when(j > 0)
    def _():
        xs_ref[0:halo, :] = xs_ref[ts:ts + halo, :]

    xs_ref[halo:halo + ts, :] = qkv_ref[0].astype(F32)

    def conv_silu(lane0):
        acc = None
        for k in range(SHORT_CONV):
            off = halo - (SHORT_CONV - 1) + k
            term = cw_ref[k:k + 1, lane0:lane0 + LANES] * xs_ref[off:off + ts, lane0:lane0 + LANES]
            acc = term if acc is None else acc + term
        return _silu(acc)

    def l2n(a):
        return a * lax.rsqrt(jnp.sum(a * a, axis=-1, keepdims=True) + EPS)

    sm = sm_ref[0]
    lane = lax.broadcasted_iota(I32, sm.shape, 1)
    sp, logsig = _softplus_parts(sm + par_ref[1:2, :])
    vals = jnp.where(lane < LANE_G, _sigmoid(sm),
                     jnp.where(lane < LANE_F, par_ref[0:1, :] * sp,
                               jnp.where(lane < LANE_F + H_FOX, logsig, 0.0)))
    row = lax.broadcasted_iota(I32, (ts, ts), 0)
    colm = lax.broadcasted_iota(I32, (ts, ts), 1)
    log_chunk = CHUNK.bit_length() - 1
    causal = (row >= colm) & ((row >> log_chunk) == (colm >> log_chunk))

    hi = vals.astype(BF16)
    rem = vals - hi.astype(F32)
    mid = rem.astype(BF16)
    lo = (rem - mid.astype(F32)).astype(BF16)
    pieces = jnp.concatenate([hi, mid, lo], axis=-1)

    def cumsum(mask):
        y = jnp.dot(jnp.where(mask, 1.0, 0.0).astype(BF16), pieces, preferred_element_type=F32)
        return (y[:, :LANES] + y[:, LANES:2 * LANES]) + y[:, 2 * LANES:]

    ccum = cumsum(row >= colm) + carry_ref[...]
    gcum = cumsum(causal)
    carry_ref[...] = ccum[ts - 1:ts, :]
    gcum_t = gcum.T

    cl = ccum * LOG2E
    c_hi = cl.astype(BF16)
    c_rem = cl - c_hi.astype(F32)
    c_mid = c_rem.astype(BF16)
    c_lo = (c_rem - c_mid.astype(F32)).astype(BF16)
    c_pieces = jnp.concatenate([c_hi, c_mid, c_lo], axis=-1)
    pr = lax.broadcasted_iota(I32, (3 * LANES, LANES), 0)
    pc = lax.broadcasted_iota(I32, (3 * LANES, LANES), 1)
    src_lane, piece = pr & (LANES - 1), pr >> (LANES.bit_length() - 1)
    owned = (src_lane >= LANE_F) & (src_lane < LANE_F + H_FOX) & ((pc >> 3) == src_lane - LANE_F)
    place_q = jnp.where(owned & ((pc & 7) == piece), 1.0, 0.0).astype(BF16)
    place_k = jnp.where(owned & ((pc & 7) == piece + 3), -1.0, 0.0).astype(BF16)
    slot = lane & 7
    in_heads = lane < 8 * H_FOX
    ones_q = jnp.where(in_heads & (slot >= 3) & (slot < 6), 1.0, 0.0)
    ones_k = jnp.where(in_heads & (slot < 3), 1.0, 0.0)
    qaug_ref[0] = (jnp.dot(c_pieces, place_q, preferred_element_type=F32) + ones_q).astype(BF16)
    kaug_ref[0] = (jnp.dot(c_pieces, place_k, preferred_element_type=F32) + ones_k).astype(BF16)

    scale = DK_DN ** -0.5
    dn_norm = par_ref[2:3, :]

    pw = 2 * CHUNK
    prow = lax.broadcasted_iota(I32, (pw, pw), 0)
    pcol = lax.broadcasted_iota(I32, (pw, pw), 1)
    same = (prow >> log_chunk) == (pcol >> log_chunk)
    causal_p = (prow >= pcol) & same
    strict_p = (prow > pcol) & same
    levels = []
    s = 1
    while s < CHUNK:
        levels.append(((prow >> s.bit_length()) == (pcol >> s.bit_length()))
                      & ((prow & s) != 0) & ((pcol & s) == 0))
        s *= 2

    heads = range(H_DN)
    pairs = range(ts // pw)
    q, k, v = [], [], []
    for h in heads:
        q.append(l2n(conv_silu(h * DK_DN)))
        k.append(l2n(conv_silu(H_DN * DK_DN + h * DK_DN)))
        v.append(conv_silu(2 * H_DN * DK_DN + h * DV_DN))

    ctx = []
    for h in heads:
        for p in pairs:
            pr = slice(p * pw, (p + 1) * pw)
            gcol = gcum[pr, LANE_G + h:LANE_G + h + 1]
            grow = gcum_t[LANE_G + h:LANE_G + h + 1, pr]
            beta = vals[pr, LANE_BETA + h:LANE_BETA + h + 1]
            decay = jnp.where(causal_p, jnp.exp(jnp.where(causal_p, gcol - grow, 0.0)), 0.0)
            egc = jnp.exp(gcol)
            kp = k[h][pr]
            kb = kp * beta
            ctx.append(dict(
                h=h, p=p, gcol=gcol, kp=kp,
                a=jnp.where(strict_p, _mm(kb, kp, NT) * decay, 0.0),
                qk=jnp.where(causal_p, _mm(q[h][pr] * scale, kp, NT) * decay, 0.0),
                rhs=jnp.concatenate([v[h][pr] * beta, kb * egc], axis=-1),
                qg=q[h][pr] * (scale * egc)))

    for c in ctx:
        c["n"] = -jnp.where(levels[0], c["a"], 0.0)
    for level in levels[1:]:
        for c in ctx:
            m = jnp.where(level, c["a"], 0.0)
            c["y"] = m + _mm(c["n"], m)
        for c in ctx:
            c["n"] = c["n"] - (c["y"] + _mm(c["y"], c["n"]))
    for c in ctx:
        c["sol"] = c["rhs"] + _mm(c["n"], c["rhs"])

    state = [s_ref[h] for h in heads]
    for p in pairs:
        group = [c for c in ctx if c["p"] == p]
        v_prev = [None] * H_DN
        for ch in range(2):
            rows = slice(ch * CHUNK, (ch + 1) * CHUNK)
            out_rows = slice(p * pw + ch * CHUNK, p * pw + (ch + 1) * CHUNK)
            v_new = [c["sol"][rows, :DV_DN] - _mm(c["sol"][rows, DV_DN:], state[c["h"]]) for c in group]
            for c, vn in zip(group, v_new):
                h = c["h"]
                v_pair = jnp.concatenate([vn, jnp.zeros_like(vn)] if ch == 0 else [v_prev[h], vn], axis=0)
                o = _mm(c["qg"][rows], state[h]) + _mm(c["qk"][rows], v_pair)
                glast = c["gcol"][(ch + 1) * CHUNK - 1:(ch + 1) * CHUNK]
                state[h] = (state[h] * jnp.exp(glast)
                            + _mm(c["kp"][rows] * jnp.exp(glast - c["gcol"][rows]), vn, TN))
                v_prev[h] = vn
                o = o * lax.rsqrt(jnp.mean(o * o, axis=-1, keepdims=True) + EPS) * dn_norm
                za = za_ref[0, out_rows, h * DV_DN:(h + 1) * DV_DN].astype(F32)
                oa_ref[0, out_rows, h * DV_DN:(h + 1) * DV_DN] = (o * _silu(za)).astype(oa_ref.dtype)
    for h in heads:
        s_ref[h] = state[h]


def _delta_rule(qkv, za, small, conv_w, par, *, ts=256):
    b, t, _ = qkv.shape
    assert t % ts == 0 and ts % (2 * CHUNK) == 0
    blk = lambda width: pl.BlockSpec((1, ts, width), lambda bi, j: (bi, j, 0))
    fixed = lambda bi, j: (0, 0)
    return pl.pallas_call(
        functools.partial(_delta_body, ts=ts),
        grid=(b, t // ts),
        in_specs=[blk(W_DN), blk(H_DN * DV_DN), blk(LANES),
                  pl.BlockSpec(conv_w.shape, fixed), pl.BlockSpec(par.shape, fixed)],
        out_specs=[blk(H_DN * DV_DN), blk(LANES), blk(LANES)],
        out_shape=[jax.ShapeDtypeStruct((b, t, H_DN * DV_DN), BF16),
                   jax.ShapeDtypeStruct((b, t, LANES), BF16),
                   jax.ShapeDtypeStruct((b, t, LANES), BF16)],
        scratch_shapes=[pltpu.VMEM((ts + 8, W_DN), F32),
                        pltpu.VMEM((H_DN, DK_DN, DV_DN), F32),
                        pltpu.VMEM((1, LANES), F32)],
        compiler_params=_params("parallel", "arbitrary"),
        name="delta_rule",
    )(qkv, za, small, conv_w, par)


def _fox_body(q_ref, k_ref, v_ref, qa_ref, ka_ref, o_ref, m_ref, acc_ref, *, tq, tk):
    i = pl.program_id(1)
    m_ref[...] = jnp.full(m_ref.shape, -jnp.inf, F32)
    acc_ref[...] = jnp.zeros_like(acc_ref)
    head_lanes = [slice(h * D_FOX, (h + 1) * D_FOX) for h in range(H_FOX)]
    lane = lax.broadcasted_iota(I32, (tk, LANES), 1)
    own = [jnp.where((lane >> 3) == h, 1.0, 0.0).astype(BF16) for h in range(H_FOX)]
    ones = jnp.ones((tk, D_FOX), BF16)
    keep = lax.broadcasted_iota(I32, (tq, tk), 1) <= lax.broadcasted_iota(I32, (tq, tk), 0)

    def block(start, diag_offset):
        rows = slice(0 if diag_offset is None else diag_offset, tq)
        ka = ka_ref[0, pl.ds(start, tk), :]
        for h, lanes in enumerate(head_lanes):
            q_aug = jnp.concatenate([q_ref[0, rows, lanes], qa_ref[0, rows, :]], axis=1)
            k_aug = jnp.concatenate([k_ref[0, pl.ds(start, tk), lanes], ka * own[h]], axis=1)
            s = lax.dot_general(q_aug, k_aug, NT, preferred_element_type=F32)
            if diag_offset is not None:
                s = jnp.where(keep[:tq - diag_offset], s, -jnp.inf)
            m_prev = m_ref[h, rows]
            m_next = jnp.maximum(m_prev, jnp.max(s, axis=-1, keepdims=True))
            p = jnp.exp2(s - jnp.concatenate([m_next] * (tk // LANES), axis=1))
            alpha = jnp.exp2(m_prev - m_next)
            v_aug = jnp.concatenate([v_ref[0, pl.ds(start, tk), lanes], ones], axis=1)
            acc_ref[h, rows] = (jnp.concatenate([alpha, alpha], axis=1) * acc_ref[h, rows]
                                + jnp.dot(p.astype(BF16), v_aug, preferred_element_type=F32))
            m_ref[h, rows] = m_next

    def full_block(jb, carry):
        block(pl.multiple_of(jb * tk, tk), None)
        return carry

    lax.fori_loop(0, i * (tq // tk), full_block, 0)
    for d in range(tq // tk):
        block(pl.multiple_of(i * tq + d * tk, tk), d * tk)
    for h, lanes in enumerate(head_lanes):
        acc = acc_ref[h]
        o_ref[0, :, lanes] = (acc[:, :D_FOX] / acc[:, D_FOX:]).astype(o_ref.dtype)


def _fox_attention(qkv, qaug, kaug, *, tq=1024, tk=256):
    b, t, _ = qkv.shape
    hd = H_FOX * D_FOX
    assert t % tq == 0 and tq % tk == 0 and tk % LANES == 0
    return pl.pallas_call(
        functools.partial(_fox_body, tq=tq, tk=tk),
        grid=(b, t // tq),
        in_specs=[pl.BlockSpec((1, tq, hd), lambda bi, i: (bi, i, 0)),
                  pl.BlockSpec((1, t, hd), lambda bi, i: (bi, 0, 1)),
                  pl.BlockSpec((1, t, hd), lambda bi, i: (bi, 0, 2)),
                  pl.BlockSpec((1, tq, LANES), lambda bi, i: (bi, i, 0)),
                  pl.BlockSpec((1, t, LANES), lambda bi, i: (bi, 0, 0))],
        out_specs=pl.BlockSpec((1, tq, hd), lambda bi, i: (bi, i, 0)),
        out_shape=jax.ShapeDtypeStruct((b, t, hd), BF16),
        scratch_shapes=[pltpu.VMEM((H_FOX, tq, LANES), F32), pltpu.VMEM((H_FOX, tq, 2 * D_FOX), F32)],
        compiler_params=_params("parallel", "arbitrary"),
        name="fox_attention",
    )(qkv, qkv, qkv, qaug, kaug)


def _conv_body(cur_ref, halo_ref, dw_ref, vec_ref, o_ref, z_ref, zs_ref, y_ref, *, tt, hrows, rchunk):
    i = pl.program_id(1)
    cur = cur_ref[0].astype(F32)
    z_ref[hrows:hrows + tt, :] = cur[:, :C_CONV] * _sigmoid(cur[:, C_CONV:])
    hal = halo_ref[0].astype(F32)
    zh = hal[:, :C_CONV] * _sigmoid(hal[:, C_CONV:])
    z_ref[0:hrows, :] = jnp.where(i > 0, zh, 0.0)

    sub = SUBLANES
    span = tt + hrows - sub
    for phase in range(1, sub):
        zs_ref[phase - 1, 0:span, :] = z_ref[phase:phase + span, :]

    for lg in range(C_CONV // LANES):
        lanes = slice(lg * LANES, (lg + 1) * LANES)
        for r0 in range(0, tt, rchunk):
            acc = jnp.broadcast_to(vec_ref[0:1, lanes], (rchunk, LANES))
            for k in range(CONV_WIDTH):
                off = hrows - (CONV_WIDTH - 1) + k
                phase, base = off % sub, off - off % sub + r0
                if phase == 0:
                    tap = z_ref[base:base + rchunk, lanes]
                else:
                    tap = zs_ref[phase - 1, base:base + rchunk, lanes]
                acc = acc + dw_ref[k:k + 1, lanes] * tap
            y_ref[r0:r0 + rchunk, lanes] = acc

    y = y_ref[...]
    mu = jnp.mean(y, axis=-1, keepdims=True)
    yc = y - mu
    var = jnp.mean(yc * yc, axis=-1, keepdims=True)
    yn = yc * lax.rsqrt(var + EPS) * vec_ref[1:2, :] + vec_ref[2:3, :]
    o_ref[0] = _silu(yn).astype(o_ref.dtype)


def _conv_module(uc, dw, vec, *, tt=256, hrows=32, rchunk=64):
    b, t, _ = uc.shape
    assert t % tt == 0 and tt % hrows == 0 and hrows >= CONV_WIDTH - 1
    ratio = tt // hrows
    return pl.pallas_call(
        functools.partial(_conv_body, tt=tt, hrows=hrows, rchunk=rchunk),
        grid=(b, t // tt),
        in_specs=[pl.BlockSpec((1, tt, 2 * C_CONV), lambda bi, i: (bi, i, 0)),
                  pl.BlockSpec((1, hrows, 2 * C_CONV), lambda bi, i: (bi, jnp.maximum(i * ratio - 1, 0), 0)),
                  pl.BlockSpec(dw.shape, lambda bi, i: (0, 0)),
                  pl.BlockSpec(vec.shape, lambda bi, i: (0, 0))],
        out_specs=pl.BlockSpec((1, tt, C_CONV), lambda bi, i: (bi, i, 0)),
        out_shape=jax.ShapeDtypeStruct((b, t, C_CONV), BF16),
        scratch_shapes=[pltpu.VMEM((tt + hrows, C_CONV), F32), pltpu.VMEM((7, tt + hrows, C_CONV), F32),
                        pltpu.VMEM((tt, C_CONV), F32)],
        compiler_params=_params("parallel", "parallel"),
        name="conv_module",
    )(uc, uc, dw, vec)


def _merge_body(oa_ref, ob_ref, zc_ref, gates_ref, x_ref, wa_ref, wb_ref, wc_ref, wo_ref, g_ref,
                wr_ref, br_ref, xn_ref, hp_ref, route_ref, meta_ref, cnt_ref, run_ref, *, tm):
    step = pl.program_id(0)
    d = x_ref.shape[-1]

    @pl.when(step == 0)
    def _():
        run_ref[...] = jnp.zeros_like(run_ref)

    merged = None
    for idx, (m_ref, w_ref) in enumerate(((oa_ref, wa_ref), (ob_ref, wb_ref), (zc_ref, wc_ref))):
        y = jnp.dot(m_ref[...], w_ref[...], preferred_element_type=F32)
        term = _sigmoid(gates_ref[:, idx * d:(idx + 1) * d].astype(F32)) * y
        merged = term if merged is None else merged + term
    xn = x_ref[...] + jnp.dot(merged.astype(BF16), wo_ref[...], preferred_element_type=F32)
    xn_ref[...] = xn
    h2 = xn * lax.rsqrt(jnp.mean(xn * xn, axis=-1, keepdims=True) + EPS) * g_ref[...]

    lo = pltpu.bitcast(h2[:, :d // 2].astype(BF16).astype(F32), U32) >> 16
    hi = pltpu.bitcast(h2[:, d // 2:].astype(BF16).astype(F32), U32) & jnp.uint32(0xFFFF0000)
    hp_ref[...] = lo | hi

    logits = _mm(h2, wr_ref[...]) + br_ref[...]
    lane = lax.broadcasted_iota(I32, logits.shape, 1)
    big = jnp.int32(4 * LANES)
    in_grp = lane < N_GROUPS
    gl = jnp.where(in_grp, logits, -jnp.inf)
    gmax = jnp.max(gl, axis=-1, keepdims=True)
    grp = jnp.min(jnp.where(gl == gmax, lane, big), axis=-1, keepdims=True)
    p_grp = 1.0 / jnp.sum(jnp.where(in_grp, jnp.exp(gl - gmax), 0.0), axis=-1, keepdims=True)
    e_lo = LANE_EXP + grp * EXPERTS_PER_GROUP
    el = jnp.where((lane >= e_lo) & (lane < e_lo + EXPERTS_PER_GROUP), logits, -jnp.inf)
    v0 = jnp.max(el, axis=-1, keepdims=True)
    i0 = jnp.min(jnp.where(el == v0, lane, big), axis=-1, keepdims=True)
    el1 = jnp.where(lane == i0, -jnp.inf, el)
    v1 = jnp.max(el1, axis=-1, keepdims=True)
    i1 = jnp.min(jnp.where(el1 == v1, lane, big), axis=-1, keepdims=True)
    e1 = jnp.exp(v1 - v0)
    gate0 = p_grp / (1.0 + e1)
    gate1 = p_grp * e1 / (1.0 + e1)

    hot0 = lane == i0
    hot1 = lane == i1
    onehot = jnp.where(hot0 | hot1, 1.0, 0.0)
    ri = lax.broadcasted_iota(I32, (tm, tm), 0)
    cj = lax.broadcasted_iota(I32, (tm, tm), 1)
    prefix = _mm(jnp.where(cj < ri, 1.0, 0.0), onehot) + run_ref[...]
    rank0 = jnp.sum(jnp.where(hot0, prefix, 0.0), axis=-1, keepdims=True)
    rank1 = jnp.sum(jnp.where(hot1, prefix, 0.0), axis=-1, keepdims=True)
    run_ref[...] = run_ref[...] + jnp.sum(onehot, axis=0, keepdims=True)
    cnt_ref[...] = jnp.broadcast_to(run_ref[...], cnt_ref.shape)

    route = jnp.where(lane == R_GATE0, gate0, 0.0)
    route = jnp.where(lane == R_GATE1, gate1, route)
    route = jnp.where(lane == R_EID0, (i0 - LANE_EXP).astype(F32), route)
    route = jnp.where(lane == R_EID1, (i1 - LANE_EXP).astype(F32), route)
    route = jnp.where(lane == R_RANK0, rank0, route)
    route = jnp.where(lane == R_RANK1, rank1, route)
    route_ref[...] = route
    meta_ref[...] = route.T[0:8, :].astype(I32)


def _merge(oa, ob, zc, gates, x2, wa, wb, wc, wo, g, wr, br, *, tm=256):
    n, d = x2.shape
    assert n % tm == 0
    row = lambda i: (i, 0)
    fixed = lambda i: (0, 0)
    full = lambda a: pl.BlockSpec(a.shape, fixed)
    return pl.pallas_call(
        functools.partial(_merge_body, tm=tm),
        grid=(n // tm,),
        in_specs=[pl.BlockSpec((tm, oa.shape[1]), row), pl.BlockSpec((tm, ob.shape[1]), row),
                  pl.BlockSpec((tm, zc.shape[1]), row), pl.BlockSpec((tm, 3 * d), row),
                  pl.BlockSpec((tm, d), row),
                  full(wa), full(wb), full(wc), full(wo), full(g), full(wr), full(br)],
        out_specs=[pl.BlockSpec((tm, d), row), pl.BlockSpec((tm, d // 2), row),
                   pl.BlockSpec((tm, LANES), row), pl.BlockSpec((8, tm), lambda i: (0, i)),
                   pl.BlockSpec((8, LANES), fixed)],
        out_shape=[jax.ShapeDtypeStruct((n, d), F32), jax.ShapeDtypeStruct((n, d // 2), U32),
                   jax.ShapeDtypeStruct((n, LANES), F32), jax.ShapeDtypeStruct((8, n), I32),
                   jax.ShapeDtypeStruct((8, LANES), F32)],
        scratch_shapes=[pltpu.VMEM((1, LANES), F32)],
        compiler_params=_params("arbitrary"),
        name="merge_router",
    )(oa, ob, zc, gates, x2, wa, wb, wc, wo, g, wr, br)


def _row_copy(src_ref, src_row, dst_ref, dst_row, sem):
    return pltpu.make_async_copy(src_ref.at[pl.ds(src_row, 1)], dst_ref.at[pl.ds(dst_row, 1)], sem)


def _dispatch_body(dest_ref, hp_ref, xs_in_ref, xs_ref, sem, *, tm):
    del xs_in_ref
    base = pl.program_id(0) * (2 * tm)

    def issue(n, carry):
        _row_copy(hp_ref, n, xs_ref, dest_ref[base + n], sem.at[0]).start(priority=0)
        _row_copy(hp_ref, n, xs_ref, dest_ref[base + tm + n], sem.at[1]).start(priority=1)
        return carry

    lax.fori_loop(0, tm, issue, 0)
    for slot in range(2):
        pltpu.make_async_copy(hp_ref, xs_ref.at[pl.ds(0, tm)], sem.at[slot]).wait()


def _dispatch(dest3, hp, xs_zero, *, tm):
    n = hp.shape[0]
    grid_spec = pltpu.PrefetchScalarGridSpec(
        num_scalar_prefetch=1,
        grid=(n // tm,),
        in_specs=[pl.BlockSpec((tm, hp.shape[1]), lambda i, dest: (i, 0)), pl.BlockSpec(memory_space=pl.ANY)],
        out_specs=pl.BlockSpec(memory_space=pl.ANY),
        scratch_shapes=[pltpu.SemaphoreType.DMA((2,))],
    )
    return pl.pallas_call(
        functools.partial(_dispatch_body, tm=tm),
        grid_spec=grid_spec,
        out_shape=jax.ShapeDtypeStruct(xs_zero.shape, xs_zero.dtype),
        input_output_aliases={2: 0},
        compiler_params=pltpu.CompilerParams(dimension_semantics=("arbitrary",), has_side_effects=True),
        name="moe_dispatch",
    )(dest3.reshape(-1), hp, xs_zero)


def _expert_body(be_ref, nu_ref, xs_ref, w1_ref, w3_ref, w2_ref, ys_ref):
    del be_ref
    live = pl.program_id(0) < nu_ref[0]

    @pl.when(jnp.logical_not(live))
    def _():
        ys_ref[...] = jnp.zeros_like(ys_ref)

    @pl.when(live)
    def _():
        xp = xs_ref[...]
        half = xp.shape[-1]
        lo = pltpu.bitcast(xp << 16, F32).astype(BF16)
        hi = pltpu.bitcast(xp & jnp.uint32(0xFFFF0000), F32).astype(BF16)

        def up(w_ref):
            return (jnp.dot(lo, w_ref[0, :half, :].astype(BF16), preferred_element_type=F32)
                    + jnp.dot(hi, w_ref[0, half:, :].astype(BF16), preferred_element_type=F32))

        act = (_silu(up(w1_ref)) * up(w3_ref)).astype(BF16)
        ys_ref[...] = jnp.dot(act, w2_ref[0].astype(BF16), preferred_element_type=F32)


def _experts(blk_eid, n_used, xs, w1, w3, w2, *, rb):
    p, half = xs.shape
    d = 2 * half
    nb = p // rb
    used = lambda i, be, nu: jnp.maximum(jnp.minimum(i, nu[0] - 1), 0)
    wmap = lambda i, be, nu: (be[used(i, be, nu)], 0, 0)
    grid_spec = pltpu.PrefetchScalarGridSpec(
        num_scalar_prefetch=2,
        grid=(nb,),
        in_specs=[pl.BlockSpec((rb, half), lambda i, be, nu: (used(i, be, nu), 0)),
                  pl.BlockSpec((1, d, D_EXPERT), wmap), pl.BlockSpec((1, d, D_EXPERT), wmap),
                  pl.BlockSpec((1, D_EXPERT, d), wmap)],
        out_specs=pl.BlockSpec((rb, d), lambda i, be, nu: (i, 0)),
    )
    return pl.pallas_call(
        _expert_body,
        grid_spec=grid_spec,
        out_shape=jax.ShapeDtypeStruct((p, d), F32),
        compiler_params=_params("arbitrary"),
        name="moe_experts",
    )(blk_eid, n_used, xs, w1, w3, w2)


def _combine_body(dest_ref, ys_ref, x_ref, route_ref, g_ref, o_ref, buf_ref, sem, *, tm, final_norm):
    def issue(n, carry):
        _row_copy(ys_ref, dest_ref[0, 0, n], buf_ref.at[0], n, sem.at[0]).start(priority=0)
        _row_copy(ys_ref, dest_ref[0, 0, tm + n], buf_ref.at[1], n, sem.at[1]).start(priority=1)
        return carry

    lax.fori_loop(0, tm, issue, 0)
    for slot in range(2):
        pltpu.make_async_copy(ys_ref.at[pl.ds(0, tm)], buf_ref.at[slot], sem.at[slot]).wait()

    route = route_ref[...]
    out = (x_ref[...] + route[:, R_GATE0:R_GATE0 + 1] * buf_ref[0]
           + route[:, R_GATE1:R_GATE1 + 1] * buf_ref[1])
    if final_norm:
        out = out * lax.rsqrt(jnp.mean(out * out, axis=-1, keepdims=True) + EPS) * g_ref[...]
    o_ref[...] = out


def _combine(dest3, ys, xn, route, g, *, tm, final_norm):
    n, d = xn.shape
    row = lambda i: (i, 0)
    return pl.pallas_call(
        functools.partial(_combine_body, tm=tm, final_norm=final_norm),
        grid=(n // tm,),
        in_specs=[pl.BlockSpec((1, 1, 2 * tm), lambda i: (i, 0, 0), memory_space=pltpu.SMEM),
                  pl.BlockSpec(memory_space=pl.ANY),
                  pl.BlockSpec((tm, d), row), pl.BlockSpec((tm, LANES), row),
                  pl.BlockSpec((1, d), lambda i: (0, 0))],
        out_specs=pl.BlockSpec((tm, d), row),
        out_shape=jax.ShapeDtypeStruct((n, d), F32),
        scratch_shapes=[pltpu.VMEM((2, tm, d), F32), pltpu.SemaphoreType.DMA((2,))],
        compiler_params=_params("arbitrary"),
        name="moe_combine",
    )(dest3, ys, xn, route, g)


def _moe(hp, route, meta, counts, xn, w1, w3, w2, g_final, *, layer, tm, rb, final_norm):
    n = hp.shape[0]
    cnt = counts[0, LANE_EXP:LANE_EXP + N_EXPERTS].astype(I32)
    nblk = (cnt + rb - 1) // rb
    bend = jnp.cumsum(nblk)
    pstart = (bend - nblk) * rb
    nb = (2 * n) // rb + N_EXPERTS
    n_used = bend[-1:].astype(I32)
    blk_eid = jnp.minimum(jnp.sum(bend[None, :] <= jnp.arange(nb, dtype=I32)[:, None], axis=1),
                          N_EXPERTS - 1).astype(I32)
    dest = pstart[meta[R_EID0:R_EID1 + 1]] + meta[R_RANK0:R_RANK1 + 1]
    dest3 = dest.reshape(2, n // tm, tm).transpose(1, 0, 2).reshape(n // tm, 1, 2 * tm)

    xs = _dispatch(dest3, hp, jnp.zeros((nb * rb, hp.shape[1]), hp.dtype), tm=tm)
    ys = _experts(blk_eid + layer * N_EXPERTS, n_used, xs, w1, w3, w2, rb=rb)
    return _combine(dest3, ys, xn, route, g_final, tm=tm, final_norm=final_norm)


def _lane_row(pairs):
    row = jnp.zeros((LANES,), F32)
    for off, vec in pairs:
        row = row.at[off:off + vec.shape[0]].set(vec.astype(F32))
    return row


def kernel(x, norm_mix, w_in, conv_qkv, dn_a_log, dn_dt_bias, dn_norm, fox_bias, conv_dw, conv_dw_b,
           conv_ln_g, conv_ln_b, w_a, w_b, w_c, w_out, norm_ffn, router_group_w, router_group_b,
           router_expert_w, router_expert_b, expert_w1, expert_w3, expert_w2, norm_final):
    b, t, d = x.shape
    n = b * t
    depth = w_in.shape[0]
    qk_dn = H_DN * DK_DN
    in_sizes = (qk_dn, qk_dn, H_DN * DV_DN, H_DN * DV_DN, H_DN, H_DN,
                H_FOX * D_FOX, H_FOX * D_FOX, H_FOX * D_FOX, H_FOX, 2 * C_CONV, d, d, d)
    splits = np.cumsum(in_sizes)[:-1].tolist()
    tm_moe, rb = 256, 256

    x2 = x.reshape(n, d)
    w1_all = expert_w1.reshape(depth * N_EXPERTS, d, D_EXPERT)
    w3_all = expert_w3.reshape(depth * N_EXPERTS, d, D_EXPERT)
    w2_all = expert_w2.reshape(depth * N_EXPERTS, D_EXPERT, d)
    for l in range(depth):
        (qa, ka, va, za, ba, aa, qb, kb, vb, fb, uc, ga, gb, gc) = jnp.split(w_in[l], splits, axis=1)
        w_main = jnp.concatenate([qa, ka, va, za, qb * (D_FOX ** -0.5 * LOG2E), kb, vb, uc, ga, gb, gc],
                                 axis=1).astype(BF16)
        w_small = jnp.concatenate([ba, aa, fb, jnp.zeros((d, LANES - 3 * H_DN), F32)], axis=1).astype(BF16)
        dn_qkv, za_p, fox_qkv, uc_p, gates, small = _in_proj(x2, norm_mix[l][None, :], w_main, w_small)

        par = jnp.stack([_lane_row([(LANE_G, -jnp.exp(dn_a_log[l]))]),
                         _lane_row([(LANE_G, dn_dt_bias[l]), (LANE_F, fox_bias[l])]),
                         _lane_row([(0, dn_norm[l])])] + [jnp.zeros((LANES,), F32)] * 5)
        oa, qaug, kaug = _delta_rule(dn_qkv.reshape(b, t, -1), za_p.reshape(b, t, -1),
                                     small.reshape(b, t, LANES), conv_qkv[l], par)
        ob = _fox_attention(fox_qkv.reshape(b, t, -1), qaug, kaug)
        vec = jnp.stack([conv_dw_b[l], conv_ln_g[l], conv_ln_b[l]] + [jnp.zeros((C_CONV,), F32)] * 5)
        zc = _conv_module(uc_p.reshape(b, t, -1), conv_dw[l], vec)

        w_r = jnp.concatenate([router_group_w[l], jnp.zeros((d, LANE_EXP - N_GROUPS), F32),
                               router_expert_w[l], jnp.zeros((d, LANES - LANE_EXP - N_EXPERTS), F32)], axis=1)
        b_r = _lane_row([(LANE_GRP, router_group_b[l]), (LANE_EXP, router_expert_b[l])])[None, :]
        xn, hp, route, meta, counts = _merge(
            oa.reshape(n, -1), ob.reshape(n, -1), zc.reshape(n, -1), gates, x2,
            w_a[l].astype(BF16), w_b[l].astype(BF16), w_c[l].astype(BF16), w_out[l].astype(BF16),
            norm_ffn[l][None, :], w_r.astype(BF16), b_r)

        x2 = _moe(hp, route, meta, counts, xn, w1_all, w3_all, w2_all, norm_final[None, :],
                  layer=l, tm=tm_moe, rb=rb, final_norm=(l == depth - 1))
    return x2.reshape(b, t, d)
```

```python
import functools

import jax
import jax.numpy as jnp
import numpy as np
from jax import lax
from jax.experimental import pallas as pl
from jax.experimental.pallas import tpu as pltpu

F32 = jnp.float32
BF16 = jnp.bfloat16
U32 = jnp.uint32
I32 = jnp.int32
HIGHEST = lax.Precision.HIGHEST

EPS = 1e-6
LOG2E = 1.4426950408889634
LANES = 128
SUBLANES = 8
H_DN, DK_DN, DV_DN = 4, 128, 128
SHORT_CONV = 4
CHUNK = 64
H_FOX, D_FOX = 4, 128
C_CONV = 512
CONV_WIDTH = 31
N_GROUPS, EXPERTS_PER_GROUP = 4, 8
N_EXPERTS = N_GROUPS * EXPERTS_PER_GROUP
D_EXPERT = 256

W_DN = 3 * H_DN * DK_DN
W_FOX = 3 * H_FOX * D_FOX
VMEM_LIMIT = 56 * 1024 * 1024

LANE_BETA, LANE_G, LANE_F = 0, 4, 8
LANE_GRP, LANE_EXP = 0, 32
R_GATE0, R_GATE1, R_EID0, R_EID1, R_RANK0, R_RANK1 = 0, 1, 2, 3, 4, 5

NN = (((1,), (0,)), ((), ()))
NT = (((1,), (1,)), ((), ()))
TN = (((0,), (0,)), ((), ()))


def _mm(a, b, dims=NN):
    return lax.dot_general(a.astype(BF16), b.astype(BF16), dims, preferred_element_type=F32)


def _sigmoid(x):
    return 0.5 * jnp.tanh(0.5 * x) + 0.5


def _silu(x):
    return x * _sigmoid(x)


def _params(*sem):
    return pltpu.CompilerParams(dimension_semantics=sem, vmem_limit_bytes=VMEM_LIMIT)


def _in_proj_body(x_ref, g_ref, w_ref, ws_ref, dn_ref, za_ref, fox_ref, uc_ref, gates_ref, small_ref,
                  *, col_chunk):
    x = x_ref[...]
    h = x * lax.rsqrt(jnp.mean(x * x, axis=-1, keepdims=True) + EPS) * g_ref[...]
    hb = h.astype(BF16)
    col = 0
    for ref in (dn_ref, za_ref, fox_ref, uc_ref, gates_ref):
        width = ref.shape[-1]
        for c in range(0, width, col_chunk):
            ref[:, c:c + col_chunk] = jnp.dot(
                hb, w_ref[:, col + c:col + c + col_chunk], preferred_element_type=F32).astype(ref.dtype)
        col += width
    small_ref[...] = jnp.dot(hb, ws_ref[...], preferred_element_type=F32)


def _in_proj(x2, g, w, ws, *, tm=512, col_chunk=512):
    n, d = x2.shape
    d_gate = 3 * d
    widths = (W_DN, H_DN * DV_DN, W_FOX, 2 * C_CONV, d_gate)
    assert w.shape == (d, sum(widths)) and n % tm == 0
    out_shape = [jax.ShapeDtypeStruct((n, wd), BF16) for wd in widths]
    out_shape.append(jax.ShapeDtypeStruct((n, LANES), F32))
    row = lambda i: (i, 0)
    fixed = lambda i: (0, 0)
    return pl.pallas_call(
        functools.partial(_in_proj_body, col_chunk=col_chunk),
        grid=(n // tm,),
        in_specs=[pl.BlockSpec((tm, d), row), pl.BlockSpec((1, d), fixed),
                  pl.BlockSpec(w.shape, fixed, pipeline_mode=pl.Buffered(1)),
                  pl.BlockSpec(ws.shape, fixed, pipeline_mode=pl.Buffered(1))],
        out_specs=[pl.BlockSpec((tm, wd), row) for wd in widths] + [pl.BlockSpec((tm, LANES), row)],
        out_shape=out_shape,
        compiler_params=_params("parallel"),
        name="in_proj",
    )(x2, g, w, ws)


def _softplus_parts(z):
    t = jnp.log1p(jnp.exp(-jnp.abs(z)))
    return jnp.maximum(z, 0.0) + t, -(jnp.maximum(-z, 0.0) + t)


def _delta_body(qkv_ref, za_ref, sm_ref, cw_ref, par_ref, oa_ref, qaug_ref, kaug_ref,
                xs_ref, s_ref, carry_ref, *, ts):
    j = pl.program_id(1)
    halo = SUBLANES
    pack = 2 * SUBLANES

    @pl.when(j == 0)
    def _():
        xs_ref[0:halo, :] = jnp.zeros((halo, W_DN), F32)
        s_ref[...] = jnp.zeros_like(s_ref)
        carry_ref[...] = jnp.zeros_like(carry_ref)

    @pl.when(j > 0)
    def _():
        xs_ref[0:halo, :] = xs_ref[2 * halo:3 * halo, :]

    xb = qkv_ref[0]
    xs_ref[halo:2 * halo, :] = qkv_ref[0, 0:pack, :].astype(F32)[0:halo]
    xs_ref[2 * halo:3 * halo, :] = qkv_ref[0, ts - pack:ts, :].astype(F32)[pack - halo:pack]

    lag = lax.broadcasted_iota(I32, (ts, ts), 0) - lax.broadcasted_iota(I32, (ts, ts), 1)
    shifted = [jnp.dot(jnp.where(lag == s, 1.0, 0.0).astype(BF16), xb, preferred_element_type=F32)
               for s in range(1, SHORT_CONV)]
    head_row = lax.broadcasted_iota(I32, (halo, LANES), 0)

    def conv_silu(lane0):
        lanes = slice(lane0, lane0 + LANES)
        acc = cw_ref[SHORT_CONV - 1:SHORT_CONV, lanes] * xb[:, lanes].astype(F32)
        for s in range(1, SHORT_CONV):
            acc = acc + cw_ref[SHORT_CONV - 1 - s:SHORT_CONV - s, lanes] * shifted[s - 1][:, lanes]
        head = acc[0:halo]
        for s in range(1, SHORT_CONV):
            prev = jnp.where(head_row < s, xs_ref[halo - s:2 * halo - s, lanes], 0.0)
            head = head + cw_ref[SHORT_CONV - 1 - s:SHORT_CONV - s, lanes] * prev
        return _silu(jnp.concatenate([head, acc[halo:]], axis=0))

    def l2n(a):
        return a * lax.rsqrt(jnp.sum(a * a, axis=-1, keepdims=True) + EPS)

    sm = sm_ref[0]
    lane = lax.broadcasted_iota(I32, sm.shape, 1)
    sp, logsig = _softplus_parts(sm + par_ref[1:2, :])
    vals = jnp.where(lane < LANE_G, _sigmoid(sm),
                     jnp.where(lane < LANE_F, par_ref[0:1, :] * sp,
                               jnp.where(lane < LANE_F + H_FOX, logsig, 0.0)))
    row = lax.broadcasted_iota(I32, (ts, ts), 0)
    colm = lax.broadcasted_iota(I32, (ts, ts), 1)
    log_chunk = CHUNK.bit_length() - 1
    causal = (row >= colm) & ((row >> log_chunk) == (colm >> log_chunk))

    hi = vals.astype(BF16)
    rem = vals - hi.astype(F32)
    mid = rem.astype(BF16)
    lo = (rem - mid.astype(F32)).astype(BF16)
    pieces = jnp.concatenate([hi, mid, lo], axis=-1)

    def cumsum(mask):
        y = jnp.dot(jnp.where(mask, 1.0, 0.0).astype(BF16), pieces, preferred_element_type=F32)
        return (y[:, :LANES] + y[:, LANES:2 * LANES]) + y[:, 2 * LANES:]

    ccum = cumsum(row >= colm) + carry_ref[...]
    gcum = cumsum(causal)
    carry_ref[...] = ccum[ts - 1:ts, :]
    gcum_t = gcum.T

    cl = ccum * LOG2E
    c_hi = cl.astype(BF16)
    c_rem = cl - c_hi.astype(F32)
    c_mid = c_rem.astype(BF16)
    c_lo = (c_rem - c_mid.astype(F32)).astype(BF16)
    c_pieces = jnp.concatenate([c_hi, c_mid, c_lo], axis=-1)
    pr = lax.broadcasted_iota(I32, (3 * LANES, LANES), 0)
    pc = lax.broadcasted_iota(I32, (3 * LANES, LANES), 1)
    src_lane, piece = pr & (LANES - 1), pr >> (LANES.bit_length() - 1)
    owned = (src_lane >= LANE_F) & (src_lane < LANE_F + H_FOX) & ((pc >> 3) == src_lane - LANE_F)
    place_q = jnp.where(owned & ((pc & 7) == piece), 1.0, 0.0).astype(BF16)
    place_k = jnp.where(owned & ((pc & 7) == piece + 3), -1.0, 0.0).astype(BF16)
    slot = lane & 7
    in_heads = lane < 8 * H_FOX
    ones_q = jnp.where(in_heads & (slot >= 3) & (slot < 6), 1.0, 0.0)
    ones_k = jnp.where(in_heads & (slot < 3), 1.0, 0.0)
    qaug_ref[0] = (jnp.dot(c_pieces, place_q, preferred_element_type=F32) + ones_q).astype(BF16)
    kaug_ref[0] = (jnp.dot(c_pieces, place_k, preferred_element_type=F32) + ones_k).astype(BF16)

    scale = DK_DN ** -0.5
    dn_norm = par_ref[2:3, :]

    pw = 2 * CHUNK
    prow = lax.broadcasted_iota(I32, (pw, pw), 0)
    pcol = lax.broadcasted_iota(I32, (pw, pw), 1)
    same = (prow >> log_chunk) == (pcol >> log_chunk)
    causal_p = (prow >= pcol) & same
    strict_p = (prow > pcol) & same
    levels = []
    s = 1
    while s < CHUNK:
        levels.append(((prow >> s.bit_length()) == (pcol >> s.bit_length()))
                      & ((prow & s) != 0) & ((pcol & s) == 0))
        s *= 2

    heads = range(H_DN)
    pairs = range(ts // pw)
    q, k, v = [], [], []
    for h in heads:
        q.append(l2n(conv_silu(h * DK_DN)))
        k.append(l2n(conv_silu(H_DN * DK_DN + h * DK_DN)))
        v.append(conv_silu(2 * H_DN * DK_DN + h * DV_DN))

    ctx = []
    for h in heads:
        for p in pairs:
            pr = slice(p * pw, (p + 1) * pw)
            gcol = gcum[pr, LANE_G + h:LANE_G + h + 1]
            grow = gcum_t[LANE_G + h:LANE_G + h + 1, pr]
            beta = vals[pr, LANE_BETA + h:LANE_BETA + h + 1]
            decay = jnp.where(causal_p, jnp.exp(jnp.where(causal_p, gcol - grow, 0.0)), 0.0)
            egc = jnp.exp(gcol)
            kp = k[h][pr]
            kb = kp * beta
            ctx.append(dict(
                h=h, p=p, gcol=gcol, kp=kp,
                a=jnp.where(strict_p, _mm(kb, kp, NT) * decay, 0.0),
                qk=jnp.where(causal_p, _mm(q[h][pr] * scale, kp, NT) * decay, 0.0),
                rhs=jnp.concatenate([v[h][pr] * beta, kb * egc], axis=-1),
                qg=q[h][pr] * (scale * egc)))

    for c in ctx:
        c["n"] = -jnp.where(levels[0], c["a"], 0.0)
    for level in levels[1:]:
        for c in ctx:
            m = jnp.where(level, c["a"], 0.0)
            c["y"] = m + _mm(c["n"], m)
        for c in ctx:
            c["n"] = c["n"] - (c["y"] + _mm(c["y"], c["n"]))
    for c in ctx:
        c["sol"] = c["rhs"] + _mm(c["n"], c["rhs"])

    state = [s_ref[h] for h in heads]
    for p in pairs:
        group = [c for c in ctx if c["p"] == p]
        v_prev = [None] * H_DN
        for ch in range(2):
            rows = slice(ch * CHUNK, (ch + 1) * CHUNK)
            out_rows = slice(p * pw + ch * CHUNK, p * pw + (ch + 1) * CHUNK)
            v_new = [c["sol"][rows, :DV_DN] - _mm(c["sol"][rows, DV_DN:], state[c["h"]]) for c in group]
            for c, vn in zip(group, v_new):
                h = c["h"]
                v_pair = jnp.concatenate([vn, jnp.zeros_like(vn)] if ch == 0 else [v_prev[h], vn], axis=0)
                o = _mm(c["qg"][rows], state[h]) + _mm(c["qk"][rows], v_pair)
                glast = c["gcol"][(ch + 1) * CHUNK - 1:(ch + 1) * CHUNK]
                state[h] = (state[h] * jnp.exp(glast)
                            + _mm(c["kp"][rows] * jnp.exp(glast - c["gcol"][rows]), vn, TN))
                v_prev[h] = vn
                o = o * lax.rsqrt(jnp.mean(o * o, axis=-1, keepdims=True) + EPS) * dn_norm
                za = za_ref[0, out_rows, h * DV_DN:(h + 1) * DV_DN].astype(F32)
                oa_ref[0, out_rows, h * DV_DN:(h + 1) * DV_DN] = (o * _silu(za)).astype(oa_ref.dtype)
    for h in heads:
        s_ref[h] = state[h]


def _delta_rule(qkv, za, small, conv_w, par, *, ts=256):
    b, t, _ = qkv.shape
    assert t % ts == 0 and ts % (2 * CHUNK) == 0
    blk = lambda width: pl.BlockSpec((1, ts, width), lambda bi, j: (bi, j, 0))
    fixed = lambda bi, j: (0, 0)
    return pl.pallas_call(
        functools.partial(_delta_body, ts=ts),
        grid=(b, t // ts),
        in_specs=[blk(W_DN), blk(H_DN * DV_DN), blk(LANES),
                  pl.BlockSpec(conv_w.shape, fixed), pl.BlockSpec(par.shape, fixed)],
        out_specs=[blk(H_DN * DV_DN), blk(LANES), blk(LANES)],
        out_shape=[jax.ShapeDtypeStruct((b, t, H_DN * DV_DN), BF16),
                   jax.ShapeDtypeStruct((b, t, LANES), BF16),
                   jax.ShapeDtypeStruct((b, t, LANES), BF16)],
        scratch_shapes=[pltpu.VMEM((3 * SUBLANES, W_DN), F32),
                        pltpu.VMEM((H_DN, DK_DN, DV_DN), F32),
                        pltpu.VMEM((1, LANES), F32)],
        compiler_params=_params("parallel", "arbitrary"),
        name="delta_rule",
    )(qkv, za, small, conv_w, par)


def _fox_body(q_ref, k_ref, v_ref, qa_ref, ka_ref, o_ref, m_ref, acc_ref, *, tq, tk):
    i = pl.program_id(1)
    m_ref[...] = jnp.full(m_ref.shape, -jnp.inf, F32)
    acc_ref[...] = jnp.zeros_like(acc_ref)
    head_lanes = [slice(h * D_FOX, (h + 1) * D_FOX) for h in range(H_FOX)]
    lane = lax.broadcasted_iota(I32, (tk, LANES), 1)
    own = [jnp.where((lane >> 3) == h, 1.0, 0.0).astype(BF16) for h in range(H_FOX)]
    ones = jnp.ones((tk, D_FOX), BF16)
    keep = lax.broadcasted_iota(I32, (tq, tk), 1) <= lax.broadcasted_iota(I32, (tq, tk), 0)

    def block(start, diag_offset):
        rows = slice(0 if diag_offset is None else diag_offset, tq)
        ka = ka_ref[0, pl.ds(start, tk), :]
        for h, lanes in enumerate(head_lanes):
            q_aug = jnp.concatenate([q_ref[0, rows, lanes], qa_ref[0, rows, :]], axis=1)
            k_aug = jnp.concatenate([k_ref[0, pl.ds(start, tk), lanes], ka * own[h]], axis=1)
            s = lax.dot_general(q_aug, k_aug, NT, preferred_element_type=F32)
            if diag_offset is not None:
                s = jnp.where(keep[:tq - diag_offset], s, -jnp.inf)
            m_prev = m_ref[h, rows]
            m_next = jnp.maximum(m_prev, jnp.max(s, axis=-1, keepdims=True))
            p = jnp.exp2(s - jnp.concatenate([m_next] * (tk // LANES), axis=1))
            alpha = jnp.exp2(m_prev - m_next)
            v_aug = jnp.concatenate([v_ref[0, pl.ds(start, tk), lanes], ones], axis=1)
            acc_ref[h, rows] = (jnp.concatenate([alpha, alpha], axis=1) * acc_ref[h, rows]
                                + jnp.dot(p.astype(BF16), v_aug, preferred_element_type=F32))
            m_ref[h, rows] = m_next

    def full_block(jb, carry):
        block(pl.multiple_of(jb * tk, tk), None)
        return carry

    lax.fori_loop(0, i * (tq // tk), full_block, 0)
    for d in range(tq // tk):
        block(pl.multiple_of(i * tq + d * tk, tk), d * tk)
    for h, lanes in enumerate(head_lanes):
        acc = acc_ref[h]
        o_ref[0, :, lanes] = (acc[:, :D_FOX] / acc[:, D_FOX:]).astype(o_ref.dtype)


def _fox_attention(qkv, qaug, kaug, *, tq=1024, tk=256):
    b, t, _ = qkv.shape
    hd = H_FOX * D_FOX
    assert t % tq == 0 and tq % tk == 0 and tk % LANES == 0
    return pl.pallas_call(
        functools.partial(_fox_body, tq=tq, tk=tk),
        grid=(b, t // tq),
        in_specs=[pl.BlockSpec((1, tq, hd), lambda bi, i: (bi, i, 0)),
                  pl.BlockSpec((1, t, hd), lambda bi, i: (bi, 0, 1)),
                  pl.BlockSpec((1, t, hd), lambda bi, i: (bi, 0, 2)),
                  pl.BlockSpec((1, tq, LANES), lambda bi, i: (bi, i, 0)),
                  pl.BlockSpec((1, t, LANES), lambda bi, i: (bi, 0, 0))],
        out_specs=pl.BlockSpec((1, tq, hd), lambda bi, i: (bi, i, 0)),
        out_shape=jax.ShapeDtypeStruct((b, t, hd), BF16),
        scratch_shapes=[pltpu.VMEM((H_FOX, tq, LANES), F32), pltpu.VMEM((H_FOX, tq, 2 * D_FOX), F32)],
        compiler_params=_params("parallel", "arbitrary"),
        name="fox_attention",
    )(qkv, qkv, qkv, qaug, kaug)


def _conv_body(cur_ref, halo_ref, dw_ref, vec_ref, o_ref, z_ref, zs_ref, y_ref, *, tt, hrows, rchunk):
    i = pl.program_id(1)
    cur = cur_ref[0].astype(F32)
    z_ref[hrows:hrows + tt, :] = cur[:, :C_CONV] * _sigmoid(cur[:, C_CONV:])
    hal = halo_ref[0].astype(F32)
    zh = hal[:, :C_CONV] * _sigmoid(hal[:, C_CONV:])
    z_ref[0:hrows, :] = jnp.where(i > 0, zh, 0.0)

    sub = SUBLANES
    span = tt + hrows - sub
    for phase in range(1, sub):
        zs_ref[phase - 1, 0:span, :] = z_ref[phase:phase + span, :]

    for lg in range(C_CONV // LANES):
        lanes = slice(lg * LANES, (lg + 1) * LANES)
        for r0 in range(0, tt, rchunk):
            acc = jnp.broadcast_to(vec_ref[0:1, lanes], (rchunk, LANES))
            for k in range(CONV_WIDTH):
                off = hrows - (CONV_WIDTH - 1) + k
                phase, base = off % sub, off - off % sub + r0
                if phase == 0:
                    tap = z_ref[base:base + rchunk, lanes]
                else:
                    tap = zs_ref[phase - 1, base:base + rchunk, lanes]
                acc = acc + dw_ref[k:k + 1, lanes] * tap
            y_ref[r0:r0 + rchunk, lanes] = acc

    y = y_ref[...]
    mu = jnp.mean(y, axis=-1, keepdims=True)
    yc = y - mu
    var = jnp.mean(yc * yc, axis=-1, keepdims=True)
    yn = yc * lax.rsqrt(var + EPS) * vec_ref[1:2, :] + vec_ref[2:3, :]
    o_ref[0] = _silu(yn).astype(o_ref.dtype)


def _conv_module(uc, dw, vec, *, tt=256, hrows=32, rchunk=64):
    b, t, _ = uc.shape
    assert t % tt == 0 and tt % hrows == 0 and hrows >= CONV_WIDTH - 1
    ratio = tt // hrows
    return pl.pallas_call(
        functools.partial(_conv_body, tt=tt, hrows=hrows, rchunk=rchunk),
        grid=(b, t // tt),
        in_specs=[pl.BlockSpec((1, tt, 2 * C_CONV), lambda bi, i: (bi, i, 0)),
                  pl.BlockSpec((1, hrows, 2 * C_CONV), lambda bi, i: (bi, jnp.maximum(i * ratio - 1, 0), 0)),
                  pl.BlockSpec(dw.shape, lambda bi, i: (0, 0)),
                  pl.BlockSpec(vec.shape, lambda bi, i: (0, 0))],
        out_specs=pl.BlockSpec((1, tt, C_CONV), lambda bi, i: (bi, i, 0)),
        out_shape=jax.ShapeDtypeStruct((b, t, C_CONV), BF16),
        scratch_shapes=[pltpu.VMEM((tt + hrows, C_CONV), F32), pltpu.VMEM((7, tt + hrows, C_CONV), F32),
                        pltpu.VMEM((tt, C_CONV), F32)],
        compiler_params=_params("parallel", "parallel"),
        name="conv_module",
    )(uc, uc, dw, vec)


def _merge_body(oa_ref, ob_ref, zc_ref, gates_ref, x_ref, wa_ref, wb_ref, wc_ref, wo_ref, g_ref,
                wr_ref, br_ref, xn_ref, hp_ref, route_ref, meta_ref, cnt_ref, run_ref, *, tm):
    step = pl.program_id(0)
    d = x_ref.shape[-1]

    @pl.when(step == 0)
    def _():
        run_ref[...] = jnp.zeros_like(run_ref)

    merged = None
    for idx, (m_ref, w_ref) in enumerate(((oa_ref, wa_ref), (ob_ref, wb_ref), (zc_ref, wc_ref))):
        y = jnp.dot(m_ref[...], w_ref[...], preferred_element_type=F32)
        term = _sigmoid(gates_ref[:, idx * d:(idx + 1) * d].astype(F32)) * y
        merged = term if merged is None else merged + term
    xn = x_ref[...] + jnp.dot(merged.astype(BF16), wo_ref[...], preferred_element_type=F32)
    xn_ref[...] = xn
    h2 = xn * lax.rsqrt(jnp.mean(xn * xn, axis=-1, keepdims=True) + EPS) * g_ref[...]

    lo = pltpu.bitcast(h2[:, :d // 2].astype(BF16).astype(F32), U32) >> 16
    hi = pltpu.bitcast(h2[:, d // 2:].astype(BF16).astype(F32), U32) & jnp.uint32(0xFFFF0000)
    hp_ref[...] = lo | hi

    logits = _mm(h2, wr_ref[...]) + br_ref[...]
    lane = lax.broadcasted_iota(I32, logits.shape, 1)
    big = jnp.int32(4 * LANES)
    in_grp = lane < N_GROUPS
    gl = jnp.where(in_grp, logits, -jnp.inf)
    gmax = jnp.max(gl, axis=-1, keepdims=True)
    grp = jnp.min(jnp.where(gl == gmax, lane, big), axis=-1, keepdims=True)
    p_grp = 1.0 / jnp.sum(jnp.where(in_grp, jnp.exp(gl - gmax), 0.0), axis=-1, keepdims=True)
    e_lo = LANE_EXP + grp * EXPERTS_PER_GROUP
    el = jnp.where((lane >= e_lo) & (lane < e_lo + EXPERTS_PER_GROUP), logits, -jnp.inf)
    v0 = jnp.max(el, axis=-1, keepdims=True)
    i0 = jnp.min(jnp.where(el == v0, lane, big), axis=-1, keepdims=True)
    el1 = jnp.where(lane == i0, -jnp.inf, el)
    v1 = jnp.max(el1, axis=-1, keepdims=True)
    i1 = jnp.min(jnp.where(el1 == v1, lane, big), axis=-1, keepdims=True)
    e1 = jnp.exp(v1 - v0)
    gate0 = p_grp / (1.0 + e1)
    gate1 = p_grp * e1 / (1.0 + e1)

    hot0 = lane == i0
    hot1 = lane == i1
    onehot = jnp.where(hot0 | hot1, 1.0, 0.0)
    ri = lax.broadcasted_iota(I32, (tm, tm), 0)
    cj = lax.broadcasted_iota(I32, (tm, tm), 1)
    prefix = _mm(jnp.where(cj < ri, 1.0, 0.0), onehot) + run_ref[...]
    rank0 = jnp.sum(jnp.where(hot0, prefix, 0.0), axis=-1, keepdims=True)
    rank1 = jnp.sum(jnp.where(hot1, prefix, 0.0), axis=-1, keepdims=True)
    run_ref[...] = run_ref[...] + jnp.sum(onehot, axis=0, keepdims=True)
    cnt_ref[...] = jnp.broadcast_to(run_ref[...], cnt_ref.shape)

    route = jnp.where(lane == R_GATE0, gate0, 0.0)
    route = jnp.where(lane == R_GATE1, gate1, route)
    route = jnp.where(lane == R_EID0, (i0 - LANE_EXP).astype(F32), route)
    route = jnp.where(lane == R_EID1, (i1 - LANE_EXP).astype(F32), route)
    route = jnp.where(lane == R_RANK0, rank0, route)
    route = jnp.where(lane == R_RANK1, rank1, route)
    route_ref[...] = route
    meta_ref[...] = route.T[0:8, :].astype(I32)


def _merge(oa, ob, zc, gates, x2, wa, wb, wc, wo, g, wr, br, *, tm=256):
    n, d = x2.shape
    assert n % tm == 0
    row = lambda i: (i, 0)
    fixed = lambda i: (0, 0)
    full = lambda a: pl.BlockSpec(a.shape, fixed)
    return pl.pallas_call(
        functools.partial(_merge_body, tm=tm),
        grid=(n // tm,),
        in_specs=[pl.BlockSpec((tm, oa.shape[1]), row), pl.BlockSpec((tm, ob.shape[1]), row),
                  pl.BlockSpec((tm, zc.shape[1]), row), pl.BlockSpec((tm, 3 * d), row),
                  pl.BlockSpec((tm, d), row),
                  full(wa), full(wb), full(wc), full(wo), full(g), full(wr), full(br)],
        out_specs=[pl.BlockSpec((tm, d), row), pl.BlockSpec((tm, d // 2), row),
                   pl.BlockSpec((tm, LANES), row), pl.BlockSpec((8, tm), lambda i: (0, i)),
                   pl.BlockSpec((8, LANES), fixed)],
        out_shape=[jax.ShapeDtypeStruct((n, d), F32), jax.ShapeDtypeStruct((n, d // 2), U32),
                   jax.ShapeDtypeStruct((n, LANES), F32), jax.ShapeDtypeStruct((8, n), I32),
                   jax.ShapeDtypeStruct((8, LANES), F32)],
        scratch_shapes=[pltpu.VMEM((1, LANES), F32)],
        compiler_params=_params("arbitrary"),
        name="merge_router",
    )(oa, ob, zc, gates, x2, wa, wb, wc, wo, g, wr, br)


def _row_copy(src_ref, src_row, dst_ref, dst_row, sem):
    return pltpu.make_async_copy(src_ref.at[pl.ds(src_row, 1)], dst_ref.at[pl.ds(dst_row, 1)], sem)


def _dispatch_body(dest_ref, hp_ref, xs_in_ref, xs_ref, sem, *, tm):
    del xs_in_ref
    base = pl.program_id(0) * (2 * tm)

    def issue(n, carry):
        _row_copy(hp_ref, n, xs_ref, dest_ref[base + n], sem.at[0]).start(priority=0)
        _row_copy(hp_ref, n, xs_ref, dest_ref[base + tm + n], sem.at[1]).start(priority=1)
        return carry

    lax.fori_loop(0, tm, issue, 0)
    for slot in range(2):
        pltpu.make_async_copy(hp_ref, xs_ref.at[pl.ds(0, tm)], sem.at[slot]).wait()


def _dispatch(dest3, hp, xs_zero, *, tm):
    n = hp.shape[0]
    grid_spec = pltpu.PrefetchScalarGridSpec(
        num_scalar_prefetch=1,
        grid=(n // tm,),
        in_specs=[pl.BlockSpec((tm, hp.shape[1]), lambda i, dest: (i, 0)), pl.BlockSpec(memory_space=pl.ANY)],
        out_specs=pl.BlockSpec(memory_space=pl.ANY),
        scratch_shapes=[pltpu.SemaphoreType.DMA((2,))],
    )
    return pl.pallas_call(
        functools.partial(_dispatch_body, tm=tm),
        grid_spec=grid_spec,
        out_shape=jax.ShapeDtypeStruct(xs_zero.shape, xs_zero.dtype),
        input_output_aliases={2: 0},
        compiler_params=pltpu.CompilerParams(dimension_semantics=("arbitrary",), has_side_effects=True),
        name="moe_dispatch",
    )(dest3.reshape(-1), hp, xs_zero)


def _expert_body(be_ref, nu_ref, xs_ref, w1_ref, w3_ref, w2_ref, ys_ref):
    del be_ref
    live = pl.program_id(0) < nu_ref[0]

    @pl.when(jnp.logical_not(live))
    def _():
        ys_ref[...] = jnp.zeros_like(ys_ref)

    @pl.when(live)
    def _():
        xp = xs_ref[...]
        half = xp.shape[-1]
        lo = pltpu.bitcast(xp << 16, F32).astype(BF16)
        hi = pltpu.bitcast(xp & jnp.uint32(0xFFFF0000), F32).astype(BF16)

        def up(w_ref):
            return (jnp.dot(lo, w_ref[0, :half, :].astype(BF16), preferred_element_type=F32)
                    + jnp.dot(hi, w_ref[0, half:, :].astype(BF16), preferred_element_type=F32))

        act = (_silu(up(w1_ref)) * up(w3_ref)).astype(BF16)
        ys_ref[...] = jnp.dot(act, w2_ref[0].astype(BF16), preferred_element_type=F32)


def _experts(blk_eid, n_used, xs, w1, w3, w2, *, rb):
    p, half = xs.shape
    d = 2 * half
    nb = p // rb
    used = lambda i, be, nu: jnp.maximum(jnp.minimum(i, nu[0] - 1), 0)
    wmap = lambda i, be, nu: (be[used(i, be, nu)], 0, 0)
    grid_spec = pltpu.PrefetchScalarGridSpec(
        num_scalar_prefetch=2,
        grid=(nb,),
        in_specs=[pl.BlockSpec((rb, half), lambda i, be, nu: (used(i, be, nu), 0)),
                  pl.BlockSpec((1, d, D_EXPERT), wmap), pl.BlockSpec((1, d, D_EXPERT), wmap),
                  pl.BlockSpec((1, D_EXPERT, d), wmap)],
        out_specs=pl.BlockSpec((rb, d), lambda i, be, nu: (i, 0)),
    )
    return pl.pallas_call(
        _expert_body,
        grid_spec=grid_spec,
        out_shape=jax.ShapeDtypeStruct((p, d), F32),
        compiler_params=_params("arbitrary"),
        name="moe_experts",
    )(blk_eid, n_used, xs, w1, w3, w2)


def _combine_body(dest_ref, ys_ref, x_ref, route_ref, g_ref, o_ref, buf_ref, sem, *, tm, final_norm):
    def issue(n, carry):
        _row_copy(ys_ref, dest_ref[0, 0, n], buf_ref.at[0], n, sem.at[0]).start(priority=0)
        _row_copy(ys_ref, dest_ref[0, 0, tm + n], buf_ref.at[1], n, sem.at[1]).start(priority=1)
        return carry

    lax.fori_loop(0, tm, issue, 0)
    for slot in range(2):
        pltpu.make_async_copy(ys_ref.at[pl.ds(0, tm)], buf_ref.at[slot], sem.at[slot]).wait()

    route = route_ref[...]
    out = (x_ref[...] + route[:, R_GATE0:R_GATE0 + 1] * buf_ref[0]
           + route[:, R_GATE1:R_GATE1 + 1] * buf_ref[1])
    if final_norm:
        out = out * lax.rsqrt(jnp.mean(out * out, axis=-1, keepdims=True) + EPS) * g_ref[...]
    o_ref[...] = out


def _combine(dest3, ys, xn, route, g, *, tm, final_norm):
    n, d = xn.shape
    row = lambda i: (i, 0)
    return pl.pallas_call(
        functools.partial(_combine_body, tm=tm, final_norm=final_norm),
        grid=(n // tm,),
        in_specs=[pl.BlockSpec((1, 1, 2 * tm), lambda i: (i, 0, 0), memory_space=pltpu.SMEM),
                  pl.BlockSpec(memory_space=pl.ANY),
                  pl.BlockSpec((tm, d), row), pl.BlockSpec((tm, LANES), row),
                  pl.BlockSpec((1, d), lambda i: (0, 0))],
        out_specs=pl.BlockSpec((tm, d), row),
        out_shape=jax.ShapeDtypeStruct((n, d), F32),
        scratch_shapes=[pltpu.VMEM((2, tm, d), F32), pltpu.SemaphoreType.DMA((2,))],
        compiler_params=_params("arbitrary"),
        name="moe_combine",
    )(dest3, ys, xn, route, g)


def _moe(hp, route, meta, counts, xn, w1, w3, w2, g_final, *, layer, tm, rb, final_norm):
    n = hp.shape[0]
    cnt = counts[0, LANE_EXP:LANE_EXP + N_EXPERTS].astype(I32)
    nblk = (cnt + rb - 1) // rb
    bend = jnp.cumsum(nblk)
    pstart = (bend - nblk) * rb
    nb = (2 * n) // rb + N_EXPERTS
    n_used = bend[-1:].astype(I32)
    blk_eid = jnp.minimum(jnp.sum(bend[None, :] <= jnp.arange(nb, dtype=I32)[:, None], axis=1),
                          N_EXPERTS - 1).astype(I32)
    eid = meta[R_EID0:R_EID1 + 1]
    first = jnp.sum(jnp.where(eid[None] == jnp.arange(N_EXPERTS, dtype=I32)[:, None, None],
                              pstart[:, None, None], 0), axis=0)
    dest = first + meta[R_RANK0:R_RANK1 + 1]
    dest3 = dest.reshape(2, n // tm, tm).transpose(1, 0, 2).reshape(n // tm, 1, 2 * tm)

    xs = _dispatch(dest3, hp, jnp.zeros((nb * rb, hp.shape[1]), hp.dtype), tm=tm)
    ys = _experts(blk_eid + layer * N_EXPERTS, n_used, xs, w1, w3, w2, rb=rb)
    return _combine(dest3, ys, xn, route, g_final, tm=tm, final_norm=final_norm)


def _lane_row(pairs):
    row = jnp.zeros((LANES,), F32)
    for off, vec in pairs:
        row = row.at[off:off + vec.shape[0]].set(vec.astype(F32))
    return row


def kernel(x, norm_mix, w_in, conv_qkv, dn_a_log, dn_dt_bias, dn_norm, fox_bias, conv_dw, conv_dw_b,
           conv_ln_g, conv_ln_b, w_a, w_b, w_c, w_out, norm_ffn, router_group_w, router_group_b,
           router_expert_w, router_expert_b, expert_w1, expert_w3, expert_w2, norm_final):
    b, t, d = x.shape
    n = b * t
    depth = w_in.shape[0]
    qk_dn = H_DN * DK_DN
    in_sizes = (qk_dn, qk_dn, H_DN * DV_DN, H_DN * DV_DN, H_DN, H_DN,
                H_FOX * D_FOX, H_FOX * D_FOX, H_FOX * D_FOX, H_FOX, 2 * C_CONV, d, d, d)
    splits = np.cumsum(in_sizes)[:-1].tolist()
    tm_moe, rb = 256, 256

    x2 = x.reshape(n, d)
    w1_all = expert_w1.reshape(depth * N_EXPERTS, d, D_EXPERT)
    w3_all = expert_w3.reshape(depth * N_EXPERTS, d, D_EXPERT)
    w2_all = expert_w2.reshape(depth * N_EXPERTS, D_EXPERT, d)
    for l in range(depth):
        (qa, ka, va, za, ba, aa, qb, kb, vb, fb, uc, ga, gb, gc) = jnp.split(w_in[l], splits, axis=1)
        w_main = jnp.concatenate([qa, ka, va, za, qb * (D_FOX ** -0.5 * LOG2E), kb, vb, uc, ga, gb, gc],
                                 axis=1).astype(BF16)
        w_small = jnp.concatenate([ba, aa, fb, jnp.zeros((d, LANES - 3 * H_DN), F32)], axis=1).astype(BF16)
        dn_qkv, za_p, fox_qkv, uc_p, gates, small = _in_proj(x2, norm_mix[l][None, :], w_main, w_small)

        par = jnp.stack([_lane_row([(LANE_G, -jnp.exp(dn_a_log[l]))]),
                         _lane_row([(LANE_G, dn_dt_bias[l]), (LANE_F, fox_bias[l])]),
                         _lane_row([(0, dn_norm[l])])] + [jnp.zeros((LANES,), F32)] * 5)
        oa, qaug, kaug = _delta_rule(dn_qkv.reshape(b, t, -1), za_p.reshape(b, t, -1),
                                     small.reshape(b, t, LANES), conv_qkv[l], par)
        ob = _fox_attention(fox_qkv.reshape(b, t, -1), qaug, kaug)
        vec = jnp.stack([conv_dw_b[l], conv_ln_g[l], conv_ln_b[l]] + [jnp.zeros((C_CONV,), F32)] * 5)
        zc = _conv_module(uc_p.reshape(b, t, -1), conv_dw[l], vec)

        w_r = jnp.concatenate([router_group_w[l], jnp.zeros((d, LANE_EXP - N_GROUPS), F32),
                               router_expert_w[l], jnp.zeros((d, LANES - LANE_EXP - N_EXPERTS), F32)], axis=1)
        b_r = _lane_row([(LANE_GRP, router_group_b[l]), (LANE_EXP, router_expert_b[l])])[None, :]
        xn, hp, route, meta, counts = _merge(
            oa.reshape(n, -1), ob.reshape(n, -1), zc.reshape(n, -1), gates, x2,
            w_a[l].astype(BF16), w_b[l].astype(BF16), w_c[l].astype(BF16), w_out[l].astype(BF16),
            norm_ffn[l][None, :], w_r.astype(BF16), b_r)

        x2 = _moe(hp, route, meta, counts, xn, w1_all, w3_all, w2_all, norm_final[None, :],
                  layer=l, tm=tm_moe, rb=rb, final_norm=(l == depth - 1))
    return x2.reshape(b, t, d)
```

```python
import functools

import jax
import jax.numpy as jnp
import numpy as np
from jax import lax
from jax.experimental import pallas as pl
from jax.experimental.pallas import tpu as pltpu
from jax.experimental.pallas import tpu_sc as plsc

F32 = jnp.float32
BF16 = jnp.bfloat16
U32 = jnp.uint32
I32 = jnp.int32
HIGHEST = lax.Precision.HIGHEST

EPS = 1e-6
LOG2E = 1.4426950408889634
LANES = 128
SUBLANES = 8
SC_CORES, SC_SUBCORES = 2, 16
H_DN, DK_DN, DV_DN = 4, 128, 128
SHORT_CONV = 4
CHUNK = 64
H_FOX, D_FOX = 4, 128
C_CONV = 512
CONV_WIDTH = 31
N_GROUPS, EXPERTS_PER_GROUP = 4, 8
N_EXPERTS = N_GROUPS * EXPERTS_PER_GROUP
D_EXPERT = 256

W_DN = 3 * H_DN * DK_DN
W_FOX = 3 * H_FOX * D_FOX
VMEM_LIMIT = 56 * 1024 * 1024

LANE_BETA, LANE_G, LANE_F = 0, 4, 8
LANE_GRP, LANE_EXP = 0, 32
R_GATE0, R_GATE1, R_EID0, R_EID1, R_RANK0, R_RANK1 = 0, 1, 2, 3, 4, 5

NN = (((1,), (0,)), ((), ()))
NT = (((1,), (1,)), ((), ()))
TN = (((0,), (0,)), ((), ()))


def _mm(a, b, dims=NN):
    return lax.dot_general(a.astype(BF16), b.astype(BF16), dims, preferred_element_type=F32)


def _sigmoid(x):
    return 0.5 * jnp.tanh(0.5 * x) + 0.5


def _silu(x):
    return x * _sigmoid(x)


def _params(*sem):
    return pltpu.CompilerParams(dimension_semantics=sem, vmem_limit_bytes=VMEM_LIMIT)


def _in_proj_body(x_ref, g_ref, w_ref, ws_ref, dn_ref, za_ref, fox_ref, uc_ref, gates_ref, small_ref,
                  *, col_chunk):
    x = x_ref[...]
    h = x * lax.rsqrt(jnp.mean(x * x, axis=-1, keepdims=True) + EPS) * g_ref[...]
    hb = h.astype(BF16)
    col = 0
    for ref in (dn_ref, za_ref, fox_ref, uc_ref, gates_ref):
        width = ref.shape[-1]
        for c in range(0, width, col_chunk):
            ref[:, c:c + col_chunk] = jnp.dot(
                hb, w_ref[:, col + c:col + c + col_chunk], preferred_element_type=F32).astype(ref.dtype)
        col += width
    small_ref[...] = jnp.dot(hb, ws_ref[...], preferred_element_type=F32)


def _in_proj(x2, g, w, ws, *, tm=512, col_chunk=512):
    n, d = x2.shape
    d_gate = 3 * d
    widths = (W_DN, H_DN * DV_DN, W_FOX, 2 * C_CONV, d_gate)
    assert w.shape == (d, sum(widths)) and n % tm == 0
    out_shape = [jax.ShapeDtypeStruct((n, wd), BF16) for wd in widths]
    out_shape.append(jax.ShapeDtypeStruct((n, LANES), F32))
    row = lambda i: (i, 0)
    fixed = lambda i: (0, 0)
    return pl.pallas_call(
        functools.partial(_in_proj_body, col_chunk=col_chunk),
        grid=(n // tm,),
        in_specs=[pl.BlockSpec((tm, d), row), pl.BlockSpec((1, d), fixed),
                  pl.BlockSpec(w.shape, fixed, pipeline_mode=pl.Buffered(1)),
                  pl.BlockSpec(ws.shape, fixed, pipeline_mode=pl.Buffered(1))],
        out_specs=[pl.BlockSpec((tm, wd), row) for wd in widths] + [pl.BlockSpec((tm, LANES), row)],
        out_shape=out_shape,
        compiler_params=_params("parallel"),
        name="in_proj",
    )(x2, g, w, ws)


def _softplus_parts(z):
    t = jnp.log1p(jnp.exp(-jnp.abs(z)))
    return jnp.maximum(z, 0.0) + t, -(jnp.maximum(-z, 0.0) + t)


def _delta_body(qkv_ref, za_ref, sm_ref, cw_ref, par_ref, oa_ref, qaug_ref, kaug_ref,
                xs_ref, s_ref, carry_ref, *, ts):
    j = pl.program_id(1)
    halo = SUBLANES
    pack = 2 * SUBLANES

    @pl.when(j == 0)
    def _():
        xs_ref[0:halo, :] = jnp.zeros((halo, W_DN), F32)
        s_ref[...] = jnp.zeros_like(s_ref)
        carry_ref[...] = jnp.zeros_like(carry_ref)

    @pl.when(j > 0)
    def _():
        xs_ref[0:halo, :] = xs_ref[2 * halo:3 * halo, :]

    xb = qkv_ref[0]
    xs_ref[halo:2 * halo, :] = qkv_ref[0, 0:pack, :].astype(F32)[0:halo]
    xs_ref[2 * halo:3 * halo, :] = qkv_ref[0, ts - pack:ts, :].astype(F32)[pack - halo:pack]

    lag = lax.broadcasted_iota(I32, (ts, ts), 0) - lax.broadcasted_iota(I32, (ts, ts), 1)
    shifted = [jnp.dot(jnp.where(lag == s, 1.0, 0.0).astype(BF16), xb, preferred_element_type=F32)
               for s in range(1, SHORT_CONV)]
    head_row = lax.broadcasted_iota(I32, (halo, LANES), 0)

    def conv_silu(lane0):
        lanes = slice(lane0, lane0 + LANES)
        acc = cw_ref[SHORT_CONV - 1:SHORT_CONV, lanes] * xb[:, lanes].astype(F32)
        for s in range(1, SHORT_CONV):
            acc = acc + cw_ref[SHORT_CONV - 1 - s:SHORT_CONV - s, lanes] * shifted[s - 1][:, lanes]
        head = acc[0:halo]
        for s in range(1, SHORT_CONV):
            prev = jnp.where(head_row < s, xs_ref[halo - s:2 * halo - s, lanes], 0.0)
            head = head + cw_ref[SHORT_CONV - 1 - s:SHORT_CONV - s, lanes] * prev
        return _silu(jnp.concatenate([head, acc[halo:]], axis=0))

    def l2n(a):
        return a * lax.rsqrt(jnp.sum(a * a, axis=-1, keepdims=True) + EPS)

    sm = sm_ref[0]
    lane = lax.broadcasted_iota(I32, sm.shape, 1)
    sp, logsig = _softplus_parts(sm + par_ref[1:2, :])
    vals = jnp.where(lane < LANE_G, _sigmoid(sm),
                     jnp.where(lane < LANE_F, par_ref[0:1, :] * sp,
                               jnp.where(lane < LANE_F + H_FOX, logsig, 0.0)))
    row = lax.broadcasted_iota(I32, (ts, ts), 0)
    colm = lax.broadcasted_iota(I32, (ts, ts), 1)
    log_chunk = CHUNK.bit_length() - 1
    causal = (row >= colm) & ((row >> log_chunk) == (colm >> log_chunk))

    hi = vals.astype(BF16)
    rem = vals - hi.astype(F32)
    mid = rem.astype(BF16)
    lo = (rem - mid.astype(F32)).astype(BF16)
    pieces = jnp.concatenate([hi, mid, lo], axis=-1)

    def cumsum(mask):
        y = jnp.dot(jnp.where(mask, 1.0, 0.0).astype(BF16), pieces, preferred_element_type=F32)
        return (y[:, :LANES] + y[:, LANES:2 * LANES]) + y[:, 2 * LANES:]

    ccum = cumsum(row >= colm) + carry_ref[...]
    gcum = cumsum(causal)
    carry_ref[...] = ccum[ts - 1:ts, :]
    gcum_t = gcum.T

    cl = ccum * LOG2E
    c_hi = cl.astype(BF16)
    c_rem = cl - c_hi.astype(F32)
    c_mid = c_rem.astype(BF16)
    c_lo = (c_rem - c_mid.astype(F32)).astype(BF16)
    c_pieces = jnp.concatenate([c_hi, c_mid, c_lo], axis=-1)
    pr = lax.broadcasted_iota(I32, (3 * LANES, LANES), 0)
    pc = lax.broadcasted_iota(I32, (3 * LANES, LANES), 1)
    src_lane, piece = pr & (LANES - 1), pr >> (LANES.bit_length() - 1)
    owned = (src_lane >= LANE_F) & (src_lane < LANE_F + H_FOX) & ((pc >> 3) == src_lane - LANE_F)
    place_q = jnp.where(owned & ((pc & 7) == piece), 1.0, 0.0).astype(BF16)
    place_k = jnp.where(owned & ((pc & 7) == piece + 3), -1.0, 0.0).astype(BF16)
    slot = lane & 7
    in_heads = lane < 8 * H_FOX
    ones_q = jnp.where(in_heads & (slot >= 3) & (slot < 6), 1.0, 0.0)
    ones_k = jnp.where(in_heads & (slot < 3), 1.0, 0.0)
    qaug_ref[0] = (jnp.dot(c_pieces, place_q, preferred_element_type=F32) + ones_q).astype(BF16)
    kaug_ref[0] = (jnp.dot(c_pieces, place_k, preferred_element_type=F32) + ones_k).astype(BF16)

    scale = DK_DN ** -0.5
    dn_norm = par_ref[2:3, :]

    pw = 2 * CHUNK
    prow = lax.broadcasted_iota(I32, (pw, pw), 0)
    pcol = lax.broadcasted_iota(I32, (pw, pw), 1)
    same = (prow >> log_chunk) == (pcol >> log_chunk)
    causal_p = (prow >= pcol) & same
    strict_p = (prow > pcol) & same
    levels = []
    s = 1
    while s < CHUNK:
        levels.append(((prow >> s.bit_length()) == (pcol >> s.bit_length()))
                      & ((prow & s) != 0) & ((pcol & s) == 0))
        s *= 2

    heads = range(H_DN)
    pairs = range(ts // pw)
    q, k, v = [], [], []
    for h in heads:
        q.append(l2n(conv_silu(h * DK_DN)))
        k.append(l2n(conv_silu(H_DN * DK_DN + h * DK_DN)))
        v.append(conv_silu(2 * H_DN * DK_DN + h * DV_DN))

    ctx = []
    for h in heads:
        for p in pairs:
            pr = slice(p * pw, (p + 1) * pw)
            gcol = gcum[pr, LANE_G + h:LANE_G + h + 1]
            grow = gcum_t[LANE_G + h:LANE_G + h + 1, pr]
            beta = vals[pr, LANE_BETA + h:LANE_BETA + h + 1]
            decay = jnp.where(causal_p, jnp.exp(jnp.where(causal_p, gcol - grow, 0.0)), 0.0)
            egc = jnp.exp(gcol)
            kp = k[h][pr]
            kb = kp * beta
            ctx.append(dict(
                h=h, p=p, gcol=gcol, kp=kp,
                a=jnp.where(strict_p, _mm(kb, kp, NT) * decay, 0.0),
                qk=jnp.where(causal_p, _mm(q[h][pr] * scale, kp, NT) * decay, 0.0),
                rhs=jnp.concatenate([v[h][pr] * beta, kb * egc], axis=-1),
                qg=q[h][pr] * (scale * egc)))

    for c in ctx:
        c["n"] = -jnp.where(levels[0], c["a"], 0.0)
    for level in levels[1:]:
        for c in ctx:
            m = jnp.where(level, c["a"], 0.0)
            c["y"] = m + _mm(c["n"], m)
        for c in ctx:
            c["n"] = c["n"] - (c["y"] + _mm(c["y"], c["n"]))
    for c in ctx:
        c["sol"] = c["rhs"] + _mm(c["n"], c["rhs"])

    state = [s_ref[h] for h in heads]
    for p in pairs:
        group = [c for c in ctx if c["p"] == p]
        v_prev = [None] * H_DN
        for ch in range(2):
            rows = slice(ch * CHUNK, (ch + 1) * CHUNK)
            out_rows = slice(p * pw + ch * CHUNK, p * pw + (ch + 1) * CHUNK)
            v_new = [c["sol"][rows, :DV_DN] - _mm(c["sol"][rows, DV_DN:], state[c["h"]]) for c in group]
            for c, vn in zip(group, v_new):
                h = c["h"]
                v_pair = jnp.concatenate([vn, jnp.zeros_like(vn)] if ch == 0 else [v_prev[h], vn], axis=0)
                o = _mm(c["qg"][rows], state[h]) + _mm(c["qk"][rows], v_pair)
                glast = c["gcol"][(ch + 1) * CHUNK - 1:(ch + 1) * CHUNK]
                state[h] = (state[h] * jnp.exp(glast)
                            + _mm(c["kp"][rows] * jnp.exp(glast - c["gcol"][rows]), vn, TN))
                v_prev[h] = vn
                o = o * lax.rsqrt(jnp.mean(o * o, axis=-1, keepdims=True) + EPS) * dn_norm
                za = za_ref[0, out_rows, h * DV_DN:(h + 1) * DV_DN].astype(F32)
                oa_ref[0, out_rows, h * DV_DN:(h + 1) * DV_DN] = (o * _silu(za)).astype(oa_ref.dtype)
    for h in heads:
        s_ref[h] = state[h]


def _delta_rule(qkv, za, small, conv_w, par, *, ts=256):
    b, t, _ = qkv.shape
    assert t % ts == 0 and ts % (2 * CHUNK) == 0
    blk = lambda width: pl.BlockSpec((1, ts, width), lambda bi, j: (bi, j, 0))
    fixed = lambda bi, j: (0, 0)
    return pl.pallas_call(
        functools.partial(_delta_body, ts=ts),
        grid=(b, t // ts),
        in_specs=[blk(W_DN), blk(H_DN * DV_DN), blk(LANES),
                  pl.BlockSpec(conv_w.shape, fixed), pl.BlockSpec(par.shape, fixed)],
        out_specs=[blk(H_DN * DV_DN), blk(LANES), blk(LANES)],
        out_shape=[jax.ShapeDtypeStruct((b, t, H_DN * DV_DN), BF16),
                   jax.ShapeDtypeStruct((b, t, LANES), BF16),
                   jax.ShapeDtypeStruct((b, t, LANES), BF16)],
        scratch_shapes=[pltpu.VMEM((3 * SUBLANES, W_DN), F32),
                        pltpu.VMEM((H_DN, DK_DN, DV_DN), F32),
                        pltpu.VMEM((1, LANES), F32)],
        compiler_params=_params("parallel", "arbitrary"),
        name="delta_rule",
    )(qkv, za, small, conv_w, par)


def _fox_body(q_ref, k_ref, v_ref, qa_ref, ka_ref, o_ref, m_ref, acc_ref, *, tq, tk):
    i = pl.program_id(1)
    m_ref[...] = jnp.full(m_ref.shape, -jnp.inf, F32)
    acc_ref[...] = jnp.zeros_like(acc_ref)
    head_lanes = [slice(h * D_FOX, (h + 1) * D_FOX) for h in range(H_FOX)]
    lane = lax.broadcasted_iota(I32, (tk, LANES), 1)
    own = [jnp.where((lane >> 3) == h, 1.0, 0.0).astype(BF16) for h in range(H_FOX)]
    ones = jnp.ones((tk, D_FOX), BF16)
    keep = lax.broadcasted_iota(I32, (tq, tk), 1) <= lax.broadcasted_iota(I32, (tq, tk), 0)

    def block(start, diag_offset):
        rows = slice(0 if diag_offset is None else diag_offset, tq)
        ka = ka_ref[0, pl.ds(start, tk), :]
        for h, lanes in enumerate(head_lanes):
            q_aug = jnp.concatenate([q_ref[0, rows, lanes], qa_ref[0, rows, :]], axis=1)
            k_aug = jnp.concatenate([k_ref[0, pl.ds(start, tk), lanes], ka * own[h]], axis=1)
            s = lax.dot_general(q_aug, k_aug, NT, preferred_element_type=F32)
            if diag_offset is not None:
                s = jnp.where(keep[:tq - diag_offset], s, -jnp.inf)
            m_prev = m_ref[h, rows]
            m_next = jnp.maximum(m_prev, jnp.max(s, axis=-1, keepdims=True))
            p = jnp.exp2(s - jnp.concatenate([m_next] * (tk // LANES), axis=1))
            alpha = jnp.exp2(m_prev - m_next)
            v_aug = jnp.concatenate([v_ref[0, pl.ds(start, tk), lanes], ones], axis=1)
            acc_ref[h, rows] = (jnp.concatenate([alpha, alpha], axis=1) * acc_ref[h, rows]
                                + jnp.dot(p.astype(BF16), v_aug, preferred_element_type=F32))
            m_ref[h, rows] = m_next

    def full_block(jb, carry):
        block(pl.multiple_of(jb * tk, tk), None)
        return carry

    lax.fori_loop(0, i * (tq // tk), full_block, 0)
    for d in range(tq // tk):
        block(pl.multiple_of(i * tq + d * tk, tk), d * tk)
    for h, lanes in enumerate(head_lanes):
        acc = acc_ref[h]
        o_ref[0, :, lanes] = (acc[:, :D_FOX] / acc[:, D_FOX:]).astype(o_ref.dtype)


def _fox_attention(qkv, qaug, kaug, *, tq=1024, tk=256):
    b, t, _ = qkv.shape
    hd = H_FOX * D_FOX
    assert t % tq == 0 and tq % tk == 0 and tk % LANES == 0
    return pl.pallas_call(
        functools.partial(_fox_body, tq=tq, tk=tk),
        grid=(b, t // tq),
        in_specs=[pl.BlockSpec((1, tq, hd), lambda bi, i: (bi, i, 0)),
                  pl.BlockSpec((1, t, hd), lambda bi, i: (bi, 0, 1)),
                  pl.BlockSpec((1, t, hd), lambda bi, i: (bi, 0, 2)),
                  pl.BlockSpec((1, tq, LANES), lambda bi, i: (bi, i, 0)),
                  pl.BlockSpec((1, t, LANES), lambda bi, i: (bi, 0, 0))],
        out_specs=pl.BlockSpec((1, tq, hd), lambda bi, i: (bi, i, 0)),
        out_shape=jax.ShapeDtypeStruct((b, t, hd), BF16),
        scratch_shapes=[pltpu.VMEM((H_FOX, tq, LANES), F32), pltpu.VMEM((H_FOX, tq, 2 * D_FOX), F32)],
        compiler_params=_params("parallel", "arbitrary"),
        name="fox_attention",
    )(qkv, qkv, qkv, qaug, kaug)


def _conv_body(cur_ref, halo_ref, dw_ref, vec_ref, o_ref, z_ref, zs_ref, y_ref, *, tt, hrows, rchunk):
    i = pl.program_id(1)
    cur = cur_ref[0].astype(F32)
    z_ref[hrows:hrows + tt, :] = cur[:, :C_CONV] * _sigmoid(cur[:, C_CONV:])
    hal = halo_ref[0].astype(F32)
    zh = hal[:, :C_CONV] * _sigmoid(hal[:, C_CONV:])
    z_ref[0:hrows, :] = jnp.where(i > 0, zh, 0.0)

    sub = SUBLANES
    span = tt + hrows - sub
    for phase in range(1, sub):
        zs_ref[phase - 1, 0:span, :] = z_ref[phase:phase + span, :]

    for lg in range(C_CONV // LANES):
        lanes = slice(lg * LANES, (lg + 1) * LANES)
        for r0 in range(0, tt, rchunk):
            acc = jnp.broadcast_to(vec_ref[0:1, lanes], (rchunk, LANES))
            for k in range(CONV_WIDTH):
                off = hrows - (CONV_WIDTH - 1) + k
                phase, base = off % sub, off - off % sub + r0
                if phase == 0:
                    tap = z_ref[base:base + rchunk, lanes]
                else:
                    tap = zs_ref[phase - 1, base:base + rchunk, lanes]
                acc = acc + dw_ref[k:k + 1, lanes] * tap
            y_ref[r0:r0 + rchunk, lanes] = acc

    y = y_ref[...]
    mu = jnp.mean(y, axis=-1, keepdims=True)
    yc = y - mu
    var = jnp.mean(yc * yc, axis=-1, keepdims=True)
    yn = yc * lax.rsqrt(var + EPS) * vec_ref[1:2, :] + vec_ref[2:3, :]
    o_ref[0] = _silu(yn).astype(o_ref.dtype)


def _conv_module(uc, dw, vec, *, tt=256, hrows=32, rchunk=64):
    b, t, _ = uc.shape
    assert t % tt == 0 and tt % hrows == 0 and hrows >= CONV_WIDTH - 1
    ratio = tt // hrows
    return pl.pallas_call(
        functools.partial(_conv_body, tt=tt, hrows=hrows, rchunk=rchunk),
        grid=(b, t // tt),
        in_specs=[pl.BlockSpec((1, tt, 2 * C_CONV), lambda bi, i: (bi, i, 0)),
                  pl.BlockSpec((1, hrows, 2 * C_CONV), lambda bi, i: (bi, jnp.maximum(i * ratio - 1, 0), 0)),
                  pl.BlockSpec(dw.shape, lambda bi, i: (0, 0)),
                  pl.BlockSpec(vec.shape, lambda bi, i: (0, 0))],
        out_specs=pl.BlockSpec((1, tt, C_CONV), lambda bi, i: (bi, i, 0)),
        out_shape=jax.ShapeDtypeStruct((b, t, C_CONV), BF16),
        scratch_shapes=[pltpu.VMEM((tt + hrows, C_CONV), F32), pltpu.VMEM((7, tt + hrows, C_CONV), F32),
                        pltpu.VMEM((tt, C_CONV), F32)],
        compiler_params=_params("parallel", "parallel"),
        name="conv_module",
    )(uc, uc, dw, vec)


def _merge_body(oa_ref, ob_ref, zc_ref, gates_ref, x_ref, wa_ref, wb_ref, wc_ref, wo_ref, g_ref,
                wr_ref, br_ref, xn_ref, hp_ref, route_ref, meta_ref, cnt_ref, run_ref, *, tm):
    step = pl.program_id(0)
    d = x_ref.shape[-1]

    @pl.when(step == 0)
    def _():
        run_ref[...] = jnp.zeros_like(run_ref)

    merged = None
    for idx, (m_ref, w_ref) in enumerate(((oa_ref, wa_ref), (ob_ref, wb_ref), (zc_ref, wc_ref))):
        y = jnp.dot(m_ref[...], w_ref[...], preferred_element_type=F32)
        term = _sigmoid(gates_ref[:, idx * d:(idx + 1) * d].astype(F32)) * y
        merged = term if merged is None else merged + term
    xn = x_ref[...] + jnp.dot(merged.astype(BF16), wo_ref[...], preferred_element_type=F32)
    xn_ref[...] = xn
    h2 = xn * lax.rsqrt(jnp.mean(xn * xn, axis=-1, keepdims=True) + EPS) * g_ref[...]

    lo = pltpu.bitcast(h2[:, :d // 2].astype(BF16).astype(F32), U32) >> 16
    hi = pltpu.bitcast(h2[:, d // 2:].astype(BF16).astype(F32), U32) & jnp.uint32(0xFFFF0000)
    hp_ref[...] = lo | hi

    logits = _mm(h2, wr_ref[...]) + br_ref[...]
    lane = lax.broadcasted_iota(I32, logits.shape, 1)
    big = jnp.int32(4 * LANES)
    in_grp = lane < N_GROUPS
    gl = jnp.where(in_grp, logits, -jnp.inf)
    gmax = jnp.max(gl, axis=-1, keepdims=True)
    grp = jnp.min(jnp.where(gl == gmax, lane, big), axis=-1, keepdims=True)
    p_grp = 1.0 / jnp.sum(jnp.where(in_grp, jnp.exp(gl - gmax), 0.0), axis=-1, keepdims=True)
    e_lo = LANE_EXP + grp * EXPERTS_PER_GROUP
    el = jnp.where((lane >= e_lo) & (lane < e_lo + EXPERTS_PER_GROUP), logits, -jnp.inf)
    v0 = jnp.max(el, axis=-1, keepdims=True)
    i0 = jnp.min(jnp.where(el == v0, lane, big), axis=-1, keepdims=True)
    el1 = jnp.where(lane == i0, -jnp.inf, el)
    v1 = jnp.max(el1, axis=-1, keepdims=True)
    i1 = jnp.min(jnp.where(el1 == v1, lane, big), axis=-1, keepdims=True)
    e1 = jnp.exp(v1 - v0)
    gate0 = p_grp / (1.0 + e1)
    gate1 = p_grp * e1 / (1.0 + e1)

    hot0 = lane == i0
    hot1 = lane == i1
    onehot = jnp.where(hot0 | hot1, 1.0, 0.0)
    ri = lax.broadcasted_iota(I32, (tm, tm), 0)
    cj = lax.broadcasted_iota(I32, (tm, tm), 1)
    prefix = _mm(jnp.where(cj < ri, 1.0, 0.0), onehot) + run_ref[...]
    rank0 = jnp.sum(jnp.where(hot0, prefix, 0.0), axis=-1, keepdims=True)
    rank1 = jnp.sum(jnp.where(hot1, prefix, 0.0), axis=-1, keepdims=True)
    run_ref[...] = run_ref[...] + jnp.sum(onehot, axis=0, keepdims=True)
    cnt_ref[...] = jnp.broadcast_to(run_ref[...], cnt_ref.shape)

    route = jnp.where(lane == R_GATE0, gate0, 0.0)
    route = jnp.where(lane == R_GATE1, gate1, route)
    route = jnp.where(lane == R_EID0, (i0 - LANE_EXP).astype(F32), route)
    route = jnp.where(lane == R_EID1, (i1 - LANE_EXP).astype(F32), route)
    route = jnp.where(lane == R_RANK0, rank0, route)
    route = jnp.where(lane == R_RANK1, rank1, route)
    route_ref[...] = route
    meta_ref[...] = route.T[0:8, :].astype(I32)


def _merge(oa, ob, zc, gates, x2, wa, wb, wc, wo, g, wr, br, *, tm=256):
    n, d = x2.shape
    assert n % tm == 0
    row = lambda i: (i, 0)
    fixed = lambda i: (0, 0)
    full = lambda a: pl.BlockSpec(a.shape, fixed)
    return pl.pallas_call(
        functools.partial(_merge_body, tm=tm),
        grid=(n // tm,),
        in_specs=[pl.BlockSpec((tm, oa.shape[1]), row), pl.BlockSpec((tm, ob.shape[1]), row),
                  pl.BlockSpec((tm, zc.shape[1]), row), pl.BlockSpec((tm, 3 * d), row),
                  pl.BlockSpec((tm, d), row),
                  full(wa), full(wb), full(wc), full(wo), full(g), full(wr), full(br)],
        out_specs=[pl.BlockSpec((tm, d), row), pl.BlockSpec((tm, d // 2), row),
                   pl.BlockSpec((tm, LANES), row), pl.BlockSpec((8, tm), lambda i: (0, i)),
                   pl.BlockSpec((8, LANES), fixed)],
        out_shape=[jax.ShapeDtypeStruct((n, d), F32), jax.ShapeDtypeStruct((n, d // 2), U32),
                   jax.ShapeDtypeStruct((n, LANES), F32), jax.ShapeDtypeStruct((8, n), I32),
                   jax.ShapeDtypeStruct((8, LANES), F32)],
        scratch_shapes=[pltpu.VMEM((1, LANES), F32)],
        compiler_params=_params("arbitrary"),
        name="merge_router",
    )(oa, ob, zc, gates, x2, wa, wb, wc, wo, g, wr, br)


def _row_copy(src_ref, src_row, dst_ref, dst_row, sem):
    return pltpu.make_async_copy(src_ref.at[pl.ds(src_row, 1)], dst_ref.at[pl.ds(dst_row, 1)], sem)


def _dispatch_body(dest_ref, hp_ref, xs_in_ref, xs_ref, sem, *, tm):
    del xs_in_ref
    base = pl.program_id(0) * (2 * tm)

    def issue(n, carry):
        _row_copy(hp_ref, n, xs_ref, dest_ref[base + n], sem.at[0]).start(priority=0)
        _row_copy(hp_ref, n, xs_ref, dest_ref[base + tm + n], sem.at[1]).start(priority=1)
        return carry

    lax.fori_loop(0, tm, issue, 0)
    for slot in range(2):
        pltpu.make_async_copy(hp_ref, xs_ref.at[pl.ds(0, tm)], sem.at[slot]).wait()


def _dispatch(dest3, hp, xs_zero, *, tm):
    n = hp.shape[0]
    grid_spec = pltpu.PrefetchScalarGridSpec(
        num_scalar_prefetch=1,
        grid=(n // tm,),
        in_specs=[pl.BlockSpec((tm, hp.shape[1]), lambda i, dest: (i, 0)), pl.BlockSpec(memory_space=pl.ANY)],
        out_specs=pl.BlockSpec(memory_space=pl.ANY),
        scratch_shapes=[pltpu.SemaphoreType.DMA((2,))],
    )
    return pl.pallas_call(
        functools.partial(_dispatch_body, tm=tm),
        grid_spec=grid_spec,
        out_shape=jax.ShapeDtypeStruct(xs_zero.shape, xs_zero.dtype),
        input_output_aliases={2: 0},
        compiler_params=pltpu.CompilerParams(dimension_semantics=("arbitrary",), has_side_effects=True),
        name="moe_dispatch",
    )(dest3.reshape(-1), hp, xs_zero)


def _expert_body(be_ref, nu_ref, xs_ref, w1_ref, w3_ref, w2_ref, ys_ref):
    del be_ref
    live = pl.program_id(0) < nu_ref[0]

    @pl.when(jnp.logical_not(live))
    def _():
        ys_ref[...] = jnp.zeros_like(ys_ref)

    @pl.when(live)
    def _():
        xp = xs_ref[...]
        half = xp.shape[-1]
        lo = pltpu.bitcast(xp << 16, F32).astype(BF16)
        hi = pltpu.bitcast(xp & jnp.uint32(0xFFFF0000), F32).astype(BF16)

        def up(w_ref):
            return (jnp.dot(lo, w_ref[0, :half, :].astype(BF16), preferred_element_type=F32)
                    + jnp.dot(hi, w_ref[0, half:, :].astype(BF16), preferred_element_type=F32))

        act = (_silu(up(w1_ref)) * up(w3_ref)).astype(BF16)
        ys_ref[...] = jnp.dot(act, w2_ref[0].astype(BF16), preferred_element_type=F32)


def _experts(blk_eid, n_used, xs, w1, w3, w2, *, rb):
    p, half = xs.shape
    d = 2 * half
    nb = p // rb
    used = lambda i, be, nu: jnp.maximum(jnp.minimum(i, nu[0] - 1), 0)
    wmap = lambda i, be, nu: (be[used(i, be, nu)], 0, 0)
    grid_spec = pltpu.PrefetchScalarGridSpec(
        num_scalar_prefetch=2,
        grid=(nb,),
        in_specs=[pl.BlockSpec((rb, half), lambda i, be, nu: (used(i, be, nu), 0)),
                  pl.BlockSpec((1, d, D_EXPERT), wmap), pl.BlockSpec((1, d, D_EXPERT), wmap),
                  pl.BlockSpec((1, D_EXPERT, d), wmap)],
        out_specs=pl.BlockSpec((rb, d), lambda i, be, nu: (i, 0)),
    )
    return pl.pallas_call(
        _expert_body,
        grid_spec=grid_spec,
        out_shape=jax.ShapeDtypeStruct((p, d), F32),
        compiler_params=_params("arbitrary"),
        name="moe_experts",
    )(blk_eid, n_used, xs, w1, w3, w2)


def _combine_body(dest_ref, ys_ref, x_ref, route_ref, g_ref, o_ref, buf_ref, sem, *, tm, final_norm):
    def issue(n, carry):
        _row_copy(ys_ref, dest_ref[0, 0, n], buf_ref.at[0], n, sem.at[0]).start(priority=0)
        _row_copy(ys_ref, dest_ref[0, 0, tm + n], buf_ref.at[1], n, sem.at[1]).start(priority=1)
        return carry

    lax.fori_loop(0, tm, issue, 0)
    for slot in range(2):
        pltpu.make_async_copy(ys_ref.at[pl.ds(0, tm)], buf_ref.at[slot], sem.at[slot]).wait()

    route = route_ref[...]
    out = (x_ref[...] + route[:, R_GATE0:R_GATE0 + 1] * buf_ref[0]
           + route[:, R_GATE1:R_GATE1 + 1] * buf_ref[1])
    if final_norm:
        out = out * lax.rsqrt(jnp.mean(out * out, axis=-1, keepdims=True) + EPS) * g_ref[...]
    o_ref[...] = out


def _combine(dest3, ys, xn, route, g, *, tm, final_norm):
    n, d = xn.shape
    row = lambda i: (i, 0)
    return pl.pallas_call(
        functools.partial(_combine_body, tm=tm, final_norm=final_norm),
        grid=(n // tm,),
        in_specs=[pl.BlockSpec((1, 1, 2 * tm), lambda i: (i, 0, 0), memory_space=pltpu.SMEM),
                  pl.BlockSpec(memory_space=pl.ANY),
                  pl.BlockSpec((tm, d), row), pl.BlockSpec((tm, LANES), row),
                  pl.BlockSpec((1, d), lambda i: (0, 0))],
        out_specs=pl.BlockSpec((tm, d), row),
        out_shape=jax.ShapeDtypeStruct((n, d), F32),
        scratch_shapes=[pltpu.VMEM((2, tm, d), F32), pltpu.SemaphoreType.DMA((2,))],
        compiler_params=_params("arbitrary"),
        name="moe_combine",
    )(dest3, ys, xn, route, g)


def _sc_row_gather(table, idx, *, chunk=64):
    rows, d = idx.shape[0], table.shape[1]
    workers = SC_CORES * SC_SUBCORES
    per_worker = rows // workers
    assert rows % workers == 0 and per_worker % chunk == 0
    mesh = plsc.VectorSubcoreMesh(core_axis_name="c", subcore_axis_name="s")

    @functools.partial(
        pl.kernel, mesh=mesh, out_type=jax.ShapeDtypeStruct((rows, d), table.dtype),
        scratch_types=[pltpu.VMEM((chunk,), I32), pltpu.VMEM((chunk, d), table.dtype), pltpu.SemaphoreType.DMA])
    def gather(table_hbm, idx_hbm, out_hbm, idx_v, rows_v, sem):
        base = (lax.axis_index("s") * SC_CORES + lax.axis_index("c")) * per_worker

        @pl.loop(0, per_worker // chunk)
        def _(step):
            off = base + step * chunk
            pltpu.sync_copy(idx_hbm.at[pl.ds(off, chunk)], idx_v)
            pltpu.async_copy(table_hbm.at[idx_v], rows_v, sem).wait()
            pltpu.sync_copy(rows_v, out_hbm.at[pl.ds(off, chunk)])

    return gather(table, idx)


def _combine_dense_body(r0_ref, r1_ref, x_ref, route_ref, g_ref, o_ref, *, final_norm):
    route = route_ref[...]
    out = (x_ref[...] + route[:, R_GATE0:R_GATE0 + 1] * r0_ref[...]
           + route[:, R_GATE1:R_GATE1 + 1] * r1_ref[...])
    if final_norm:
        out = out * lax.rsqrt(jnp.mean(out * out, axis=-1, keepdims=True) + EPS) * g_ref[...]
    o_ref[...] = out


def _combine_dense(rows2, xn, route, g, *, tm, final_norm):
    n, d = xn.shape
    nt = n // tm
    row = lambda i: (i, 0)
    return pl.pallas_call(
        functools.partial(_combine_dense_body, final_norm=final_norm),
        grid=(nt,),
        in_specs=[pl.BlockSpec((tm, d), row), pl.BlockSpec((tm, d), lambda i: (i + nt, 0)),
                  pl.BlockSpec((tm, d), row), pl.BlockSpec((tm, LANES), row),
                  pl.BlockSpec((1, d), lambda i: (0, 0))],
        out_specs=pl.BlockSpec((tm, d), row),
        out_shape=jax.ShapeDtypeStruct((n, d), F32),
        compiler_params=_params("parallel"),
        name="moe_combine_dense",
    )(rows2, rows2, xn, route, g)


def _moe(hp, route, meta, counts, xn, w1, w3, w2, g_final, *, layer, tm, rb, final_norm):
    n = hp.shape[0]
    cnt = counts[0, LANE_EXP:LANE_EXP + N_EXPERTS].astype(I32)
    nblk = (cnt + rb - 1) // rb
    bend = jnp.cumsum(nblk)
    pstart = (bend - nblk) * rb
    nb = (2 * n) // rb + N_EXPERTS
    n_used = bend[-1:].astype(I32)
    blk_eid = jnp.minimum(jnp.sum(bend[None, :] <= jnp.arange(nb, dtype=I32)[:, None], axis=1),
                          N_EXPERTS - 1).astype(I32)
    eid = meta[R_EID0:R_EID1 + 1]
    first = jnp.sum(jnp.where(eid[None] == jnp.arange(N_EXPERTS, dtype=I32)[:, None, None],
                              pstart[:, None, None], 0), axis=0)
    dest = first + meta[R_RANK0:R_RANK1 + 1]
    dest3 = dest.reshape(2, n // tm, tm).transpose(1, 0, 2).reshape(n // tm, 1, 2 * tm)

    xs = _dispatch(dest3, hp, jnp.zeros((nb * rb, hp.shape[1]), hp.dtype), tm=tm)
    ys = _experts(blk_eid + layer * N_EXPERTS, n_used, xs, w1, w3, w2, rb=rb)
    rows2 = _sc_row_gather(ys, dest.reshape(-1))
    return _combine_dense(rows2, xn, route, g_final, tm=tm, final_norm=final_norm)


def _lane_row(pairs):
    row = jnp.zeros((LANES,), F32)
    for off, vec in pairs:
        row = row.at[off:off + vec.shape[0]].set(vec.astype(F32))
    return row


def kernel(x, norm_mix, w_in, conv_qkv, dn_a_log, dn_dt_bias, dn_norm, fox_bias, conv_dw, conv_dw_b,
           conv_ln_g, conv_ln_b, w_a, w_b, w_c, w_out, norm_ffn, router_group_w, router_group_b,
           router_expert_w, router_expert_b, expert_w1, expert_w3, expert_w2, norm_final):
    b, t, d = x.shape
    n = b * t
    depth = w_in.shape[0]
    qk_dn = H_DN * DK_DN
    in_sizes = (qk_dn, qk_dn, H_DN * DV_DN, H_DN * DV_DN, H_DN, H_DN,
                H_FOX * D_FOX, H_FOX * D_FOX, H_FOX * D_FOX, H_FOX, 2 * C_CONV, d, d, d)
    splits = np.cumsum(in_sizes)[:-1].tolist()
    tm_moe, rb = 256, 256

    x2 = x.reshape(n, d)
    w1_all = expert_w1.reshape(depth * N_EXPERTS, d, D_EXPERT)
    w3_all = expert_w3.reshape(depth * N_EXPERTS, d, D_EXPERT)
    w2_all = expert_w2.reshape(depth * N_EXPERTS, D_EXPERT, d)
    for l in range(depth):
        (qa, ka, va, za, ba, aa, qb, kb, vb, fb, uc, ga, gb, gc) = jnp.split(w_in[l], splits, axis=1)
        w_main = jnp.concatenate([qa, ka, va, za, qb * (D_FOX ** -0.5 * LOG2E), kb, vb, uc, ga, gb, gc],
                                 axis=1).astype(BF16)
        w_small = jnp.concatenate([ba, aa, fb, jnp.zeros((d, LANES - 3 * H_DN), F32)], axis=1).astype(BF16)
        dn_qkv, za_p, fox_qkv, uc_p, gates, small = _in_proj(x2, norm_mix[l][None, :], w_main, w_small)

        par = jnp.stack([_lane_row([(LANE_G, -jnp.exp(dn_a_log[l]))]),
                         _lane_row([(LANE_G, dn_dt_bias[l]), (LANE_F, fox_bias[l])]),
                         _lane_row([(0, dn_norm[l])])] + [jnp.zeros((LANES,), F32)] * 5)
        oa, qaug, kaug = _delta_rule(dn_qkv.reshape(b, t, -1), za_p.reshape(b, t, -1),
                                     small.reshape(b, t, LANES), conv_qkv[l], par)
        ob = _fox_attention(fox_qkv.reshape(b, t, -1), qaug, kaug)
        vec = jnp.stack([conv_dw_b[l], conv_ln_g[l], conv_ln_b[l]] + [jnp.zeros((C_CONV,), F32)] * 5)
        zc = _conv_module(uc_p.reshape(b, t, -1), conv_dw[l], vec)

        w_r = jnp.concatenate([router_group_w[l], jnp.zeros((d, LANE_EXP - N_GROUPS), F32),
                               router_expert_w[l], jnp.zeros((d, LANES - LANE_EXP - N_EXPERTS), F32)], axis=1)
        b_r = _lane_row([(LANE_GRP, router_group_b[l]), (LANE_EXP, router_expert_b[l])])[None, :]
        xn, hp, route, meta, counts = _merge(
            oa.reshape(n, -1), ob.reshape(n, -1), zc.reshape(n, -1), gates, x2,
            w_a[l].astype(BF16), w_b[l].astype(BF16), w_c[l].astype(BF16), w_out[l].astype(BF16),
            norm_ffn[l][None, :], w_r.astype(BF16), b_r)

        x2 = _moe(hp, route, meta, counts, xn, w1_all, w3_all, w2_all, norm_final[None, :],
                  layer=l, tm=tm_moe, rb=rb, final_norm=(l == depth - 1))
    return x2.reshape(b, t, d)
```

```python
import functools

import jax
import jax.numpy as jnp
import numpy as np
from jax import lax
from jax.experimental import pallas as pl
from jax.experimental.pallas import tpu as pltpu
from jax.experimental.pallas import tpu_sc as plsc

F32 = jnp.float32
BF16 = jnp.bfloat16
U32 = jnp.uint32
I32 = jnp.int32
HIGHEST = lax.Precision.HIGHEST

EPS = 1e-6
LOG2E = 1.4426950408889634
LANES = 128
SUBLANES = 8
SC_CORES, SC_SUBCORES = 2, 16
H_DN, DK_DN, DV_DN = 4, 128, 128
SHORT_CONV = 4
CHUNK = 64
H_FOX, D_FOX = 4, 128
C_CONV = 512
CONV_WIDTH = 31
N_GROUPS, EXPERTS_PER_GROUP = 4, 8
N_EXPERTS = N_GROUPS * EXPERTS_PER_GROUP
D_EXPERT = 256

W_DN = 3 * H_DN * DK_DN
W_FOX = 3 * H_FOX * D_FOX
VMEM_LIMIT = 56 * 1024 * 1024

LANE_BETA, LANE_G, LANE_F = 0, 4, 8
LANE_GRP, LANE_EXP = 0, 32
R_GATE0, R_GATE1, R_EID0, R_EID1, R_RANK0, R_RANK1 = 0, 1, 2, 3, 4, 5

NN = (((1,), (0,)), ((), ()))
NT = (((1,), (1,)), ((), ()))
TN = (((0,), (0,)), ((), ()))


def _mm(a, b, dims=NN):
    return lax.dot_general(a.astype(BF16), b.astype(BF16), dims, preferred_element_type=F32)


def _sigmoid(x):
    return 0.5 * jnp.tanh(0.5 * x) + 0.5


def _silu(x):
    return x * _sigmoid(x)


def _params(*sem):
    return pltpu.CompilerParams(dimension_semantics=sem, vmem_limit_bytes=VMEM_LIMIT)


def _in_proj_body(x_ref, g_ref, w_ref, ws_ref, dn_ref, za_ref, fox_ref, uc_ref, gates_ref, small_ref,
                  *, col_chunk):
    x = x_ref[...]
    h = x * lax.rsqrt(jnp.mean(x * x, axis=-1, keepdims=True) + EPS) * g_ref[...]
    hb = h.astype(BF16)
    col = 0
    for ref in (dn_ref, za_ref, fox_ref, uc_ref, gates_ref):
        width = ref.shape[-1]
        for c in range(0, width, col_chunk):
            ref[:, c:c + col_chunk] = jnp.dot(
                hb, w_ref[:, col + c:col + c + col_chunk], preferred_element_type=F32).astype(ref.dtype)
        col += width
    small_ref[...] = jnp.dot(hb, ws_ref[...], preferred_element_type=F32)


def _in_proj(x2, g, w, ws, *, tm=512, col_chunk=512):
    n, d = x2.shape
    d_gate = 3 * d
    widths = (W_DN, H_DN * DV_DN, W_FOX, 2 * C_CONV, d_gate)
    assert w.shape == (d, sum(widths)) and n % tm == 0
    out_shape = [jax.ShapeDtypeStruct((n, wd), BF16) for wd in widths]
    out_shape.append(jax.ShapeDtypeStruct((n, LANES), F32))
    row = lambda i: (i, 0)
    fixed = lambda i: (0, 0)
    return pl.pallas_call(
        functools.partial(_in_proj_body, col_chunk=col_chunk),
        grid=(n // tm,),
        in_specs=[pl.BlockSpec((tm, d), row), pl.BlockSpec((1, d), fixed),
                  pl.BlockSpec(w.shape, fixed, pipeline_mode=pl.Buffered(1)),
                  pl.BlockSpec(ws.shape, fixed, pipeline_mode=pl.Buffered(1))],
        out_specs=[pl.BlockSpec((tm, wd), row) for wd in widths] + [pl.BlockSpec((tm, LANES), row)],
        out_shape=out_shape,
        compiler_params=_params("parallel"),
        name="in_proj",
    )(x2, g, w, ws)


def _softplus_parts(z):
    t = jnp.log1p(jnp.exp(-jnp.abs(z)))
    return jnp.maximum(z, 0.0) + t, -(jnp.maximum(-z, 0.0) + t)


def _delta_body(qkv_ref, za_ref, sm_ref, cw_ref, par_ref, oa_ref, qaug_ref, kaug_ref,
                xs_ref, s_ref, carry_ref, *, ts):
    j = pl.program_id(1)
    halo = SUBLANES
    pack = 2 * SUBLANES

    @pl.when(j == 0)
    def _():
        xs_ref[0:halo, :] = jnp.zeros((halo, W_DN), F32)
        s_ref[...] = jnp.zeros_like(s_ref)
        carry_ref[...] = jnp.zeros_like(carry_ref)

    @pl.when(j > 0)
    def _():
        xs_ref[0:halo, :] = xs_ref[2 * halo:3 * halo, :]

    xb = qkv_ref[0]
    xs_ref[halo:2 * halo, :] = qkv_ref[0, 0:pack, :].astype(F32)[0:halo]
    xs_ref[2 * halo:3 * halo, :] = qkv_ref[0, ts - pack:ts, :].astype(F32)[pack - halo:pack]

    lag = lax.broadcasted_iota(I32, (ts, ts), 0) - lax.broadcasted_iota(I32, (ts, ts), 1)
    shifted = [jnp.dot(jnp.where(lag == s, 1.0, 0.0).astype(BF16), xb, preferred_element_type=F32)
               for s in range(1, SHORT_CONV)]
    head_row = lax.broadcasted_iota(I32, (halo, LANES), 0)

    def conv_silu(lane0):
        lanes = slice(lane0, lane0 + LANES)
        acc = cw_ref[SHORT_CONV - 1:SHORT_CONV, lanes] * xb[:, lanes].astype(F32)
        for s in range(1, SHORT_CONV):
            acc = acc + cw_ref[SHORT_CONV - 1 - s:SHORT_CONV - s, lanes] * shifted[s - 1][:, lanes]
        head = acc[0:halo]
        for s in range(1, SHORT_CONV):
            prev = jnp.where(head_row < s, xs_ref[halo - s:2 * halo - s, lanes], 0.0)
            head = head + cw_ref[SHORT_CONV - 1 - s:SHORT_CONV - s, lanes] * prev
        return _silu(jnp.concatenate([head, acc[halo:]], axis=0))

    def l2n(a):
        return a * lax.rsqrt(jnp.sum(a * a, axis=-1, keepdims=True) + EPS)

    sm = sm_ref[0]
    lane = lax.broadcasted_iota(I32, sm.shape, 1)
    sp, logsig = _softplus_parts(sm + par_ref[1:2, :])
    vals = jnp.where(lane < LANE_G, _sigmoid(sm),
                     jnp.where(lane < LANE_F, par_ref[0:1, :] * sp,
                               jnp.where(lane < LANE_F + H_FOX, logsig, 0.0)))
    row = lax.broadcasted_iota(I32, (ts, ts), 0)
    colm = lax.broadcasted_iota(I32, (ts, ts), 1)
    log_chunk = CHUNK.bit_length() - 1
    causal = (row >= colm) & ((row >> log_chunk) == (colm >> log_chunk))

    hi = vals.astype(BF16)
    rem = vals - hi.astype(F32)
    mid = rem.astype(BF16)
    lo = (rem - mid.astype(F32)).astype(BF16)
    pieces = jnp.concatenate([hi, mid, lo], axis=-1)

    def cumsum(mask):
        y = jnp.dot(jnp.where(mask, 1.0, 0.0).astype(BF16), pieces, preferred_element_type=F32)
        return (y[:, :LANES] + y[:, LANES:2 * LANES]) + y[:, 2 * LANES:]

    ccum = cumsum(row >= colm) + carry_ref[...]
    gcum = cumsum(causal)
    carry_ref[...] = ccum[ts - 1:ts, :]
    gcum_t = gcum.T

    cl = ccum * LOG2E
    c_hi = cl.astype(BF16)
    c_rem = cl - c_hi.astype(F32)
    c_mid = c_rem.astype(BF16)
    c_lo = (c_rem - c_mid.astype(F32)).astype(BF16)
    c_pieces = jnp.concatenate([c_hi, c_mid, c_lo], axis=-1)
    pr = lax.broadcasted_iota(I32, (3 * LANES, LANES), 0)
    pc = lax.broadcasted_iota(I32, (3 * LANES, LANES), 1)
    src_lane, piece = pr & (LANES - 1), pr >> (LANES.bit_length() - 1)
    owned = (src_lane >= LANE_F) & (src_lane < LANE_F + H_FOX) & ((pc >> 3) == src_lane - LANE_F)
    place_q = jnp.where(owned & ((pc & 7) == piece), 1.0, 0.0).astype(BF16)
    place_k = jnp.where(owned & ((pc & 7) == piece + 3), -1.0, 0.0).astype(BF16)
    slot = lane & 7
    in_heads = lane < 8 * H_FOX
    ones_q = jnp.where(in_heads & (slot >= 3) & (slot < 6), 1.0, 0.0)
    ones_k = jnp.where(in_heads & (slot < 3), 1.0, 0.0)
    qaug_ref[0] = (jnp.dot(c_pieces, place_q, preferred_element_type=F32) + ones_q).astype(BF16)
    kaug_ref[0] = (jnp.dot(c_pieces, place_k, preferred_element_type=F32) + ones_k).astype(BF16)

    scale = DK_DN ** -0.5
    dn_norm = par_ref[2:3, :]

    pw = 2 * CHUNK
    prow = lax.broadcasted_iota(I32, (pw, pw), 0)
    pcol = lax.broadcasted_iota(I32, (pw, pw), 1)
    same = (prow >> log_chunk) == (pcol >> log_chunk)
    causal_p = (prow >= pcol) & same
    strict_p = (prow > pcol) & same
    levels = []
    s = 1
    while s < CHUNK:
        levels.append(((prow >> s.bit_length()) == (pcol >> s.bit_length()))
                      & ((prow & s) != 0) & ((pcol & s) == 0))
        s *= 2

    heads = range(H_DN)
    pairs = range(ts // pw)
    q, k, v = [], [], []
    for h in heads:
        q.append(l2n(conv_silu(h * DK_DN)))
        k.append(l2n(conv_silu(H_DN * DK_DN + h * DK_DN)))
        v.append(conv_silu(2 * H_DN * DK_DN + h * DV_DN))

    ctx = []
    for h in heads:
        for p in pairs:
            pr = slice(p * pw, (p + 1) * pw)
            gcol = gcum[pr, LANE_G + h:LANE_G + h + 1]
            grow = gcum_t[LANE_G + h:LANE_G + h + 1, pr]
            beta = vals[pr, LANE_BETA + h:LANE_BETA + h + 1]
            decay = jnp.where(causal_p, jnp.exp(jnp.where(causal_p, gcol - grow, 0.0)), 0.0)
            egc = jnp.exp(gcol)
            kp = k[h][pr]
            kb = kp * beta
            ctx.append(dict(
                h=h, p=p, gcol=gcol, kp=kp,
                a=jnp.where(strict_p, _mm(kb, kp, NT) * decay, 0.0),
                qk=jnp.where(causal_p, _mm(q[h][pr] * scale, kp, NT) * decay, 0.0),
                rhs=jnp.concatenate([v[h][pr] * beta, kb * egc], axis=-1),
                qg=q[h][pr] * (scale * egc)))

    for c in ctx:
        c["n"] = -jnp.where(levels[0], c["a"], 0.0)
    for level in levels[1:]:
        for c in ctx:
            m = jnp.where(level, c["a"], 0.0)
            c["y"] = m + _mm(c["n"], m)
        for c in ctx:
            c["n"] = c["n"] - (c["y"] + _mm(c["y"], c["n"]))
    for c in ctx:
        c["sol"] = c["rhs"] + _mm(c["n"], c["rhs"])

    state = [s_ref[h] for h in heads]
    for p in pairs:
        group = [c for c in ctx if c["p"] == p]
        v_prev = [None] * H_DN
        for ch in range(2):
            rows = slice(ch * CHUNK, (ch + 1) * CHUNK)
            out_rows = slice(p * pw + ch * CHUNK, p * pw + (ch + 1) * CHUNK)
            v_new = [c["sol"][rows, :DV_DN] - _mm(c["sol"][rows, DV_DN:], state[c["h"]]) for c in group]
            for c, vn in zip(group, v_new):
                h = c["h"]
                v_pair = jnp.concatenate([vn, jnp.zeros_like(vn)] if ch == 0 else [v_prev[h], vn], axis=0)
                o = _mm(c["qg"][rows], state[h]) + _mm(c["qk"][rows], v_pair)
                glast = c["gcol"][(ch + 1) * CHUNK - 1:(ch + 1) * CHUNK]
                state[h] = (state[h] * jnp.exp(glast)
                            + _mm(c["kp"][rows] * jnp.exp(glast - c["gcol"][rows]), vn, TN))
                v_prev[h] = vn
                o = o * lax.rsqrt(jnp.mean(o * o, axis=-1, keepdims=True) + EPS) * dn_norm
                za = za_ref[0, out_rows, h * DV_DN:(h + 1) * DV_DN].astype(F32)
                oa_ref[0, out_rows, h * DV_DN:(h + 1) * DV_DN] = (o * _silu(za)).astype(oa_ref.dtype)
    for h in heads:
        s_ref[h] = state[h]


def _delta_rule(qkv, za, small, conv_w, par, *, ts=256):
    b, t, _ = qkv.shape
    assert t % ts == 0 and ts % (2 * CHUNK) == 0
    blk = lambda width: pl.BlockSpec((1, ts, width), lambda bi, j: (bi, j, 0))
    fixed = lambda bi, j: (0, 0)
    return pl.pallas_call(
        functools.partial(_delta_body, ts=ts),
        grid=(b, t // ts),
        in_specs=[blk(W_DN), blk(H_DN * DV_DN), blk(LANES),
                  pl.BlockSpec(conv_w.shape, fixed), pl.BlockSpec(par.shape, fixed)],
        out_specs=[blk(H_DN * DV_DN), blk(LANES), blk(LANES)],
        out_shape=[jax.ShapeDtypeStruct((b, t, H_DN * DV_DN), BF16),
                   jax.ShapeDtypeStruct((b, t, LANES), BF16),
                   jax.ShapeDtypeStruct((b, t, LANES), BF16)],
        scratch_shapes=[pltpu.VMEM((3 * SUBLANES, W_DN), F32),
                        pltpu.VMEM((H_DN, DK_DN, DV_DN), F32),
                        pltpu.VMEM((1, LANES), F32)],
        compiler_params=_params("parallel", "arbitrary"),
        name="delta_rule",
    )(qkv, za, small, conv_w, par)


def _fox_body(q_ref, k_ref, v_ref, qa_ref, ka_ref, o_ref, m_ref, acc_ref, *, tq, tk):
    i = pl.program_id(1)
    m_ref[...] = jnp.full(m_ref.shape, -jnp.inf, F32)
    acc_ref[...] = jnp.zeros_like(acc_ref)
    head_lanes = [slice(h * D_FOX, (h + 1) * D_FOX) for h in range(H_FOX)]
    lane = lax.broadcasted_iota(I32, (tk, LANES), 1)
    own = [jnp.where((lane >> 3) == h, 1.0, 0.0).astype(BF16) for h in range(H_FOX)]
    ones = jnp.ones((tk, D_FOX), BF16)
    keep = lax.broadcasted_iota(I32, (tq, tk), 1) <= lax.broadcasted_iota(I32, (tq, tk), 0)

    def block(start, diag_offset):
        rows = slice(0 if diag_offset is None else diag_offset, tq)
        ka = ka_ref[0, pl.ds(start, tk), :]
        for h, lanes in enumerate(head_lanes):
            q_aug = jnp.concatenate([q_ref[0, rows, lanes], qa_ref[0, rows, :]], axis=1)
            k_aug = jnp.concatenate([k_ref[0, pl.ds(start, tk), lanes], ka * own[h]], axis=1)
            s = lax.dot_general(q_aug, k_aug, NT, preferred_element_type=F32)
            if diag_offset is not None:
                s = jnp.where(keep[:tq - diag_offset], s, -jnp.inf)
            m_prev = m_ref[h, rows]
            m_next = jnp.maximum(m_prev, jnp.max(s, axis=-1, keepdims=True))
            p = jnp.exp2(s - jnp.concatenate([m_next] * (tk // LANES), axis=1))
            alpha = jnp.exp2(m_prev - m_next)
            v_aug = jnp.concatenate([v_ref[0, pl.ds(start, tk), lanes], ones], axis=1)
            acc_ref[h, rows] = (jnp.concatenate([alpha, alpha], axis=1) * acc_ref[h, rows]
                                + jnp.dot(p.astype(BF16), v_aug, preferred_element_type=F32))
            m_ref[h, rows] = m_next

    def full_block(jb, carry):
        block(pl.multiple_of(jb * tk, tk), None)
        return carry

    lax.fori_loop(0, i * (tq // tk), full_block, 0)
    for d in range(tq // tk):
        block(pl.multiple_of(i * tq + d * tk, tk), d * tk)
    for h, lanes in enumerate(head_lanes):
        acc = acc_ref[h]
        o_ref[0, :, lanes] = (acc[:, :D_FOX] / acc[:, D_FOX:]).astype(o_ref.dtype)


def _fox_attention(qkv, qaug, kaug, *, tq=1024, tk=256):
    b, t, _ = qkv.shape
    hd = H_FOX * D_FOX
    assert t % tq == 0 and tq % tk == 0 and tk % LANES == 0
    return pl.pallas_call(
        functools.partial(_fox_body, tq=tq, tk=tk),
        grid=(b, t // tq),
        in_specs=[pl.BlockSpec((1, tq, hd), lambda bi, i: (bi, i, 0)),
                  pl.BlockSpec((1, t, hd), lambda bi, i: (bi, 0, 1)),
                  pl.BlockSpec((1, t, hd), lambda bi, i: (bi, 0, 2)),
                  pl.BlockSpec((1, tq, LANES), lambda bi, i: (bi, i, 0)),
                  pl.BlockSpec((1, t, LANES), lambda bi, i: (bi, 0, 0))],
        out_specs=pl.BlockSpec((1, tq, hd), lambda bi, i: (bi, i, 0)),
        out_shape=jax.ShapeDtypeStruct((b, t, hd), BF16),
        scratch_shapes=[pltpu.VMEM((H_FOX, tq, LANES), F32), pltpu.VMEM((H_FOX, tq, 2 * D_FOX), F32)],
        compiler_params=_params("parallel", "arbitrary"),
        name="fox_attention",
    )(qkv, qkv, qkv, qaug, kaug)


def _conv_body(cur_ref, halo_ref, dw_ref, vec_ref, o_ref, z_ref, zs_ref, y_ref, *, tt, hrows, rchunk):
    i = pl.program_id(1)
    cur = cur_ref[0].astype(F32)
    z_ref[hrows:hrows + tt, :] = cur[:, :C_CONV] * _sigmoid(cur[:, C_CONV:])
    hal = halo_ref[0].astype(F32)
    zh = hal[:, :C_CONV] * _sigmoid(hal[:, C_CONV:])
    z_ref[0:hrows, :] = jnp.where(i > 0, zh, 0.0)

    sub = SUBLANES
    span = tt + hrows - sub
    for phase in range(1, sub):
        zs_ref[phase - 1, 0:span, :] = z_ref[phase:phase + span, :]

    for lg in range(C_CONV // LANES):
        lanes = slice(lg * LANES, (lg + 1) * LANES)
        for r0 in range(0, tt, rchunk):
            acc = jnp.broadcast_to(vec_ref[0:1, lanes], (rchunk, LANES))
            for k in range(CONV_WIDTH):
                off = hrows - (CONV_WIDTH - 1) + k
                phase, base = off % sub, off - off % sub + r0
                if phase == 0:
                    tap = z_ref[base:base + rchunk, lanes]
                else:
                    tap = zs_ref[phase - 1, base:base + rchunk, lanes]
                acc = acc + dw_ref[k:k + 1, lanes] * tap
            y_ref[r0:r0 + rchunk, lanes] = acc

    y = y_ref[...]
    mu = jnp.mean(y, axis=-1, keepdims=True)
    yc = y - mu
    var = jnp.mean(yc * yc, axis=-1, keepdims=True)
    yn = yc * lax.rsqrt(var + EPS) * vec_ref[1:2, :] + vec_ref[2:3, :]
    o_ref[0] = _silu(yn).astype(o_ref.dtype)


def _conv_module(uc, dw, vec, *, tt=256, hrows=32, rchunk=64):
    b, t, _ = uc.shape
    assert t % tt == 0 and tt % hrows == 0 and hrows >= CONV_WIDTH - 1
    ratio = tt // hrows
    return pl.pallas_call(
        functools.partial(_conv_body, tt=tt, hrows=hrows, rchunk=rchunk),
        grid=(b, t // tt),
        in_specs=[pl.BlockSpec((1, tt, 2 * C_CONV), lambda bi, i: (bi, i, 0)),
                  pl.BlockSpec((1, hrows, 2 * C_CONV), lambda bi, i: (bi, jnp.maximum(i * ratio - 1, 0), 0)),
                  pl.BlockSpec(dw.shape, lambda bi, i: (0, 0)),
                  pl.BlockSpec(vec.shape, lambda bi, i: (0, 0))],
        out_specs=pl.BlockSpec((1, tt, C_CONV), lambda bi, i: (bi, i, 0)),
        out_shape=jax.ShapeDtypeStruct((b, t, C_CONV), BF16),
        scratch_shapes=[pltpu.VMEM((tt + hrows, C_CONV), F32), pltpu.VMEM((7, tt + hrows, C_CONV), F32),
                        pltpu.VMEM((tt, C_CONV), F32)],
        compiler_params=_params("parallel", "parallel"),
        name="conv_module",
    )(uc, uc, dw, vec)


def _merge_body(oa_ref, ob_ref, zc_ref, gates_ref, x_ref, wa_ref, wb_ref, wc_ref, wo_ref, g_ref,
                wr_ref, br_ref, xn_ref, hp_ref, route_ref, meta_ref, cnt_ref, run_ref, *, tm):
    step = pl.program_id(0)
    d = x_ref.shape[-1]

    @pl.when(step == 0)
    def _():
        run_ref[...] = jnp.zeros_like(run_ref)

    merged = None
    for idx, (m_ref, w_ref) in enumerate(((oa_ref, wa_ref), (ob_ref, wb_ref), (zc_ref, wc_ref))):
        y = jnp.dot(m_ref[...], w_ref[...], preferred_element_type=F32)
        term = _sigmoid(gates_ref[:, idx * d:(idx + 1) * d].astype(F32)) * y
        merged = term if merged is None else merged + term
    xn = x_ref[...] + jnp.dot(merged.astype(BF16), wo_ref[...], preferred_element_type=F32)
    xn_ref[...] = xn
    h2 = xn * lax.rsqrt(jnp.mean(xn * xn, axis=-1, keepdims=True) + EPS) * g_ref[...]

    hp_ref[...] = _pack_bf16_pairs(h2)

    logits = _mm(h2, wr_ref[...]) + br_ref[...]
    lane = lax.broadcasted_iota(I32, logits.shape, 1)
    big = jnp.int32(4 * LANES)
    in_grp = lane < N_GROUPS
    gl = jnp.where(in_grp, logits, -jnp.inf)
    gmax = jnp.max(gl, axis=-1, keepdims=True)
    grp = jnp.min(jnp.where(gl == gmax, lane, big), axis=-1, keepdims=True)
    p_grp = 1.0 / jnp.sum(jnp.where(in_grp, jnp.exp(gl - gmax), 0.0), axis=-1, keepdims=True)
    e_lo = LANE_EXP + grp * EXPERTS_PER_GROUP
    el = jnp.where((lane >= e_lo) & (lane < e_lo + EXPERTS_PER_GROUP), logits, -jnp.inf)
    v0 = jnp.max(el, axis=-1, keepdims=True)
    i0 = jnp.min(jnp.where(el == v0, lane, big), axis=-1, keepdims=True)
    el1 = jnp.where(lane == i0, -jnp.inf, el)
    v1 = jnp.max(el1, axis=-1, keepdims=True)
    i1 = jnp.min(jnp.where(el1 == v1, lane, big), axis=-1, keepdims=True)
    e1 = jnp.exp(v1 - v0)
    gate0 = p_grp / (1.0 + e1)
    gate1 = p_grp * e1 / (1.0 + e1)

    hot0 = lane == i0
    hot1 = lane == i1
    onehot = jnp.where(hot0 | hot1, 1.0, 0.0)
    ri = lax.broadcasted_iota(I32, (tm, tm), 0)
    cj = lax.broadcasted_iota(I32, (tm, tm), 1)
    prefix = _mm(jnp.where(cj < ri, 1.0, 0.0), onehot) + run_ref[...]
    rank0 = jnp.sum(jnp.where(hot0, prefix, 0.0), axis=-1, keepdims=True)
    rank1 = jnp.sum(jnp.where(hot1, prefix, 0.0), axis=-1, keepdims=True)
    run_ref[...] = run_ref[...] + jnp.sum(onehot, axis=0, keepdims=True)
    cnt_ref[...] = jnp.broadcast_to(run_ref[...], cnt_ref.shape)

    route = jnp.where(lane == R_GATE0, gate0, 0.0)
    route = jnp.where(lane == R_GATE1, gate1, route)
    route = jnp.where(lane == R_EID0, (i0 - LANE_EXP).astype(F32), route)
    route = jnp.where(lane == R_EID1, (i1 - LANE_EXP).astype(F32), route)
    route = jnp.where(lane == R_RANK0, rank0, route)
    route = jnp.where(lane == R_RANK1, rank1, route)
    route_ref[...] = route
    meta_ref[...] = route.T[0:8, :].astype(I32)


def _merge(oa, ob, zc, gates, x2, wa, wb, wc, wo, g, wr, br, *, tm=256):
    n, d = x2.shape
    assert n % tm == 0
    row = lambda i: (i, 0)
    fixed = lambda i: (0, 0)
    full = lambda a: pl.BlockSpec(a.shape, fixed)
    return pl.pallas_call(
        functools.partial(_merge_body, tm=tm),
        grid=(n // tm,),
        in_specs=[pl.BlockSpec((tm, oa.shape[1]), row), pl.BlockSpec((tm, ob.shape[1]), row),
                  pl.BlockSpec((tm, zc.shape[1]), row), pl.BlockSpec((tm, 3 * d), row),
                  pl.BlockSpec((tm, d), row),
                  full(wa), full(wb), full(wc), full(wo), full(g), full(wr), full(br)],
        out_specs=[pl.BlockSpec((tm, d), row), pl.BlockSpec((tm, d // 2), row),
                   pl.BlockSpec((tm, LANES), row), pl.BlockSpec((8, tm), lambda i: (0, i)),
                   pl.BlockSpec((8, LANES), fixed)],
        out_shape=[jax.ShapeDtypeStruct((n, d), F32), jax.ShapeDtypeStruct((n, d // 2), U32),
                   jax.ShapeDtypeStruct((n, LANES), F32), jax.ShapeDtypeStruct((8, n), I32),
                   jax.ShapeDtypeStruct((8, LANES), F32)],
        scratch_shapes=[pltpu.VMEM((1, LANES), F32)],
        compiler_params=_params("arbitrary"),
        name="merge_router",
    )(oa, ob, zc, gates, x2, wa, wb, wc, wo, g, wr, br)


def _sc_mesh():
    return plsc.VectorSubcoreMesh(core_axis_name="c", subcore_axis_name="s")


def _sc_worker_base(per_worker):
    return (lax.axis_index("s") * SC_CORES + lax.axis_index("c")) * per_worker


def _sc_row_scatter(src, idx, out_rows, *, chunk=128):
    n, d = src.shape
    per_worker = n // (SC_CORES * SC_SUBCORES)
    assert idx.shape == (2 * n,) and n % (SC_CORES * SC_SUBCORES) == 0 and per_worker % chunk == 0

    @functools.partial(
        pl.kernel, mesh=_sc_mesh(), out_type=jax.ShapeDtypeStruct((out_rows, d), src.dtype),
        scratch_types=[pltpu.VMEM((chunk,), I32), pltpu.VMEM((chunk, d), src.dtype), pltpu.SemaphoreType.DMA])
    def scatter(src_hbm, idx_hbm, out_hbm, idx_v, rows_v, sem):
        base = _sc_worker_base(per_worker)

        @pl.loop(0, per_worker // chunk)
        def _(step):
            off = base + step * chunk
            pltpu.sync_copy(src_hbm.at[pl.ds(off, chunk)], rows_v)
            for slot in range(2):
                pltpu.sync_copy(idx_hbm.at[pl.ds(slot * n + off, chunk)], idx_v)
                pltpu.async_copy(rows_v, out_hbm.at[idx_v], sem).wait()

    return scatter(src, idx)


def _pack_bf16_pairs(x):
    half = x.shape[-1] // 2
    lo = pltpu.bitcast(x[:, :half].astype(BF16).astype(F32), U32) >> 16
    hi = pltpu.bitcast(x[:, half:].astype(BF16).astype(F32), U32) & jnp.uint32(0xFFFF0000)
    return lo | hi


def _unpack_bf16_pairs(xp):
    return pltpu.bitcast(xp << 16, F32), pltpu.bitcast(xp & jnp.uint32(0xFFFF0000), F32)


def _expert_body(be_ref, valid_ref, nu_ref, xs_ref, w1_ref, w3_ref, w2_ref, ys_ref):
    del be_ref
    step = pl.program_id(0)
    live = step < nu_ref[0]

    @pl.when(jnp.logical_not(live))
    def _():
        ys_ref[...] = jnp.zeros_like(ys_ref)

    @pl.when(live)
    def _():
        row = lax.broadcasted_iota(I32, xs_ref.shape, 0)
        xp = jnp.where(row < valid_ref[step], xs_ref[...], jnp.uint32(0))
        half = xp.shape[-1]
        lo, hi = (part.astype(BF16) for part in _unpack_bf16_pairs(xp))

        def up(w_ref):
            return (jnp.dot(lo, w_ref[0, :half, :].astype(BF16), preferred_element_type=F32)
                    + jnp.dot(hi, w_ref[0, half:, :].astype(BF16), preferred_element_type=F32))

        act = (_silu(up(w1_ref)) * up(w3_ref)).astype(BF16)
        ys_ref[...] = _pack_bf16_pairs(jnp.dot(act, w2_ref[0].astype(BF16), preferred_element_type=F32))


def _experts(blk_eid, blk_valid, n_used, xs, w1, w3, w2, *, rb):
    p, half = xs.shape
    d = 2 * half
    nb = p // rb
    used = lambda i, be, bv, nu: jnp.maximum(jnp.minimum(i, nu[0] - 1), 0)
    wmap = lambda i, be, bv, nu: (be[used(i, be, bv, nu)], 0, 0)
    grid_spec = pltpu.PrefetchScalarGridSpec(
        num_scalar_prefetch=3,
        grid=(nb,),
        in_specs=[pl.BlockSpec((rb, half), lambda i, be, bv, nu: (used(i, be, bv, nu), 0)),
                  pl.BlockSpec((1, d, D_EXPERT), wmap), pl.BlockSpec((1, d, D_EXPERT), wmap),
                  pl.BlockSpec((1, D_EXPERT, d), wmap)],
        out_specs=pl.BlockSpec((rb, half), lambda i, be, bv, nu: (i, 0)),
    )
    return pl.pallas_call(
        _expert_body,
        grid_spec=grid_spec,
        out_shape=jax.ShapeDtypeStruct((p, half), U32),
        compiler_params=_params("arbitrary"),
        name="moe_experts",
    )(blk_eid, blk_valid, n_used, xs, w1, w3, w2)


def _sc_row_gather(table, idx, *, chunk=128):
    rows, d = idx.shape[0], table.shape[1]
    per_worker = rows // (SC_CORES * SC_SUBCORES)
    assert rows % (SC_CORES * SC_SUBCORES) == 0 and per_worker % chunk == 0

    @functools.partial(
        pl.kernel, mesh=_sc_mesh(), out_type=jax.ShapeDtypeStruct((rows, d), table.dtype),
        scratch_types=[pltpu.VMEM((chunk,), I32), pltpu.VMEM((chunk, d), table.dtype), pltpu.SemaphoreType.DMA])
    def gather(table_hbm, idx_hbm, out_hbm, idx_v, rows_v, sem):
        base = _sc_worker_base(per_worker)

        @pl.loop(0, per_worker // chunk)
        def _(step):
            off = base + step * chunk
            pltpu.sync_copy(idx_hbm.at[pl.ds(off, chunk)], idx_v)
            pltpu.async_copy(table_hbm.at[idx_v], rows_v, sem).wait()
            pltpu.sync_copy(rows_v, out_hbm.at[pl.ds(off, chunk)])

    return gather(table, idx)


def _combine_body(r0_ref, r1_ref, x_ref, route_ref, g_ref, o_ref, *, final_norm):
    route = route_ref[...]
    half = r0_ref.shape[-1]
    lo0, hi0 = _unpack_bf16_pairs(r0_ref[...])
    lo1, hi1 = _unpack_bf16_pairs(r1_ref[...])
    g0, g1 = route[:, R_GATE0:R_GATE0 + 1], route[:, R_GATE1:R_GATE1 + 1]
    out = jnp.concatenate([x_ref[:, :half] + g0 * lo0 + g1 * lo1,
                           x_ref[:, half:] + g0 * hi0 + g1 * hi1], axis=1)
    if final_norm:
        out = out * lax.rsqrt(jnp.mean(out * out, axis=-1, keepdims=True) + EPS) * g_ref[...]
    o_ref[...] = out


def _combine(rows2, xn, route, g, *, tm, final_norm):
    n, d = xn.shape
    nt = n // tm
    row = lambda i: (i, 0)
    return pl.pallas_call(
        functools.partial(_combine_body, final_norm=final_norm),
        grid=(nt,),
        in_specs=[pl.BlockSpec((tm, d // 2), row), pl.BlockSpec((tm, d // 2), lambda i: (i + nt, 0)),
                  pl.BlockSpec((tm, d), row), pl.BlockSpec((tm, LANES), row),
                  pl.BlockSpec((1, d), lambda i: (0, 0))],
        out_specs=pl.BlockSpec((tm, d), row),
        out_shape=jax.ShapeDtypeStruct((n, d), F32),
        compiler_params=_params("parallel"),
        name="moe_combine",
    )(rows2, rows2, xn, route, g)


def _moe(hp, route, meta, counts, xn, w1, w3, w2, g_final, *, layer, tm, rb, final_norm):
    n = hp.shape[0]
    cnt = counts[0, LANE_EXP:LANE_EXP + N_EXPERTS].astype(I32)
    nblk = (cnt + rb - 1) // rb
    bend = jnp.cumsum(nblk)
    pstart = (bend - nblk) * rb
    nb = (2 * n) // rb + N_EXPERTS
    n_used = bend[-1:].astype(I32)
    blk = jnp.arange(nb, dtype=I32)
    blk_eid = jnp.minimum(jnp.sum(bend[None, :] <= blk[:, None], axis=1), N_EXPERTS - 1).astype(I32)
    experts = jnp.arange(N_EXPERTS, dtype=I32)
    mine = blk_eid[:, None] == experts[None, :]
    blk_valid = jnp.clip(jnp.sum(jnp.where(mine, (cnt + pstart)[None, :], 0), axis=1) - blk * rb, 0, rb)
    eid = meta[R_EID0:R_EID1 + 1]
    first = jnp.sum(jnp.where(eid[None] == experts[:, None, None], pstart[:, None, None], 0), axis=0)
    dest = (first + meta[R_RANK0:R_RANK1 + 1]).reshape(-1)

    xs = _sc_row_scatter(hp, dest, nb * rb)
    ys = _experts(blk_eid + layer * N_EXPERTS, blk_valid.astype(I32), n_used, xs, w1, w3, w2, rb=rb)
    return _combine(_sc_row_gather(ys, dest), xn, route, g_final, tm=tm, final_norm=final_norm)


def _lane_row(pairs):
    row = jnp.zeros((LANES,), F32)
    for off, vec in pairs:
        row = row.at[off:off + vec.shape[0]].set(vec.astype(F32))
    return row


def kernel(x, norm_mix, w_in, conv_qkv, dn_a_log, dn_dt_bias, dn_norm, fox_bias, conv_dw, conv_dw_b,
           conv_ln_g, conv_ln_b, w_a, w_b, w_c, w_out, norm_ffn, router_group_w, router_group_b,
           router_expert_w, router_expert_b, expert_w1, expert_w3, expert_w2, norm_final):
    b, t, d = x.shape
    n = b * t
    depth = w_in.shape[0]
    qk_dn = H_DN * DK_DN
    in_sizes = (qk_dn, qk_dn, H_DN * DV_DN, H_DN * DV_DN, H_DN, H_DN,
                H_FOX * D_FOX, H_FOX * D_FOX, H_FOX * D_FOX, H_FOX, 2 * C_CONV, d, d, d)
    splits = np.cumsum(in_sizes)[:-1].tolist()
    tm_moe, rb = 256, 256

    x2 = x.reshape(n, d)
    w1_all = expert_w1.reshape(depth * N_EXPERTS, d, D_EXPERT)
    w3_all = expert_w3.reshape(depth * N_EXPERTS, d, D_EXPERT)
    w2_all = expert_w2.reshape(depth * N_EXPERTS, D_EXPERT, d)
    for l in range(depth):
        (qa, ka, va, za, ba, aa, qb, kb, vb, fb, uc, ga, gb, gc) = jnp.split(w_in[l], splits, axis=1)
        w_main = jnp.concatenate([qa, ka, va, za, qb * (D_FOX ** -0.5 * LOG2E), kb, vb, uc, ga, gb, gc],
                                 axis=1).astype(BF16)
        w_small = jnp.concatenate([ba, aa, fb, jnp.zeros((d, LANES - 3 * H_DN), F32)], axis=1).astype(BF16)
        dn_qkv, za_p, fox_qkv, uc_p, gates, small = _in_proj(x2, norm_mix[l][None, :], w_main, w_small)

        par = jnp.stack([_lane_row([(LANE_G, -jnp.exp(dn_a_log[l]))]),
                         _lane_row([(LANE_G, dn_dt_bias[l]), (LANE_F, fox_bias[l])]),
                         _lane_row([(0, dn_norm[l])])] + [jnp.zeros((LANES,), F32)] * 5)
        oa, qaug, kaug = _delta_rule(dn_qkv.reshape(b, t, -1), za_p.reshape(b, t, -1),
                                     small.reshape(b, t, LANES), conv_qkv[l], par)
        ob = _fox_attention(fox_qkv.reshape(b, t, -1), qaug, kaug)
        vec = jnp.stack([conv_dw_b[l], conv_ln_g[l], conv_ln_b[l]] + [jnp.zeros((C_CONV,), F32)] * 5)
        zc = _conv_module(uc_p.reshape(b, t, -1), conv_dw[l], vec)

        w_r = jnp.concatenate([router_group_w[l], jnp.zeros((d, LANE_EXP - N_GROUPS), F32),
                               router_expert_w[l], jnp.zeros((d, LANES - LANE_EXP - N_EXPERTS), F32)], axis=1)
        b_r = _lane_row([(LANE_GRP, router_group_b[l]), (LANE_EXP, router_expert_b[l])])[None, :]
        xn, hp, route, meta, counts = _merge(
            oa.reshape(n, -1), ob.reshape(n, -1), zc.reshape(n, -1), gates, x2,
            w_a[l].astype(BF16), w_b[l].astype(BF16), w_c[l].astype(BF16), w_out[l].astype(BF16),
            norm_ffn[l][None, :], w_r.astype(BF16), b_r)

        x2 = _moe(hp, route, meta, counts, xn, w1_all, w3_all, w2_all, norm_final[None, :],
                  layer=l, tm=tm_moe, rb=rb, final_norm=(l == depth - 1))
    return x2.reshape(b, t, d)
```

```python
import functools

import jax
import jax.numpy as jnp
import numpy as np
from jax import lax
from jax.experimental import pallas as pl
from jax.experimental.pallas import tpu as pltpu
from jax.experimental.pallas import tpu_sc as plsc

F32 = jnp.float32
BF16 = jnp.bfloat16
U32 = jnp.uint32
I32 = jnp.int32

EPS = 1e-6
LOG2E = 1.4426950408889634
LANES = 128
SUBLANES = 8
SC_CORES, SC_SUBCORES = 2, 16
H_DN, DK_DN, DV_DN = 4, 128, 128
SHORT_CONV = 4
CHUNK = 64
H_FOX, D_FOX = 4, 128
C_CONV = 512
CONV_WIDTH = 31
N_GROUPS, EXPERTS_PER_GROUP = 4, 8
N_EXPERTS = N_GROUPS * EXPERTS_PER_GROUP
D_EXPERT = 256

W_DN = 3 * H_DN * DK_DN
W_FOX = 3 * H_FOX * D_FOX
VMEM_LIMIT = 56 * 1024 * 1024

LANE_BETA, LANE_G, LANE_F = 0, 4, 8
LANE_GRP, LANE_EXP = 0, 32
R_GATE0, R_GATE1, R_EID0, R_EID1, R_RANK0, R_RANK1 = 0, 1, 2, 3, 4, 5

NN = (((1,), (0,)), ((), ()))
NT = (((1,), (1,)), ((), ()))
TN = (((0,), (0,)), ((), ()))


def _mm(a, b, dims=NN):
    return lax.dot_general(a.astype(BF16), b.astype(BF16), dims, preferred_element_type=F32)


def _sigmoid(x):
    return 0.5 * jnp.tanh(0.5 * x) + 0.5


def _silu(x):
    return x * _sigmoid(x)


def _params(*sem):
    return pltpu.CompilerParams(dimension_semantics=sem, vmem_limit_bytes=VMEM_LIMIT)


def _in_proj_body(x_ref, g_ref, w_ref, ws_ref, dn_ref, za_ref, fox_ref, uc_ref, gates_ref, small_ref,
                  *, col_chunk):
    x = x_ref[...]
    h = x * lax.rsqrt(jnp.mean(x * x, axis=-1, keepdims=True) + EPS) * g_ref[...]
    hb = h.astype(BF16)
    col = 0
    for ref in (dn_ref, za_ref, fox_ref, uc_ref, gates_ref):
        width = ref.shape[-1]
        for c in range(0, width, col_chunk):
            ref[:, c:c + col_chunk] = jnp.dot(
                hb, w_ref[:, col + c:col + c + col_chunk], preferred_element_type=F32).astype(ref.dtype)
        col += width
    small_ref[...] = jnp.dot(hb, ws_ref[...], preferred_element_type=F32)


def _in_proj(x2, g, w, ws, *, tm=512, col_chunk=512):
    n, d = x2.shape
    d_gate = 3 * d
    widths = (W_DN, H_DN * DV_DN, W_FOX, 2 * C_CONV, d_gate)
    assert w.shape == (d, sum(widths)) and n % tm == 0
    out_shape = [jax.ShapeDtypeStruct((n, wd), BF16) for wd in widths]
    out_shape.append(jax.ShapeDtypeStruct((n, LANES), F32))
    row = lambda i: (i, 0)
    fixed = lambda i: (0, 0)
    return pl.pallas_call(
        functools.partial(_in_proj_body, col_chunk=col_chunk),
        grid=(n // tm,),
        in_specs=[pl.BlockSpec((tm, d), row), pl.BlockSpec((1, d), fixed),
                  pl.BlockSpec(w.shape, fixed, pipeline_mode=pl.Buffered(1)),
                  pl.BlockSpec(ws.shape, fixed, pipeline_mode=pl.Buffered(1))],
        out_specs=[pl.BlockSpec((tm, wd), row) for wd in widths] + [pl.BlockSpec((tm, LANES), row)],
        out_shape=out_shape,
        compiler_params=_params("parallel"),
        name="in_proj",
    )(x2, g, w, ws)


def _softplus_parts(z):
    t = jnp.log1p(jnp.exp(-jnp.abs(z)))
    return jnp.maximum(z, 0.0) + t, -(jnp.maximum(-z, 0.0) + t)


def _delta_body(qkv_ref, za_ref, sm_ref, cw_ref, par_ref, oa_ref, qaug_ref, kaug_ref,
                xs_ref, s_ref, carry_ref, *, ts):
    j = pl.program_id(1)
    halo = SUBLANES
    pack = 2 * SUBLANES

    @pl.when(j == 0)
    def _():
        xs_ref[0:halo, :] = jnp.zeros((halo, W_DN), F32)
        s_ref[...] = jnp.zeros_like(s_ref)
        carry_ref[...] = jnp.zeros_like(carry_ref)

    @pl.when(j > 0)
    def _():
        xs_ref[0:halo, :] = xs_ref[2 * halo:3 * halo, :]

    xb = qkv_ref[0]
    xs_ref[halo:2 * halo, :] = qkv_ref[0, 0:pack, :].astype(F32)[0:halo]
    xs_ref[2 * halo:3 * halo, :] = qkv_ref[0, ts - pack:ts, :].astype(F32)[pack - halo:pack]

    lag = lax.broadcasted_iota(I32, (ts, ts), 0) - lax.broadcasted_iota(I32, (ts, ts), 1)
    shifted = [jnp.dot(jnp.where(lag == s, 1.0, 0.0).astype(BF16), xb, preferred_element_type=F32)
               for s in range(1, SHORT_CONV)]
    head_row = lax.broadcasted_iota(I32, (halo, LANES), 0)

    def conv_silu(lane0):
        lanes = slice(lane0, lane0 + LANES)
        acc = cw_ref[SHORT_CONV - 1:SHORT_CONV, lanes] * xb[:, lanes].astype(F32)
        for s in range(1, SHORT_CONV):
            acc = acc + cw_ref[SHORT_CONV - 1 - s:SHORT_CONV - s, lanes] * shifted[s - 1][:, lanes]
        head = acc[0:halo]
        for s in range(1, SHORT_CONV):
            prev = jnp.where(head_row < s, xs_ref[halo - s:2 * halo - s, lanes], 0.0)
            head = head + cw_ref[SHORT_CONV - 1 - s:SHORT_CONV - s, lanes] * prev
        return _silu(jnp.concatenate([head, acc[halo:]], axis=0))

    def l2n(a):
        return a * lax.rsqrt(jnp.sum(a * a, axis=-1, keepdims=True) + EPS)

    sm = sm_ref[0]
    lane = lax.broadcasted_iota(I32, sm.shape, 1)
    sp, logsig = _softplus_parts(sm + par_ref[1:2, :])
    vals = jnp.where(lane < LANE_G, _sigmoid(sm),
                     jnp.where(lane < LANE_F, par_ref[0:1, :] * sp,
                               jnp.where(lane < LANE_F + H_FOX, logsig, 0.0)))
    row = lax.broadcasted_iota(I32, (ts, ts), 0)
    colm = lax.broadcasted_iota(I32, (ts, ts), 1)
    log_chunk = CHUNK.bit_length() - 1
    causal = (row >= colm) & ((row >> log_chunk) == (colm >> log_chunk))

    hi = vals.astype(BF16)
    rem = vals - hi.astype(F32)
    mid = rem.astype(BF16)
    lo = (rem - mid.astype(F32)).astype(BF16)
    pieces = jnp.concatenate([hi, mid, lo], axis=-1)

    def cumsum(mask):
        y = jnp.dot(jnp.where(mask, 1.0, 0.0).astype(BF16), pieces, preferred_element_type=F32)
        return (y[:, :LANES] + y[:, LANES:2 * LANES]) + y[:, 2 * LANES:]

    ccum = cumsum(row >= colm) + carry_ref[...]
    gcum = cumsum(causal)
    carry_ref[...] = ccum[ts - 1:ts, :]
    gcum_t = gcum.T

    cl = ccum * LOG2E
    c_hi = cl.astype(BF16)
    c_rem = cl - c_hi.astype(F32)
    c_mid = c_rem.astype(BF16)
    c_lo = (c_rem - c_mid.astype(F32)).astype(BF16)
    c_pieces = jnp.concatenate([c_hi, c_mid, c_lo], axis=-1)
    pr = lax.broadcasted_iota(I32, (3 * LANES, LANES), 0)
    pc = lax.broadcasted_iota(I32, (3 * LANES, LANES), 1)
    src_lane, piece = pr & (LANES - 1), pr >> (LANES.bit_length() - 1)
    owned = (src_lane >= LANE_F) & (src_lane < LANE_F + H_FOX) & ((pc >> 3) == src_lane - LANE_F)
    place_q = jnp.where(owned & ((pc & 7) == piece), 1.0, 0.0).astype(BF16)
    place_k = jnp.where(owned & ((pc & 7) == piece + 3), -1.0, 0.0).astype(BF16)
    slot = lane & 7
    in_heads = lane < 8 * H_FOX
    ones_q = jnp.where(in_heads & (slot >= 3) & (slot < 6), 1.0, 0.0)
    ones_k = jnp.where(in_heads & (slot < 3), 1.0, 0.0)
    qaug_ref[0] = (jnp.dot(c_pieces, place_q, preferred_element_type=F32) + ones_q).astype(BF16)
    kaug_ref[0] = (jnp.dot(c_pieces, place_k, preferred_element_type=F32) + ones_k).astype(BF16)

    scale = DK_DN ** -0.5
    dn_norm = par_ref[2:3, :]

    pw = 2 * CHUNK
    prow = lax.broadcasted_iota(I32, (pw, pw), 0)
    pcol = lax.broadcasted_iota(I32, (pw, pw), 1)
    same = (prow >> log_chunk) == (pcol >> log_chunk)
    causal_p = (prow >= pcol) & same
    strict_p = (prow > pcol) & same
    levels = []
    s = 1
    while s < CHUNK:
        levels.append(((prow >> s.bit_length()) == (pcol >> s.bit_length()))
                      & ((prow & s) != 0) & ((pcol & s) == 0))
        s *= 2

    heads = range(H_DN)
    pairs = range(ts // pw)
    q, k, v = [], [], []
    for h in heads:
        q.append(l2n(conv_silu(h * DK_DN)))
        k.append(l2n(conv_silu(H_DN * DK_DN + h * DK_DN)))
        v.append(conv_silu(2 * H_DN * DK_DN + h * DV_DN))

    ctx = []
    for h in heads:
        for p in pairs:
            pr = slice(p * pw, (p + 1) * pw)
            gcol = gcum[pr, LANE_G + h:LANE_G + h + 1]
            grow = gcum_t[LANE_G + h:LANE_G + h + 1, pr]
            beta = vals[pr, LANE_BETA + h:LANE_BETA + h + 1]
            decay = jnp.where(causal_p, jnp.exp(jnp.where(causal_p, gcol - grow, 0.0)), 0.0)
            egc = jnp.exp(gcol)
            kp = k[h][pr]
            kb = kp * beta
            ctx.append(dict(
                h=h, p=p, gcol=gcol, kp=kp,
                a=jnp.where(strict_p, _mm(kb, kp, NT) * decay, 0.0),
                qk=jnp.where(causal_p, _mm(q[h][pr] * scale, kp, NT) * decay, 0.0),
                rhs=jnp.concatenate([v[h][pr] * beta, kb * egc], axis=-1),
                qg=q[h][pr] * (scale * egc)))

    for c in ctx:
        c["n"] = -jnp.where(levels[0], c["a"], 0.0)
    for level in levels[1:]:
        for c in ctx:
            m = jnp.where(level, c["a"], 0.0)
            c["y"] = m + _mm(c["n"], m)
        for c in ctx:
            c["n"] = c["n"] - (c["y"] + _mm(c["y"], c["n"]))
    for c in ctx:
        c["sol"] = c["rhs"] + _mm(c["n"], c["rhs"])

    state = [s_ref[h] for h in heads]
    for p in pairs:
        group = [c for c in ctx if c["p"] == p]
        v_prev = [None] * H_DN
        for ch in range(2):
            rows = slice(ch * CHUNK, (ch + 1) * CHUNK)
            out_rows = slice(p * pw + ch * CHUNK, p * pw + (ch + 1) * CHUNK)
            v_new = [c["sol"][rows, :DV_DN] - _mm(c["sol"][rows, DV_DN:], state[c["h"]]) for c in group]
            for c, vn in zip(group, v_new):
                h = c["h"]
                v_pair = jnp.concatenate([vn, jnp.zeros_like(vn)] if ch == 0 else [v_prev[h], vn], axis=0)
                o = _mm(c["qg"][rows], state[h]) + _mm(c["qk"][rows], v_pair)
                glast = c["gcol"][(ch + 1) * CHUNK - 1:(ch + 1) * CHUNK]
                state[h] = (state[h] * jnp.exp(glast)
                            + _mm(c["kp"][rows] * jnp.exp(glast - c["gcol"][rows]), vn, TN))
                v_prev[h] = vn
                o = o * lax.rsqrt(jnp.mean(o * o, axis=-1, keepdims=True) + EPS) * dn_norm
                za = za_ref[0, out_rows, h * DV_DN:(h + 1) * DV_DN].astype(F32)
                oa_ref[0, out_rows, h * DV_DN:(h + 1) * DV_DN] = (o * _silu(za)).astype(oa_ref.dtype)
    for h in heads:
        s_ref[h] = state[h]


def _delta_rule(qkv, za, small, conv_w, par, *, ts=256):
    b, t, _ = qkv.shape
    assert t % ts == 0 and ts % (2 * CHUNK) == 0
    blk = lambda width: pl.BlockSpec((1, ts, width), lambda bi, j: (bi, j, 0))
    fixed = lambda bi, j: (0, 0)
    return pl.pallas_call(
        functools.partial(_delta_body, ts=ts),
        grid=(b, t // ts),
        in_specs=[blk(W_DN), blk(H_DN * DV_DN), blk(LANES),
                  pl.BlockSpec(conv_w.shape, fixed), pl.BlockSpec(par.shape, fixed)],
        out_specs=[blk(H_DN * DV_DN), blk(LANES), blk(LANES)],
        out_shape=[jax.ShapeDtypeStruct((b, t, H_DN * DV_DN), BF16),
                   jax.ShapeDtypeStruct((b, t, LANES), BF16),
                   jax.ShapeDtypeStruct((b, t, LANES), BF16)],
        scratch_shapes=[pltpu.VMEM((3 * SUBLANES, W_DN), F32),
                        pltpu.VMEM((H_DN, DK_DN, DV_DN), F32),
                        pltpu.VMEM((1, LANES), F32)],
        compiler_params=_params("parallel", "arbitrary"),
        name="delta_rule",
    )(qkv, za, small, conv_w, par)


def _fox_body(q_ref, k_ref, v_ref, qa_ref, ka_ref, o_ref, m_ref, acc_ref, *, tq, tk):
    i = pl.program_id(1)
    m_ref[...] = jnp.full(m_ref.shape, -jnp.inf, F32)
    acc_ref[...] = jnp.zeros_like(acc_ref)
    head_lanes = [slice(h * D_FOX, (h + 1) * D_FOX) for h in range(H_FOX)]
    lane = lax.broadcasted_iota(I32, (tk, LANES), 1)
    own = [jnp.where((lane >> 3) == h, 1.0, 0.0).astype(BF16) for h in range(H_FOX)]
    ones = jnp.ones((tk, D_FOX), BF16)
    keep = lax.broadcasted_iota(I32, (tq, tk), 1) <= lax.broadcasted_iota(I32, (tq, tk), 0)

    def block(start, diag_offset):
        rows = slice(0 if diag_offset is None else diag_offset, tq)
        ka = ka_ref[0, pl.ds(start, tk), :]
        for h, lanes in enumerate(head_lanes):
            q_aug = jnp.concatenate([q_ref[0, rows, lanes], qa_ref[0, rows, :]], axis=1)
            k_aug = jnp.concatenate([k_ref[0, pl.ds(start, tk), lanes], ka * own[h]], axis=1)
            s = lax.dot_general(q_aug, k_aug, NT, preferred_element_type=F32)
            if diag_offset is not None:
                s = jnp.where(keep[:tq - diag_offset], s, -jnp.inf)
            m_prev = m_ref[h, rows]
            m_next = jnp.maximum(m_prev, jnp.max(s, axis=-1, keepdims=True))
            p = jnp.exp2(s - jnp.concatenate([m_next] * (tk // LANES), axis=1))
            alpha = jnp.exp2(m_prev - m_next)
            v_aug = jnp.concatenate([v_ref[0, pl.ds(start, tk), lanes], ones], axis=1)
            acc_ref[h, rows] = (jnp.concatenate([alpha, alpha], axis=1) * acc_ref[h, rows]
                                + jnp.dot(p.astype(BF16), v_aug, preferred_element_type=F32))
            m_ref[h, rows] = m_next

    def full_block(jb, carry):
        block(pl.multiple_of(jb * tk, tk), None)
        return carry

    lax.fori_loop(0, i * (tq // tk), full_block, 0)
    for d in range(tq // tk):
        block(pl.multiple_of(i * tq + d * tk, tk), d * tk)
    for h, lanes in enumerate(head_lanes):
        acc = acc_ref[h]
        o_ref[0, :, lanes] = (acc[:, :D_FOX] / acc[:, D_FOX:]).astype(o_ref.dtype)


def _fox_attention(qkv, qaug, kaug, *, tq=1024, tk=512):
    b, t, _ = qkv.shape
    hd = H_FOX * D_FOX
    assert t % tq == 0 and tq % tk == 0 and tk % LANES == 0
    return pl.pallas_call(
        functools.partial(_fox_body, tq=tq, tk=tk),
        grid=(b, t // tq),
        in_specs=[pl.BlockSpec((1, tq, hd), lambda bi, i: (bi, i, 0)),
                  pl.BlockSpec((1, t, hd), lambda bi, i: (bi, 0, 1)),
                  pl.BlockSpec((1, t, hd), lambda bi, i: (bi, 0, 2)),
                  pl.BlockSpec((1, tq, LANES), lambda bi, i: (bi, i, 0)),
                  pl.BlockSpec((1, t, LANES), lambda bi, i: (bi, 0, 0))],
        out_specs=pl.BlockSpec((1, tq, hd), lambda bi, i: (bi, i, 0)),
        out_shape=jax.ShapeDtypeStruct((b, t, hd), BF16),
        scratch_shapes=[pltpu.VMEM((H_FOX, tq, LANES), F32), pltpu.VMEM((H_FOX, tq, 2 * D_FOX), F32)],
        compiler_params=_params("parallel", "arbitrary"),
        name="fox_attention",
    )(qkv, qkv, qkv, qaug, kaug)


def _conv_body(cur_ref, halo_ref, dw_ref, vec_ref, o_ref, z_ref, zs_ref, y_ref, *, tt, hrows, rchunk):
    i = pl.program_id(1)
    cur = cur_ref[0].astype(F32)
    z_ref[hrows:hrows + tt, :] = cur[:, :C_CONV] * _sigmoid(cur[:, C_CONV:])
    hal = halo_ref[0].astype(F32)
    zh = hal[:, :C_CONV] * _sigmoid(hal[:, C_CONV:])
    z_ref[0:hrows, :] = jnp.where(i > 0, zh, 0.0)

    sub = SUBLANES
    span = tt + hrows - sub
    for phase in range(1, sub):
        zs_ref[phase - 1, 0:span, :] = z_ref[phase:phase + span, :]

    for lg in range(C_CONV // LANES):
        lanes = slice(lg * LANES, (lg + 1) * LANES)
        for r0 in range(0, tt, rchunk):
            acc = jnp.broadcast_to(vec_ref[0:1, lanes], (rchunk, LANES))
            for k in range(CONV_WIDTH):
                off = hrows - (CONV_WIDTH - 1) + k
                phase, base = off % sub, off - off % sub + r0
                if phase == 0:
                    tap = z_ref[base:base + rchunk, lanes]
                else:
                    tap = zs_ref[phase - 1, base:base + rchunk, lanes]
                acc = acc + dw_ref[k:k + 1, lanes] * tap
            y_ref[r0:r0 + rchunk, lanes] = acc

    y = y_ref[...]
    mu = jnp.mean(y, axis=-1, keepdims=True)
    yc = y - mu
    var = jnp.mean(yc * yc, axis=-1, keepdims=True)
    yn = yc * lax.rsqrt(var + EPS) * vec_ref[1:2, :] + vec_ref[2:3, :]
    o_ref[0] = _silu(yn).astype(o_ref.dtype)


def _conv_module(uc, dw, vec, *, tt=256, hrows=32, rchunk=64):
    b, t, _ = uc.shape
    assert t % tt == 0 and tt % hrows == 0 and hrows >= CONV_WIDTH - 1
    ratio = tt // hrows
    return pl.pallas_call(
        functools.partial(_conv_body, tt=tt, hrows=hrows, rchunk=rchunk),
        grid=(b, t // tt),
        in_specs=[pl.BlockSpec((1, tt, 2 * C_CONV), lambda bi, i: (bi, i, 0)),
                  pl.BlockSpec((1, hrows, 2 * C_CONV), lambda bi, i: (bi, jnp.maximum(i * ratio - 1, 0), 0)),
                  pl.BlockSpec(dw.shape, lambda bi, i: (0, 0)),
                  pl.BlockSpec(vec.shape, lambda bi, i: (0, 0))],
        out_specs=pl.BlockSpec((1, tt, C_CONV), lambda bi, i: (bi, i, 0)),
        out_shape=jax.ShapeDtypeStruct((b, t, C_CONV), BF16),
        scratch_shapes=[pltpu.VMEM((tt + hrows, C_CONV), F32), pltpu.VMEM((7, tt + hrows, C_CONV), F32),
                        pltpu.VMEM((tt, C_CONV), F32)],
        compiler_params=_params("parallel", "parallel"),
        name="conv_module",
    )(uc, uc, dw, vec)


def _merge_body(oa_ref, ob_ref, zc_ref, gates_ref, x_ref, wa_ref, wb_ref, wc_ref, wo_ref, g_ref,
                wr_ref, br_ref, xn_ref, hp_ref, route_ref, meta_ref, cnt_ref, run_ref, *, tm):
    step = pl.program_id(0)
    d = x_ref.shape[-1]

    @pl.when(step == 0)
    def _():
        run_ref[...] = jnp.zeros_like(run_ref)

    merged = None
    for idx, (m_ref, w_ref) in enumerate(((oa_ref, wa_ref), (ob_ref, wb_ref), (zc_ref, wc_ref))):
        y = jnp.dot(m_ref[...], w_ref[...], preferred_element_type=F32)
        term = _sigmoid(gates_ref[:, idx * d:(idx + 1) * d].astype(F32)) * y
        merged = term if merged is None else merged + term
    xn = x_ref[...] + jnp.dot(merged.astype(BF16), wo_ref[...], preferred_element_type=F32)
    xn_ref[...] = xn
    h2 = xn * lax.rsqrt(jnp.mean(xn * xn, axis=-1, keepdims=True) + EPS) * g_ref[...]

    hp_ref[...] = _pack_bf16_pairs(h2)

    logits = _mm(h2, wr_ref[...]) + br_ref[...]
    lane = lax.broadcasted_iota(I32, logits.shape, 1)
    big = jnp.int32(4 * LANES)
    in_grp = lane < N_GROUPS
    gl = jnp.where(in_grp, logits, -jnp.inf)
    gmax = jnp.max(gl, axis=-1, keepdims=True)
    grp = jnp.min(jnp.where(gl == gmax, lane, big), axis=-1, keepdims=True)
    p_grp = 1.0 / jnp.sum(jnp.where(in_grp, jnp.exp(gl - gmax), 0.0), axis=-1, keepdims=True)
    e_lo = LANE_EXP + grp * EXPERTS_PER_GROUP
    el = jnp.where((lane >= e_lo) & (lane < e_lo + EXPERTS_PER_GROUP), logits, -jnp.inf)
    v0 = jnp.max(el, axis=-1, keepdims=True)
    i0 = jnp.min(jnp.where(el == v0, lane, big), axis=-1, keepdims=True)
    el1 = jnp.where(lane == i0, -jnp.inf, el)
    v1 = jnp.max(el1, axis=-1, keepdims=True)
    i1 = jnp.min(jnp.where(el1 == v1, lane, big), axis=-1, keepdims=True)
    e1 = jnp.exp(v1 - v0)
    gate0 = p_grp / (1.0 + e1)
    gate1 = p_grp * e1 / (1.0 + e1)

    hot0 = lane == i0
    hot1 = lane == i1
    onehot = jnp.where(hot0 | hot1, 1.0, 0.0)
    ri = lax.broadcasted_iota(I32, (tm, tm), 0)
    cj = lax.broadcasted_iota(I32, (tm, tm), 1)
    prefix = _mm(jnp.where(cj < ri, 1.0, 0.0), onehot) + run_ref[...]
    rank0 = jnp.sum(jnp.where(hot0, prefix, 0.0), axis=-1, keepdims=True)
    rank1 = jnp.sum(jnp.where(hot1, prefix, 0.0), axis=-1, keepdims=True)
    run_ref[...] = run_ref[...] + jnp.sum(onehot, axis=0, keepdims=True)
    cnt_ref[...] = jnp.broadcast_to(run_ref[...], cnt_ref.shape)

    route = jnp.where(lane == R_GATE0, gate0, 0.0)
    route = jnp.where(lane == R_GATE1, gate1, route)
    route = jnp.where(lane == R_EID0, (i0 - LANE_EXP).astype(F32), route)
    route = jnp.where(lane == R_EID1, (i1 - LANE_EXP).astype(F32), route)
    route = jnp.where(lane == R_RANK0, rank0, route)
    route = jnp.where(lane == R_RANK1, rank1, route)
    route_ref[...] = route
    meta_ref[...] = route.T[0:8, :].astype(I32)


def _merge(oa, ob, zc, gates, x2, wa, wb, wc, wo, g, wr, br, *, tm=512):
    n, d = x2.shape
    assert n % tm == 0
    row = lambda i: (i, 0)
    fixed = lambda i: (0, 0)
    full = lambda a: pl.BlockSpec(a.shape, fixed)
    return pl.pallas_call(
        functools.partial(_merge_body, tm=tm),
        grid=(n // tm,),
        in_specs=[pl.BlockSpec((tm, oa.shape[1]), row), pl.BlockSpec((tm, ob.shape[1]), row),
                  pl.BlockSpec((tm, zc.shape[1]), row), pl.BlockSpec((tm, 3 * d), row),
                  pl.BlockSpec((tm, d), row),
                  full(wa), full(wb), full(wc), full(wo), full(g), full(wr), full(br)],
        out_specs=[pl.BlockSpec((tm, d), row), pl.BlockSpec((tm, d // 2), row),
                   pl.BlockSpec((tm, LANES), row), pl.BlockSpec((8, tm), lambda i: (0, i)),
                   pl.BlockSpec((8, LANES), fixed)],
        out_shape=[jax.ShapeDtypeStruct((n, d), F32), jax.ShapeDtypeStruct((n, d // 2), U32),
                   jax.ShapeDtypeStruct((n, LANES), F32), jax.ShapeDtypeStruct((8, n), I32),
                   jax.ShapeDtypeStruct((8, LANES), F32)],
        scratch_shapes=[pltpu.VMEM((1, LANES), F32)],
        compiler_params=_params("arbitrary"),
        name="merge_router",
    )(oa, ob, zc, gates, x2, wa, wb, wc, wo, g, wr, br)


def _sc_mesh():
    return plsc.VectorSubcoreMesh(core_axis_name="c", subcore_axis_name="s")


def _sc_worker_base(per_worker):
    return (lax.axis_index("s") * SC_CORES + lax.axis_index("c")) * per_worker


def _sc_row_scatter(src, idx, out_rows, *, chunk=128):
    n, d = src.shape
    per_worker = n // (SC_CORES * SC_SUBCORES)
    chunk = min(chunk, per_worker)
    assert idx.shape == (2 * n,) and n % (SC_CORES * SC_SUBCORES) == 0 and per_worker % chunk == 0

    @functools.partial(
        pl.kernel, mesh=_sc_mesh(), out_type=jax.ShapeDtypeStruct((out_rows, d), src.dtype),
        scratch_types=[pltpu.VMEM((chunk,), I32), pltpu.VMEM((chunk, d), src.dtype), pltpu.SemaphoreType.DMA])
    def scatter(src_hbm, idx_hbm, out_hbm, idx_v, rows_v, sem):
        base = _sc_worker_base(per_worker)

        @pl.loop(0, per_worker // chunk)
        def _(step):
            off = base + step * chunk
            pltpu.sync_copy(src_hbm.at[pl.ds(off, chunk)], rows_v)
            for slot in range(2):
                pltpu.sync_copy(idx_hbm.at[pl.ds(slot * n + off, chunk)], idx_v)
                pltpu.async_copy(rows_v, out_hbm.at[idx_v], sem).wait()

    return scatter(src, idx)


def _pack_bf16_pairs(x):
    half = x.shape[-1] // 2
    lo = pltpu.bitcast(x[:, :half].astype(BF16).astype(F32), U32) >> 16
    hi = pltpu.bitcast(x[:, half:].astype(BF16).astype(F32), U32) & jnp.uint32(0xFFFF0000)
    return lo | hi


def _unpack_bf16_pairs(xp):
    return pltpu.bitcast(xp << 16, F32), pltpu.bitcast(xp & jnp.uint32(0xFFFF0000), F32)


def _expert_body(be_ref, valid_ref, nu_ref, xs_ref, w1_ref, w3_ref, w2_ref, ys_ref):
    del be_ref
    step = pl.program_id(0)
    live = step < nu_ref[0]

    @pl.when(jnp.logical_not(live))
    def _():
        ys_ref[...] = jnp.zeros_like(ys_ref)

    @pl.when(live)
    def _():
        row = lax.broadcasted_iota(I32, xs_ref.shape, 0)
        xp = jnp.where(row < valid_ref[step], xs_ref[...], jnp.uint32(0))
        half = xp.shape[-1]
        lo, hi = (part.astype(BF16) for part in _unpack_bf16_pairs(xp))

        def up(w_ref):
            return (jnp.dot(lo, w_ref[0, :half, :].astype(BF16), preferred_element_type=F32)
                    + jnp.dot(hi, w_ref[0, half:, :].astype(BF16), preferred_element_type=F32))

        act = (_silu(up(w1_ref)) * up(w3_ref)).astype(BF16)
        ys_ref[...] = _pack_bf16_pairs(jnp.dot(act, w2_ref[0].astype(BF16), preferred_element_type=F32))


def _experts(blk_eid, blk_valid, n_used, xs, w1, w3, w2, *, rb):
    p, half = xs.shape
    d = 2 * half
    nb = p // rb
    used = lambda i, be, bv, nu: jnp.maximum(jnp.minimum(i, nu[0] - 1), 0)
    wmap = lambda i, be, bv, nu: (be[used(i, be, bv, nu)], 0, 0)
    grid_spec = pltpu.PrefetchScalarGridSpec(
        num_scalar_prefetch=3,
        grid=(nb,),
        in_specs=[pl.BlockSpec((rb, half), lambda i, be, bv, nu: (used(i, be, bv, nu), 0)),
                  pl.BlockSpec((1, d, D_EXPERT), wmap), pl.BlockSpec((1, d, D_EXPERT), wmap),
                  pl.BlockSpec((1, D_EXPERT, d), wmap)],
        out_specs=pl.BlockSpec((rb, half), lambda i, be, bv, nu: (i, 0)),
    )
    return pl.pallas_call(
        _expert_body,
        grid_spec=grid_spec,
        out_shape=jax.ShapeDtypeStruct((p, half), U32),
        compiler_params=_params("arbitrary"),
        name="moe_experts",
    )(blk_eid, blk_valid, n_used, xs, w1, w3, w2)


def _sc_row_gather(table, idx, *, chunk=128):
    rows, d = idx.shape[0], table.shape[1]
    per_worker = rows // (SC_CORES * SC_SUBCORES)
    chunk = min(chunk, per_worker)
    assert rows % (SC_CORES * SC_SUBCORES) == 0 and per_worker % chunk == 0

    @functools.partial(
        pl.kernel, mesh=_sc_mesh(), out_type=jax.ShapeDtypeStruct((rows, d), table.dtype),
        scratch_types=[pltpu.VMEM((chunk,), I32), pltpu.VMEM((chunk, d), table.dtype), pltpu.SemaphoreType.DMA])
    def gather(table_hbm, idx_hbm, out_hbm, idx_v, rows_v, sem):
        base = _sc_worker_base(per_worker)

        @pl.loop(0, per_worker // chunk)
        def _(step):
            off = base + step * chunk
            pltpu.sync_copy(idx_hbm.at[pl.ds(off, chunk)], idx_v)
            pltpu.async_copy(table_hbm.at[idx_v], rows_v, sem).wait()
            pltpu.sync_copy(rows_v, out_hbm.at[pl.ds(off, chunk)])

    return gather(table, idx)


def _combine_body(r0_ref, r1_ref, x_ref, route_ref, g_ref, o_ref, *, final_norm):
    route = route_ref[...]
    half = r0_ref.shape[-1]
    lo0, hi0 = _unpack_bf16_pairs(r0_ref[...])
    lo1, hi1 = _unpack_bf16_pairs(r1_ref[...])
    g0, g1 = route[:, R_GATE0:R_GATE0 + 1], route[:, R_GATE1:R_GATE1 + 1]
    out = jnp.concatenate([x_ref[:, :half] + g0 * lo0 + g1 * lo1,
                           x_ref[:, half:] + g0 * hi0 + g1 * hi1], axis=1)
    if final_norm:
        out = out * lax.rsqrt(jnp.mean(out * out, axis=-1, keepdims=True) + EPS) * g_ref[...]
    o_ref[...] = out


def _combine(rows2, xn, route, g, *, tm, final_norm):
    n, d = xn.shape
    nt = n // tm
    row = lambda i: (i, 0)
    return pl.pallas_call(
        functools.partial(_combine_body, final_norm=final_norm),
        grid=(nt,),
        in_specs=[pl.BlockSpec((tm, d // 2), row), pl.BlockSpec((tm, d // 2), lambda i: (i + nt, 0)),
                  pl.BlockSpec((tm, d), row), pl.BlockSpec((tm, LANES), row),
                  pl.BlockSpec((1, d), lambda i: (0, 0))],
        out_specs=pl.BlockSpec((tm, d), row),
        out_shape=jax.ShapeDtypeStruct((n, d), F32),
        compiler_params=_params("parallel"),
        name="moe_combine",
    )(rows2, rows2, xn, route, g)


def _moe(hp, route, meta, counts, xn, w1, w3, w2, g_final, *, layer, tm, rb, final_norm):
    n = hp.shape[0]
    cnt = counts[0, LANE_EXP:LANE_EXP + N_EXPERTS].astype(I32)
    nblk = (cnt + rb - 1) // rb
    bend = jnp.cumsum(nblk)
    pstart = (bend - nblk) * rb
    nb = (2 * n) // rb + N_EXPERTS
    n_used = bend[-1:].astype(I32)
    blk = jnp.arange(nb, dtype=I32)
    blk_eid = jnp.minimum(jnp.sum(bend[None, :] <= blk[:, None], axis=1), N_EXPERTS - 1).astype(I32)
    experts = jnp.arange(N_EXPERTS, dtype=I32)
    mine = blk_eid[:, None] == experts[None, :]
    blk_valid = jnp.clip(jnp.sum(jnp.where(mine, (cnt + pstart)[None, :], 0), axis=1) - blk * rb, 0, rb)
    eid = meta[R_EID0:R_EID1 + 1]
    first = jnp.sum(jnp.where(eid[None] == experts[:, None, None], pstart[:, None, None], 0), axis=0)
    dest = (first + meta[R_RANK0:R_RANK1 + 1]).reshape(-1)

    xs = _sc_row_scatter(hp, dest, nb * rb)
    ys = _experts(blk_eid + layer * N_EXPERTS, blk_valid.astype(I32), n_used, xs, w1, w3, w2, rb=rb)
    return _combine(_sc_row_gather(ys, dest), xn, route, g_final, tm=tm, final_norm=final_norm)


def _lane_row(pairs):
    row = jnp.zeros((LANES,), F32)
    for off, vec in pairs:
        row = row.at[off:off + vec.shape[0]].set(vec.astype(F32))
    return row


def kernel(x, norm_mix, w_in, conv_qkv, dn_a_log, dn_dt_bias, dn_norm, fox_bias, conv_dw, conv_dw_b,
           conv_ln_g, conv_ln_b, w_a, w_b, w_c, w_out, norm_ffn, router_group_w, router_group_b,
           router_expert_w, router_expert_b, expert_w1, expert_w3, expert_w2, norm_final):
    b, t, d = x.shape
    n = b * t
    depth = w_in.shape[0]
    qk_dn = H_DN * DK_DN
    in_sizes = (qk_dn, qk_dn, H_DN * DV_DN, H_DN * DV_DN, H_DN, H_DN,
                H_FOX * D_FOX, H_FOX * D_FOX, H_FOX * D_FOX, H_FOX, 2 * C_CONV, d, d, d)
    splits = np.cumsum(in_sizes)[:-1].tolist()
    tm_moe, rb = 256, 512

    x2 = x.reshape(n, d)
    w1_all = expert_w1.reshape(depth * N_EXPERTS, d, D_EXPERT)
    w3_all = expert_w3.reshape(depth * N_EXPERTS, d, D_EXPERT)
    w2_all = expert_w2.reshape(depth * N_EXPERTS, D_EXPERT, d)
    for l in range(depth):
        (qa, ka, va, za, ba, aa, qb, kb, vb, fb, uc, ga, gb, gc) = jnp.split(w_in[l], splits, axis=1)
        w_main = jnp.concatenate([qa, ka, va, za, qb * (D_FOX ** -0.5 * LOG2E), kb, vb, uc, ga, gb, gc],
                                 axis=1).astype(BF16)
        w_small = jnp.concatenate([ba, aa, fb, jnp.zeros((d, LANES - 3 * H_DN), F32)], axis=1).astype(BF16)
        dn_qkv, za_p, fox_qkv, uc_p, gates, small = _in_proj(x2, norm_mix[l][None, :], w_main, w_small)

        par = jnp.stack([_lane_row([(LANE_G, -jnp.exp(dn_a_log[l]))]),
                         _lane_row([(LANE_G, dn_dt_bias[l]), (LANE_F, fox_bias[l])]),
                         _lane_row([(0, dn_norm[l])])] + [jnp.zeros((LANES,), F32)] * 5)
        oa, qaug, kaug = _delta_rule(dn_qkv.reshape(b, t, -1), za_p.reshape(b, t, -1),
                                     small.reshape(b, t, LANES), conv_qkv[l], par)
        ob = _fox_attention(fox_qkv.reshape(b, t, -1), qaug, kaug)
        vec = jnp.stack([conv_dw_b[l], conv_ln_g[l], conv_ln_b[l]] + [jnp.zeros((C_CONV,), F32)] * 5)
        zc = _conv_module(uc_p.reshape(b, t, -1), conv_dw[l], vec)

        w_r = jnp.concatenate([router_group_w[l], jnp.zeros((d, LANE_EXP - N_GROUPS), F32),
                               router_expert_w[l], jnp.zeros((d, LANES - LANE_EXP - N_EXPERTS), F32)], axis=1)
        b_r = _lane_row([(LANE_GRP, router_group_b[l]), (LANE_EXP, router_expert_b[l])])[None, :]
        xn, hp, route, meta, counts = _merge(
            oa.reshape(n, -1), ob.reshape(n, -1), zc.reshape(n, -1), gates, x2,
            w_a[l].astype(BF16), w_b[l].astype(BF16), w_c[l].astype(BF16), w_out[l].astype(BF16),
            norm_ffn[l][None, :], w_r.astype(BF16), b_r)

        x2 = _moe(hp, route, meta, counts, xn, w1_all, w3_all, w2_all, norm_final[None, :],
                  layer=l, tm=tm_moe, rb=rb, final_norm=(l == depth - 1))
    return x2.reshape(b, t, d)
```

```python
import functools

import jax
import jax.numpy as jnp
import numpy as np
from jax import lax
from jax.experimental import pallas as pl
from jax.experimental.pallas import tpu as pltpu
from jax.experimental.pallas import tpu_sc as plsc

F32 = jnp.float32
BF16 = jnp.bfloat16
U32 = jnp.uint32
I32 = jnp.int32

EPS = 1e-6
LOG2E = 1.4426950408889634
LANES = 128
SUBLANES = 8
SC_CORES, SC_SUBCORES = 2, 16
H_DN, DK_DN, DV_DN = 4, 128, 128
SHORT_CONV = 4
CHUNK = 64
H_FOX, D_FOX = 4, 128
C_CONV = 512
CONV_WIDTH = 31
N_GROUPS, EXPERTS_PER_GROUP = 4, 8
N_EXPERTS = N_GROUPS * EXPERTS_PER_GROUP
D_EXPERT = 256

W_DN = 3 * H_DN * DK_DN
W_FOX = 3 * H_FOX * D_FOX
VMEM_LIMIT = 56 * 1024 * 1024

LANE_BETA, LANE_G, LANE_F = 0, 4, 8
LANE_GRP, LANE_EXP = 0, 32
R_GATE0, R_GATE1, R_EID0, R_EID1, R_RANK0, R_RANK1 = 0, 1, 2, 3, 4, 5

NN = (((1,), (0,)), ((), ()))
NT = (((1,), (1,)), ((), ()))
TN = (((0,), (0,)), ((), ()))


def _mm(a, b, dims=NN):
    return lax.dot_general(a.astype(BF16), b.astype(BF16), dims, preferred_element_type=F32)


def _sigmoid(x):
    return 0.5 * jnp.tanh(0.5 * x) + 0.5


def _silu(x):
    return x * _sigmoid(x)


def _params(*sem):
    return pltpu.CompilerParams(dimension_semantics=sem, vmem_limit_bytes=VMEM_LIMIT)


def _in_proj_body(*refs, col_chunk, combine):
    if combine:
        r0_ref, r1_ref, xn_ref, route_ref, g_ref, w_ref, ws_ref, x_out_ref = refs[:8]
        x = _combine_rows(r0_ref, r1_ref, xn_ref, route_ref)
        x_out_ref[...] = x
        dn_ref, za_ref, fox_ref, uc_ref, gates_ref, small_ref = refs[8:]
    else:
        x_ref, g_ref, w_ref, ws_ref, dn_ref, za_ref, fox_ref, uc_ref, gates_ref, small_ref = refs
        x = x_ref[...]
    h = x * lax.rsqrt(jnp.mean(x * x, axis=-1, keepdims=True) + EPS) * g_ref[...]
    hb = h.astype(BF16)
    col = 0
    for ref in (dn_ref, za_ref, fox_ref, uc_ref, gates_ref):
        width = ref.shape[-1]
        for c in range(0, width, col_chunk):
            ref[:, c:c + col_chunk] = jnp.dot(
                hb, w_ref[:, col + c:col + c + col_chunk], preferred_element_type=F32).astype(ref.dtype)
        col += width
    small_ref[...] = jnp.dot(hb, ws_ref[...], preferred_element_type=F32)


def _in_proj(x_src, g, w, ws, *, tm=512, col_chunk=512):
    combine = isinstance(x_src, tuple)
    n, d = (x_src[1] if combine else x_src).shape
    nt = n // tm
    widths = (W_DN, H_DN * DV_DN, W_FOX, 2 * C_CONV, 3 * d)
    assert w.shape == (d, sum(widths)) and n % tm == 0
    row = lambda i: (i, 0)
    fixed = lambda i: (0, 0)
    out_shape = [jax.ShapeDtypeStruct((n, wd), BF16) for wd in widths] + [jax.ShapeDtypeStruct((n, LANES), F32)]
    out_specs = [pl.BlockSpec((tm, wd), row) for wd in widths] + [pl.BlockSpec((tm, LANES), row)]
    if combine:
        rows2, xn, route = x_src
        args = (rows2, rows2, xn, route)
        in_specs = [pl.BlockSpec((tm, d // 2), row), pl.BlockSpec((tm, d // 2), lambda i: (i + nt, 0)),
                    pl.BlockSpec((tm, d), row), pl.BlockSpec((tm, LANES), row)]
        out_shape = [jax.ShapeDtypeStruct((n, d), F32)] + out_shape
        out_specs = [pl.BlockSpec((tm, d), row)] + out_specs
    else:
        args = (x_src,)
        in_specs = [pl.BlockSpec((tm, d), row)]
    return pl.pallas_call(
        functools.partial(_in_proj_body, col_chunk=col_chunk, combine=combine),
        grid=(nt,),
        in_specs=in_specs + [pl.BlockSpec((1, d), fixed),
                             pl.BlockSpec(w.shape, fixed, pipeline_mode=pl.Buffered(1)),
                             pl.BlockSpec(ws.shape, fixed, pipeline_mode=pl.Buffered(1))],
        out_specs=out_specs,
        out_shape=out_shape,
        compiler_params=_params("parallel"),
        name="in_proj",
    )(*args, g, w, ws)


def _softplus_parts(z):
    t = jnp.log1p(jnp.exp(-jnp.abs(z)))
    return jnp.maximum(z, 0.0) + t, -(jnp.maximum(-z, 0.0) + t)


def _delta_body(qkv_ref, za_ref, sm_ref, cw_ref, par_ref, oa_ref, qaug_ref, kaug_ref,
                xs_ref, s_ref, carry_ref, *, ts):
    j = pl.program_id(1)
    halo = SUBLANES
    pack = 2 * SUBLANES

    @pl.when(j == 0)
    def _():
        xs_ref[0:halo, :] = jnp.zeros((halo, W_DN), F32)
        s_ref[...] = jnp.zeros_like(s_ref)
        carry_ref[...] = jnp.zeros_like(carry_ref)

    @pl.when(j > 0)
    def _():
        xs_ref[0:halo, :] = xs_ref[2 * halo:3 * halo, :]

    xb = qkv_ref[0]
    xs_ref[halo:2 * halo, :] = qkv_ref[0, 0:pack, :].astype(F32)[0:halo]
    xs_ref[2 * halo:3 * halo, :] = qkv_ref[0, ts - pack:ts, :].astype(F32)[pack - halo:pack]

    lag = lax.broadcasted_iota(I32, (ts, ts), 0) - lax.broadcasted_iota(I32, (ts, ts), 1)
    shifted = [jnp.dot(jnp.where(lag == s, 1.0, 0.0).astype(BF16), xb, preferred_element_type=F32)
               for s in range(1, SHORT_CONV)]
    head_row = lax.broadcasted_iota(I32, (halo, LANES), 0)

    def conv_silu(lane0):
        lanes = slice(lane0, lane0 + LANES)
        acc = cw_ref[SHORT_CONV - 1:SHORT_CONV, lanes] * xb[:, lanes].astype(F32)
        for s in range(1, SHORT_CONV):
            acc = acc + cw_ref[SHORT_CONV - 1 - s:SHORT_CONV - s, lanes] * shifted[s - 1][:, lanes]
        head = acc[0:halo]
        for s in range(1, SHORT_CONV):
            prev = jnp.where(head_row < s, xs_ref[halo - s:2 * halo - s, lanes], 0.0)
            head = head + cw_ref[SHORT_CONV - 1 - s:SHORT_CONV - s, lanes] * prev
        return _silu(jnp.concatenate([head, acc[halo:]], axis=0))

    def l2n(a):
        return a * lax.rsqrt(jnp.sum(a * a, axis=-1, keepdims=True) + EPS)

    sm = sm_ref[0]
    lane = lax.broadcasted_iota(I32, sm.shape, 1)
    sp, logsig = _softplus_parts(sm + par_ref[1:2, :])
    vals = jnp.where(lane < LANE_G, _sigmoid(sm),
                     jnp.where(lane < LANE_F, par_ref[0:1, :] * sp,
                               jnp.where(lane < LANE_F + H_FOX, logsig, 0.0)))
    row = lax.broadcasted_iota(I32, (ts, ts), 0)
    colm = lax.broadcasted_iota(I32, (ts, ts), 1)
    log_chunk = CHUNK.bit_length() - 1
    causal = (row >= colm) & ((row >> log_chunk) == (colm >> log_chunk))

    hi = vals.astype(BF16)
    rem = vals - hi.astype(F32)
    mid = rem.astype(BF16)
    lo = (rem - mid.astype(F32)).astype(BF16)
    pieces = jnp.concatenate([hi, mid, lo], axis=-1)

    def cumsum(mask):
        y = jnp.dot(jnp.where(mask, 1.0, 0.0).astype(BF16), pieces, preferred_element_type=F32)
        return (y[:, :LANES] + y[:, LANES:2 * LANES]) + y[:, 2 * LANES:]

    ccum = cumsum(row >= colm) + carry_ref[...]
    gcum = cumsum(causal)
    carry_ref[...] = ccum[ts - 1:ts, :]
    gcum_t = gcum.T

    cl = ccum * LOG2E
    c_hi = cl.astype(BF16)
    c_rem = cl - c_hi.astype(F32)
    c_mid = c_rem.astype(BF16)
    c_lo = (c_rem - c_mid.astype(F32)).astype(BF16)
    c_pieces = jnp.concatenate([c_hi, c_mid, c_lo], axis=-1)
    pr = lax.broadcasted_iota(I32, (3 * LANES, LANES), 0)
    pc = lax.broadcasted_iota(I32, (3 * LANES, LANES), 1)
    src_lane, piece = pr & (LANES - 1), pr >> (LANES.bit_length() - 1)
    owned = (src_lane >= LANE_F) & (src_lane < LANE_F + H_FOX) & ((pc >> 3) == src_lane - LANE_F)
    place_q = jnp.where(owned & ((pc & 7) == piece), 1.0, 0.0).astype(BF16)
    place_k = jnp.where(owned & ((pc & 7) == piece + 3), -1.0, 0.0).astype(BF16)
    slot = lane & 7
    in_heads = lane < 8 * H_FOX
    ones_q = jnp.where(in_heads & (slot >= 3) & (slot < 6), 1.0, 0.0)
    ones_k = jnp.where(in_heads & (slot < 3), 1.0, 0.0)
    qaug_ref[0] = (jnp.dot(c_pieces, place_q, preferred_element_type=F32) + ones_q).astype(BF16)
    kaug_ref[0] = (jnp.dot(c_pieces, place_k, preferred_element_type=F32) + ones_k).astype(BF16)

    scale = DK_DN ** -0.5
    dn_norm = par_ref[2:3, :]

    pw = 2 * CHUNK
    prow = lax.broadcasted_iota(I32, (pw, pw), 0)
    pcol = lax.broadcasted_iota(I32, (pw, pw), 1)
    same = (prow >> log_chunk) == (pcol >> log_chunk)
    causal_p = (prow >= pcol) & same
    strict_p = (prow > pcol) & same
    levels = []
    s = 1
    while s < CHUNK:
        levels.append(((prow >> s.bit_length()) == (pcol >> s.bit_length()))
                      & ((prow & s) != 0) & ((pcol & s) == 0))
        s *= 2

    heads = range(H_DN)
    pairs = range(ts // pw)
    q, k, v = [], [], []
    for h in heads:
        q.append(l2n(conv_silu(h * DK_DN)))
        k.append(l2n(conv_silu(H_DN * DK_DN + h * DK_DN)))
        v.append(conv_silu(2 * H_DN * DK_DN + h * DV_DN))

    ctx = []
    for h in heads:
        for p in pairs:
            pr = slice(p * pw, (p + 1) * pw)
            gcol = gcum[pr, LANE_G + h:LANE_G + h + 1]
            grow = gcum_t[LANE_G + h:LANE_G + h + 1, pr]
            beta = vals[pr, LANE_BETA + h:LANE_BETA + h + 1]
            decay = jnp.where(causal_p, jnp.exp(jnp.where(causal_p, gcol - grow, 0.0)), 0.0)
            egc = jnp.exp(gcol)
            kp = k[h][pr]
            kb = kp * beta
            ctx.append(dict(
                h=h, p=p, gcol=gcol, kp=kp,
                a=jnp.where(strict_p, _mm(kb, kp, NT) * decay, 0.0),
                qk=jnp.where(causal_p, _mm(q[h][pr] * scale, kp, NT) * decay, 0.0),
                rhs=jnp.concatenate([v[h][pr] * beta, kb * egc], axis=-1),
                qg=q[h][pr] * (scale * egc)))

    for c in ctx:
        c["n"] = -jnp.where(levels[0], c["a"], 0.0)
    for level in levels[1:]:
        for c in ctx:
            m = jnp.where(level, c["a"], 0.0)
            c["y"] = m + _mm(c["n"], m)
        for c in ctx:
            c["n"] = c["n"] - (c["y"] + _mm(c["y"], c["n"]))
    for c in ctx:
        c["sol"] = c["rhs"] + _mm(c["n"], c["rhs"])

    state = [s_ref[h] for h in heads]
    for p in pairs:
        group = [c for c in ctx if c["p"] == p]
        v_prev = [None] * H_DN
        for ch in range(2):
            rows = slice(ch * CHUNK, (ch + 1) * CHUNK)
            out_rows = slice(p * pw + ch * CHUNK, p * pw + (ch + 1) * CHUNK)
            on_state = [_mm(jnp.concatenate([c["sol"][rows, DV_DN:], c["qg"][rows]], axis=0), state[c["h"]])
                        for c in group]
            for c, ws_qs in zip(group, on_state):
                h = c["h"]
                vn = c["sol"][rows, :DV_DN] - ws_qs[:CHUNK]
                v_pair = jnp.concatenate([vn, jnp.zeros_like(vn)] if ch == 0 else [v_prev[h], vn], axis=0)
                o = ws_qs[CHUNK:] + _mm(c["qk"][rows], v_pair)
                glast = c["gcol"][(ch + 1) * CHUNK - 1:(ch + 1) * CHUNK]
                state[h] = (state[h] * jnp.exp(glast)
                            + _mm(c["kp"][rows] * jnp.exp(glast - c["gcol"][rows]), vn, TN))
                v_prev[h] = vn
                o = o * lax.rsqrt(jnp.mean(o * o, axis=-1, keepdims=True) + EPS) * dn_norm
                za = za_ref[0, out_rows, h * DV_DN:(h + 1) * DV_DN].astype(F32)
                oa_ref[0, out_rows, h * DV_DN:(h + 1) * DV_DN] = (o * _silu(za)).astype(oa_ref.dtype)
    for h in heads:
        s_ref[h] = state[h]


def _delta_rule(qkv, za, small, conv_w, par, *, ts=256):
    b, t, _ = qkv.shape
    assert t % ts == 0 and ts % (2 * CHUNK) == 0
    blk = lambda width: pl.BlockSpec((1, ts, width), lambda bi, j: (bi, j, 0))
    fixed = lambda bi, j: (0, 0)
    return pl.pallas_call(
        functools.partial(_delta_body, ts=ts),
        grid=(b, t // ts),
        in_specs=[blk(W_DN), blk(H_DN * DV_DN), blk(LANES),
                  pl.BlockSpec(conv_w.shape, fixed), pl.BlockSpec(par.shape, fixed)],
        out_specs=[blk(H_DN * DV_DN), blk(LANES), blk(LANES)],
        out_shape=[jax.ShapeDtypeStruct((b, t, H_DN * DV_DN), BF16),
                   jax.ShapeDtypeStruct((b, t, LANES), BF16),
                   jax.ShapeDtypeStruct((b, t, LANES), BF16)],
        scratch_shapes=[pltpu.VMEM((3 * SUBLANES, W_DN), F32),
                        pltpu.VMEM((H_DN, DK_DN, DV_DN), F32),
                        pltpu.VMEM((1, LANES), F32)],
        compiler_params=_params("parallel", "arbitrary"),
        name="delta_rule",
    )(qkv, za, small, conv_w, par)


def _fox_body(q_ref, k_ref, v_ref, qa_ref, ka_ref, o_ref, m_ref, acc_ref, *, tq, tk):
    i = pl.program_id(1)
    m_ref[...] = jnp.full(m_ref.shape, -jnp.inf, F32)
    acc_ref[...] = jnp.zeros_like(acc_ref)
    head_lanes = [slice(h * D_FOX, (h + 1) * D_FOX) for h in range(H_FOX)]
    lane = lax.broadcasted_iota(I32, (tk, LANES), 1)
    own = [jnp.where((lane >> 3) == h, 1.0, 0.0).astype(BF16) for h in range(H_FOX)]
    ones = jnp.ones((tk, D_FOX), BF16)
    keep = lax.broadcasted_iota(I32, (tq, tk), 1) <= lax.broadcasted_iota(I32, (tq, tk), 0)

    def block(start, diag_offset):
        rows = slice(0 if diag_offset is None else diag_offset, tq)
        ka = ka_ref[0, pl.ds(start, tk), :]
        for h, lanes in enumerate(head_lanes):
            q_aug = jnp.concatenate([q_ref[0, rows, lanes], qa_ref[0, rows, :]], axis=1)
            k_aug = jnp.concatenate([k_ref[0, pl.ds(start, tk), lanes], ka * own[h]], axis=1)
            s = lax.dot_general(q_aug, k_aug, NT, preferred_element_type=F32)
            if diag_offset is not None:
                s = jnp.where(keep[:tq - diag_offset], s, -jnp.inf)
            m_prev = m_ref[h, rows]
            m_next = jnp.maximum(m_prev, jnp.max(s, axis=-1, keepdims=True))
            p = jnp.exp2(s - jnp.concatenate([m_next] * (tk // LANES), axis=1))
            alpha = jnp.exp2(m_prev - m_next)
            v_aug = jnp.concatenate([v_ref[0, pl.ds(start, tk), lanes], ones], axis=1)
            acc_ref[h, rows] = (jnp.concatenate([alpha, alpha], axis=1) * acc_ref[h, rows]
                                + jnp.dot(p.astype(BF16), v_aug, preferred_element_type=F32))
            m_ref[h, rows] = m_next

    def full_block(jb, carry):
        block(pl.multiple_of(jb * tk, tk), None)
        return carry

    lax.fori_loop(0, i * (tq // tk), full_block, 0)
    for d in range(tq // tk):
        block(pl.multiple_of(i * tq + d * tk, tk), d * tk)
    for h, lanes in enumerate(head_lanes):
        acc = acc_ref[h]
        o_ref[0, :, lanes] = (acc[:, :D_FOX] / acc[:, D_FOX:]).astype(o_ref.dtype)


def _fox_attention(qkv, qaug, kaug, *, tq=1024, tk=512):
    b, t, _ = qkv.shape
    hd = H_FOX * D_FOX
    assert t % tq == 0 and tq % tk == 0 and tk % LANES == 0
    return pl.pallas_call(
        functools.partial(_fox_body, tq=tq, tk=tk),
        grid=(b, t // tq),
        in_specs=[pl.BlockSpec((1, tq, hd), lambda bi, i: (bi, i, 0)),
                  pl.BlockSpec((1, t, hd), lambda bi, i: (bi, 0, 1)),
                  pl.BlockSpec((1, t, hd), lambda bi, i: (bi, 0, 2)),
                  pl.BlockSpec((1, tq, LANES), lambda bi, i: (bi, i, 0)),
                  pl.BlockSpec((1, t, LANES), lambda bi, i: (bi, 0, 0))],
        out_specs=pl.BlockSpec((1, tq, hd), lambda bi, i: (bi, i, 0)),
        out_shape=jax.ShapeDtypeStruct((b, t, hd), BF16),
        scratch_shapes=[pltpu.VMEM((H_FOX, tq, LANES), F32), pltpu.VMEM((H_FOX, tq, 2 * D_FOX), F32)],
        compiler_params=_params("parallel", "arbitrary"),
        name="fox_attention",
    )(qkv, qkv, qkv, qaug, kaug)


def _conv_body(cur_ref, halo_ref, dw_ref, vec_ref, o_ref, z_ref, zs_ref, y_ref, *, tt, hrows, rchunk):
    i = pl.program_id(1)
    cur = cur_ref[0].astype(F32)
    z_ref[hrows:hrows + tt, :] = cur[:, :C_CONV] * _sigmoid(cur[:, C_CONV:])
    hal = halo_ref[0].astype(F32)
    zh = hal[:, :C_CONV] * _sigmoid(hal[:, C_CONV:])
    z_ref[0:hrows, :] = jnp.where(i > 0, zh, 0.0)

    sub = SUBLANES
    span = tt + hrows - sub
    for phase in range(1, sub):
        zs_ref[phase - 1, 0:span, :] = z_ref[phase:phase + span, :]

    for lg in range(C_CONV // LANES):
        lanes = slice(lg * LANES, (lg + 1) * LANES)
        for r0 in range(0, tt, rchunk):
            acc = jnp.broadcast_to(vec_ref[0:1, lanes], (rchunk, LANES))
            for k in range(CONV_WIDTH):
                off = hrows - (CONV_WIDTH - 1) + k
                phase, base = off % sub, off - off % sub + r0
                if phase == 0:
                    tap = z_ref[base:base + rchunk, lanes]
                else:
                    tap = zs_ref[phase - 1, base:base + rchunk, lanes]
                acc = acc + dw_ref[k:k + 1, lanes] * tap
            y_ref[r0:r0 + rchunk, lanes] = acc

    y = y_ref[...]
    mu = jnp.mean(y, axis=-1, keepdims=True)
    yc = y - mu
    var = jnp.mean(yc * yc, axis=-1, keepdims=True)
    yn = yc * lax.rsqrt(var + EPS) * vec_ref[1:2, :] + vec_ref[2:3, :]
    o_ref[0] = _silu(yn).astype(o_ref.dtype)


def _conv_module(uc, dw, vec, *, tt=256, hrows=32, rchunk=64):
    b, t, _ = uc.shape
    assert t % tt == 0 and tt % hrows == 0 and hrows >= CONV_WIDTH - 1
    ratio = tt // hrows
    return pl.pallas_call(
        functools.partial(_conv_body, tt=tt, hrows=hrows, rchunk=rchunk),
        grid=(b, t // tt),
        in_specs=[pl.BlockSpec((1, tt, 2 * C_CONV), lambda bi, i: (bi, i, 0)),
                  pl.BlockSpec((1, hrows, 2 * C_CONV), lambda bi, i: (bi, jnp.maximum(i * ratio - 1, 0), 0)),
                  pl.BlockSpec(dw.shape, lambda bi, i: (0, 0)),
                  pl.BlockSpec(vec.shape, lambda bi, i: (0, 0))],
        out_specs=pl.BlockSpec((1, tt, C_CONV), lambda bi, i: (bi, i, 0)),
        out_shape=jax.ShapeDtypeStruct((b, t, C_CONV), BF16),
        scratch_shapes=[pltpu.VMEM((tt + hrows, C_CONV), F32), pltpu.VMEM((7, tt + hrows, C_CONV), F32),
                        pltpu.VMEM((tt, C_CONV), F32)],
        compiler_params=_params("parallel", "parallel"),
        name="conv_module",
    )(uc, uc, dw, vec)


def _merge_body(oa_ref, ob_ref, zc_ref, gates_ref, x_ref, wa_ref, wb_ref, wc_ref, wo_ref, g_ref,
                wr_ref, br_ref, xn_ref, hp_ref, route_ref, meta_ref, cnt_ref, run_ref, *, tm):
    step = pl.program_id(0)
    d = x_ref.shape[-1]

    @pl.when(step == 0)
    def _():
        run_ref[...] = jnp.zeros_like(run_ref)

    merged = None
    for idx, (m_ref, w_ref) in enumerate(((oa_ref, wa_ref), (ob_ref, wb_ref), (zc_ref, wc_ref))):
        y = jnp.dot(m_ref[...], w_ref[...], preferred_element_type=F32)
        term = _sigmoid(gates_ref[:, idx * d:(idx + 1) * d].astype(F32)) * y
        merged = term if merged is None else merged + term
    xn = x_ref[...] + jnp.dot(merged.astype(BF16), wo_ref[...], preferred_element_type=F32)
    xn_ref[...] = xn
    h2 = xn * lax.rsqrt(jnp.mean(xn * xn, axis=-1, keepdims=True) + EPS) * g_ref[...]

    hp_ref[...] = _pack_bf16_pairs(h2)

    logits = _mm(h2, wr_ref[...]) + br_ref[...]
    lane = lax.broadcasted_iota(I32, logits.shape, 1)
    big = jnp.int32(4 * LANES)
    in_grp = lane < N_GROUPS
    gl = jnp.where(in_grp, logits, -jnp.inf)
    gmax = jnp.max(gl, axis=-1, keepdims=True)
    grp = jnp.min(jnp.where(gl == gmax, lane, big), axis=-1, keepdims=True)
    p_grp = 1.0 / jnp.sum(jnp.where(in_grp, jnp.exp(gl - gmax), 0.0), axis=-1, keepdims=True)
    e_lo = LANE_EXP + grp * EXPERTS_PER_GROUP
    el = jnp.where((lane >= e_lo) & (lane < e_lo + EXPERTS_PER_GROUP), logits, -jnp.inf)
    v0 = jnp.max(el, axis=-1, keepdims=True)
    i0 = jnp.min(jnp.where(el == v0, lane, big), axis=-1, keepdims=True)
    el1 = jnp.where(lane == i0, -jnp.inf, el)
    v1 = jnp.max(el1, axis=-1, keepdims=True)
    i1 = jnp.min(jnp.where(el1 == v1, lane, big), axis=-1, keepdims=True)
    e1 = jnp.exp(v1 - v0)
    gate0 = p_grp / (1.0 + e1)
    gate1 = p_grp * e1 / (1.0 + e1)

    hot0 = lane == i0
    hot1 = lane == i1
    onehot = jnp.where(hot0 | hot1, 1.0, 0.0)
    ri = lax.broadcasted_iota(I32, (tm, tm), 0)
    cj = lax.broadcasted_iota(I32, (tm, tm), 1)
    prefix = _mm(jnp.where(cj < ri, 1.0, 0.0), onehot) + run_ref[...]
    rank0 = jnp.sum(jnp.where(hot0, prefix, 0.0), axis=-1, keepdims=True)
    rank1 = jnp.sum(jnp.where(hot1, prefix, 0.0), axis=-1, keepdims=True)
    run_ref[...] = run_ref[...] + jnp.sum(onehot, axis=0, keepdims=True)
    cnt_ref[...] = jnp.broadcast_to(run_ref[...], cnt_ref.shape)

    route = jnp.where(lane == R_GATE0, gate0, 0.0)
    route = jnp.where(lane == R_GATE1, gate1, route)
    route = jnp.where(lane == R_EID0, (i0 - LANE_EXP).astype(F32), route)
    route = jnp.where(lane == R_EID1, (i1 - LANE_EXP).astype(F32), route)
    route = jnp.where(lane == R_RANK0, rank0, route)
    route = jnp.where(lane == R_RANK1, rank1, route)
    route_ref[...] = route
    meta_ref[...] = route.T[0:8, :].astype(I32)


def _merge(oa, ob, zc, gates, x2, wa, wb, wc, wo, g, wr, br, *, tm=512):
    n, d = x2.shape
    assert n % tm == 0
    row = lambda i: (i, 0)
    fixed = lambda i: (0, 0)
    full = lambda a: pl.BlockSpec(a.shape, fixed)
    return pl.pallas_call(
        functools.partial(_merge_body, tm=tm),
        grid=(n // tm,),
        in_specs=[pl.BlockSpec((tm, oa.shape[1]), row), pl.BlockSpec((tm, ob.shape[1]), row),
                  pl.BlockSpec((tm, zc.shape[1]), row), pl.BlockSpec((tm, 3 * d), row),
                  pl.BlockSpec((tm, d), row),
                  full(wa), full(wb), full(wc), full(wo), full(g), full(wr), full(br)],
        out_specs=[pl.BlockSpec((tm, d), row), pl.BlockSpec((tm, d // 2), row),
                   pl.BlockSpec((tm, LANES), row), pl.BlockSpec((8, tm), lambda i: (0, i)),
                   pl.BlockSpec((8, LANES), fixed)],
        out_shape=[jax.ShapeDtypeStruct((n, d), F32), jax.ShapeDtypeStruct((n, d // 2), U32),
                   jax.ShapeDtypeStruct((n, LANES), F32), jax.ShapeDtypeStruct((8, n), I32),
                   jax.ShapeDtypeStruct((8, LANES), F32)],
        scratch_shapes=[pltpu.VMEM((1, LANES), F32)],
        compiler_params=_params("arbitrary"),
        name="merge_router",
    )(oa, ob, zc, gates, x2, wa, wb, wc, wo, g, wr, br)


def _sc_mesh():
    return plsc.VectorSubcoreMesh(core_axis_name="c", subcore_axis_name="s")


def _sc_worker_base(per_worker):
    return (lax.axis_index("s") * SC_CORES + lax.axis_index("c")) * per_worker


def _sc_row_scatter(src, idx, out_rows, *, chunk=128):
    n, d = src.shape
    per_worker = n // (SC_CORES * SC_SUBCORES)
    chunk = min(chunk, per_worker)
    assert idx.shape == (2 * n,) and n % (SC_CORES * SC_SUBCORES) == 0 and per_worker % chunk == 0

    @functools.partial(
        pl.kernel, mesh=_sc_mesh(), out_type=jax.ShapeDtypeStruct((out_rows, d), src.dtype),
        scratch_types=[pltpu.VMEM((chunk,), I32), pltpu.VMEM((chunk, d), src.dtype), pltpu.SemaphoreType.DMA])
    def scatter(src_hbm, idx_hbm, out_hbm, idx_v, rows_v, sem):
        base = _sc_worker_base(per_worker)

        @pl.loop(0, per_worker // chunk)
        def _(step):
            off = base + step * chunk
            pltpu.sync_copy(src_hbm.at[pl.ds(off, chunk)], rows_v)
            for slot in range(2):
                pltpu.sync_copy(idx_hbm.at[pl.ds(slot * n + off, chunk)], idx_v)
                pltpu.async_copy(rows_v, out_hbm.at[idx_v], sem).wait()

    return scatter(src, idx)


def _pack_bf16_pairs(x):
    half = x.shape[-1] // 2
    lo = pltpu.bitcast(x[:, :half].astype(BF16).astype(F32), U32) >> 16
    hi = pltpu.bitcast(x[:, half:].astype(BF16).astype(F32), U32) & jnp.uint32(0xFFFF0000)
    return lo | hi


def _unpack_bf16_pairs(xp):
    return pltpu.bitcast(xp << 16, F32), pltpu.bitcast(xp & jnp.uint32(0xFFFF0000), F32)


def _expert_body(be_ref, valid_ref, nu_ref, xs_ref, w1_ref, w3_ref, w2_ref, ys_ref):
    del be_ref
    step = pl.program_id(0)
    live = step < nu_ref[0]

    @pl.when(jnp.logical_not(live))
    def _():
        ys_ref[...] = jnp.zeros_like(ys_ref)

    @pl.when(live)
    def _():
        row = lax.broadcasted_iota(I32, xs_ref.shape, 0)
        xp = jnp.where(row < valid_ref[step], xs_ref[...], jnp.uint32(0))
        half = xp.shape[-1]
        lo, hi = (part.astype(BF16) for part in _unpack_bf16_pairs(xp))

        def up(w_ref):
            return (jnp.dot(lo, w_ref[0, :half, :].astype(BF16), preferred_element_type=F32)
                    + jnp.dot(hi, w_ref[0, half:, :].astype(BF16), preferred_element_type=F32))

        act = (_silu(up(w1_ref)) * up(w3_ref)).astype(BF16)
        ys_ref[...] = _pack_bf16_pairs(jnp.dot(act, w2_ref[0].astype(BF16), preferred_element_type=F32))


def _experts(blk_eid, blk_valid, n_used, xs, w1, w3, w2, *, rb):
    p, half = xs.shape
    d = 2 * half
    nb = p // rb
    used = lambda i, be, bv, nu: jnp.maximum(jnp.minimum(i, nu[0] - 1), 0)
    wmap = lambda i, be, bv, nu: (be[used(i, be, bv, nu)], 0, 0)
    grid_spec = pltpu.PrefetchScalarGridSpec(
        num_scalar_prefetch=3,
        grid=(nb,),
        in_specs=[pl.BlockSpec((rb, half), lambda i, be, bv, nu: (used(i, be, bv, nu), 0)),
                  pl.BlockSpec((1, d, D_EXPERT), wmap), pl.BlockSpec((1, d, D_EXPERT), wmap),
                  pl.BlockSpec((1, D_EXPERT, d), wmap)],
        out_specs=pl.BlockSpec((rb, half), lambda i, be, bv, nu: (i, 0)),
    )
    return pl.pallas_call(
        _expert_body,
        grid_spec=grid_spec,
        out_shape=jax.ShapeDtypeStruct((p, half), U32),
        compiler_params=_params("arbitrary"),
        name="moe_experts",
    )(blk_eid, blk_valid, n_used, xs, w1, w3, w2)


def _sc_row_gather(table, idx, *, chunk=128):
    rows, d = idx.shape[0], table.shape[1]
    per_worker = rows // (SC_CORES * SC_SUBCORES)
    chunk = min(chunk, per_worker)
    assert rows % (SC_CORES * SC_SUBCORES) == 0 and per_worker % chunk == 0

    @functools.partial(
        pl.kernel, mesh=_sc_mesh(), out_type=jax.ShapeDtypeStruct((rows, d), table.dtype),
        scratch_types=[pltpu.VMEM((chunk,), I32), pltpu.VMEM((chunk, d), table.dtype), pltpu.SemaphoreType.DMA])
    def gather(table_hbm, idx_hbm, out_hbm, idx_v, rows_v, sem):
        base = _sc_worker_base(per_worker)

        @pl.loop(0, per_worker // chunk)
        def _(step):
            off = base + step * chunk
            pltpu.sync_copy(idx_hbm.at[pl.ds(off, chunk)], idx_v)
            pltpu.async_copy(table_hbm.at[idx_v], rows_v, sem).wait()
            pltpu.sync_copy(rows_v, out_hbm.at[pl.ds(off, chunk)])

    return gather(table, idx)


def _combine_rows(r0_ref, r1_ref, x_ref, route_ref):
    route = route_ref[...]
    half = r0_ref.shape[-1]
    lo0, hi0 = _unpack_bf16_pairs(r0_ref[...])
    lo1, hi1 = _unpack_bf16_pairs(r1_ref[...])
    g0, g1 = route[:, R_GATE0:R_GATE0 + 1], route[:, R_GATE1:R_GATE1 + 1]
    return jnp.concatenate([x_ref[:, :half] + g0 * lo0 + g1 * lo1,
                            x_ref[:, half:] + g0 * hi0 + g1 * hi1], axis=1)


def _final_body(r0_ref, r1_ref, x_ref, route_ref, g_ref, o_ref):
    out = _combine_rows(r0_ref, r1_ref, x_ref, route_ref)
    o_ref[...] = out * lax.rsqrt(jnp.mean(out * out, axis=-1, keepdims=True) + EPS) * g_ref[...]


def _final_combine(rows2, xn, route, g, *, tm):
    n, d = xn.shape
    nt = n // tm
    row = lambda i: (i, 0)
    return pl.pallas_call(
        _final_body,
        grid=(nt,),
        in_specs=[pl.BlockSpec((tm, d // 2), row), pl.BlockSpec((tm, d // 2), lambda i: (i + nt, 0)),
                  pl.BlockSpec((tm, d), row), pl.BlockSpec((tm, LANES), row),
                  pl.BlockSpec((1, d), lambda i: (0, 0))],
        out_specs=pl.BlockSpec((tm, d), row),
        out_shape=jax.ShapeDtypeStruct((n, d), F32),
        compiler_params=_params("parallel"),
        name="moe_combine",
    )(rows2, rows2, xn, route, g)


def _moe_rows(hp, meta, counts, w1, w3, w2, *, layer, rb):
    n = hp.shape[0]
    cnt = counts[0, LANE_EXP:LANE_EXP + N_EXPERTS].astype(I32)
    nblk = (cnt + rb - 1) // rb
    bend = jnp.cumsum(nblk)
    pstart = (bend - nblk) * rb
    nb = (2 * n) // rb + N_EXPERTS
    n_used = bend[-1:].astype(I32)
    blk = jnp.arange(nb, dtype=I32)
    blk_eid = jnp.minimum(jnp.sum(bend[None, :] <= blk[:, None], axis=1), N_EXPERTS - 1).astype(I32)
    experts = jnp.arange(N_EXPERTS, dtype=I32)
    mine = blk_eid[:, None] == experts[None, :]
    blk_valid = jnp.clip(jnp.sum(jnp.where(mine, (cnt + pstart)[None, :], 0), axis=1) - blk * rb, 0, rb)
    eid = meta[R_EID0:R_EID1 + 1]
    first = jnp.sum(jnp.where(eid[None] == experts[:, None, None], pstart[:, None, None], 0), axis=0)
    dest = (first + meta[R_RANK0:R_RANK1 + 1]).reshape(-1)

    xs = _sc_row_scatter(hp, dest, nb * rb)
    ys = _experts(blk_eid + layer * N_EXPERTS, blk_valid.astype(I32), n_used, xs, w1, w3, w2, rb=rb)
    return _sc_row_gather(ys, dest)


def _lane_row(pairs):
    row = jnp.zeros((LANES,), F32)
    for off, vec in pairs:
        row = row.at[off:off + vec.shape[0]].set(vec.astype(F32))
    return row


def kernel(x, norm_mix, w_in, conv_qkv, dn_a_log, dn_dt_bias, dn_norm, fox_bias, conv_dw, conv_dw_b,
           conv_ln_g, conv_ln_b, w_a, w_b, w_c, w_out, norm_ffn, router_group_w, router_group_b,
           router_expert_w, router_expert_b, expert_w1, expert_w3, expert_w2, norm_final):
    b, t, d = x.shape
    n = b * t
    depth = w_in.shape[0]
    qk_dn = H_DN * DK_DN
    in_sizes = (qk_dn, qk_dn, H_DN * DV_DN, H_DN * DV_DN, H_DN, H_DN,
                H_FOX * D_FOX, H_FOX * D_FOX, H_FOX * D_FOX, H_FOX, 2 * C_CONV, d, d, d)
    splits = np.cumsum(in_sizes)[:-1].tolist()
    tm_final, rb = 256, 512

    x_src = x.reshape(n, d)
    w1_all = expert_w1.reshape(depth * N_EXPERTS, d, D_EXPERT)
    w3_all = expert_w3.reshape(depth * N_EXPERTS, d, D_EXPERT)
    w2_all = expert_w2.reshape(depth * N_EXPERTS, D_EXPERT, d)
    for l in range(depth):
        (qa, ka, va, za, ba, aa, qb, kb, vb, fb, uc, ga, gb, gc) = jnp.split(w_in[l], splits, axis=1)
        w_main = jnp.concatenate([qa, ka, va, za, qb * (D_FOX ** -0.5 * LOG2E), kb, vb, uc, ga, gb, gc],
                                 axis=1).astype(BF16)
        w_small = jnp.concatenate([ba, aa, fb, jnp.zeros((d, LANES - 3 * H_DN), F32)], axis=1).astype(BF16)
        outs = _in_proj(x_src, norm_mix[l][None, :], w_main, w_small)
        x2 = outs[0] if l > 0 else x_src
        dn_qkv, za_p, fox_qkv, uc_p, gates, small = outs[-6:]

        par = jnp.stack([_lane_row([(LANE_G, -jnp.exp(dn_a_log[l]))]),
                         _lane_row([(LANE_G, dn_dt_bias[l]), (LANE_F, fox_bias[l])]),
                         _lane_row([(0, dn_norm[l])])] + [jnp.zeros((LANES,), F32)] * 5)
        oa, qaug, kaug = _delta_rule(dn_qkv.reshape(b, t, -1), za_p.reshape(b, t, -1),
                                     small.reshape(b, t, LANES), conv_qkv[l], par)
        ob = _fox_attention(fox_qkv.reshape(b, t, -1), qaug, kaug)
        vec = jnp.stack([conv_dw_b[l], conv_ln_g[l], conv_ln_b[l]] + [jnp.zeros((C_CONV,), F32)] * 5)
        zc = _conv_module(uc_p.reshape(b, t, -1), conv_dw[l], vec)

        w_r = jnp.concatenate([router_group_w[l], jnp.zeros((d, LANE_EXP - N_GROUPS), F32),
                               router_expert_w[l], jnp.zeros((d, LANES - LANE_EXP - N_EXPERTS), F32)], axis=1)
        b_r = _lane_row([(LANE_GRP, router_group_b[l]), (LANE_EXP, router_expert_b[l])])[None, :]
        xn, hp, route, meta, counts = _merge(
            oa.reshape(n, -1), ob.reshape(n, -1), zc.reshape(n, -1), gates, x2,
            w_a[l].astype(BF16), w_b[l].astype(BF16), w_c[l].astype(BF16), w_out[l].astype(BF16),
            norm_ffn[l][None, :], w_r.astype(BF16), b_r)

        x_src = (_moe_rows(hp, meta, counts, w1_all, w3_all, w2_all, layer=l, rb=rb), xn, route)
    return _final_combine(*x_src, norm_final[None, :], tm=tm_final).reshape(b, t, d)
```

```python
import functools

import jax
import jax.numpy as jnp
import numpy as np
from jax import lax
from jax.experimental import pallas as pl
from jax.experimental.pallas import tpu as pltpu
from jax.experimental.pallas import tpu_sc as plsc

F32 = jnp.float32
BF16 = jnp.bfloat16
U32 = jnp.uint32
I32 = jnp.int32

EPS = 1e-6
LOG2E = 1.4426950408889634
LANES = 128
SUBLANES = 8
SC_CORES, SC_SUBCORES = 2, 16
H_DN, DK_DN, DV_DN = 4, 128, 128
SHORT_CONV = 4
CHUNK = 64
H_FOX, D_FOX = 4, 128
C_CONV = 512
CONV_WIDTH = 31
N_GROUPS, EXPERTS_PER_GROUP = 4, 8
N_EXPERTS = N_GROUPS * EXPERTS_PER_GROUP
D_EXPERT = 256

W_DN = 3 * H_DN * DK_DN
W_FOX = 3 * H_FOX * D_FOX
VMEM_LIMIT = 56 * 1024 * 1024

LANE_BETA, LANE_G, LANE_F = 0, 4, 8
LANE_GRP, LANE_EXP = 0, 32
R_GATE0, R_GATE1, R_EID0, R_EID1, R_RANK0, R_RANK1 = 0, 1, 2, 3, 4, 5

NN = (((1,), (0,)), ((), ()))
NT = (((1,), (1,)), ((), ()))
TN = (((0,), (0,)), ((), ()))


def _mm(a, b, dims=NN):
    return lax.dot_general(a.astype(BF16), b.astype(BF16), dims, preferred_element_type=F32)


def _sigmoid(x):
    return 0.5 * jnp.tanh(0.5 * x) + 0.5


def _silu(x):
    return x * _sigmoid(x)


def _params(*sem):
    return pltpu.CompilerParams(dimension_semantics=sem, vmem_limit_bytes=VMEM_LIMIT)


def _in_proj_body(*refs, col_chunk, combine, tiles_per_seq, hrows, rchunk):
    if combine:
        r0_ref, r1_ref, xn_ref, route_ref, g_ref, w_ref, ws_ref, dw_ref, vec_ref, x_out_ref = refs[:10]
        x = _combine_rows(r0_ref, r1_ref, xn_ref, route_ref)
        x_out_ref[...] = x
        rest = refs[10:]
    else:
        x_ref, g_ref, w_ref, ws_ref, dw_ref, vec_ref = refs[:6]
        x = x_ref[...]
        rest = refs[6:]
    dn_ref, za_ref, fox_ref, zc_ref, gates_ref, small_ref, z_ref, zs_ref, y_ref = rest
    tm = x.shape[0]
    h = x * lax.rsqrt(jnp.mean(x * x, axis=-1, keepdims=True) + EPS) * g_ref[...]
    hb = h.astype(BF16)

    def project(col, width):
        return jnp.dot(hb, w_ref[:, col:col + width], preferred_element_type=F32)

    col = 0
    for ref in (dn_ref, za_ref, fox_ref):
        for c in range(0, ref.shape[-1], col_chunk):
            ref[:, c:c + col_chunk] = project(col + c, col_chunk).astype(ref.dtype)
        col += ref.shape[-1]

    first = pl.program_id(0) % tiles_per_seq == 0

    @pl.when(first)
    def _():
        z_ref[0:hrows, :] = jnp.zeros((hrows, C_CONV), F32)

    @pl.when(jnp.logical_not(first))
    def _():
        z_ref[0:hrows, :] = z_ref[tm:tm + hrows, :]

    z_ref[hrows:hrows + tm, :] = project(col, C_CONV) * _sigmoid(project(col + C_CONV, C_CONV))
    col += 2 * C_CONV
    sub = SUBLANES
    span = tm + hrows - sub
    for phase in range(1, sub):
        zs_ref[phase - 1, 0:span, :] = z_ref[phase:phase + span, :]

    def conv_lanes(lg):
        lanes = slice(lg * LANES, (lg + 1) * LANES)
        for r0 in range(0, tm, rchunk):
            acc = jnp.broadcast_to(vec_ref[0:1, lanes], (rchunk, LANES))
            for k in range(CONV_WIDTH):
                off = hrows - (CONV_WIDTH - 1) + k
                phase, base = off % sub, off - off % sub + r0
                if phase == 0:
                    tap = z_ref[base:base + rchunk, lanes]
                else:
                    tap = zs_ref[phase - 1, base:base + rchunk, lanes]
                acc = acc + dw_ref[k:k + 1, lanes] * tap
            y_ref[r0:r0 + rchunk, lanes] = acc

    lane_groups = C_CONV // LANES
    for idx, c in enumerate(range(0, gates_ref.shape[-1], col_chunk)):
        gates_ref[:, c:c + col_chunk] = project(col + c, col_chunk).astype(gates_ref.dtype)
        if idx < lane_groups:
            conv_lanes(idx)
    for lg in range(gates_ref.shape[-1] // col_chunk, lane_groups):
        conv_lanes(lg)
    small_ref[...] = jnp.dot(hb, ws_ref[...], preferred_element_type=F32)

    y = y_ref[...]
    mu = jnp.mean(y, axis=-1, keepdims=True)
    yc = y - mu
    var = jnp.mean(yc * yc, axis=-1, keepdims=True)
    yn = yc * lax.rsqrt(var + EPS) * vec_ref[1:2, :] + vec_ref[2:3, :]
    zc_ref[...] = _silu(yn).astype(zc_ref.dtype)


def _in_proj(x_src, g, w, ws, dw, vec, *, seq_len, tm=256, col_chunk=512, hrows=32, rchunk=64):
    combine = isinstance(x_src, tuple)
    n, d = (x_src[1] if combine else x_src).shape
    nt = n // tm
    widths = (W_DN, H_DN * DV_DN, W_FOX, C_CONV, 3 * d)
    assert w.shape == (d, sum(widths) + C_CONV) and n % tm == 0 and seq_len % tm == 0
    assert hrows >= CONV_WIDTH - 1 and hrows % SUBLANES == 0 and tm % rchunk == 0
    row = lambda i: (i, 0)
    fixed = lambda i: (0, 0)
    out_shape = [jax.ShapeDtypeStruct((n, wd), BF16) for wd in widths] + [jax.ShapeDtypeStruct((n, LANES), F32)]
    out_specs = [pl.BlockSpec((tm, wd), row) for wd in widths] + [pl.BlockSpec((tm, LANES), row)]
    if combine:
        rows2, xn, route = x_src
        args = (rows2, rows2, xn, route)
        in_specs = [pl.BlockSpec((tm, d // 2), row), pl.BlockSpec((tm, d // 2), lambda i: (i + nt, 0)),
                    pl.BlockSpec((tm, d), row), pl.BlockSpec((tm, LANES), row)]
        out_shape = [jax.ShapeDtypeStruct((n, d), F32)] + out_shape
        out_specs = [pl.BlockSpec((tm, d), row)] + out_specs
    else:
        args = (x_src,)
        in_specs = [pl.BlockSpec((tm, d), row)]
    return pl.pallas_call(
        functools.partial(_in_proj_body, col_chunk=col_chunk, combine=combine, tiles_per_seq=seq_len // tm,
                          hrows=hrows, rchunk=rchunk),
        grid=(nt,),
        in_specs=in_specs + [pl.BlockSpec((1, d), fixed),
                             pl.BlockSpec(w.shape, fixed, pipeline_mode=pl.Buffered(1)),
                             pl.BlockSpec(ws.shape, fixed, pipeline_mode=pl.Buffered(1)),
                             pl.BlockSpec(dw.shape, fixed), pl.BlockSpec(vec.shape, fixed)],
        out_specs=out_specs,
        out_shape=out_shape,
        scratch_shapes=[pltpu.VMEM((tm + hrows, C_CONV), F32),
                        pltpu.VMEM((SUBLANES - 1, tm + hrows, C_CONV), F32),
                        pltpu.VMEM((tm, C_CONV), F32)],
        compiler_params=_params("arbitrary"),
        name="in_proj",
    )(*args, g, w, ws, dw, vec)


def _softplus_parts(z):
    t = jnp.log1p(jnp.exp(-jnp.abs(z)))
    return jnp.maximum(z, 0.0) + t, -(jnp.maximum(-z, 0.0) + t)


def _delta_body(qkv_ref, za_ref, sm_ref, cw_ref, par_ref, oa_ref, qaug_ref, kaug_ref,
                xs_ref, s_ref, carry_ref, *, ts):
    j = pl.program_id(1)
    halo = SUBLANES
    pack = 2 * SUBLANES

    @pl.when(j == 0)
    def _():
        xs_ref[0:halo, :] = jnp.zeros((halo, W_DN), F32)
        s_ref[...] = jnp.zeros_like(s_ref)
        carry_ref[...] = jnp.zeros_like(carry_ref)

    @pl.when(j > 0)
    def _():
        xs_ref[0:halo, :] = xs_ref[2 * halo:3 * halo, :]

    xb = qkv_ref[0]
    xs_ref[halo:2 * halo, :] = qkv_ref[0, 0:pack, :].astype(F32)[0:halo]
    xs_ref[2 * halo:3 * halo, :] = qkv_ref[0, ts - pack:ts, :].astype(F32)[pack - halo:pack]

    lag = lax.broadcasted_iota(I32, (ts, ts), 0) - lax.broadcasted_iota(I32, (ts, ts), 1)
    shifted = [jnp.dot(jnp.where(lag == s, 1.0, 0.0).astype(BF16), xb, preferred_element_type=F32)
               for s in range(1, SHORT_CONV)]
    head_row = lax.broadcasted_iota(I32, (halo, LANES), 0)

    def conv_silu(lane0):
        lanes = slice(lane0, lane0 + LANES)
        acc = cw_ref[SHORT_CONV - 1:SHORT_CONV, lanes] * xb[:, lanes].astype(F32)
        for s in range(1, SHORT_CONV):
            acc = acc + cw_ref[SHORT_CONV - 1 - s:SHORT_CONV - s, lanes] * shifted[s - 1][:, lanes]
        head = acc[0:halo]
        for s in range(1, SHORT_CONV):
            prev = jnp.where(head_row < s, xs_ref[halo - s:2 * halo - s, lanes], 0.0)
            head = head + cw_ref[SHORT_CONV - 1 - s:SHORT_CONV - s, lanes] * prev
        return _silu(jnp.concatenate([head, acc[halo:]], axis=0))

    def l2n(a):
        return a * lax.rsqrt(jnp.sum(a * a, axis=-1, keepdims=True) + EPS)

    sm = sm_ref[0]
    lane = lax.broadcasted_iota(I32, sm.shape, 1)
    sp, logsig = _softplus_parts(sm + par_ref[1:2, :])
    vals = jnp.where(lane < LANE_G, _sigmoid(sm),
                     jnp.where(lane < LANE_F, par_ref[0:1, :] * sp,
                               jnp.where(lane < LANE_F + H_FOX, logsig, 0.0)))
    row = lax.broadcasted_iota(I32, (ts, ts), 0)
    colm = lax.broadcasted_iota(I32, (ts, ts), 1)
    log_chunk = CHUNK.bit_length() - 1
    causal = (row >= colm) & ((row >> log_chunk) == (colm >> log_chunk))

    hi = vals.astype(BF16)
    rem = vals - hi.astype(F32)
    mid = rem.astype(BF16)
    lo = (rem - mid.astype(F32)).astype(BF16)
    pieces = jnp.concatenate([hi, mid, lo], axis=-1)

    def cumsum(mask):
        y = jnp.dot(jnp.where(mask, 1.0, 0.0).astype(BF16), pieces, preferred_element_type=F32)
        return (y[:, :LANES] + y[:, LANES:2 * LANES]) + y[:, 2 * LANES:]

    ccum = cumsum(row >= colm) + carry_ref[...]
    gcum = cumsum(causal)
    carry_ref[...] = ccum[ts - 1:ts, :]
    gcum_t = gcum.T

    cl = ccum * LOG2E
    c_hi = cl.astype(BF16)
    c_rem = cl - c_hi.astype(F32)
    c_mid = c_rem.astype(BF16)
    c_lo = (c_rem - c_mid.astype(F32)).astype(BF16)
    c_pieces = jnp.concatenate([c_hi, c_mid, c_lo], axis=-1)
    pr = lax.broadcasted_iota(I32, (3 * LANES, LANES), 0)
    pc = lax.broadcasted_iota(I32, (3 * LANES, LANES), 1)
    src_lane, piece = pr & (LANES - 1), pr >> (LANES.bit_length() - 1)
    owned = (src_lane >= LANE_F) & (src_lane < LANE_F + H_FOX) & ((pc >> 3) == src_lane - LANE_F)
    place_q = jnp.where(owned & ((pc & 7) == piece), 1.0, 0.0).astype(BF16)
    place_k = jnp.where(owned & ((pc & 7) == piece + 3), -1.0, 0.0).astype(BF16)
    slot = lane & 7
    in_heads = lane < 8 * H_FOX
    ones_q = jnp.where(in_heads & (slot >= 3) & (slot < 6), 1.0, 0.0)
    ones_k = jnp.where(in_heads & (slot < 3), 1.0, 0.0)
    qaug_ref[0] = (jnp.dot(c_pieces, place_q, preferred_element_type=F32) + ones_q).astype(BF16)
    kaug_ref[0] = (jnp.dot(c_pieces, place_k, preferred_element_type=F32) + ones_k).astype(BF16)

    scale = DK_DN ** -0.5
    dn_norm = par_ref[2:3, :]

    pw = 2 * CHUNK
    prow = lax.broadcasted_iota(I32, (pw, pw), 0)
    pcol = lax.broadcasted_iota(I32, (pw, pw), 1)
    same = (prow >> log_chunk) == (pcol >> log_chunk)
    causal_p = (prow >= pcol) & same
    strict_p = (prow > pcol) & same
    levels = []
    s = 1
    while s < CHUNK:
        levels.append(((prow >> s.bit_length()) == (pcol >> s.bit_length()))
                      & ((prow & s) != 0) & ((pcol & s) == 0))
        s *= 2

    heads = range(H_DN)
    pairs = range(ts // pw)
    q, k, v = [], [], []
    for h in heads:
        q.append(l2n(conv_silu(h * DK_DN)))
        k.append(l2n(conv_silu(H_DN * DK_DN + h * DK_DN)))
        v.append(conv_silu(2 * H_DN * DK_DN + h * DV_DN))

    ctx = []
    for h in heads:
        for p in pairs:
            pr = slice(p * pw, (p + 1) * pw)
            gcol = gcum[pr, LANE_G + h:LANE_G + h + 1]
            grow = gcum_t[LANE_G + h:LANE_G + h + 1, pr]
            beta = vals[pr, LANE_BETA + h:LANE_BETA + h + 1]
            decay = jnp.where(causal_p, jnp.exp(jnp.where(causal_p, gcol - grow, 0.0)), 0.0)
            egc = jnp.exp(gcol)
            kp = k[h][pr]
            kb = kp * beta
            ctx.append(dict(
                h=h, p=p, gcol=gcol, kp=kp,
                a=jnp.where(strict_p, _mm(kb, kp, NT) * decay, 0.0),
                qk=jnp.where(causal_p, _mm(q[h][pr] * scale, kp, NT) * decay, 0.0),
                rhs=jnp.concatenate([v[h][pr] * beta, kb * egc], axis=-1),
                qg=q[h][pr] * (scale * egc)))

    for c in ctx:
        c["n"] = -jnp.where(levels[0], c["a"], 0.0)
    for level in levels[1:]:
        for c in ctx:
            m = jnp.where(level, c["a"], 0.0)
            c["y"] = m + _mm(c["n"], m)
        for c in ctx:
            c["n"] = c["n"] - (c["y"] + _mm(c["y"], c["n"]))
    for c in ctx:
        c["sol"] = c["rhs"] + _mm(c["n"], c["rhs"])

    state = [s_ref[h] for h in heads]
    for p in pairs:
        group = [c for c in ctx if c["p"] == p]
        v_prev = [None] * H_DN
        for ch in range(2):
            rows = slice(ch * CHUNK, (ch + 1) * CHUNK)
            out_rows = slice(p * pw + ch * CHUNK, p * pw + (ch + 1) * CHUNK)
            on_state = [_mm(jnp.concatenate([c["sol"][rows, DV_DN:], c["qg"][rows]], axis=0), state[c["h"]])
                        for c in group]
            for c, ws_qs in zip(group, on_state):
                h = c["h"]
                vn = c["sol"][rows, :DV_DN] - ws_qs[:CHUNK]
                v_pair = jnp.concatenate([vn, jnp.zeros_like(vn)] if ch == 0 else [v_prev[h], vn], axis=0)
                o = ws_qs[CHUNK:] + _mm(c["qk"][rows], v_pair)
                glast = c["gcol"][(ch + 1) * CHUNK - 1:(ch + 1) * CHUNK]
                state[h] = (state[h] * jnp.exp(glast)
                            + _mm(c["kp"][rows] * jnp.exp(glast - c["gcol"][rows]), vn, TN))
                v_prev[h] = vn
                o = o * lax.rsqrt(jnp.mean(o * o, axis=-1, keepdims=True) + EPS) * dn_norm
                za = za_ref[0, out_rows, h * DV_DN:(h + 1) * DV_DN].astype(F32)
                oa_ref[0, out_rows, h * DV_DN:(h + 1) * DV_DN] = (o * _silu(za)).astype(oa_ref.dtype)
    for h in heads:
        s_ref[h] = state[h]


def _delta_rule(qkv, za, small, conv_w, par, *, ts=256):
    b, t, _ = qkv.shape
    assert t % ts == 0 and ts % (2 * CHUNK) == 0
    blk = lambda width: pl.BlockSpec((1, ts, width), lambda bi, j: (bi, j, 0))
    fixed = lambda bi, j: (0, 0)
    return pl.pallas_call(
        functools.partial(_delta_body, ts=ts),
        grid=(b, t // ts),
        in_specs=[blk(W_DN), blk(H_DN * DV_DN), blk(LANES),
                  pl.BlockSpec(conv_w.shape, fixed), pl.BlockSpec(par.shape, fixed)],
        out_specs=[blk(H_DN * DV_DN), blk(LANES), blk(LANES)],
        out_shape=[jax.ShapeDtypeStruct((b, t, H_DN * DV_DN), BF16),
                   jax.ShapeDtypeStruct((b, t, LANES), BF16),
                   jax.ShapeDtypeStruct((b, t, LANES), BF16)],
        scratch_shapes=[pltpu.VMEM((3 * SUBLANES, W_DN), F32),
                        pltpu.VMEM((H_DN, DK_DN, DV_DN), F32),
                        pltpu.VMEM((1, LANES), F32)],
        compiler_params=_params("parallel", "arbitrary"),
        name="delta_rule",
    )(qkv, za, small, conv_w, par)


def _fox_body(q_ref, k_ref, v_ref, qa_ref, ka_ref, o_ref, m_ref, acc_ref, *, tq, tk):
    i = pl.program_id(1)
    m_ref[...] = jnp.full(m_ref.shape, -jnp.inf, F32)
    acc_ref[...] = jnp.zeros_like(acc_ref)
    head_lanes = [slice(h * D_FOX, (h + 1) * D_FOX) for h in range(H_FOX)]
    lane = lax.broadcasted_iota(I32, (tk, LANES), 1)
    own = [jnp.where((lane >> 3) == h, 1.0, 0.0).astype(BF16) for h in range(H_FOX)]
    ones = jnp.ones((tk, D_FOX), BF16)
    keep = lax.broadcasted_iota(I32, (tq, tk), 1) <= lax.broadcasted_iota(I32, (tq, tk), 0)

    def block(start, diag_offset):
        rows = slice(0 if diag_offset is None else diag_offset, tq)
        ka = ka_ref[0, pl.ds(start, tk), :]
        for h, lanes in enumerate(head_lanes):
            q_aug = jnp.concatenate([q_ref[0, rows, lanes], qa_ref[0, rows, :]], axis=1)
            k_aug = jnp.concatenate([k_ref[0, pl.ds(start, tk), lanes], ka * own[h]], axis=1)
            s = lax.dot_general(q_aug, k_aug, NT, preferred_element_type=F32)
            if diag_offset is not None:
                s = jnp.where(keep[:tq - diag_offset], s, -jnp.inf)
            m_prev = m_ref[h, rows]
            m_next = jnp.maximum(m_prev, jnp.max(s, axis=-1, keepdims=True))
            p = jnp.exp2(s - jnp.concatenate([m_next] * (tk // LANES), axis=1))
            alpha = jnp.exp2(m_prev - m_next)
            v_aug = jnp.concatenate([v_ref[0, pl.ds(start, tk), lanes], ones], axis=1)
            acc_ref[h, rows] = (jnp.concatenate([alpha, alpha], axis=1) * acc_ref[h, rows]
                                + jnp.dot(p.astype(BF16), v_aug, preferred_element_type=F32))
            m_ref[h, rows] = m_next

    def full_block(jb, carry):
        block(pl.multiple_of(jb * tk, tk), None)
        return carry

    lax.fori_loop(0, i * (tq // tk), full_block, 0)
    for d in range(tq // tk):
        block(pl.multiple_of(i * tq + d * tk, tk), d * tk)
    for h, lanes in enumerate(head_lanes):
        acc = acc_ref[h]
        o_ref[0, :, lanes] = (acc[:, :D_FOX] / acc[:, D_FOX:]).astype(o_ref.dtype)


def _fox_attention(qkv, qaug, kaug, *, tq=1024, tk=512):
    b, t, _ = qkv.shape
    hd = H_FOX * D_FOX
    assert t % tq == 0 and tq % tk == 0 and tk % LANES == 0
    return pl.pallas_call(
        functools.partial(_fox_body, tq=tq, tk=tk),
        grid=(b, t // tq),
        in_specs=[pl.BlockSpec((1, tq, hd), lambda bi, i: (bi, i, 0)),
                  pl.BlockSpec((1, t, hd), lambda bi, i: (bi, 0, 1)),
                  pl.BlockSpec((1, t, hd), lambda bi, i: (bi, 0, 2)),
                  pl.BlockSpec((1, tq, LANES), lambda bi, i: (bi, i, 0)),
                  pl.BlockSpec((1, t, LANES), lambda bi, i: (bi, 0, 0))],
        out_specs=pl.BlockSpec((1, tq, hd), lambda bi, i: (bi, i, 0)),
        out_shape=jax.ShapeDtypeStruct((b, t, hd), BF16),
        scratch_shapes=[pltpu.VMEM((H_FOX, tq, LANES), F32), pltpu.VMEM((H_FOX, tq, 2 * D_FOX), F32)],
        compiler_params=_params("parallel", "arbitrary"),
        name="fox_attention",
    )(qkv, qkv, qkv, qaug, kaug)


def _merge_body(oa_ref, ob_ref, zc_ref, gates_ref, x_ref, wa_ref, wb_ref, wc_ref, wo_ref, g_ref,
                wr_ref, br_ref, xn_ref, hp_ref, route_ref, meta_ref, cnt_ref, run_ref, *, tm):
    step = pl.program_id(0)
    d = x_ref.shape[-1]

    @pl.when(step == 0)
    def _():
        run_ref[...] = jnp.zeros_like(run_ref)

    merged = None
    for idx, (m_ref, w_ref) in enumerate(((oa_ref, wa_ref), (ob_ref, wb_ref), (zc_ref, wc_ref))):
        y = jnp.dot(m_ref[...], w_ref[...], preferred_element_type=F32)
        term = _sigmoid(gates_ref[:, idx * d:(idx + 1) * d].astype(F32)) * y
        merged = term if merged is None else merged + term
    xn = x_ref[...] + jnp.dot(merged.astype(BF16), wo_ref[...], preferred_element_type=F32)
    xn_ref[...] = xn
    h2 = xn * lax.rsqrt(jnp.mean(xn * xn, axis=-1, keepdims=True) + EPS) * g_ref[...]

    hp_ref[...] = _pack_bf16_pairs(h2)

    logits = _mm(h2, wr_ref[...]) + br_ref[...]
    lane = lax.broadcasted_iota(I32, logits.shape, 1)
    big = jnp.int32(4 * LANES)
    in_grp = lane < N_GROUPS
    gl = jnp.where(in_grp, logits, -jnp.inf)
    gmax = jnp.max(gl, axis=-1, keepdims=True)
    grp = jnp.min(jnp.where(gl == gmax, lane, big), axis=-1, keepdims=True)
    p_grp = 1.0 / jnp.sum(jnp.where(in_grp, jnp.exp(gl - gmax), 0.0), axis=-1, keepdims=True)
    e_lo = LANE_EXP + grp * EXPERTS_PER_GROUP
    el = jnp.where((lane >= e_lo) & (lane < e_lo + EXPERTS_PER_GROUP), logits, -jnp.inf)
    v0 = jnp.max(el, axis=-1, keepdims=True)
    i0 = jnp.min(jnp.where(el == v0, lane, big), axis=-1, keepdims=True)
    el1 = jnp.where(lane == i0, -jnp.inf, el)
    v1 = jnp.max(el1, axis=-1, keepdims=True)
    i1 = jnp.min(jnp.where(el1 == v1, lane, big), axis=-1, keepdims=True)
    e1 = jnp.exp(v1 - v0)
    gate0 = p_grp / (1.0 + e1)
    gate1 = p_grp * e1 / (1.0 + e1)

    hot0 = lane == i0
    hot1 = lane == i1
    onehot = jnp.where(hot0 | hot1, 1.0, 0.0)
    ri = lax.broadcasted_iota(I32, (tm, tm), 0)
    cj = lax.broadcasted_iota(I32, (tm, tm), 1)
    prefix = _mm(jnp.where(cj < ri, 1.0, 0.0), onehot) + run_ref[...]
    rank0 = jnp.sum(jnp.where(hot0, prefix, 0.0), axis=-1, keepdims=True)
    rank1 = jnp.sum(jnp.where(hot1, prefix, 0.0), axis=-1, keepdims=True)
    run_ref[...] = run_ref[...] + jnp.sum(onehot, axis=0, keepdims=True)
    cnt_ref[...] = jnp.broadcast_to(run_ref[...], cnt_ref.shape)

    route = jnp.where(lane == R_GATE0, gate0, 0.0)
    route = jnp.where(lane == R_GATE1, gate1, route)
    route = jnp.where(lane == R_EID0, (i0 - LANE_EXP).astype(F32), route)
    route = jnp.where(lane == R_EID1, (i1 - LANE_EXP).astype(F32), route)
    route = jnp.where(lane == R_RANK0, rank0, route)
    route = jnp.where(lane == R_RANK1, rank1, route)
    route_ref[...] = route
    meta_ref[...] = route.T[0:8, :].astype(I32)


def _merge(oa, ob, zc, gates, x2, wa, wb, wc, wo, g, wr, br, *, tm=512):
    n, d = x2.shape
    assert n % tm == 0
    row = lambda i: (i, 0)
    fixed = lambda i: (0, 0)
    full = lambda a: pl.BlockSpec(a.shape, fixed)
    return pl.pallas_call(
        functools.partial(_merge_body, tm=tm),
        grid=(n // tm,),
        in_specs=[pl.BlockSpec((tm, oa.shape[1]), row), pl.BlockSpec((tm, ob.shape[1]), row),
                  pl.BlockSpec((tm, zc.shape[1]), row), pl.BlockSpec((tm, 3 * d), row),
                  pl.BlockSpec((tm, d), row),
                  full(wa), full(wb), full(wc), full(wo), full(g), full(wr), full(br)],
        out_specs=[pl.BlockSpec((tm, d), row), pl.BlockSpec((tm, d // 2), row),
                   pl.BlockSpec((tm, LANES), row), pl.BlockSpec((8, tm), lambda i: (0, i)),
                   pl.BlockSpec((8, LANES), fixed)],
        out_shape=[jax.ShapeDtypeStruct((n, d), F32), jax.ShapeDtypeStruct((n, d // 2), U32),
                   jax.ShapeDtypeStruct((n, LANES), F32), jax.ShapeDtypeStruct((8, n), I32),
                   jax.ShapeDtypeStruct((8, LANES), F32)],
        scratch_shapes=[pltpu.VMEM((1, LANES), F32)],
        compiler_params=_params("arbitrary"),
        name="merge_router",
    )(oa, ob, zc, gates, x2, wa, wb, wc, wo, g, wr, br)


def _sc_mesh():
    return plsc.VectorSubcoreMesh(core_axis_name="c", subcore_axis_name="s")


def _sc_worker_base(per_worker):
    return (lax.axis_index("s") * SC_CORES + lax.axis_index("c")) * per_worker


def _sc_row_scatter(src, idx, out_rows, *, chunk=128):
    n, d = src.shape
    per_worker = n // (SC_CORES * SC_SUBCORES)
    chunk = min(chunk, per_worker)
    assert idx.shape == (2 * n,) and n % (SC_CORES * SC_SUBCORES) == 0 and per_worker % chunk == 0

    @functools.partial(
        pl.kernel, mesh=_sc_mesh(), out_type=jax.ShapeDtypeStruct((out_rows, d), src.dtype),
        scratch_types=[pltpu.VMEM((chunk,), I32), pltpu.VMEM((chunk, d), src.dtype), pltpu.SemaphoreType.DMA])
    def scatter(src_hbm, idx_hbm, out_hbm, idx_v, rows_v, sem):
        base = _sc_worker_base(per_worker)

        @pl.loop(0, per_worker // chunk)
        def _(step):
            off = base + step * chunk
            pltpu.sync_copy(src_hbm.at[pl.ds(off, chunk)], rows_v)
            for slot in range(2):
                pltpu.sync_copy(idx_hbm.at[pl.ds(slot * n + off, chunk)], idx_v)
                pltpu.async_copy(rows_v, out_hbm.at[idx_v], sem).wait()

    return scatter(src, idx)


def _pack_bf16_pairs(x):
    half = x.shape[-1] // 2
    lo = pltpu.bitcast(x[:, :half].astype(BF16).astype(F32), U32) >> 16
    hi = pltpu.bitcast(x[:, half:].astype(BF16).astype(F32), U32) & jnp.uint32(0xFFFF0000)
    return lo | hi


def _unpack_bf16_pairs(xp):
    return pltpu.bitcast(xp << 16, F32), pltpu.bitcast(xp & jnp.uint32(0xFFFF0000), F32)


def _expert_body(be_ref, valid_ref, nu_ref, xs_ref, w1_ref, w3_ref, w2_ref, ys_ref):
    del be_ref
    step = pl.program_id(0)
    live = step < nu_ref[0]

    @pl.when(jnp.logical_not(live))
    def _():
        ys_ref[...] = jnp.zeros_like(ys_ref)

    @pl.when(live)
    def _():
        row = lax.broadcasted_iota(I32, xs_ref.shape, 0)
        xp = jnp.where(row < valid_ref[step], xs_ref[...], jnp.uint32(0))
        half = xp.shape[-1]
        lo, hi = (part.astype(BF16) for part in _unpack_bf16_pairs(xp))

        def up(w_ref):
            return (jnp.dot(lo, w_ref[0, :half, :].astype(BF16), preferred_element_type=F32)
                    + jnp.dot(hi, w_ref[0, half:, :].astype(BF16), preferred_element_type=F32))

        act = (_silu(up(w1_ref)) * up(w3_ref)).astype(BF16)
        ys_ref[...] = _pack_bf16_pairs(jnp.dot(act, w2_ref[0].astype(BF16), preferred_element_type=F32))


def _experts(blk_eid, blk_valid, n_used, xs, w1, w3, w2, *, rb):
    p, half = xs.shape
    d = 2 * half
    nb = p // rb
    used = lambda i, be, bv, nu: jnp.maximum(jnp.minimum(i, nu[0] - 1), 0)
    wmap = lambda i, be, bv, nu: (be[used(i, be, bv, nu)], 0, 0)
    grid_spec = pltpu.PrefetchScalarGridSpec(
        num_scalar_prefetch=3,
        grid=(nb,),
        in_specs=[pl.BlockSpec((rb, half), lambda i, be, bv, nu: (used(i, be, bv, nu), 0)),
                  pl.BlockSpec((1, d, D_EXPERT), wmap), pl.BlockSpec((1, d, D_EXPERT), wmap),
                  pl.BlockSpec((1, D_EXPERT, d), wmap)],
        out_specs=pl.BlockSpec((rb, half), lambda i, be, bv, nu: (i, 0)),
    )
    return pl.pallas_call(
        _expert_body,
        grid_spec=grid_spec,
        out_shape=jax.ShapeDtypeStruct((p, half), U32),
        compiler_params=_params("arbitrary"),
        name="moe_experts",
    )(blk_eid, blk_valid, n_used, xs, w1, w3, w2)


def _sc_row_gather(table, idx, *, chunk=128):
    rows, d = idx.shape[0], table.shape[1]
    per_worker = rows // (SC_CORES * SC_SUBCORES)
    chunk = min(chunk, per_worker)
    assert rows % (SC_CORES * SC_SUBCORES) == 0 and per_worker % chunk == 0

    @functools.partial(
        pl.kernel, mesh=_sc_mesh(), out_type=jax.ShapeDtypeStruct((rows, d), table.dtype),
        scratch_types=[pltpu.VMEM((chunk,), I32), pltpu.VMEM((chunk, d), table.dtype), pltpu.SemaphoreType.DMA])
    def gather(table_hbm, idx_hbm, out_hbm, idx_v, rows_v, sem):
        base = _sc_worker_base(per_worker)

        @pl.loop(0, per_worker // chunk)
        def _(step):
            off = base + step * chunk
            pltpu.sync_copy(idx_hbm.at[pl.ds(off, chunk)], idx_v)
            pltpu.async_copy(table_hbm.at[idx_v], rows_v, sem).wait()
            pltpu.sync_copy(rows_v, out_hbm.at[pl.ds(off, chunk)])

    return gather(table, idx)


def _combine_rows(r0_ref, r1_ref, x_ref, route_ref):
    route = route_ref[...]
    half = r0_ref.shape[-1]
    lo0, hi0 = _unpack_bf16_pairs(r0_ref[...])
    lo1, hi1 = _unpack_bf16_pairs(r1_ref[...])
    g0, g1 = route[:, R_GATE0:R_GATE0 + 1], route[:, R_GATE1:R_GATE1 + 1]
    return jnp.concatenate([x_ref[:, :half] + g0 * lo0 + g1 * lo1,
                            x_ref[:, half:] + g0 * hi0 + g1 * hi1], axis=1)


def _final_body(r0_ref, r1_ref, x_ref, route_ref, g_ref, o_ref):
    out = _combine_rows(r0_ref, r1_ref, x_ref, route_ref)
    o_ref[...] = out * lax.rsqrt(jnp.mean(out * out, axis=-1, keepdims=True) + EPS) * g_ref[...]


def _final_combine(rows2, xn, route, g, *, tm):
    n, d = xn.shape
    nt = n // tm
    row = lambda i: (i, 0)
    return pl.pallas_call(
        _final_body,
        grid=(nt,),
        in_specs=[pl.BlockSpec((tm, d // 2), row), pl.BlockSpec((tm, d // 2), lambda i: (i + nt, 0)),
                  pl.BlockSpec((tm, d), row), pl.BlockSpec((tm, LANES), row),
                  pl.BlockSpec((1, d), lambda i: (0, 0))],
        out_specs=pl.BlockSpec((tm, d), row),
        out_shape=jax.ShapeDtypeStruct((n, d), F32),
        compiler_params=_params("parallel"),
        name="moe_combine",
    )(rows2, rows2, xn, route, g)


def _moe_rows(hp, meta, counts, w1, w3, w2, *, layer, rb):
    n = hp.shape[0]
    cnt = counts[0, LANE_EXP:LANE_EXP + N_EXPERTS].astype(I32)
    nblk = (cnt + rb - 1) // rb
    bend = jnp.cumsum(nblk)
    pstart = (bend - nblk) * rb
    nb = (2 * n) // rb + N_EXPERTS
    n_used = bend[-1:].astype(I32)
    blk = jnp.arange(nb, dtype=I32)
    blk_eid = jnp.minimum(jnp.sum(bend[None, :] <= blk[:, None], axis=1), N_EXPERTS - 1).astype(I32)
    experts = jnp.arange(N_EXPERTS, dtype=I32)
    mine = blk_eid[:, None] == experts[None, :]
    blk_valid = jnp.clip(jnp.sum(jnp.where(mine, (cnt + pstart)[None, :], 0), axis=1) - blk * rb, 0, rb)
    eid = meta[R_EID0:R_EID1 + 1]
    first = jnp.sum(jnp.where(eid[None] == experts[:, None, None], pstart[:, None, None], 0), axis=0)
    dest = (first + meta[R_RANK0:R_RANK1 + 1]).reshape(-1)

    xs = _sc_row_scatter(hp, dest, nb * rb)
    ys = _experts(blk_eid + layer * N_EXPERTS, blk_valid.astype(I32), n_used, xs, w1, w3, w2, rb=rb)
    return _sc_row_gather(ys, dest)


def _lane_row(pairs):
    row = jnp.zeros((LANES,), F32)
    for off, vec in pairs:
        row = row.at[off:off + vec.shape[0]].set(vec.astype(F32))
    return row


def kernel(x, norm_mix, w_in, conv_qkv, dn_a_log, dn_dt_bias, dn_norm, fox_bias, conv_dw, conv_dw_b,
           conv_ln_g, conv_ln_b, w_a, w_b, w_c, w_out, norm_ffn, router_group_w, router_group_b,
           router_expert_w, router_expert_b, expert_w1, expert_w3, expert_w2, norm_final):
    b, t, d = x.shape
    n = b * t
    depth = w_in.shape[0]
    qk_dn = H_DN * DK_DN
    in_sizes = (qk_dn, qk_dn, H_DN * DV_DN, H_DN * DV_DN, H_DN, H_DN,
                H_FOX * D_FOX, H_FOX * D_FOX, H_FOX * D_FOX, H_FOX, 2 * C_CONV, d, d, d)
    splits = np.cumsum(in_sizes)[:-1].tolist()
    tm_final, rb = 512, 512

    x_src = x.reshape(n, d)
    w1_all = expert_w1.reshape(depth * N_EXPERTS, d, D_EXPERT)
    w3_all = expert_w3.reshape(depth * N_EXPERTS, d, D_EXPERT)
    w2_all = expert_w2.reshape(depth * N_EXPERTS, D_EXPERT, d)
    for l in range(depth):
        (qa, ka, va, za, ba, aa, qb, kb, vb, fb, uc, ga, gb, gc) = jnp.split(w_in[l], splits, axis=1)
        w_main = jnp.concatenate([qa, ka, va, za, qb * (D_FOX ** -0.5 * LOG2E), kb, vb, uc, ga, gb, gc],
                                 axis=1).astype(BF16)
        w_small = jnp.concatenate([ba, aa, fb, jnp.zeros((d, LANES - 3 * H_DN), F32)], axis=1).astype(BF16)
        vec = jnp.stack([conv_dw_b[l], conv_ln_g[l], conv_ln_b[l]] + [jnp.zeros((C_CONV,), F32)] * 5)
        outs = _in_proj(x_src, norm_mix[l][None, :], w_main, w_small, conv_dw[l], vec, seq_len=t)
        x2 = outs[0] if l > 0 else x_src
        dn_qkv, za_p, fox_qkv, zc, gates, small = outs[-6:]

        par = jnp.stack([_lane_row([(LANE_G, -jnp.exp(dn_a_log[l]))]),
                         _lane_row([(LANE_G, dn_dt_bias[l]), (LANE_F, fox_bias[l])]),
                         _lane_row([(0, dn_norm[l])])] + [jnp.zeros((LANES,), F32)] * 5)
        oa, qaug, kaug = _delta_rule(dn_qkv.reshape(b, t, -1), za_p.reshape(b, t, -1),
                                     small.reshape(b, t, LANES), conv_qkv[l], par)
        ob = _fox_attention(fox_qkv.reshape(b, t, -1), qaug, kaug)

        w_r = jnp.concatenate([router_group_w[l], jnp.zeros((d, LANE_EXP - N_GROUPS), F32),
                               router_expert_w[l], jnp.zeros((d, LANES - LANE_EXP - N_EXPERTS), F32)], axis=1)
        b_r = _lane_row([(LANE_GRP, router_group_b[l]), (LANE_EXP, router_expert_b[l])])[None, :]
        xn, hp, route, meta, counts = _merge(
            oa.reshape(n, -1), ob.reshape(n, -1), zc.reshape(n, -1), gates, x2,
            w_a[l].astype(BF16), w_b[l].astype(BF16), w_c[l].astype(BF16), w_out[l].astype(BF16),
            norm_ffn[l][None, :], w_r.astype(BF16), b_r)

        x_src = (_moe_rows(hp, meta, counts, w1_all, w3_all, w2_all, layer=l, rb=rb), xn, route)
    return _final_combine(*x_src, norm_final[None, :], tm=tm_final).reshape(b, t, d)
```

```python
import functools

import jax
import jax.numpy as jnp
import numpy as np
from jax import lax
from jax.experimental import pallas as pl
from jax.experimental.pallas import tpu as pltpu
from jax.experimental.pallas import tpu_sc as plsc

F32 = jnp.float32
BF16 = jnp.bfloat16
U32 = jnp.uint32
I32 = jnp.int32

EPS = 1e-6
LOG2E = 1.4426950408889634
LANES = 128
SUBLANES = 8
SC_CORES, SC_SUBCORES = 2, 16
H_DN, DK_DN, DV_DN = 4, 128, 128
SHORT_CONV = 4
CHUNK = 64
H_FOX, D_FOX = 4, 128
C_CONV = 512
CONV_WIDTH = 31
N_GROUPS, EXPERTS_PER_GROUP = 4, 8
N_EXPERTS = N_GROUPS * EXPERTS_PER_GROUP
D_EXPERT = 256

W_DN = 3 * H_DN * DK_DN
W_FOX = 3 * H_FOX * D_FOX
VMEM_LIMIT = 56 * 1024 * 1024

LANE_BETA, LANE_G, LANE_F = 0, 4, 8
LANE_GRP, LANE_EXP = 0, 32
R_GATE0, R_GATE1, R_EID0, R_EID1, R_RANK0, R_RANK1 = 0, 1, 2, 3, 4, 5

NN = (((1,), (0,)), ((), ()))
NT = (((1,), (1,)), ((), ()))
TN = (((0,), (0,)), ((), ()))


def _mm(a, b, dims=NN):
    return lax.dot_general(a.astype(BF16), b.astype(BF16), dims, preferred_element_type=F32)


def _sigmoid(x):
    return 0.5 * jnp.tanh(0.5 * x) + 0.5


def _silu(x):
    return x * _sigmoid(x)


def _params(*sem):
    return pltpu.CompilerParams(dimension_semantics=sem, vmem_limit_bytes=VMEM_LIMIT)


def _in_proj_body(*refs, col_chunk, combine, tiles_per_seq, hrows, rchunk):
    if combine:
        r0_ref, r1_ref, xn_ref, route_ref, g_ref, w_ref, ws_ref, dw_ref, vec_ref, x_out_ref = refs[:10]
        x = _combine_rows(r0_ref, r1_ref, xn_ref, route_ref)
        x_out_ref[...] = x
        rest = refs[10:]
    else:
        x_ref, g_ref, w_ref, ws_ref, dw_ref, vec_ref = refs[:6]
        x = x_ref[...]
        rest = refs[6:]
    dn_ref, za_ref, fox_ref, zc_ref, gates_ref, small_ref, z_ref, zs_ref, y_ref = rest
    tm = x.shape[0]
    h = x * lax.rsqrt(jnp.mean(x * x, axis=-1, keepdims=True) + EPS) * g_ref[...]
    hb = h.astype(BF16)

    def project(col, width):
        return jnp.dot(hb, w_ref[:, col:col + width], preferred_element_type=F32)

    starts, col = {}, 0
    for name, width in (("dn", dn_ref.shape[-1]), ("za", za_ref.shape[-1]), ("fox", fox_ref.shape[-1]),
                        ("conv", 2 * C_CONV), ("gates", gates_ref.shape[-1])):
        starts[name], col = col, col + width

    first = pl.program_id(0) % tiles_per_seq == 0
    col = starts["conv"]

    @pl.when(first)
    def _():
        z_ref[0:hrows, :] = jnp.zeros((hrows, C_CONV), F32)

    @pl.when(jnp.logical_not(first))
    def _():
        z_ref[0:hrows, :] = z_ref[tm:tm + hrows, :]

    z_ref[hrows:hrows + tm, :] = project(col, C_CONV) * _sigmoid(project(col + C_CONV, C_CONV))
    sub = SUBLANES
    span = tm + hrows - sub
    for phase in range(1, sub):
        zs_ref[phase - 1, 0:span, :] = z_ref[phase:phase + span, :]

    def conv_unit(lg, r0, after):
        lanes = slice(lg * LANES, (lg + 1) * LANES)
        acc = vec_ref[0:1, lanes] + jnp.minimum(jnp.abs(after[0:rchunk, 0:LANES]), 0.0)
        for k in range(CONV_WIDTH):
            off = hrows - (CONV_WIDTH - 1) + k
            phase, base = off % sub, off - off % sub + r0
            if phase == 0:
                tap = z_ref[base:base + rchunk, lanes]
            else:
                tap = zs_ref[phase - 1, base:base + rchunk, lanes]
            acc = acc + dw_ref[k:k + 1, lanes] * tap
        y_ref[r0:r0 + rchunk, lanes] = acc

    units = [(lg, r0) for lg in range(C_CONV // LANES) for r0 in range(0, tm, rchunk)]
    chunks = [(ref, starts[name], c) for name, ref in (("dn", dn_ref), ("za", za_ref), ("fox", fox_ref),
                                                       ("gates", gates_ref))
              for c in range(0, ref.shape[-1], col_chunk)]
    share = [[] for _ in chunks]
    for u, unit in enumerate(units):
        share[u * len(chunks) // len(units)].append(unit)
    for (ref, start, c), mine in zip(chunks, share):
        res = project(start + c, col_chunk)
        ref[:, c:c + col_chunk] = res.astype(ref.dtype)
        for lg, r0 in mine:
            conv_unit(lg, r0, res)
    small_ref[...] = jnp.dot(hb, ws_ref[...], preferred_element_type=F32)

    y = y_ref[...]
    mu = jnp.mean(y, axis=-1, keepdims=True)
    yc = y - mu
    var = jnp.mean(yc * yc, axis=-1, keepdims=True)
    yn = yc * lax.rsqrt(var + EPS) * vec_ref[1:2, :] + vec_ref[2:3, :]
    zc_ref[...] = _silu(yn).astype(zc_ref.dtype)


def _in_proj(x_src, g, w, ws, dw, vec, *, seq_len, tm=256, col_chunk=512, hrows=32, rchunk=64):
    combine = isinstance(x_src, tuple)
    n, d = (x_src[1] if combine else x_src).shape
    nt = n // tm
    widths = (W_DN, H_DN * DV_DN, W_FOX, C_CONV, 3 * d)
    assert w.shape == (d, sum(widths) + C_CONV) and n % tm == 0 and seq_len % tm == 0
    assert hrows >= CONV_WIDTH - 1 and hrows % SUBLANES == 0 and tm % rchunk == 0
    row = lambda i: (i, 0)
    fixed = lambda i: (0, 0)
    out_shape = [jax.ShapeDtypeStruct((n, wd), BF16) for wd in widths] + [jax.ShapeDtypeStruct((n, LANES), F32)]
    out_specs = [pl.BlockSpec((tm, wd), row) for wd in widths] + [pl.BlockSpec((tm, LANES), row)]
    if combine:
        rows2, xn, route = x_src
        args = (rows2, rows2, xn, route)
        in_specs = [pl.BlockSpec((tm, d // 2), row), pl.BlockSpec((tm, d // 2), lambda i: (i + nt, 0)),
                    pl.BlockSpec((tm, d), row), pl.BlockSpec((tm, LANES), row)]
        out_shape = [jax.ShapeDtypeStruct((n, d), F32)] + out_shape
        out_specs = [pl.BlockSpec((tm, d), row)] + out_specs
    else:
        args = (x_src,)
        in_specs = [pl.BlockSpec((tm, d), row)]
    return pl.pallas_call(
        functools.partial(_in_proj_body, col_chunk=col_chunk, combine=combine, tiles_per_seq=seq_len // tm,
                          hrows=hrows, rchunk=rchunk),
        grid=(nt,),
        in_specs=in_specs + [pl.BlockSpec((1, d), fixed),
                             pl.BlockSpec(w.shape, fixed, pipeline_mode=pl.Buffered(1)),
                             pl.BlockSpec(ws.shape, fixed, pipeline_mode=pl.Buffered(1)),
                             pl.BlockSpec(dw.shape, fixed), pl.BlockSpec(vec.shape, fixed)],
        out_specs=out_specs,
        out_shape=out_shape,
        scratch_shapes=[pltpu.VMEM((tm + hrows, C_CONV), F32),
                        pltpu.VMEM((SUBLANES - 1, tm + hrows, C_CONV), F32),
                        pltpu.VMEM((tm, C_CONV), F32)],
        compiler_params=_params("arbitrary"),
        name="in_proj",
    )(*args, g, w, ws, dw, vec)


def _softplus_parts(z):
    t = jnp.log1p(jnp.exp(-jnp.abs(z)))
    return jnp.maximum(z, 0.0) + t, -(jnp.maximum(-z, 0.0) + t)


def _delta_body(qkv_ref, za_ref, sm_ref, cw_ref, par_ref, oa_ref, qaug_ref, kaug_ref,
                xs_ref, s_ref, carry_ref, *, ts):
    j = pl.program_id(1)
    halo = SUBLANES
    pack = 2 * SUBLANES

    @pl.when(j == 0)
    def _():
        xs_ref[0:halo, :] = jnp.zeros((halo, W_DN), F32)
        s_ref[...] = jnp.zeros_like(s_ref)
        carry_ref[...] = jnp.zeros_like(carry_ref)

    @pl.when(j > 0)
    def _():
        xs_ref[0:halo, :] = xs_ref[2 * halo:3 * halo, :]

    xb = qkv_ref[0]
    xs_ref[halo:2 * halo, :] = qkv_ref[0, 0:pack, :].astype(F32)[0:halo]
    xs_ref[2 * halo:3 * halo, :] = qkv_ref[0, ts - pack:ts, :].astype(F32)[pack - halo:pack]

    lag = lax.broadcasted_iota(I32, (ts, ts), 0) - lax.broadcasted_iota(I32, (ts, ts), 1)
    shifted = [jnp.dot(jnp.where(lag == s, 1.0, 0.0).astype(BF16), xb, preferred_element_type=F32)
               for s in range(1, SHORT_CONV)]
    head_row = lax.broadcasted_iota(I32, (halo, LANES), 0)

    def conv_silu(lane0):
        lanes = slice(lane0, lane0 + LANES)
        acc = cw_ref[SHORT_CONV - 1:SHORT_CONV, lanes] * xb[:, lanes].astype(F32)
        for s in range(1, SHORT_CONV):
            acc = acc + cw_ref[SHORT_CONV - 1 - s:SHORT_CONV - s, lanes] * shifted[s - 1][:, lanes]
        head = acc[0:halo]
        for s in range(1, SHORT_CONV):
            prev = jnp.where(head_row < s, xs_ref[halo - s:2 * halo - s, lanes], 0.0)
            head = head + cw_ref[SHORT_CONV - 1 - s:SHORT_CONV - s, lanes] * prev
        return _silu(jnp.concatenate([head, acc[halo:]], axis=0))

    def l2n(a):
        return a * lax.rsqrt(jnp.sum(a * a, axis=-1, keepdims=True) + EPS)

    sm = sm_ref[0]
    lane = lax.broadcasted_iota(I32, sm.shape, 1)
    sp, logsig = _softplus_parts(sm + par_ref[1:2, :])
    vals = jnp.where(lane < LANE_G, _sigmoid(sm),
                     jnp.where(lane < LANE_F, par_ref[0:1, :] * sp,
                               jnp.where(lane < LANE_F + H_FOX, logsig, 0.0)))
    row = lax.broadcasted_iota(I32, (ts, ts), 0)
    colm = lax.broadcasted_iota(I32, (ts, ts), 1)
    log_chunk = CHUNK.bit_length() - 1
    causal = (row >= colm) & ((row >> log_chunk) == (colm >> log_chunk))

    hi = vals.astype(BF16)
    rem = vals - hi.astype(F32)
    mid = rem.astype(BF16)
    lo = (rem - mid.astype(F32)).astype(BF16)
    pieces = jnp.concatenate([hi, mid, lo], axis=-1)

    def cumsum(mask):
        y = jnp.dot(jnp.where(mask, 1.0, 0.0).astype(BF16), pieces, preferred_element_type=F32)
        return (y[:, :LANES] + y[:, LANES:2 * LANES]) + y[:, 2 * LANES:]

    ccum = cumsum(row >= colm) + carry_ref[...]
    gcum = cumsum(causal)
    carry_ref[...] = ccum[ts - 1:ts, :]
    gcum_t = gcum.T

    cl = ccum * LOG2E
    c_hi = cl.astype(BF16)
    c_rem = cl - c_hi.astype(F32)
    c_mid = c_rem.astype(BF16)
    c_lo = (c_rem - c_mid.astype(F32)).astype(BF16)
    c_pieces = jnp.concatenate([c_hi, c_mid, c_lo], axis=-1)
    pr = lax.broadcasted_iota(I32, (3 * LANES, LANES), 0)
    pc = lax.broadcasted_iota(I32, (3 * LANES, LANES), 1)
    src_lane, piece = pr & (LANES - 1), pr >> (LANES.bit_length() - 1)
    owned = (src_lane >= LANE_F) & (src_lane < LANE_F + H_FOX) & ((pc >> 3) == src_lane - LANE_F)
    place_q = jnp.where(owned & ((pc & 7) == piece), 1.0, 0.0).astype(BF16)
    place_k = jnp.where(owned & ((pc & 7) == piece + 3), -1.0, 0.0).astype(BF16)
    slot = lane & 7
    in_heads = lane < 8 * H_FOX
    ones_q = jnp.where(in_heads & (slot >= 3) & (slot < 6), 1.0, 0.0)
    ones_k = jnp.where(in_heads & (slot < 3), 1.0, 0.0)
    qaug_ref[0] = (jnp.dot(c_pieces, place_q, preferred_element_type=F32) + ones_q).astype(BF16)
    kaug_ref[0] = (jnp.dot(c_pieces, place_k, preferred_element_type=F32) + ones_k).astype(BF16)

    scale = DK_DN ** -0.5
    dn_norm = par_ref[2:3, :]

    pw = 2 * CHUNK
    prow = lax.broadcasted_iota(I32, (pw, pw), 0)
    pcol = lax.broadcasted_iota(I32, (pw, pw), 1)
    same = (prow >> log_chunk) == (pcol >> log_chunk)
    causal_p = (prow >= pcol) & same
    strict_p = (prow > pcol) & same
    levels = []
    s = 1
    while s < CHUNK:
        levels.append(((prow >> s.bit_length()) == (pcol >> s.bit_length()))
                      & ((prow & s) != 0) & ((pcol & s) == 0))
        s *= 2

    heads = range(H_DN)
    pairs = range(ts // pw)
    q, k, v = [], [], []
    for h in heads:
        q.append(l2n(conv_silu(h * DK_DN)))
        k.append(l2n(conv_silu(H_DN * DK_DN + h * DK_DN)))
        v.append(conv_silu(2 * H_DN * DK_DN + h * DV_DN))

    ctx = []
    for h in heads:
        for p in pairs:
            pr = slice(p * pw, (p + 1) * pw)
            gcol = gcum[pr, LANE_G + h:LANE_G + h + 1]
            grow = gcum_t[LANE_G + h:LANE_G + h + 1, pr]
            beta = vals[pr, LANE_BETA + h:LANE_BETA + h + 1]
            decay = jnp.where(causal_p, jnp.exp(jnp.where(causal_p, gcol - grow, 0.0)), 0.0)
            egc = jnp.exp(gcol)
            kp = k[h][pr]
            kb = kp * beta
            ctx.append(dict(
                h=h, p=p, gcol=gcol, kp=kp,
                a=jnp.where(strict_p, _mm(kb, kp, NT) * decay, 0.0),
                qk=jnp.where(causal_p, _mm(q[h][pr] * scale, kp, NT) * decay, 0.0),
                rhs=jnp.concatenate([v[h][pr] * beta, kb * egc], axis=-1),
                qg=q[h][pr] * (scale * egc)))

    for c in ctx:
        c["n"] = -jnp.where(levels[0], c["a"], 0.0)
    for level in levels[1:]:
        for c in ctx:
            m = jnp.where(level, c["a"], 0.0)
            c["y"] = m + _mm(c["n"], m)
        for c in ctx:
            c["n"] = c["n"] - (c["y"] + _mm(c["y"], c["n"]))
    for c in ctx:
        c["sol"] = c["rhs"] + _mm(c["n"], c["rhs"])

    state = [s_ref[h] for h in heads]
    for p in pairs:
        group = [c for c in ctx if c["p"] == p]
        v_prev = [None] * H_DN
        for ch in range(2):
            rows = slice(ch * CHUNK, (ch + 1) * CHUNK)
            out_rows = slice(p * pw + ch * CHUNK, p * pw + (ch + 1) * CHUNK)
            on_state = [_mm(jnp.concatenate([c["sol"][rows, DV_DN:], c["qg"][rows]], axis=0), state[c["h"]])
                        for c in group]
            for c, ws_qs in zip(group, on_state):
                h = c["h"]
                vn = c["sol"][rows, :DV_DN] - ws_qs[:CHUNK]
                v_pair = jnp.concatenate([vn, jnp.zeros_like(vn)] if ch == 0 else [v_prev[h], vn], axis=0)
                o = ws_qs[CHUNK:] + _mm(c["qk"][rows], v_pair)
                glast = c["gcol"][(ch + 1) * CHUNK - 1:(ch + 1) * CHUNK]
                state[h] = (state[h] * jnp.exp(glast)
                            + _mm(c["kp"][rows] * jnp.exp(glast - c["gcol"][rows]), vn, TN))
                v_prev[h] = vn
                o = o * lax.rsqrt(jnp.mean(o * o, axis=-1, keepdims=True) + EPS) * dn_norm
                za = za_ref[0, out_rows, h * DV_DN:(h + 1) * DV_DN].astype(F32)
                oa_ref[0, out_rows, h * DV_DN:(h + 1) * DV_DN] = (o * _silu(za)).astype(oa_ref.dtype)
    for h in heads:
        s_ref[h] = state[h]


def _delta_rule(qkv, za, small, conv_w, par, *, ts=256):
    b, t, _ = qkv.shape
    assert t % ts == 0 and ts % (2 * CHUNK) == 0
    blk = lambda width: pl.BlockSpec((1, ts, width), lambda bi, j: (bi, j, 0))
    fixed = lambda bi, j: (0, 0)
    return pl.pallas_call(
        functools.partial(_delta_body, ts=ts),
        grid=(b, t // ts),
        in_specs=[blk(W_DN), blk(H_DN * DV_DN), blk(LANES),
                  pl.BlockSpec(conv_w.shape, fixed), pl.BlockSpec(par.shape, fixed)],
        out_specs=[blk(H_DN * DV_DN), blk(LANES), blk(LANES)],
        out_shape=[jax.ShapeDtypeStruct((b, t, H_DN * DV_DN), BF16),
                   jax.ShapeDtypeStruct((b, t, LANES), BF16),
                   jax.ShapeDtypeStruct((b, t, LANES), BF16)],
        scratch_shapes=[pltpu.VMEM((3 * SUBLANES, W_DN), F32),
                        pltpu.VMEM((H_DN, DK_DN, DV_DN), F32),
                        pltpu.VMEM((1, LANES), F32)],
        compiler_params=_params("parallel", "arbitrary"),
        name="delta_rule",
    )(qkv, za, small, conv_w, par)


def _fox_body(q_ref, k_ref, v_ref, qa_ref, ka_ref, o_ref, m_ref, acc_ref, *, tq, tk):
    i = pl.program_id(1)
    m_ref[...] = jnp.full(m_ref.shape, -jnp.inf, F32)
    acc_ref[...] = jnp.zeros_like(acc_ref)
    head_lanes = [slice(h * D_FOX, (h + 1) * D_FOX) for h in range(H_FOX)]
    lane = lax.broadcasted_iota(I32, (tk, LANES), 1)
    own = [jnp.where((lane >> 3) == h, 1.0, 0.0).astype(BF16) for h in range(H_FOX)]
    ones = jnp.ones((tk, D_FOX), BF16)
    keep = lax.broadcasted_iota(I32, (tq, tk), 1) <= lax.broadcasted_iota(I32, (tq, tk), 0)

    def block(start, diag_offset):
        rows = slice(0 if diag_offset is None else diag_offset, tq)
        ka = ka_ref[0, pl.ds(start, tk), :]
        for h, lanes in enumerate(head_lanes):
            q_aug = jnp.concatenate([q_ref[0, rows, lanes], qa_ref[0, rows, :]], axis=1)
            k_aug = jnp.concatenate([k_ref[0, pl.ds(start, tk), lanes], ka * own[h]], axis=1)
            s = lax.dot_general(q_aug, k_aug, NT, preferred_element_type=F32)
            if diag_offset is not None:
                s = jnp.where(keep[:tq - diag_offset], s, -jnp.inf)
            m_prev = m_ref[h, rows]
            m_next = jnp.maximum(m_prev, jnp.max(s, axis=-1, keepdims=True))
            p = jnp.exp2(s - jnp.concatenate([m_next] * (tk // LANES), axis=1))
            alpha = jnp.exp2(m_prev - m_next)
            v_aug = jnp.concatenate([v_ref[0, pl.ds(start, tk), lanes], ones], axis=1)
            acc_ref[h, rows] = (jnp.concatenate([alpha, alpha], axis=1) * acc_ref[h, rows]
                                + jnp.dot(p.astype(BF16), v_aug, preferred_element_type=F32))
            m_ref[h, rows] = m_next

    def full_block(jb, carry):
        block(pl.multiple_of(jb * tk, tk), None)
        return carry

    lax.fori_loop(0, i * (tq // tk), full_block, 0)
    for d in range(tq // tk):
        block(pl.multiple_of(i * tq + d * tk, tk), d * tk)
    for h, lanes in enumerate(head_lanes):
        acc = acc_ref[h]
        o_ref[0, :, lanes] = (acc[:, :D_FOX] / acc[:, D_FOX:]).astype(o_ref.dtype)


def _fox_attention(qkv, qaug, kaug, *, tq=1024, tk=512):
    b, t, _ = qkv.shape
    hd = H_FOX * D_FOX
    assert t % tq == 0 and tq % tk == 0 and tk % LANES == 0
    return pl.pallas_call(
        functools.partial(_fox_body, tq=tq, tk=tk),
        grid=(b, t // tq),
        in_specs=[pl.BlockSpec((1, tq, hd), lambda bi, i: (bi, i, 0)),
                  pl.BlockSpec((1, t, hd), lambda bi, i: (bi, 0, 1)),
                  pl.BlockSpec((1, t, hd), lambda bi, i: (bi, 0, 2)),
                  pl.BlockSpec((1, tq, LANES), lambda bi, i: (bi, i, 0)),
                  pl.BlockSpec((1, t, LANES), lambda bi, i: (bi, 0, 0))],
        out_specs=pl.BlockSpec((1, tq, hd), lambda bi, i: (bi, i, 0)),
        out_shape=jax.ShapeDtypeStruct((b, t, hd), BF16),
        scratch_shapes=[pltpu.VMEM((H_FOX, tq, LANES), F32), pltpu.VMEM((H_FOX, tq, 2 * D_FOX), F32)],
        compiler_params=_params("parallel", "arbitrary"),
        name="fox_attention",
    )(qkv, qkv, qkv, qaug, kaug)


def _merge_body(oa_ref, ob_ref, zc_ref, gates_ref, x_ref, wa_ref, wb_ref, wc_ref, wo_ref, g_ref,
                wr_ref, br_ref, xn_ref, hp_ref, route_ref, meta_ref, cnt_ref, run_ref, *, tm):
    step = pl.program_id(0)
    d = x_ref.shape[-1]

    @pl.when(step == 0)
    def _():
        run_ref[...] = jnp.zeros_like(run_ref)

    merged = None
    for idx, (m_ref, w_ref) in enumerate(((oa_ref, wa_ref), (ob_ref, wb_ref), (zc_ref, wc_ref))):
        y = jnp.dot(m_ref[...], w_ref[...], preferred_element_type=F32)
        term = _sigmoid(gates_ref[:, idx * d:(idx + 1) * d].astype(F32)) * y
        merged = term if merged is None else merged + term
    xn = x_ref[...] + jnp.dot(merged.astype(BF16), wo_ref[...], preferred_element_type=F32)
    xn_ref[...] = xn
    h2 = xn * lax.rsqrt(jnp.mean(xn * xn, axis=-1, keepdims=True) + EPS) * g_ref[...]

    hp_ref[...] = _pack_bf16_pairs(h2)

    logits = lax.dot_general(wr_ref[...], h2.astype(BF16), NT, preferred_element_type=F32) + br_ref[...]
    big = jnp.int32(LANES)
    gl = logits[LANE_GRP:LANE_GRP + N_GROUPS]
    g_row = lax.broadcasted_iota(I32, gl.shape, 0)
    gmax = jnp.max(gl, axis=0, keepdims=True)
    grp = jnp.min(jnp.where(gl == gmax, g_row, big), axis=0, keepdims=True)
    p_grp = 1.0 / jnp.sum(jnp.exp(gl - gmax), axis=0, keepdims=True)
    el_all = logits[LANE_EXP:LANE_EXP + N_EXPERTS]
    e_row = lax.broadcasted_iota(I32, el_all.shape, 0)
    el = jnp.where((e_row >> (EXPERTS_PER_GROUP.bit_length() - 1)) == grp, el_all, -jnp.inf)
    v0 = jnp.max(el, axis=0, keepdims=True)
    i0 = jnp.min(jnp.where(el == v0, e_row, big), axis=0, keepdims=True)
    el1 = jnp.where(e_row == i0, -jnp.inf, el)
    v1 = jnp.max(el1, axis=0, keepdims=True)
    i1 = jnp.min(jnp.where(el1 == v1, e_row, big), axis=0, keepdims=True)
    e1 = jnp.exp(v1 - v0)
    gate0 = p_grp / (1.0 + e1)
    gate1 = p_grp * e1 / (1.0 + e1)

    hot0 = e_row == i0
    hot1 = e_row == i1
    onehot = jnp.where(hot0 | hot1, 1.0, 0.0)
    tiles = [onehot[:, t * LANES:(t + 1) * LANES] for t in range(tm // LANES)]
    ri = lax.broadcasted_iota(I32, (LANES, LANES), 0)
    cj = lax.broadcasted_iota(I32, (LANES, LANES), 1)
    in_tile = _mm(jnp.concatenate(tiles, axis=0), jnp.where(ri < cj, 1.0, 0.0))
    before = run_ref[...]
    prefix = []
    for t, tile in enumerate(tiles):
        prefix.append(in_tile[t * N_EXPERTS:(t + 1) * N_EXPERTS] + before)
        before = before + jnp.sum(tile, axis=1, keepdims=True)
    prefix = jnp.concatenate(prefix, axis=1)
    run_ref[...] = before
    cnt_ref[...] = jnp.broadcast_to(before, cnt_ref.shape)
    rank0 = jnp.sum(jnp.where(hot0, prefix, 0.0), axis=0, keepdims=True)
    rank1 = jnp.sum(jnp.where(hot1, prefix, 0.0), axis=0, keepdims=True)

    rows = {R_GATE0: gate0, R_GATE1: gate1, R_EID0: i0.astype(F32), R_EID1: i1.astype(F32),
            R_RANK0: rank0, R_RANK1: rank1}
    fields = jnp.concatenate([rows.get(r, jnp.zeros_like(gate0)) for r in range(SUBLANES)], axis=0)
    meta_ref[...] = fields.astype(I32)
    route_ref[...] = jnp.concatenate([fields, jnp.zeros((LANES - SUBLANES, tm), F32)], axis=0).T


def _merge(oa, ob, zc, gates, x2, wa, wb, wc, wo, g, wr, br, *, tm=1024):
    n, d = x2.shape
    assert n % tm == 0
    row = lambda i: (i, 0)
    fixed = lambda i: (0, 0)
    full = lambda a: pl.BlockSpec(a.shape, fixed)
    return pl.pallas_call(
        functools.partial(_merge_body, tm=tm),
        grid=(n // tm,),
        in_specs=[pl.BlockSpec((tm, oa.shape[1]), row), pl.BlockSpec((tm, ob.shape[1]), row),
                  pl.BlockSpec((tm, zc.shape[1]), row), pl.BlockSpec((tm, 3 * d), row),
                  pl.BlockSpec((tm, d), row),
                  full(wa), full(wb), full(wc), full(wo), full(g), full(wr), full(br)],
        out_specs=[pl.BlockSpec((tm, d), row), pl.BlockSpec((tm, d // 2), row),
                   pl.BlockSpec((tm, LANES), row), pl.BlockSpec((SUBLANES, tm), lambda i: (0, i)),
                   pl.BlockSpec((N_EXPERTS, LANES), fixed)],
        out_shape=[jax.ShapeDtypeStruct((n, d), F32), jax.ShapeDtypeStruct((n, d // 2), U32),
                   jax.ShapeDtypeStruct((n, LANES), F32), jax.ShapeDtypeStruct((SUBLANES, n), I32),
                   jax.ShapeDtypeStruct((N_EXPERTS, LANES), F32)],
        scratch_shapes=[pltpu.VMEM((N_EXPERTS, 1), F32)],
        compiler_params=_params("arbitrary"),
        name="merge_router",
    )(oa, ob, zc, gates, x2, wa, wb, wc, wo, g, wr, br)


def _sc_mesh():
    return plsc.VectorSubcoreMesh(core_axis_name="c", subcore_axis_name="s")


def _sc_worker_base(per_worker):
    return (lax.axis_index("s") * SC_CORES + lax.axis_index("c")) * per_worker


def _sc_row_scatter(src, idx, out_rows, *, chunk=128):
    n, d = src.shape
    per_worker = n // (SC_CORES * SC_SUBCORES)
    chunk = min(chunk, per_worker)
    assert idx.shape == (2 * n,) and n % (SC_CORES * SC_SUBCORES) == 0 and per_worker % chunk == 0

    @functools.partial(
        pl.kernel, mesh=_sc_mesh(), out_type=jax.ShapeDtypeStruct((out_rows, d), src.dtype),
        scratch_types=[pltpu.VMEM((chunk,), I32), pltpu.VMEM((chunk, d), src.dtype), pltpu.SemaphoreType.DMA])
    def scatter(src_hbm, idx_hbm, out_hbm, idx_v, rows_v, sem):
        base = _sc_worker_base(per_worker)

        @pl.loop(0, per_worker // chunk)
        def _(step):
            off = base + step * chunk
            pltpu.sync_copy(src_hbm.at[pl.ds(off, chunk)], rows_v)
            for slot in range(2):
                pltpu.sync_copy(idx_hbm.at[pl.ds(slot * n + off, chunk)], idx_v)
                pltpu.async_copy(rows_v, out_hbm.at[idx_v], sem).wait()

    return scatter(src, idx)


def _pack_bf16_pairs(x):
    half = x.shape[-1] // 2
    lo = pltpu.bitcast(x[:, :half].astype(BF16).astype(F32), U32) >> 16
    hi = pltpu.bitcast(x[:, half:].astype(BF16).astype(F32), U32) & jnp.uint32(0xFFFF0000)
    return lo | hi


def _unpack_bf16_pairs(xp):
    return pltpu.bitcast(xp << 16, F32), pltpu.bitcast(xp & jnp.uint32(0xFFFF0000), F32)


def _expert_body(be_ref, valid_ref, nu_ref, xs_ref, w1_ref, w3_ref, w2_ref, ys_ref):
    del be_ref
    step = pl.program_id(0)
    live = step < nu_ref[0]

    @pl.when(jnp.logical_not(live))
    def _():
        ys_ref[...] = jnp.zeros_like(ys_ref)

    @pl.when(live)
    def _():
        row = lax.broadcasted_iota(I32, xs_ref.shape, 0)
        xp = jnp.where(row < valid_ref[step], xs_ref[...], jnp.uint32(0))
        half = xp.shape[-1]
        lo, hi = (part.astype(BF16) for part in _unpack_bf16_pairs(xp))

        def up(w_ref):
            return (jnp.dot(lo, w_ref[0, :half, :].astype(BF16), preferred_element_type=F32)
                    + jnp.dot(hi, w_ref[0, half:, :].astype(BF16), preferred_element_type=F32))

        act = (_silu(up(w1_ref)) * up(w3_ref)).astype(BF16)
        ys_ref[...] = _pack_bf16_pairs(jnp.dot(act, w2_ref[0].astype(BF16), preferred_element_type=F32))


def _experts(blk_eid, blk_valid, n_used, xs, w1, w3, w2, *, rb):
    p, half = xs.shape
    d = 2 * half
    nb = p // rb
    used = lambda i, be, bv, nu: jnp.maximum(jnp.minimum(i, nu[0] - 1), 0)
    wmap = lambda i, be, bv, nu: (be[used(i, be, bv, nu)], 0, 0)
    grid_spec = pltpu.PrefetchScalarGridSpec(
        num_scalar_prefetch=3,
        grid=(nb,),
        in_specs=[pl.BlockSpec((rb, half), lambda i, be, bv, nu: (used(i, be, bv, nu), 0)),
                  pl.BlockSpec((1, d, D_EXPERT), wmap), pl.BlockSpec((1, d, D_EXPERT), wmap),
                  pl.BlockSpec((1, D_EXPERT, d), wmap)],
        out_specs=pl.BlockSpec((rb, half), lambda i, be, bv, nu: (i, 0)),
    )
    return pl.pallas_call(
        _expert_body,
        grid_spec=grid_spec,
        out_shape=jax.ShapeDtypeStruct((p, half), U32),
        compiler_params=_params("arbitrary"),
        name="moe_experts",
    )(blk_eid, blk_valid, n_used, xs, w1, w3, w2)


def _sc_row_gather(table, idx, *, chunk=128):
    rows, d = idx.shape[0], table.shape[1]
    per_worker = rows // (SC_CORES * SC_SUBCORES)
    chunk = min(chunk, per_worker)
    assert rows % (SC_CORES * SC_SUBCORES) == 0 and per_worker % chunk == 0

    @functools.partial(
        pl.kernel, mesh=_sc_mesh(), out_type=jax.ShapeDtypeStruct((rows, d), table.dtype),
        scratch_types=[pltpu.VMEM((chunk,), I32), pltpu.VMEM((chunk, d), table.dtype), pltpu.SemaphoreType.DMA])
    def gather(table_hbm, idx_hbm, out_hbm, idx_v, rows_v, sem):
        base = _sc_worker_base(per_worker)

        @pl.loop(0, per_worker // chunk)
        def _(step):
            off = base + step * chunk
            pltpu.sync_copy(idx_hbm.at[pl.ds(off, chunk)], idx_v)
            pltpu.async_copy(table_hbm.at[idx_v], rows_v, sem).wait()
            pltpu.sync_copy(rows_v, out_hbm.at[pl.ds(off, chunk)])

    return gather(table, idx)


def _combine_rows(r0_ref, r1_ref, x_ref, route_ref):
    route = route_ref[...]
    half = r0_ref.shape[-1]
    lo0, hi0 = _unpack_bf16_pairs(r0_ref[...])
    lo1, hi1 = _unpack_bf16_pairs(r1_ref[...])
    g0, g1 = route[:, R_GATE0:R_GATE0 + 1], route[:, R_GATE1:R_GATE1 + 1]
    return jnp.concatenate([x_ref[:, :half] + g0 * lo0 + g1 * lo1,
                            x_ref[:, half:] + g0 * hi0 + g1 * hi1], axis=1)


def _final_body(r0_ref, r1_ref, x_ref, route_ref, g_ref, o_ref):
    out = _combine_rows(r0_ref, r1_ref, x_ref, route_ref)
    o_ref[...] = out * lax.rsqrt(jnp.mean(out * out, axis=-1, keepdims=True) + EPS) * g_ref[...]


def _final_combine(rows2, xn, route, g, *, tm):
    n, d = xn.shape
    nt = n // tm
    row = lambda i: (i, 0)
    return pl.pallas_call(
        _final_body,
        grid=(nt,),
        in_specs=[pl.BlockSpec((tm, d // 2), row), pl.BlockSpec((tm, d // 2), lambda i: (i + nt, 0)),
                  pl.BlockSpec((tm, d), row), pl.BlockSpec((tm, LANES), row),
                  pl.BlockSpec((1, d), lambda i: (0, 0))],
        out_specs=pl.BlockSpec((tm, d), row),
        out_shape=jax.ShapeDtypeStruct((n, d), F32),
        compiler_params=_params("parallel"),
        name="moe_combine",
    )(rows2, rows2, xn, route, g)


def _moe_rows(hp, meta, counts, w1, w3, w2, *, layer, rb):
    n = hp.shape[0]
    cnt = counts[:, 0].astype(I32)
    nblk = (cnt + rb - 1) // rb
    bend = jnp.cumsum(nblk)
    pstart = (bend - nblk) * rb
    nb = (2 * n) // rb + N_EXPERTS
    n_used = bend[-1:].astype(I32)
    blk = jnp.arange(nb, dtype=I32)
    blk_eid = jnp.minimum(jnp.sum(bend[None, :] <= blk[:, None], axis=1), N_EXPERTS - 1).astype(I32)
    experts = jnp.arange(N_EXPERTS, dtype=I32)
    mine = blk_eid[:, None] == experts[None, :]
    blk_valid = jnp.clip(jnp.sum(jnp.where(mine, (cnt + pstart)[None, :], 0), axis=1) - blk * rb, 0, rb)
    eid = meta[R_EID0:R_EID1 + 1]
    first = jnp.sum(jnp.where(eid[None] == experts[:, None, None], pstart[:, None, None], 0), axis=0)
    dest = (first + meta[R_RANK0:R_RANK1 + 1]).reshape(-1)

    xs = _sc_row_scatter(hp, dest, nb * rb)
    ys = _experts(blk_eid + layer * N_EXPERTS, blk_valid.astype(I32), n_used, xs, w1, w3, w2, rb=rb)
    return _sc_row_gather(ys, dest)


def _lane_row(pairs):
    row = jnp.zeros((LANES,), F32)
    for off, vec in pairs:
        row = row.at[off:off + vec.shape[0]].set(vec.astype(F32))
    return row


def kernel(x, norm_mix, w_in, conv_qkv, dn_a_log, dn_dt_bias, dn_norm, fox_bias, conv_dw, conv_dw_b,
           conv_ln_g, conv_ln_b, w_a, w_b, w_c, w_out, norm_ffn, router_group_w, router_group_b,
           router_expert_w, router_expert_b, expert_w1, expert_w3, expert_w2, norm_final):
    b, t, d = x.shape
    n = b * t
    depth = w_in.shape[0]
    qk_dn = H_DN * DK_DN
    in_sizes = (qk_dn, qk_dn, H_DN * DV_DN, H_DN * DV_DN, H_DN, H_DN,
                H_FOX * D_FOX, H_FOX * D_FOX, H_FOX * D_FOX, H_FOX, 2 * C_CONV, d, d, d)
    splits = np.cumsum(in_sizes)[:-1].tolist()
    tm_final, rb = 512, 512

    x_src = x.reshape(n, d)
    w1_all = expert_w1.reshape(depth * N_EXPERTS, d, D_EXPERT)
    w3_all = expert_w3.reshape(depth * N_EXPERTS, d, D_EXPERT)
    w2_all = expert_w2.reshape(depth * N_EXPERTS, D_EXPERT, d)
    for l in range(depth):
        (qa, ka, va, za, ba, aa, qb, kb, vb, fb, uc, ga, gb, gc) = jnp.split(w_in[l], splits, axis=1)
        w_main = jnp.concatenate([qa, ka, va, za, qb * (D_FOX ** -0.5 * LOG2E), kb, vb, uc, ga, gb, gc],
                                 axis=1).astype(BF16)
        w_small = jnp.concatenate([ba, aa, fb, jnp.zeros((d, LANES - 3 * H_DN), F32)], axis=1).astype(BF16)
        vec = jnp.stack([conv_dw_b[l], conv_ln_g[l], conv_ln_b[l]] + [jnp.zeros((C_CONV,), F32)] * 5)
        outs = _in_proj(x_src, norm_mix[l][None, :], w_main, w_small, conv_dw[l], vec, seq_len=t)
        x2 = outs[0] if l > 0 else x_src
        dn_qkv, za_p, fox_qkv, zc, gates, small = outs[-6:]

        par = jnp.stack([_lane_row([(LANE_G, -jnp.exp(dn_a_log[l]))]),
                         _lane_row([(LANE_G, dn_dt_bias[l]), (LANE_F, fox_bias[l])]),
                         _lane_row([(0, dn_norm[l])])] + [jnp.zeros((LANES,), F32)] * 5)
        oa, qaug, kaug = _delta_rule(dn_qkv.reshape(b, t, -1), za_p.reshape(b, t, -1),
                                     small.reshape(b, t, LANES), conv_qkv[l], par)
        ob = _fox_attention(fox_qkv.reshape(b, t, -1), qaug, kaug)

        w_r = jnp.concatenate([router_group_w[l], jnp.zeros((d, LANE_EXP - N_GROUPS), F32),
                               router_expert_w[l], jnp.zeros((d, LANES - LANE_EXP - N_EXPERTS), F32)], axis=1)
        b_r = _lane_row([(LANE_GRP, router_group_b[l]), (LANE_EXP, router_expert_b[l])])[None, :]
        xn, hp, route, meta, counts = _merge(
            oa.reshape(n, -1), ob.reshape(n, -1), zc.reshape(n, -1), gates, x2,
            w_a[l].astype(BF16), w_b[l].astype(BF16), w_c[l].astype(BF16), w_out[l].astype(BF16),
            norm_ffn[l][None, :], w_r.T.astype(BF16), b_r.reshape(LANES, 1))

        x_src = (_moe_rows(hp, meta, counts, w1_all, w3_all, w2_all, layer=l, rb=rb), xn, route)
    return _final_combine(*x_src, norm_final[None, :], tm=tm_final).reshape(b, t, d)
```

```python
import functools

import jax
import jax.numpy as jnp
import numpy as np
from jax import lax
from jax.experimental import pallas as pl
from jax.experimental.pallas import tpu as pltpu
from jax.experimental.pallas import tpu_sc as plsc

F32 = jnp.float32
BF16 = jnp.bfloat16
U32 = jnp.uint32
I32 = jnp.int32

EPS = 1e-6
LOG2E = 1.4426950408889634
LANES = 128
SUBLANES = 8
SC_CORES, SC_SUBCORES = 2, 16
H_DN, DK_DN, DV_DN = 4, 128, 128
SHORT_CONV = 4
CHUNK = 64
H_FOX, D_FOX = 4, 128
C_CONV = 512
CONV_WIDTH = 31
N_GROUPS, EXPERTS_PER_GROUP = 4, 8
N_EXPERTS = N_GROUPS * EXPERTS_PER_GROUP
D_EXPERT = 256

W_DN = 3 * H_DN * DK_DN
W_FOX = 3 * H_FOX * D_FOX
VMEM_LIMIT = 56 * 1024 * 1024

LANE_BETA, LANE_G, LANE_F = 0, 4, 8
LANE_GRP, LANE_EXP = 0, 32
R_GATE0, R_GATE1, R_EID0, R_EID1, R_RANK0, R_RANK1 = 0, 1, 2, 3, 4, 5

NN = (((1,), (0,)), ((), ()))
NT = (((1,), (1,)), ((), ()))
TN = (((0,), (0,)), ((), ()))


def _mm(a, b, dims=NN):
    return lax.dot_general(a.astype(BF16), b.astype(BF16), dims, preferred_element_type=F32)


def _sigmoid(x):
    return 0.5 * jnp.tanh(0.5 * x) + 0.5


def _silu(x):
    return x * _sigmoid(x)


def _params(*sem):
    return pltpu.CompilerParams(dimension_semantics=sem, vmem_limit_bytes=VMEM_LIMIT)


def _in_proj_body(*refs, col_chunk, combine, tiles_per_seq, hrows, rchunk):
    if combine:
        r0_ref, r1_ref, xn_ref, route_ref, g_ref, w_ref, ws_ref, dw_ref, vec_ref, x_out_ref = refs[:10]
        x = _combine_rows(r0_ref, r1_ref, xn_ref, route_ref)
        x_out_ref[...] = x
        rest = refs[10:]
    else:
        x_ref, g_ref, w_ref, ws_ref, dw_ref, vec_ref = refs[:6]
        x = x_ref[...]
        rest = refs[6:]
    dn_ref, za_ref, fox_ref, zc_ref, gates_ref, small_ref, z_ref, zs_ref, y_ref = rest
    tm = x.shape[0]
    h = x * lax.rsqrt(jnp.mean(x * x, axis=-1, keepdims=True) + EPS) * g_ref[...]
    hb = h.astype(BF16)

    def project(col, width):
        return jnp.dot(hb, w_ref[:, col:col + width], preferred_element_type=F32)

    starts, col = {}, 0
    for name, width in (("dn", dn_ref.shape[-1]), ("za", za_ref.shape[-1]), ("fox", fox_ref.shape[-1]),
                        ("conv", 2 * C_CONV), ("gates", gates_ref.shape[-1])):
        starts[name], col = col, col + width

    first = pl.program_id(0) % tiles_per_seq == 0
    col = starts["conv"]

    @pl.when(first)
    def _():
        z_ref[0:hrows, :] = jnp.zeros((hrows, C_CONV), F32)

    @pl.when(jnp.logical_not(first))
    def _():
        z_ref[0:hrows, :] = z_ref[tm:tm + hrows, :]

    z_ref[hrows:hrows + tm, :] = project(col, C_CONV) * _sigmoid(project(col + C_CONV, C_CONV))
    sub = SUBLANES
    span = tm + hrows - sub
    for phase in range(1, sub):
        zs_ref[phase - 1, 0:span, :] = z_ref[phase:phase + span, :]

    def conv_unit(lg, r0, after):
        lanes = slice(lg * LANES, (lg + 1) * LANES)
        acc = vec_ref[0:1, lanes] + jnp.minimum(jnp.abs(after[0:rchunk, 0:LANES]), 0.0)
        for k in range(CONV_WIDTH):
            off = hrows - (CONV_WIDTH - 1) + k
            phase, base = off % sub, off - off % sub + r0
            if phase == 0:
                tap = z_ref[base:base + rchunk, lanes]
            else:
                tap = zs_ref[phase - 1, base:base + rchunk, lanes]
            acc = acc + dw_ref[k:k + 1, lanes] * tap
        y_ref[r0:r0 + rchunk, lanes] = acc

    units = [(lg, r0) for lg in range(C_CONV // LANES) for r0 in range(0, tm, rchunk)]
    chunks = [(ref, starts[name], c) for name, ref in (("dn", dn_ref), ("za", za_ref), ("fox", fox_ref),
                                                       ("gates", gates_ref))
              for c in range(0, ref.shape[-1], col_chunk)]
    share = [[] for _ in chunks]
    for u, unit in enumerate(units):
        share[u * len(chunks) // len(units)].append(unit)
    for (ref, start, c), mine in zip(chunks, share):
        res = project(start + c, col_chunk)
        ref[:, c:c + col_chunk] = res.astype(ref.dtype)
        for lg, r0 in mine:
            conv_unit(lg, r0, res)
    small_ref[...] = jnp.dot(hb, ws_ref[...], preferred_element_type=F32)

    y = y_ref[...]
    mu = jnp.mean(y, axis=-1, keepdims=True)
    yc = y - mu
    var = jnp.mean(yc * yc, axis=-1, keepdims=True)
    yn = yc * lax.rsqrt(var + EPS) * vec_ref[1:2, :] + vec_ref[2:3, :]
    zc_ref[...] = _silu(yn).astype(zc_ref.dtype)


def _in_proj(x_src, g, w, ws, dw, vec, *, seq_len, tm=256, col_chunk=512, hrows=32, rchunk=32):
    combine = isinstance(x_src, tuple)
    n, d = (x_src[1] if combine else x_src).shape
    nt = n // tm
    widths = (W_DN, H_DN * DV_DN, W_FOX, C_CONV, 3 * d)
    assert w.shape == (d, sum(widths) + C_CONV) and n % tm == 0 and seq_len % tm == 0
    assert hrows >= CONV_WIDTH - 1 and hrows % SUBLANES == 0 and tm % rchunk == 0
    row = lambda i: (i, 0)
    fixed = lambda i: (0, 0)
    out_shape = [jax.ShapeDtypeStruct((n, wd), BF16) for wd in widths] + [jax.ShapeDtypeStruct((n, LANES), F32)]
    out_specs = [pl.BlockSpec((tm, wd), row) for wd in widths] + [pl.BlockSpec((tm, LANES), row)]
    if combine:
        rows2, xn, route = x_src
        args = (rows2, rows2, xn, route)
        in_specs = [pl.BlockSpec((tm, d // 2), row), pl.BlockSpec((tm, d // 2), lambda i: (i + nt, 0)),
                    pl.BlockSpec((tm, d), row), pl.BlockSpec((tm, LANES), row)]
        out_shape = [jax.ShapeDtypeStruct((n, d), F32)] + out_shape
        out_specs = [pl.BlockSpec((tm, d), row)] + out_specs
    else:
        args = (x_src,)
        in_specs = [pl.BlockSpec((tm, d), row)]
    return pl.pallas_call(
        functools.partial(_in_proj_body, col_chunk=col_chunk, combine=combine, tiles_per_seq=seq_len // tm,
                          hrows=hrows, rchunk=rchunk),
        grid=(nt,),
        in_specs=in_specs + [pl.BlockSpec((1, d), fixed),
                             pl.BlockSpec(w.shape, fixed, pipeline_mode=pl.Buffered(1)),
                             pl.BlockSpec(ws.shape, fixed, pipeline_mode=pl.Buffered(1)),
                             pl.BlockSpec(dw.shape, fixed), pl.BlockSpec(vec.shape, fixed)],
        out_specs=out_specs,
        out_shape=out_shape,
        scratch_shapes=[pltpu.VMEM((tm + hrows, C_CONV), F32),
                        pltpu.VMEM((SUBLANES - 1, tm + hrows, C_CONV), F32),
                        pltpu.VMEM((tm, C_CONV), F32)],
        compiler_params=_params("arbitrary"),
        name="in_proj",
    )(*args, g, w, ws, dw, vec)


def _softplus_parts(z):
    t = jnp.log1p(jnp.exp(-jnp.abs(z)))
    return jnp.maximum(z, 0.0) + t, -(jnp.maximum(-z, 0.0) + t)


def _delta_body(qkv_ref, za_ref, sm_ref, cw_ref, par_ref, oa_ref, qaug_ref, kaug_ref,
                xs_ref, s_ref, carry_ref, *, ts):
    j = pl.program_id(1)
    halo = SUBLANES
    pack = 2 * SUBLANES

    @pl.when(j == 0)
    def _():
        xs_ref[0:halo, :] = jnp.zeros((halo, W_DN), F32)
        s_ref[...] = jnp.zeros_like(s_ref)
        carry_ref[...] = jnp.zeros_like(carry_ref)

    @pl.when(j > 0)
    def _():
        xs_ref[0:halo, :] = xs_ref[2 * halo:3 * halo, :]

    xb = qkv_ref[0]
    xs_ref[halo:2 * halo, :] = qkv_ref[0, 0:pack, :].astype(F32)[0:halo]
    xs_ref[2 * halo:3 * halo, :] = qkv_ref[0, ts - pack:ts, :].astype(F32)[pack - halo:pack]

    lag = lax.broadcasted_iota(I32, (ts, ts), 0) - lax.broadcasted_iota(I32, (ts, ts), 1)
    shifts = jnp.concatenate([jnp.where(lag == s, 1.0, 0.0).astype(BF16) for s in range(1, SHORT_CONV)], axis=0)
    stacked = jnp.dot(shifts, xb, preferred_element_type=F32)
    shifted = [stacked[(s - 1) * ts:s * ts] for s in range(1, SHORT_CONV)]
    head_row = lax.broadcasted_iota(I32, (halo, LANES), 0)

    def conv_silu(lane0):
        lanes = slice(lane0, lane0 + LANES)
        acc = cw_ref[SHORT_CONV - 1:SHORT_CONV, lanes] * xb[:, lanes].astype(F32)
        for s in range(1, SHORT_CONV):
            acc = acc + cw_ref[SHORT_CONV - 1 - s:SHORT_CONV - s, lanes] * shifted[s - 1][:, lanes]
        head = acc[0:halo]
        for s in range(1, SHORT_CONV):
            prev = jnp.where(head_row < s, xs_ref[halo - s:2 * halo - s, lanes], 0.0)
            head = head + cw_ref[SHORT_CONV - 1 - s:SHORT_CONV - s, lanes] * prev
        return _silu(jnp.concatenate([head, acc[halo:]], axis=0))

    def l2n(a):
        return a * lax.rsqrt(jnp.sum(a * a, axis=-1, keepdims=True) + EPS)

    sm = sm_ref[0]
    lane = lax.broadcasted_iota(I32, sm.shape, 1)
    sp, logsig = _softplus_parts(sm + par_ref[1:2, :])
    vals = jnp.where(lane < LANE_G, _sigmoid(sm),
                     jnp.where(lane < LANE_F, par_ref[0:1, :] * sp,
                               jnp.where(lane < LANE_F + H_FOX, logsig, 0.0)))
    row = lax.broadcasted_iota(I32, (ts, ts), 0)
    colm = lax.broadcasted_iota(I32, (ts, ts), 1)
    log_chunk = CHUNK.bit_length() - 1
    causal = (row >= colm) & ((row >> log_chunk) == (colm >> log_chunk))

    hi = vals.astype(BF16)
    rem = vals - hi.astype(F32)
    mid = rem.astype(BF16)
    lo = (rem - mid.astype(F32)).astype(BF16)
    pieces = jnp.concatenate([hi, mid, lo], axis=-1)

    def cumsum(mask):
        y = jnp.dot(jnp.where(mask, 1.0, 0.0).astype(BF16), pieces, preferred_element_type=F32)
        return (y[:, :LANES] + y[:, LANES:2 * LANES]) + y[:, 2 * LANES:]

    ccum = cumsum(row >= colm) + carry_ref[...]
    gcum = cumsum(causal)
    carry_ref[...] = ccum[ts - 1:ts, :]
    gcum_t = gcum.T

    cl = ccum * LOG2E
    c_hi = cl.astype(BF16)
    c_rem = cl - c_hi.astype(F32)
    c_mid = c_rem.astype(BF16)
    c_lo = (c_rem - c_mid.astype(F32)).astype(BF16)
    c_pieces = jnp.concatenate([c_hi, c_mid, c_lo], axis=-1)
    pr = lax.broadcasted_iota(I32, (3 * LANES, LANES), 0)
    pc = lax.broadcasted_iota(I32, (3 * LANES, LANES), 1)
    src_lane, piece = pr & (LANES - 1), pr >> (LANES.bit_length() - 1)
    owned = (src_lane >= LANE_F) & (src_lane < LANE_F + H_FOX) & ((pc >> 3) == src_lane - LANE_F)
    place_q = jnp.where(owned & ((pc & 7) == piece), 1.0, 0.0).astype(BF16)
    place_k = jnp.where(owned & ((pc & 7) == piece + 3), -1.0, 0.0).astype(BF16)
    slot = lane & 7
    in_heads = lane < 8 * H_FOX
    ones_q = jnp.where(in_heads & (slot >= 3) & (slot < 6), 1.0, 0.0)
    ones_k = jnp.where(in_heads & (slot < 3), 1.0, 0.0)
    qaug_ref[0] = (jnp.dot(c_pieces, place_q, preferred_element_type=F32) + ones_q).astype(BF16)
    kaug_ref[0] = (jnp.dot(c_pieces, place_k, preferred_element_type=F32) + ones_k).astype(BF16)

    scale = DK_DN ** -0.5
    dn_norm = par_ref[2:3, :]

    pw = 2 * CHUNK
    prow = lax.broadcasted_iota(I32, (pw, pw), 0)
    pcol = lax.broadcasted_iota(I32, (pw, pw), 1)
    same = (prow >> log_chunk) == (pcol >> log_chunk)
    causal_p = (prow >= pcol) & same
    strict_p = (prow > pcol) & same
    levels = []
    s = 1
    while s < CHUNK:
        levels.append(((prow >> s.bit_length()) == (pcol >> s.bit_length()))
                      & ((prow & s) != 0) & ((pcol & s) == 0))
        s *= 2

    heads = range(H_DN)
    pairs = range(ts // pw)
    q, k, v = [], [], []
    for h in heads:
        q.append(l2n(conv_silu(h * DK_DN)))
        k.append(l2n(conv_silu(H_DN * DK_DN + h * DK_DN)))
        v.append(conv_silu(2 * H_DN * DK_DN + h * DV_DN))

    ctx = []
    for h in heads:
        for p in pairs:
            pr = slice(p * pw, (p + 1) * pw)
            gcol = gcum[pr, LANE_G + h:LANE_G + h + 1]
            grow = gcum_t[LANE_G + h:LANE_G + h + 1, pr]
            beta = vals[pr, LANE_BETA + h:LANE_BETA + h + 1]
            decay = jnp.where(causal_p, jnp.exp(jnp.where(causal_p, gcol - grow, 0.0)), 0.0)
            egc = jnp.exp(gcol)
            kp = k[h][pr]
            kb = kp * beta
            ctx.append(dict(
                h=h, p=p, gcol=gcol, kp=kp,
                a=jnp.where(strict_p, _mm(kb, kp, NT) * decay, 0.0),
                qk=jnp.where(causal_p, _mm(q[h][pr] * scale, kp, NT) * decay, 0.0),
                rhs=jnp.concatenate([v[h][pr] * beta, kb * egc], axis=-1),
                qg=q[h][pr] * (scale * egc)))

    for c in ctx:
        c["n"] = -jnp.where(levels[0], c["a"], 0.0)
    for level in levels[1:]:
        for c in ctx:
            m = jnp.where(level, c["a"], 0.0)
            c["y"] = m + _mm(c["n"], m)
        for c in ctx:
            c["n"] = c["n"] - (c["y"] + _mm(c["y"], c["n"]))
    for c in ctx:
        c["sol"] = c["rhs"] + _mm(c["n"], c["rhs"])

    state = [s_ref[h] for h in heads]
    for p in pairs:
        group = [c for c in ctx if c["p"] == p]
        v_prev = [None] * H_DN
        for ch in range(2):
            rows = slice(ch * CHUNK, (ch + 1) * CHUNK)
            out_rows = slice(p * pw + ch * CHUNK, p * pw + (ch + 1) * CHUNK)
            on_state = [_mm(jnp.concatenate([c["sol"][rows, DV_DN:], c["qg"][rows]], axis=0), state[c["h"]])
                        for c in group]
            for c, ws_qs in zip(group, on_state):
                h = c["h"]
                vn = c["sol"][rows, :DV_DN] - ws_qs[:CHUNK]
                v_pair = jnp.concatenate([vn, jnp.zeros_like(vn)] if ch == 0 else [v_prev[h], vn], axis=0)
                o = ws_qs[CHUNK:] + _mm(c["qk"][rows], v_pair)
                glast = c["gcol"][(ch + 1) * CHUNK - 1:(ch + 1) * CHUNK]
                state[h] = (state[h] * jnp.exp(glast)
                            + _mm(c["kp"][rows] * jnp.exp(glast - c["gcol"][rows]), vn, TN))
                v_prev[h] = vn
                o = o * lax.rsqrt(jnp.mean(o * o, axis=-1, keepdims=True) + EPS) * dn_norm
                za = za_ref[0, out_rows, h * DV_DN:(h + 1) * DV_DN].astype(F32)
                oa_ref[0, out_rows, h * DV_DN:(h + 1) * DV_DN] = (o * _silu(za)).astype(oa_ref.dtype)
    for h in heads:
        s_ref[h] = state[h]


def _delta_rule(qkv, za, small, conv_w, par, *, ts=256):
    b, t, _ = qkv.shape
    assert t % ts == 0 and ts % (2 * CHUNK) == 0
    blk = lambda width: pl.BlockSpec((1, ts, width), lambda bi, j: (bi, j, 0))
    fixed = lambda bi, j: (0, 0)
    return pl.pallas_call(
        functools.partial(_delta_body, ts=ts),
        grid=(b, t // ts),
        in_specs=[blk(W_DN), blk(H_DN * DV_DN), blk(LANES),
                  pl.BlockSpec(conv_w.shape, fixed), pl.BlockSpec(par.shape, fixed)],
        out_specs=[blk(H_DN * DV_DN), blk(LANES), blk(LANES)],
        out_shape=[jax.ShapeDtypeStruct((b, t, H_DN * DV_DN), BF16),
                   jax.ShapeDtypeStruct((b, t, LANES), BF16),
                   jax.ShapeDtypeStruct((b, t, LANES), BF16)],
        scratch_shapes=[pltpu.VMEM((3 * SUBLANES, W_DN), F32),
                        pltpu.VMEM((H_DN, DK_DN, DV_DN), F32),
                        pltpu.VMEM((1, LANES), F32)],
        compiler_params=_params("parallel", "arbitrary"),
        name="delta_rule",
    )(qkv, za, small, conv_w, par)


def _fox_body(q_ref, k_ref, v_ref, qa_ref, ka_ref, o_ref, m_ref, acc_ref, *, tq, tk):
    i = pl.program_id(1)
    m_ref[...] = jnp.full(m_ref.shape, -jnp.inf, F32)
    acc_ref[...] = jnp.zeros_like(acc_ref)
    head_lanes = [slice(h * D_FOX, (h + 1) * D_FOX) for h in range(H_FOX)]
    lane = lax.broadcasted_iota(I32, (tk, LANES), 1)
    own = [jnp.where((lane >> 3) == h, 1.0, 0.0).astype(BF16) for h in range(H_FOX)]
    ones = jnp.ones((tk, D_FOX), BF16)
    keep = lax.broadcasted_iota(I32, (tq, tk), 1) <= lax.broadcasted_iota(I32, (tq, tk), 0)

    def block(start, diag_offset):
        rows = slice(0 if diag_offset is None else diag_offset, tq)
        ka = ka_ref[0, pl.ds(start, tk), :]
        for h, lanes in enumerate(head_lanes):
            q_aug = jnp.concatenate([q_ref[0, rows, lanes], qa_ref[0, rows, :]], axis=1)
            k_aug = jnp.concatenate([k_ref[0, pl.ds(start, tk), lanes], ka * own[h]], axis=1)
            s = lax.dot_general(q_aug, k_aug, NT, preferred_element_type=F32)
            if diag_offset is not None:
                s = jnp.where(keep[:tq - diag_offset], s, -jnp.inf)
            m_prev = m_ref[h, rows]
            m_next = jnp.maximum(m_prev, jnp.max(s, axis=-1, keepdims=True))
            p = jnp.exp2(s - jnp.concatenate([m_next] * (tk // LANES), axis=1))
            alpha = jnp.exp2(m_prev - m_next)
            v_aug = jnp.concatenate([v_ref[0, pl.ds(start, tk), lanes], ones], axis=1)
            acc_ref[h, rows] = (jnp.concatenate([alpha, alpha], axis=1) * acc_ref[h, rows]
                                + jnp.dot(p.astype(BF16), v_aug, preferred_element_type=F32))
            m_ref[h, rows] = m_next

    def full_block(jb, carry):
        block(pl.multiple_of(jb * tk, tk), None)
        return carry

    lax.fori_loop(0, i * (tq // tk), full_block, 0)
    for d in range(tq // tk):
        block(pl.multiple_of(i * tq + d * tk, tk), d * tk)
    for h, lanes in enumerate(head_lanes):
        acc = acc_ref[h]
        o_ref[0, :, lanes] = (acc[:, :D_FOX] / acc[:, D_FOX:]).astype(o_ref.dtype)


def _fox_attention(qkv, qaug, kaug, *, tq=1024, tk=512):
    b, t, _ = qkv.shape
    hd = H_FOX * D_FOX
    assert t % tq == 0 and tq % tk == 0 and tk % LANES == 0
    return pl.pallas_call(
        functools.partial(_fox_body, tq=tq, tk=tk),
        grid=(b, t // tq),
        in_specs=[pl.BlockSpec((1, tq, hd), lambda bi, i: (bi, i, 0)),
                  pl.BlockSpec((1, t, hd), lambda bi, i: (bi, 0, 1)),
                  pl.BlockSpec((1, t, hd), lambda bi, i: (bi, 0, 2)),
                  pl.BlockSpec((1, tq, LANES), lambda bi, i: (bi, i, 0)),
                  pl.BlockSpec((1, t, LANES), lambda bi, i: (bi, 0, 0))],
        out_specs=pl.BlockSpec((1, tq, hd), lambda bi, i: (bi, i, 0)),
        out_shape=jax.ShapeDtypeStruct((b, t, hd), BF16),
        scratch_shapes=[pltpu.VMEM((H_FOX, tq, LANES), F32), pltpu.VMEM((H_FOX, tq, 2 * D_FOX), F32)],
        compiler_params=_params("parallel", "arbitrary"),
        name="fox_attention",
    )(qkv, qkv, qkv, qaug, kaug)


def _merge_body(oa_ref, ob_ref, zc_ref, gates_ref, x_ref, wa_ref, wb_ref, wc_ref, wo_ref, g_ref,
                wr_ref, br_ref, xn_ref, hp_ref, route_ref, meta_ref, cnt_ref, run_ref, *, tm):
    step = pl.program_id(0)
    d = x_ref.shape[-1]

    @pl.when(step == 0)
    def _():
        run_ref[...] = jnp.zeros_like(run_ref)

    merged = None
    for idx, (m_ref, w_ref) in enumerate(((oa_ref, wa_ref), (ob_ref, wb_ref), (zc_ref, wc_ref))):
        y = jnp.dot(m_ref[...], w_ref[...], preferred_element_type=F32)
        term = _sigmoid(gates_ref[:, idx * d:(idx + 1) * d].astype(F32)) * y
        merged = term if merged is None else merged + term
    xn = x_ref[...] + jnp.dot(merged.astype(BF16), wo_ref[...], preferred_element_type=F32)
    xn_ref[...] = xn
    h2 = xn * lax.rsqrt(jnp.mean(xn * xn, axis=-1, keepdims=True) + EPS) * g_ref[...]

    hp_ref[...] = _pack_bf16_pairs(h2)

    logits = lax.dot_general(wr_ref[...], h2.astype(BF16), NT, preferred_element_type=F32) + br_ref[...]
    big = jnp.int32(LANES)
    gl = logits[LANE_GRP:LANE_GRP + N_GROUPS]
    g_row = lax.broadcasted_iota(I32, gl.shape, 0)
    gmax = jnp.max(gl, axis=0, keepdims=True)
    grp = jnp.min(jnp.where(gl == gmax, g_row, big), axis=0, keepdims=True)
    p_grp = 1.0 / jnp.sum(jnp.exp(gl - gmax), axis=0, keepdims=True)
    el_all = logits[LANE_EXP:LANE_EXP + N_EXPERTS]
    e_row = lax.broadcasted_iota(I32, el_all.shape, 0)
    el = jnp.where((e_row >> (EXPERTS_PER_GROUP.bit_length() - 1)) == grp, el_all, -jnp.inf)
    v0 = jnp.max(el, axis=0, keepdims=True)
    i0 = jnp.min(jnp.where(el == v0, e_row, big), axis=0, keepdims=True)
    el1 = jnp.where(e_row == i0, -jnp.inf, el)
    v1 = jnp.max(el1, axis=0, keepdims=True)
    i1 = jnp.min(jnp.where(el1 == v1, e_row, big), axis=0, keepdims=True)
    e1 = jnp.exp(v1 - v0)
    gate0 = p_grp / (1.0 + e1)
    gate1 = p_grp * e1 / (1.0 + e1)

    hot0 = e_row == i0
    hot1 = e_row == i1
    onehot = jnp.where(hot0 | hot1, 1.0, 0.0)
    tiles = [onehot[:, t * LANES:(t + 1) * LANES] for t in range(tm // LANES)]
    ri = lax.broadcasted_iota(I32, (LANES, LANES), 0)
    cj = lax.broadcasted_iota(I32, (LANES, LANES), 1)
    in_tile = _mm(jnp.concatenate(tiles, axis=0), jnp.where(ri < cj, 1.0, 0.0))
    before = run_ref[...]
    prefix = []
    for t, tile in enumerate(tiles):
        prefix.append(in_tile[t * N_EXPERTS:(t + 1) * N_EXPERTS] + before)
        before = before + jnp.sum(tile, axis=1, keepdims=True)
    prefix = jnp.concatenate(prefix, axis=1)
    run_ref[...] = before
    cnt_ref[...] = jnp.broadcast_to(before, cnt_ref.shape)
    rank0 = jnp.sum(jnp.where(hot0, prefix, 0.0), axis=0, keepdims=True)
    rank1 = jnp.sum(jnp.where(hot1, prefix, 0.0), axis=0, keepdims=True)

    rows = {R_GATE0: gate0, R_GATE1: gate1, R_EID0: i0.astype(F32), R_EID1: i1.astype(F32),
            R_RANK0: rank0, R_RANK1: rank1}
    fields = jnp.concatenate([rows.get(r, jnp.zeros_like(gate0)) for r in range(SUBLANES)], axis=0)
    meta_ref[...] = fields.astype(I32)
    route_ref[...] = jnp.concatenate([fields, jnp.zeros((LANES - SUBLANES, tm), F32)], axis=0).T


def _merge(oa, ob, zc, gates, x2, wa, wb, wc, wo, g, wr, br, *, tm=1024):
    n, d = x2.shape
    assert n % tm == 0
    row = lambda i: (i, 0)
    fixed = lambda i: (0, 0)
    full = lambda a: pl.BlockSpec(a.shape, fixed)
    return pl.pallas_call(
        functools.partial(_merge_body, tm=tm),
        grid=(n // tm,),
        in_specs=[pl.BlockSpec((tm, oa.shape[1]), row), pl.BlockSpec((tm, ob.shape[1]), row),
                  pl.BlockSpec((tm, zc.shape[1]), row), pl.BlockSpec((tm, 3 * d), row),
                  pl.BlockSpec((tm, d), row),
                  full(wa), full(wb), full(wc), full(wo), full(g), full(wr), full(br)],
        out_specs=[pl.BlockSpec((tm, d), row), pl.BlockSpec((tm, d // 2), row),
                   pl.BlockSpec((tm, LANES), row), pl.BlockSpec((SUBLANES, tm), lambda i: (0, i)),
                   pl.BlockSpec((N_EXPERTS, LANES), fixed)],
        out_shape=[jax.ShapeDtypeStruct((n, d), F32), jax.ShapeDtypeStruct((n, d // 2), U32),
                   jax.ShapeDtypeStruct((n, LANES), F32), jax.ShapeDtypeStruct((SUBLANES, n), I32),
                   jax.ShapeDtypeStruct((N_EXPERTS, LANES), F32)],
        scratch_shapes=[pltpu.VMEM((N_EXPERTS, 1), F32)],
        compiler_params=_params("arbitrary"),
        name="merge_router",
    )(oa, ob, zc, gates, x2, wa, wb, wc, wo, g, wr, br)


def _sc_mesh():
    return plsc.VectorSubcoreMesh(core_axis_name="c", subcore_axis_name="s")


def _sc_worker_base(per_worker):
    return (lax.axis_index("s") * SC_CORES + lax.axis_index("c")) * per_worker


def _sc_row_scatter(src, idx, out_rows, *, chunk=128):
    n, d = src.shape
    per_worker = n // (SC_CORES * SC_SUBCORES)
    chunk = min(chunk, per_worker)
    assert idx.shape == (2 * n,) and n % (SC_CORES * SC_SUBCORES) == 0 and per_worker % chunk == 0

    @functools.partial(
        pl.kernel, mesh=_sc_mesh(), out_type=jax.ShapeDtypeStruct((out_rows, d), src.dtype),
        scratch_types=[pltpu.VMEM((chunk,), I32), pltpu.VMEM((chunk, d), src.dtype), pltpu.SemaphoreType.DMA])
    def scatter(src_hbm, idx_hbm, out_hbm, idx_v, rows_v, sem):
        base = _sc_worker_base(per_worker)

        @pl.loop(0, per_worker // chunk)
        def _(step):
            off = base + step * chunk
            pltpu.sync_copy(src_hbm.at[pl.ds(off, chunk)], rows_v)
            for slot in range(2):
                pltpu.sync_copy(idx_hbm.at[pl.ds(slot * n + off, chunk)], idx_v)
                pltpu.async_copy(rows_v, out_hbm.at[idx_v], sem).wait()

    return scatter(src, idx)


def _pack_bf16_pairs(x):
    half = x.shape[-1] // 2
    lo = pltpu.bitcast(x[:, :half].astype(BF16).astype(F32), U32) >> 16
    hi = pltpu.bitcast(x[:, half:].astype(BF16).astype(F32), U32) & jnp.uint32(0xFFFF0000)
    return lo | hi


def _unpack_bf16_pairs(xp):
    return pltpu.bitcast(xp << 16, F32), pltpu.bitcast(xp & jnp.uint32(0xFFFF0000), F32)


def _expert_body(be_ref, valid_ref, nu_ref, xs_ref, w1_ref, w3_ref, w2_ref, ys_ref):
    del be_ref
    step = pl.program_id(0)
    live = step < nu_ref[0]

    @pl.when(jnp.logical_not(live))
    def _():
        ys_ref[...] = jnp.zeros_like(ys_ref)

    @pl.when(live)
    def _():
        row = lax.broadcasted_iota(I32, xs_ref.shape, 0)
        xp = jnp.where(row < valid_ref[step], xs_ref[...], jnp.uint32(0))
        half = xp.shape[-1]
        lo, hi = (part.astype(BF16) for part in _unpack_bf16_pairs(xp))

        def up(w_ref):
            return (jnp.dot(lo, w_ref[0, :half, :].astype(BF16), preferred_element_type=F32)
                    + jnp.dot(hi, w_ref[0, half:, :].astype(BF16), preferred_element_type=F32))

        act = (_silu(up(w1_ref)) * up(w3_ref)).astype(BF16)
        ys_ref[...] = _pack_bf16_pairs(jnp.dot(act, w2_ref[0].astype(BF16), preferred_element_type=F32))


def _experts(blk_eid, blk_valid, n_used, xs, w1, w3, w2, *, rb):
    p, half = xs.shape
    d = 2 * half
    nb = p // rb
    used = lambda i, be, bv, nu: jnp.maximum(jnp.minimum(i, nu[0] - 1), 0)
    wmap = lambda i, be, bv, nu: (be[used(i, be, bv, nu)], 0, 0)
    grid_spec = pltpu.PrefetchScalarGridSpec(
        num_scalar_prefetch=3,
        grid=(nb,),
        in_specs=[pl.BlockSpec((rb, half), lambda i, be, bv, nu: (used(i, be, bv, nu), 0)),
                  pl.BlockSpec((1, d, D_EXPERT), wmap), pl.BlockSpec((1, d, D_EXPERT), wmap),
                  pl.BlockSpec((1, D_EXPERT, d), wmap)],
        out_specs=pl.BlockSpec((rb, half), lambda i, be, bv, nu: (i, 0)),
    )
    return pl.pallas_call(
        _expert_body,
        grid_spec=grid_spec,
        out_shape=jax.ShapeDtypeStruct((p, half), U32),
        compiler_params=_params("arbitrary"),
        name="moe_experts",
    )(blk_eid, blk_valid, n_used, xs, w1, w3, w2)


def _sc_row_gather(table, idx, *, chunk=128):
    rows, d = idx.shape[0], table.shape[1]
    per_worker = rows // (SC_CORES * SC_SUBCORES)
    chunk = min(chunk, per_worker)
    assert rows % (SC_CORES * SC_SUBCORES) == 0 and per_worker % chunk == 0

    @functools.partial(
        pl.kernel, mesh=_sc_mesh(), out_type=jax.ShapeDtypeStruct((rows, d), table.dtype),
        scratch_types=[pltpu.VMEM((chunk,), I32), pltpu.VMEM((chunk, d), table.dtype), pltpu.SemaphoreType.DMA])
    def gather(table_hbm, idx_hbm, out_hbm, idx_v, rows_v, sem):
        base = _sc_worker_base(per_worker)

        @pl.loop(0, per_worker // chunk)
        def _(step):
            off = base + step * chunk
            pltpu.sync_copy(idx_hbm.at[pl.ds(off, chunk)], idx_v)
            pltpu.async_copy(table_hbm.at[idx_v], rows_v, sem).wait()
            pltpu.sync_copy(rows_v, out_hbm.at[pl.ds(off, chunk)])

    return gather(table, idx)


def _combine_rows(r0_ref, r1_ref, x_ref, route_ref):
    route = route_ref[...]
    half = r0_ref.shape[-1]
    lo0, hi0 = _unpack_bf16_pairs(r0_ref[...])
    lo1, hi1 = _unpack_bf16_pairs(r1_ref[...])
    g0, g1 = route[:, R_GATE0:R_GATE0 + 1], route[:, R_GATE1:R_GATE1 + 1]
    return jnp.concatenate([x_ref[:, :half] + g0 * lo0 + g1 * lo1,
                            x_ref[:, half:] + g0 * hi0 + g1 * hi1], axis=1)


def _final_body(r0_ref, r1_ref, x_ref, route_ref, g_ref, o_ref):
    out = _combine_rows(r0_ref, r1_ref, x_ref, route_ref)
    o_ref[...] = out * lax.rsqrt(jnp.mean(out * out, axis=-1, keepdims=True) + EPS) * g_ref[...]


def _final_combine(rows2, xn, route, g, *, tm):
    n, d = xn.shape
    nt = n // tm
    row = lambda i: (i, 0)
    return pl.pallas_call(
        _final_body,
        grid=(nt,),
        in_specs=[pl.BlockSpec((tm, d // 2), row), pl.BlockSpec((tm, d // 2), lambda i: (i + nt, 0)),
                  pl.BlockSpec((tm, d), row), pl.BlockSpec((tm, LANES), row),
                  pl.BlockSpec((1, d), lambda i: (0, 0))],
        out_specs=pl.BlockSpec((tm, d), row),
        out_shape=jax.ShapeDtypeStruct((n, d), F32),
        compiler_params=_params("parallel"),
        name="moe_combine",
    )(rows2, rows2, xn, route, g)


def _moe_rows(hp, meta, counts, w1, w3, w2, *, layer, rb):
    n = hp.shape[0]
    cnt = counts[:, 0].astype(I32)
    nblk = (cnt + rb - 1) // rb
    bend = jnp.cumsum(nblk)
    pstart = (bend - nblk) * rb
    nb = (2 * n) // rb + N_EXPERTS
    n_used = bend[-1:].astype(I32)
    blk = jnp.arange(nb, dtype=I32)
    blk_eid = jnp.minimum(jnp.sum(bend[None, :] <= blk[:, None], axis=1), N_EXPERTS - 1).astype(I32)
    experts = jnp.arange(N_EXPERTS, dtype=I32)
    mine = blk_eid[:, None] == experts[None, :]
    blk_valid = jnp.clip(jnp.sum(jnp.where(mine, (cnt + pstart)[None, :], 0), axis=1) - blk * rb, 0, rb)
    eid = meta[R_EID0:R_EID1 + 1]
    first = jnp.sum(jnp.where(eid[None] == experts[:, None, None], pstart[:, None, None], 0), axis=0)
    dest = (first + meta[R_RANK0:R_RANK1 + 1]).reshape(-1)

    xs = _sc_row_scatter(hp, dest, nb * rb)
    ys = _experts(blk_eid + layer * N_EXPERTS, blk_valid.astype(I32), n_used, xs, w1, w3, w2, rb=rb)
    return _sc_row_gather(ys, dest)


def _lane_row(pairs):
    row = jnp.zeros((LANES,), F32)
    for off, vec in pairs:
        row = row.at[off:off + vec.shape[0]].set(vec.astype(F32))
    return row


def kernel(x, norm_mix, w_in, conv_qkv, dn_a_log, dn_dt_bias, dn_norm, fox_bias, conv_dw, conv_dw_b,
           conv_ln_g, conv_ln_b, w_a, w_b, w_c, w_out, norm_ffn, router_group_w, router_group_b,
           router_expert_w, router_expert_b, expert_w1, expert_w3, expert_w2, norm_final):
    b, t, d = x.shape
    n = b * t
    depth = w_in.shape[0]
    qk_dn = H_DN * DK_DN
    in_sizes = (qk_dn, qk_dn, H_DN * DV_DN, H_DN * DV_DN, H_DN, H_DN,
                H_FOX * D_FOX, H_FOX * D_FOX, H_FOX * D_FOX, H_FOX, 2 * C_CONV, d, d, d)
    splits = np.cumsum(in_sizes)[:-1].tolist()
    tm_final, rb = 512, 512

    x_src = x.reshape(n, d)
    w1_all = expert_w1.reshape(depth * N_EXPERTS, d, D_EXPERT)
    w3_all = expert_w3.reshape(depth * N_EXPERTS, d, D_EXPERT)
    w2_all = expert_w2.reshape(depth * N_EXPERTS, D_EXPERT, d)
    for l in range(depth):
        (qa, ka, va, za, ba, aa, qb, kb, vb, fb, uc, ga, gb, gc) = jnp.split(w_in[l], splits, axis=1)
        w_main = jnp.concatenate([qa, ka, va, za, qb * (D_FOX ** -0.5 * LOG2E), kb, vb, uc, ga, gb, gc],
                                 axis=1).astype(BF16)
        w_small = jnp.concatenate([ba, aa, fb, jnp.zeros((d, LANES - 3 * H_DN), F32)], axis=1).astype(BF16)
        vec = jnp.stack([conv_dw_b[l], conv_ln_g[l], conv_ln_b[l]] + [jnp.zeros((C_CONV,), F32)] * 5)
        outs = _in_proj(x_src, norm_mix[l][None, :], w_main, w_small, conv_dw[l], vec, seq_len=t)
        x2 = outs[0] if l > 0 else x_src
        dn_qkv, za_p, fox_qkv, zc, gates, small = outs[-6:]

        par = jnp.stack([_lane_row([(LANE_G, -jnp.exp(dn_a_log[l]))]),
                         _lane_row([(LANE_G, dn_dt_bias[l]), (LANE_F, fox_bias[l])]),
                         _lane_row([(0, dn_norm[l])])] + [jnp.zeros((LANES,), F32)] * 5)
        oa, qaug, kaug = _delta_rule(dn_qkv.reshape(b, t, -1), za_p.reshape(b, t, -1),
                                     small.reshape(b, t, LANES), conv_qkv[l], par)
        ob = _fox_attention(fox_qkv.reshape(b, t, -1), qaug, kaug)

        w_r = jnp.concatenate([router_group_w[l], jnp.zeros((d, LANE_EXP - N_GROUPS), F32),
                               router_expert_w[l], jnp.zeros((d, LANES - LANE_EXP - N_EXPERTS), F32)], axis=1)
        b_r = _lane_row([(LANE_GRP, router_group_b[l]), (LANE_EXP, router_expert_b[l])])[None, :]
        xn, hp, route, meta, counts = _merge(
            oa.reshape(n, -1), ob.reshape(n, -1), zc.reshape(n, -1), gates, x2,
            w_a[l].astype(BF16), w_b[l].astype(BF16), w_c[l].astype(BF16), w_out[l].astype(BF16),
            norm_ffn[l][None, :], w_r.T.astype(BF16), b_r.reshape(LANES, 1))

        x_src = (_moe_rows(hp, meta, counts, w1_all, w3_all, w2_all, layer=l, rb=rb), xn, route)
    return _final_combine(*x_src, norm_final[None, :], tm=tm_final).reshape(b, t, d)
```

```python
import functools

import jax
import jax.numpy as jnp
import numpy as np
from jax import lax
from jax.experimental import pallas as pl
from jax.experimental.pallas import tpu as pltpu
from jax.experimental.pallas import tpu_sc as plsc

F32 = jnp.float32
BF16 = jnp.bfloat16
U32 = jnp.uint32
I32 = jnp.int32

EPS = 1e-6
LOG2E = 1.4426950408889634
LANES = 128
SUBLANES = 8
SC_CORES, SC_SUBCORES = 2, 16
H_DN, DK_DN, DV_DN = 4, 128, 128
SHORT_CONV = 4
CHUNK = 64
H_FOX, D_FOX = 4, 128
C_CONV = 512
CONV_WIDTH = 31
N_GROUPS, EXPERTS_PER_GROUP = 4, 8
N_EXPERTS = N_GROUPS * EXPERTS_PER_GROUP
D_EXPERT = 256

W_DN = 3 * H_DN * DK_DN
W_FOX = 3 * H_FOX * D_FOX
VMEM_LIMIT = 56 * 1024 * 1024

LANE_BETA, LANE_G, LANE_F = 0, 4, 8
LANE_GRP, LANE_EXP = 0, 32
R_GATE0, R_GATE1, R_EID0, R_EID1, R_RANK0, R_RANK1 = 0, 1, 2, 3, 4, 5

NN = (((1,), (0,)), ((), ()))
NT = (((1,), (1,)), ((), ()))
TN = (((0,), (0,)), ((), ()))


def _mm(a, b, dims=NN):
    return lax.dot_general(a.astype(BF16), b.astype(BF16), dims, preferred_element_type=F32)


def _sigmoid(x):
    return 0.5 * jnp.tanh(0.5 * x) + 0.5


def _silu(x):
    half = 0.5 * x
    return half * jnp.tanh(half) + half


def _params(*sem):
    return pltpu.CompilerParams(dimension_semantics=sem, vmem_limit_bytes=VMEM_LIMIT)


def _in_proj_body(*refs, col_chunk, combine, tiles_per_seq, hrows, rchunk):
    if combine:
        r0_ref, r1_ref, xn_ref, route_ref, g_ref, w_ref, ws_ref, dw_ref, vec_ref, x_out_ref = refs[:10]
        x = _combine_rows(r0_ref, r1_ref, xn_ref, route_ref)
        x_out_ref[...] = x
        rest = refs[10:]
    else:
        x_ref, g_ref, w_ref, ws_ref, dw_ref, vec_ref = refs[:6]
        x = x_ref[...]
        rest = refs[6:]
    dn_ref, za_ref, fox_ref, zc_ref, gates_ref, small_ref, z_ref, zs_ref, y_ref = rest
    tm = x.shape[0]
    h = x * lax.rsqrt(jnp.mean(x * x, axis=-1, keepdims=True) + EPS) * g_ref[...]
    hb = h.astype(BF16)

    def project(col, width, hold=None):
        lhs = hb
        if hold is not None:
            zero = jnp.minimum(jnp.abs(hold), 0.0).astype(BF16)
            head = jnp.concatenate([hb[0:rchunk, 0:LANES] + zero, hb[0:rchunk, LANES:]], axis=1)
            lhs = jnp.concatenate([head, hb[rchunk:]], axis=0)
        return jnp.dot(lhs, w_ref[:, col:col + width], preferred_element_type=F32)

    starts, col = {}, 0
    for name, width in (("dn", dn_ref.shape[-1]), ("za", za_ref.shape[-1]), ("fox", fox_ref.shape[-1]),
                        ("conv", 2 * C_CONV), ("gates", gates_ref.shape[-1])):
        starts[name], col = col, col + width

    first = pl.program_id(0) % tiles_per_seq == 0
    col = starts["conv"]

    @pl.when(first)
    def _():
        z_ref[0:hrows, :] = jnp.zeros((hrows, C_CONV), F32)

    @pl.when(jnp.logical_not(first))
    def _():
        z_ref[0:hrows, :] = z_ref[tm:tm + hrows, :]

    z_ref[hrows:hrows + tm, :] = project(col, C_CONV) * _sigmoid(project(col + C_CONV, C_CONV))
    sub = SUBLANES
    span = tm + hrows - sub
    for phase in range(1, sub):
        zs_ref[phase - 1, 0:span, :] = z_ref[phase:phase + span, :]

    def conv_unit(lg, r0, after):
        lanes = slice(lg * LANES, (lg + 1) * LANES)
        acc = vec_ref[0:1, lanes] + jnp.minimum(jnp.abs(after[0:rchunk, 0:LANES]), 0.0)
        for k in range(CONV_WIDTH):
            off = hrows - (CONV_WIDTH - 1) + k
            phase, base = off % sub, off - off % sub + r0
            if phase == 0:
                tap = z_ref[base:base + rchunk, lanes]
            else:
                tap = zs_ref[phase - 1, base:base + rchunk, lanes]
            acc = acc + dw_ref[k:k + 1, lanes] * tap
        y_ref[r0:r0 + rchunk, lanes] = acc
        return acc

    units = [(lg, r0) for lg in range(C_CONV // LANES) for r0 in range(0, tm, rchunk)]
    chunks = [(ref, starts[name], c) for name, ref in (("dn", dn_ref), ("za", za_ref), ("fox", fox_ref),
                                                       ("gates", gates_ref))
              for c in range(0, ref.shape[-1], col_chunk)]
    share = [[] for _ in chunks]
    for u, unit in enumerate(units):
        share[u * len(chunks) // len(units)].append(unit)
    last = []
    for idx, ((ref, start, c), mine) in enumerate(zip(chunks, share)):
        res = project(start + c, col_chunk, hold=last[idx - 2] if idx >= 2 else None)
        ref[:, c:c + col_chunk] = res.astype(ref.dtype)
        tile = None
        for lg, r0 in mine:
            tile = conv_unit(lg, r0, res)
        last.append(tile)
    small_ref[...] = jnp.dot(hb, ws_ref[...], preferred_element_type=F32)

    y = y_ref[...]
    mu = jnp.mean(y, axis=-1, keepdims=True)
    yc = y - mu
    var = jnp.mean(yc * yc, axis=-1, keepdims=True)
    yn = yc * lax.rsqrt(var + EPS) * vec_ref[1:2, :] + vec_ref[2:3, :]
    zc_ref[...] = _silu(yn).astype(zc_ref.dtype)


def _in_proj(x_src, g, w, ws, dw, vec, *, seq_len, tm=256, col_chunk=512, hrows=32, rchunk=32):
    combine = isinstance(x_src, tuple)
    n, d = (x_src[1] if combine else x_src).shape
    nt = n // tm
    widths = (W_DN, H_DN * DV_DN, W_FOX, C_CONV, 3 * d)
    assert w.shape == (d, sum(widths) + C_CONV) and n % tm == 0 and seq_len % tm == 0
    assert hrows >= CONV_WIDTH - 1 and hrows % SUBLANES == 0 and tm % rchunk == 0
    row = lambda i: (i, 0)
    fixed = lambda i: (0, 0)
    out_shape = [jax.ShapeDtypeStruct((n, wd), BF16) for wd in widths] + [jax.ShapeDtypeStruct((n, LANES), F32)]
    out_specs = [pl.BlockSpec((tm, wd), row) for wd in widths] + [pl.BlockSpec((tm, LANES), row)]
    if combine:
        rows2, xn, route = x_src
        args = (rows2, rows2, xn, route)
        in_specs = [pl.BlockSpec((tm, d // 2), row), pl.BlockSpec((tm, d // 2), lambda i: (i + nt, 0)),
                    pl.BlockSpec((tm, d), row), pl.BlockSpec((tm, LANES), row)]
        out_shape = [jax.ShapeDtypeStruct((n, d), F32)] + out_shape
        out_specs = [pl.BlockSpec((tm, d), row)] + out_specs
    else:
        args = (x_src,)
        in_specs = [pl.BlockSpec((tm, d), row)]
    return pl.pallas_call(
        functools.partial(_in_proj_body, col_chunk=col_chunk, combine=combine, tiles_per_seq=seq_len // tm,
                          hrows=hrows, rchunk=rchunk),
        grid=(nt,),
        in_specs=in_specs + [pl.BlockSpec((1, d), fixed),
                             pl.BlockSpec(w.shape, fixed, pipeline_mode=pl.Buffered(1)),
                             pl.BlockSpec(ws.shape, fixed, pipeline_mode=pl.Buffered(1)),
                             pl.BlockSpec(dw.shape, fixed), pl.BlockSpec(vec.shape, fixed)],
        out_specs=out_specs,
        out_shape=out_shape,
        scratch_shapes=[pltpu.VMEM((tm + hrows, C_CONV), F32),
                        pltpu.VMEM((SUBLANES - 1, tm + hrows, C_CONV), F32),
                        pltpu.VMEM((tm, C_CONV), F32)],
        compiler_params=_params("arbitrary"),
        name="in_proj",
    )(*args, g, w, ws, dw, vec)


def _softplus_parts(z):
    t = jnp.log1p(jnp.exp(-jnp.abs(z)))
    return jnp.maximum(z, 0.0) + t, -(jnp.maximum(-z, 0.0) + t)


def _delta_body(qkv_ref, za_ref, sm_ref, cw_ref, par_ref, oa_ref, qaug_ref, kaug_ref,
                xs_ref, s_ref, carry_ref, *, ts):
    j = pl.program_id(1)
    halo = SUBLANES
    pack = 2 * SUBLANES

    @pl.when(j == 0)
    def _():
        xs_ref[0:halo, :] = jnp.zeros((halo, W_DN), F32)
        s_ref[...] = jnp.zeros_like(s_ref)
        carry_ref[...] = jnp.zeros_like(carry_ref)

    @pl.when(j > 0)
    def _():
        xs_ref[0:halo, :] = xs_ref[2 * halo:3 * halo, :]

    xb = qkv_ref[0]
    xs_ref[halo:2 * halo, :] = qkv_ref[0, 0:pack, :].astype(F32)[0:halo]
    xs_ref[2 * halo:3 * halo, :] = qkv_ref[0, ts - pack:ts, :].astype(F32)[pack - halo:pack]

    lag = lax.broadcasted_iota(I32, (ts, ts), 0) - lax.broadcasted_iota(I32, (ts, ts), 1)
    shifts = jnp.concatenate([jnp.where(lag == s, 1.0, 0.0).astype(BF16) for s in range(1, SHORT_CONV)], axis=0)
    stacked = jnp.dot(shifts, xb, preferred_element_type=F32)
    shifted = [stacked[(s - 1) * ts:s * ts] for s in range(1, SHORT_CONV)]
    head_row = lax.broadcasted_iota(I32, (halo, LANES), 0)

    def conv_silu(lane0):
        lanes = slice(lane0, lane0 + LANES)
        acc = cw_ref[SHORT_CONV - 1:SHORT_CONV, lanes] * xb[:, lanes].astype(F32)
        for s in range(1, SHORT_CONV):
            acc = acc + cw_ref[SHORT_CONV - 1 - s:SHORT_CONV - s, lanes] * shifted[s - 1][:, lanes]
        head = acc[0:halo]
        for s in range(1, SHORT_CONV):
            prev = jnp.where(head_row < s, xs_ref[halo - s:2 * halo - s, lanes], 0.0)
            head = head + cw_ref[SHORT_CONV - 1 - s:SHORT_CONV - s, lanes] * prev
        return _silu(jnp.concatenate([head, acc[halo:]], axis=0))

    def l2n(a):
        return a * lax.rsqrt(jnp.sum(a * a, axis=-1, keepdims=True) + EPS)

    sm = sm_ref[0]
    lane = lax.broadcasted_iota(I32, sm.shape, 1)
    sp, logsig = _softplus_parts(sm + par_ref[1:2, :])
    vals = jnp.where(lane < LANE_G, _sigmoid(sm),
                     jnp.where(lane < LANE_F, par_ref[0:1, :] * sp,
                               jnp.where(lane < LANE_F + H_FOX, logsig, 0.0)))
    row = lax.broadcasted_iota(I32, (ts, ts), 0)
    colm = lax.broadcasted_iota(I32, (ts, ts), 1)
    log_chunk = CHUNK.bit_length() - 1
    causal = (row >= colm) & ((row >> log_chunk) == (colm >> log_chunk))

    hi = vals.astype(BF16)
    rem = vals - hi.astype(F32)
    mid = rem.astype(BF16)
    lo = (rem - mid.astype(F32)).astype(BF16)
    pieces = jnp.concatenate([hi, mid, lo], axis=-1)

    def cumsum(mask):
        y = jnp.dot(jnp.where(mask, 1.0, 0.0).astype(BF16), pieces, preferred_element_type=F32)
        return (y[:, :LANES] + y[:, LANES:2 * LANES]) + y[:, 2 * LANES:]

    ccum = cumsum(row >= colm) + carry_ref[...]
    gcum = cumsum(causal)
    carry_ref[...] = ccum[ts - 1:ts, :]
    gcum_t = gcum.T

    cl = ccum * LOG2E
    c_hi = cl.astype(BF16)
    c_rem = cl - c_hi.astype(F32)
    c_mid = c_rem.astype(BF16)
    c_lo = (c_rem - c_mid.astype(F32)).astype(BF16)
    c_pieces = jnp.concatenate([c_hi, c_mid, c_lo], axis=-1)
    pr = lax.broadcasted_iota(I32, (3 * LANES, LANES), 0)
    pc = lax.broadcasted_iota(I32, (3 * LANES, LANES), 1)
    src_lane, piece = pr & (LANES - 1), pr >> (LANES.bit_length() - 1)
    owned = (src_lane >= LANE_F) & (src_lane < LANE_F + H_FOX) & ((pc >> 3) == src_lane - LANE_F)
    place_q = jnp.where(owned & ((pc & 7) == piece), 1.0, 0.0).astype(BF16)
    place_k = jnp.where(owned & ((pc & 7) == piece + 3), -1.0, 0.0).astype(BF16)
    slot = lane & 7
    in_heads = lane < 8 * H_FOX
    ones_q = jnp.where(in_heads & (slot >= 3) & (slot < 6), 1.0, 0.0)
    ones_k = jnp.where(in_heads & (slot < 3), 1.0, 0.0)
    qaug_ref[0] = (jnp.dot(c_pieces, place_q, preferred_element_type=F32) + ones_q).astype(BF16)
    kaug_ref[0] = (jnp.dot(c_pieces, place_k, preferred_element_type=F32) + ones_k).astype(BF16)

    scale = DK_DN ** -0.5
    dn_norm = par_ref[2:3, :]

    pw = 2 * CHUNK
    prow = lax.broadcasted_iota(I32, (pw, pw), 0)
    pcol = lax.broadcasted_iota(I32, (pw, pw), 1)
    same = (prow >> log_chunk) == (pcol >> log_chunk)
    causal_p = (prow >= pcol) & same
    strict_p = (prow > pcol) & same
    levels = []
    s = 1
    while s < CHUNK:
        levels.append(((prow >> s.bit_length()) == (pcol >> s.bit_length()))
                      & ((prow & s) != 0) & ((pcol & s) == 0))
        s *= 2

    heads = range(H_DN)
    pairs = range(ts // pw)
    q, k, v = [], [], []
    for h in heads:
        q.append(l2n(conv_silu(h * DK_DN)))
        k.append(l2n(conv_silu(H_DN * DK_DN + h * DK_DN)))
        v.append(conv_silu(2 * H_DN * DK_DN + h * DV_DN))

    ctx = []
    for h in heads:
        for p in pairs:
            pr = slice(p * pw, (p + 1) * pw)
            gcol = gcum[pr, LANE_G + h:LANE_G + h + 1]
            grow = gcum_t[LANE_G + h:LANE_G + h + 1, pr]
            beta = vals[pr, LANE_BETA + h:LANE_BETA + h + 1]
            decay = jnp.where(causal_p, jnp.exp(jnp.where(causal_p, gcol - grow, 0.0)), 0.0)
            egc = jnp.exp(gcol)
            kp = k[h][pr]
            kb = kp * beta
            ctx.append(dict(
                h=h, p=p, gcol=gcol, kp=kp,
                a=jnp.where(strict_p, _mm(kb, kp, NT) * decay, 0.0),
                qk=jnp.where(causal_p, _mm(q[h][pr] * scale, kp, NT) * decay, 0.0),
                rhs=jnp.concatenate([v[h][pr] * beta, kb * egc], axis=-1),
                qg=q[h][pr] * (scale * egc)))

    for c in ctx:
        c["n"] = -jnp.where(levels[0], c["a"], 0.0)
    for level in levels[1:]:
        for c in ctx:
            m = jnp.where(level, c["a"], 0.0)
            c["y"] = m + _mm(c["n"], m)
        for c in ctx:
            c["n"] = c["n"] - (c["y"] + _mm(c["y"], c["n"]))
    for c in ctx:
        c["sol"] = c["rhs"] + _mm(c["n"], c["rhs"])

    state = [s_ref[h] for h in heads]
    for p in pairs:
        group = [c for c in ctx if c["p"] == p]
        v_prev = [None] * H_DN
        for ch in range(2):
            rows = slice(ch * CHUNK, (ch + 1) * CHUNK)
            out_rows = slice(p * pw + ch * CHUNK, p * pw + (ch + 1) * CHUNK)
            on_state = [_mm(jnp.concatenate([c["sol"][rows, DV_DN:], c["qg"][rows]], axis=0), state[c["h"]])
                        for c in group]
            for c, ws_qs in zip(group, on_state):
                h = c["h"]
                vn = c["sol"][rows, :DV_DN] - ws_qs[:CHUNK]
                v_pair = jnp.concatenate([vn, jnp.zeros_like(vn)] if ch == 0 else [v_prev[h], vn], axis=0)
                o = ws_qs[CHUNK:] + _mm(c["qk"][rows], v_pair)
                glast = c["gcol"][(ch + 1) * CHUNK - 1:(ch + 1) * CHUNK]
                state[h] = (state[h] * jnp.exp(glast)
                            + _mm(c["kp"][rows] * jnp.exp(glast - c["gcol"][rows]), vn, TN))
                v_prev[h] = vn
                o = o * lax.rsqrt(jnp.mean(o * o, axis=-1, keepdims=True) + EPS) * dn_norm
                za = za_ref[0, out_rows, h * DV_DN:(h + 1) * DV_DN].astype(F32)
                oa_ref[0, out_rows, h * DV_DN:(h + 1) * DV_DN] = (o * _silu(za)).astype(oa_ref.dtype)
    for h in heads:
        s_ref[h] = state[h]


def _delta_rule(qkv, za, small, conv_w, par, *, ts=256):
    b, t, _ = qkv.shape
    assert t % ts == 0 and ts % (2 * CHUNK) == 0
    blk = lambda width: pl.BlockSpec((1, ts, width), lambda bi, j: (bi, j, 0))
    fixed = lambda bi, j: (0, 0)
    return pl.pallas_call(
        functools.partial(_delta_body, ts=ts),
        grid=(b, t // ts),
        in_specs=[blk(W_DN), blk(H_DN * DV_DN), blk(LANES),
                  pl.BlockSpec(conv_w.shape, fixed), pl.BlockSpec(par.shape, fixed)],
        out_specs=[blk(H_DN * DV_DN), blk(LANES), blk(LANES)],
        out_shape=[jax.ShapeDtypeStruct((b, t, H_DN * DV_DN), BF16),
                   jax.ShapeDtypeStruct((b, t, LANES), BF16),
                   jax.ShapeDtypeStruct((b, t, LANES), BF16)],
        scratch_shapes=[pltpu.VMEM((3 * SUBLANES, W_DN), F32),
                        pltpu.VMEM((H_DN, DK_DN, DV_DN), F32),
                        pltpu.VMEM((1, LANES), F32)],
        compiler_params=_params("parallel", "arbitrary"),
        name="delta_rule",
    )(qkv, za, small, conv_w, par)


def _fox_body(q_ref, k_ref, v_ref, qa_ref, ka_ref, o_ref, m_ref, acc_ref, *, tq, tk):
    i = pl.program_id(1)
    m_ref[...] = jnp.full(m_ref.shape, -jnp.inf, F32)
    acc_ref[...] = jnp.zeros_like(acc_ref)
    head_lanes = [slice(h * D_FOX, (h + 1) * D_FOX) for h in range(H_FOX)]
    lane = lax.broadcasted_iota(I32, (tk, LANES), 1)
    own = [jnp.where((lane >> 3) == h, 1.0, 0.0).astype(BF16) for h in range(H_FOX)]
    ones = jnp.ones((tk, D_FOX), BF16)
    keep = lax.broadcasted_iota(I32, (tq, tk), 1) <= lax.broadcasted_iota(I32, (tq, tk), 0)

    def block(start, diag_offset):
        rows = slice(0 if diag_offset is None else diag_offset, tq)
        ka = ka_ref[0, pl.ds(start, tk), :]
        for h, lanes in enumerate(head_lanes):
            q_aug = jnp.concatenate([q_ref[0, rows, lanes], qa_ref[0, rows, :]], axis=1)
            k_aug = jnp.concatenate([k_ref[0, pl.ds(start, tk), lanes], ka * own[h]], axis=1)
            s = lax.dot_general(q_aug, k_aug, NT, preferred_element_type=F32)
            if diag_offset is not None:
                s = jnp.where(keep[:tq - diag_offset], s, -jnp.inf)
            m_prev = m_ref[h, rows]
            m_next = jnp.maximum(m_prev, jnp.max(s, axis=-1, keepdims=True))
            p = jnp.exp2(s - jnp.concatenate([m_next] * (tk // LANES), axis=1))
            alpha = jnp.exp2(m_prev - m_next)
            v_aug = jnp.concatenate([v_ref[0, pl.ds(start, tk), lanes], ones], axis=1)
            acc_ref[h, rows] = (jnp.concatenate([alpha, alpha], axis=1) * acc_ref[h, rows]
                                + jnp.dot(p.astype(BF16), v_aug, preferred_element_type=F32))
            m_ref[h, rows] = m_next

    def full_block(jb, carry):
        block(pl.multiple_of(jb * tk, tk), None)
        return carry

    lax.fori_loop(0, i * (tq // tk), full_block, 0)
    for d in range(tq // tk):
        block(pl.multiple_of(i * tq + d * tk, tk), d * tk)
    for h, lanes in enumerate(head_lanes):
        acc = acc_ref[h]
        o_ref[0, :, lanes] = (acc[:, :D_FOX] / acc[:, D_FOX:]).astype(o_ref.dtype)


def _fox_attention(qkv, qaug, kaug, *, tq=1024, tk=512):
    b, t, _ = qkv.shape
    hd = H_FOX * D_FOX
    assert t % tq == 0 and tq % tk == 0 and tk % LANES == 0
    return pl.pallas_call(
        functools.partial(_fox_body, tq=tq, tk=tk),
        grid=(b, t // tq),
        in_specs=[pl.BlockSpec((1, tq, hd), lambda bi, i: (bi, i, 0)),
                  pl.BlockSpec((1, t, hd), lambda bi, i: (bi, 0, 1)),
                  pl.BlockSpec((1, t, hd), lambda bi, i: (bi, 0, 2)),
                  pl.BlockSpec((1, tq, LANES), lambda bi, i: (bi, i, 0)),
                  pl.BlockSpec((1, t, LANES), lambda bi, i: (bi, 0, 0))],
        out_specs=pl.BlockSpec((1, tq, hd), lambda bi, i: (bi, i, 0)),
        out_shape=jax.ShapeDtypeStruct((b, t, hd), BF16),
        scratch_shapes=[pltpu.VMEM((H_FOX, tq, LANES), F32), pltpu.VMEM((H_FOX, tq, 2 * D_FOX), F32)],
        compiler_params=_params("parallel", "arbitrary"),
        name="fox_attention",
    )(qkv, qkv, qkv, qaug, kaug)


def _merge_body(oa_ref, ob_ref, zc_ref, gates_ref, x_ref, wa_ref, wb_ref, wc_ref, wo_ref, g_ref,
                wr_ref, br_ref, xn_ref, hp_ref, route_ref, meta_ref, cnt_ref, run_ref, *, tm):
    step = pl.program_id(0)
    d = x_ref.shape[-1]

    @pl.when(step == 0)
    def _():
        run_ref[...] = jnp.zeros_like(run_ref)

    merged = None
    for idx, (m_ref, w_ref) in enumerate(((oa_ref, wa_ref), (ob_ref, wb_ref), (zc_ref, wc_ref))):
        y = jnp.dot(m_ref[...], w_ref[...], preferred_element_type=F32)
        term = _sigmoid(gates_ref[:, idx * d:(idx + 1) * d].astype(F32)) * y
        merged = term if merged is None else merged + term
    xn = x_ref[...] + jnp.dot(merged.astype(BF16), wo_ref[...], preferred_element_type=F32)
    xn_ref[...] = xn
    h2 = xn * lax.rsqrt(jnp.mean(xn * xn, axis=-1, keepdims=True) + EPS) * g_ref[...]

    hp_ref[...] = _pack_bf16_pairs(h2)

    logits = lax.dot_general(wr_ref[...], h2.astype(BF16), NT, preferred_element_type=F32) + br_ref[...]
    big = jnp.int32(LANES)
    gl = logits[LANE_GRP:LANE_GRP + N_GROUPS]
    g_row = lax.broadcasted_iota(I32, gl.shape, 0)
    gmax = jnp.max(gl, axis=0, keepdims=True)
    grp = jnp.min(jnp.where(gl == gmax, g_row, big), axis=0, keepdims=True)
    p_grp = 1.0 / jnp.sum(jnp.exp(gl - gmax), axis=0, keepdims=True)
    el_all = logits[LANE_EXP:LANE_EXP + N_EXPERTS]
    e_row = lax.broadcasted_iota(I32, el_all.shape, 0)
    el = jnp.where((e_row >> (EXPERTS_PER_GROUP.bit_length() - 1)) == grp, el_all, -jnp.inf)
    v0 = jnp.max(el, axis=0, keepdims=True)
    i0 = jnp.min(jnp.where(el == v0, e_row, big), axis=0, keepdims=True)
    el1 = jnp.where(e_row == i0, -jnp.inf, el)
    v1 = jnp.max(el1, axis=0, keepdims=True)
    i1 = jnp.min(jnp.where(el1 == v1, e_row, big), axis=0, keepdims=True)
    e1 = jnp.exp(v1 - v0)
    gate0 = p_grp / (1.0 + e1)
    gate1 = p_grp * e1 / (1.0 + e1)

    hot0 = e_row == i0
    hot1 = e_row == i1
    onehot = jnp.where(hot0 | hot1, 1.0, 0.0)
    tiles = [onehot[:, t * LANES:(t + 1) * LANES] for t in range(tm // LANES)]
    ri = lax.broadcasted_iota(I32, (LANES, LANES), 0)
    cj = lax.broadcasted_iota(I32, (LANES, LANES), 1)
    in_tile = _mm(jnp.concatenate(tiles, axis=0), jnp.where(ri < cj, 1.0, 0.0))
    before = run_ref[...]
    prefix = []
    for t, tile in enumerate(tiles):
        prefix.append(in_tile[t * N_EXPERTS:(t + 1) * N_EXPERTS] + before)
        before = before + jnp.sum(tile, axis=1, keepdims=True)
    prefix = jnp.concatenate(prefix, axis=1)
    run_ref[...] = before
    cnt_ref[...] = jnp.broadcast_to(before, cnt_ref.shape)
    rank0 = jnp.sum(jnp.where(hot0, prefix, 0.0), axis=0, keepdims=True)
    rank1 = jnp.sum(jnp.where(hot1, prefix, 0.0), axis=0, keepdims=True)

    rows = {R_GATE0: gate0, R_GATE1: gate1, R_EID0: i0.astype(F32), R_EID1: i1.astype(F32),
            R_RANK0: rank0, R_RANK1: rank1}
    fields = jnp.concatenate([rows.get(r, jnp.zeros_like(gate0)) for r in range(SUBLANES)], axis=0)
    meta_ref[...] = fields.astype(I32)
    route_ref[...] = jnp.concatenate([fields, jnp.zeros((LANES - SUBLANES, tm), F32)], axis=0).T


def _merge(oa, ob, zc, gates, x2, wa, wb, wc, wo, g, wr, br, *, tm=1024):
    n, d = x2.shape
    assert n % tm == 0
    row = lambda i: (i, 0)
    fixed = lambda i: (0, 0)
    full = lambda a: pl.BlockSpec(a.shape, fixed)
    return pl.pallas_call(
        functools.partial(_merge_body, tm=tm),
        grid=(n // tm,),
        in_specs=[pl.BlockSpec((tm, oa.shape[1]), row), pl.BlockSpec((tm, ob.shape[1]), row),
                  pl.BlockSpec((tm, zc.shape[1]), row), pl.BlockSpec((tm, 3 * d), row),
                  pl.BlockSpec((tm, d), row),
                  full(wa), full(wb), full(wc), full(wo), full(g), full(wr), full(br)],
        out_specs=[pl.BlockSpec((tm, d), row), pl.BlockSpec((tm, d // 2), row),
                   pl.BlockSpec((tm, LANES), row), pl.BlockSpec((SUBLANES, tm), lambda i: (0, i)),
                   pl.BlockSpec((N_EXPERTS, LANES), fixed)],
        out_shape=[jax.ShapeDtypeStruct((n, d), F32), jax.ShapeDtypeStruct((n, d // 2), U32),
                   jax.ShapeDtypeStruct((n, LANES), F32), jax.ShapeDtypeStruct((SUBLANES, n), I32),
                   jax.ShapeDtypeStruct((N_EXPERTS, LANES), F32)],
        scratch_shapes=[pltpu.VMEM((N_EXPERTS, 1), F32)],
        compiler_params=_params("arbitrary"),
        name="merge_router",
    )(oa, ob, zc, gates, x2, wa, wb, wc, wo, g, wr, br)


def _sc_mesh():
    return plsc.VectorSubcoreMesh(core_axis_name="c", subcore_axis_name="s")


def _sc_worker_base(per_worker):
    return (lax.axis_index("s") * SC_CORES + lax.axis_index("c")) * per_worker


def _sc_row_scatter(src, idx, out_rows, *, chunk=128):
    n, d = src.shape
    per_worker = n // (SC_CORES * SC_SUBCORES)
    chunk = min(chunk, per_worker)
    assert idx.shape == (2 * n,) and n % (SC_CORES * SC_SUBCORES) == 0 and per_worker % chunk == 0

    @functools.partial(
        pl.kernel, mesh=_sc_mesh(), out_type=jax.ShapeDtypeStruct((out_rows, d), src.dtype),
        scratch_types=[pltpu.VMEM((chunk,), I32), pltpu.VMEM((chunk, d), src.dtype), pltpu.SemaphoreType.DMA])
    def scatter(src_hbm, idx_hbm, out_hbm, idx_v, rows_v, sem):
        base = _sc_worker_base(per_worker)

        @pl.loop(0, per_worker // chunk)
        def _(step):
            off = base + step * chunk
            pltpu.sync_copy(src_hbm.at[pl.ds(off, chunk)], rows_v)
            for slot in range(2):
                pltpu.sync_copy(idx_hbm.at[pl.ds(slot * n + off, chunk)], idx_v)
                pltpu.async_copy(rows_v, out_hbm.at[idx_v], sem).wait()

    return scatter(src, idx)


def _pack_bf16_pairs(x):
    half = x.shape[-1] // 2
    lo = pltpu.bitcast(x[:, :half].astype(BF16).astype(F32), U32) >> 16
    hi = pltpu.bitcast(x[:, half:].astype(BF16).astype(F32), U32) & jnp.uint32(0xFFFF0000)
    return lo | hi


def _unpack_bf16_pairs(xp):
    return pltpu.bitcast(xp << 16, F32), pltpu.bitcast(xp & jnp.uint32(0xFFFF0000), F32)


def _expert_body(be_ref, valid_ref, nu_ref, xs_ref, w1_ref, w3_ref, w2_ref, ys_ref):
    del be_ref
    step = pl.program_id(0)
    live = step < nu_ref[0]

    @pl.when(jnp.logical_not(live))
    def _():
        ys_ref[...] = jnp.zeros_like(ys_ref)

    @pl.when(live)
    def _():
        row = lax.broadcasted_iota(I32, xs_ref.shape, 0)
        xp = jnp.where(row < valid_ref[step], xs_ref[...], jnp.uint32(0))
        half = xp.shape[-1]
        lo, hi = (part.astype(BF16) for part in _unpack_bf16_pairs(xp))

        def up(w_ref):
            return (jnp.dot(lo, w_ref[0, :half, :].astype(BF16), preferred_element_type=F32)
                    + jnp.dot(hi, w_ref[0, half:, :].astype(BF16), preferred_element_type=F32))

        act = (_silu(up(w1_ref)) * up(w3_ref)).astype(BF16)
        ys_ref[...] = _pack_bf16_pairs(jnp.dot(act, w2_ref[0].astype(BF16), preferred_element_type=F32))


def _experts(blk_eid, blk_valid, n_used, xs, w1, w3, w2, *, rb):
    p, half = xs.shape
    d = 2 * half
    nb = p // rb
    used = lambda i, be, bv, nu: jnp.maximum(jnp.minimum(i, nu[0] - 1), 0)
    wmap = lambda i, be, bv, nu: (be[used(i, be, bv, nu)], 0, 0)
    grid_spec = pltpu.PrefetchScalarGridSpec(
        num_scalar_prefetch=3,
        grid=(nb,),
        in_specs=[pl.BlockSpec((rb, half), lambda i, be, bv, nu: (used(i, be, bv, nu), 0)),
                  pl.BlockSpec((1, d, D_EXPERT), wmap), pl.BlockSpec((1, d, D_EXPERT), wmap),
                  pl.BlockSpec((1, D_EXPERT, d), wmap)],
        out_specs=pl.BlockSpec((rb, half), lambda i, be, bv, nu: (i, 0)),
    )
    return pl.pallas_call(
        _expert_body,
        grid_spec=grid_spec,
        out_shape=jax.ShapeDtypeStruct((p, half), U32),
        compiler_params=_params("arbitrary"),
        name="moe_experts",
    )(blk_eid, blk_valid, n_used, xs, w1, w3, w2)


def _sc_row_gather(table, idx, *, chunk=128):
    rows, d = idx.shape[0], table.shape[1]
    per_worker = rows // (SC_CORES * SC_SUBCORES)
    chunk = min(chunk, per_worker)
    assert rows % (SC_CORES * SC_SUBCORES) == 0 and per_worker % chunk == 0

    @functools.partial(
        pl.kernel, mesh=_sc_mesh(), out_type=jax.ShapeDtypeStruct((rows, d), table.dtype),
        scratch_types=[pltpu.VMEM((chunk,), I32), pltpu.VMEM((chunk, d), table.dtype), pltpu.SemaphoreType.DMA])
    def gather(table_hbm, idx_hbm, out_hbm, idx_v, rows_v, sem):
        base = _sc_worker_base(per_worker)

        @pl.loop(0, per_worker // chunk)
        def _(step):
            off = base + step * chunk
            pltpu.sync_copy(idx_hbm.at[pl.ds(off, chunk)], idx_v)
            pltpu.async_copy(table_hbm.at[idx_v], rows_v, sem).wait()
            pltpu.sync_copy(rows_v, out_hbm.at[pl.ds(off, chunk)])

    return gather(table, idx)


def _combine_rows(r0_ref, r1_ref, x_ref, route_ref):
    route = route_ref[...]
    half = r0_ref.shape[-1]
    lo0, hi0 = _unpack_bf16_pairs(r0_ref[...])
    lo1, hi1 = _unpack_bf16_pairs(r1_ref[...])
    g0, g1 = route[:, R_GATE0:R_GATE0 + 1], route[:, R_GATE1:R_GATE1 + 1]
    return jnp.concatenate([x_ref[:, :half] + g0 * lo0 + g1 * lo1,
                            x_ref[:, half:] + g0 * hi0 + g1 * hi1], axis=1)


def _final_body(r0_ref, r1_ref, x_ref, route_ref, g_ref, o_ref):
    out = _combine_rows(r0_ref, r1_ref, x_ref, route_ref)
    o_ref[...] = out * lax.rsqrt(jnp.mean(out * out, axis=-1, keepdims=True) + EPS) * g_ref[...]


def _final_combine(rows2, xn, route, g, *, tm):
    n, d = xn.shape
    nt = n // tm
    row = lambda i: (i, 0)
    return pl.pallas_call(
        _final_body,
        grid=(nt,),
        in_specs=[pl.BlockSpec((tm, d // 2), row), pl.BlockSpec((tm, d // 2), lambda i: (i + nt, 0)),
                  pl.BlockSpec((tm, d), row), pl.BlockSpec((tm, LANES), row),
                  pl.BlockSpec((1, d), lambda i: (0, 0))],
        out_specs=pl.BlockSpec((tm, d), row),
        out_shape=jax.ShapeDtypeStruct((n, d), F32),
        compiler_params=_params("parallel"),
        name="moe_combine",
    )(rows2, rows2, xn, route, g)


def _moe_rows(hp, meta, counts, w1, w3, w2, *, layer, rb):
    n = hp.shape[0]
    cnt = counts[:, 0].astype(I32)
    nblk = (cnt + rb - 1) // rb
    bend = jnp.cumsum(nblk)
    pstart = (bend - nblk) * rb
    nb = (2 * n) // rb + N_EXPERTS
    n_used = bend[-1:].astype(I32)
    blk = jnp.arange(nb, dtype=I32)
    blk_eid = jnp.minimum(jnp.sum(bend[None, :] <= blk[:, None], axis=1), N_EXPERTS - 1).astype(I32)
    experts = jnp.arange(N_EXPERTS, dtype=I32)
    mine = blk_eid[:, None] == experts[None, :]
    blk_valid = jnp.clip(jnp.sum(jnp.where(mine, (cnt + pstart)[None, :], 0), axis=1) - blk * rb, 0, rb)
    eid = meta[R_EID0:R_EID1 + 1]
    first = jnp.sum(jnp.where(eid[None] == experts[:, None, None], pstart[:, None, None], 0), axis=0)
    dest = (first + meta[R_RANK0:R_RANK1 + 1]).reshape(-1)

    xs = _sc_row_scatter(hp, dest, nb * rb)
    ys = _experts(blk_eid + layer * N_EXPERTS, blk_valid.astype(I32), n_used, xs, w1, w3, w2, rb=rb)
    return _sc_row_gather(ys, dest)


def _lane_row(pairs):
    row = jnp.zeros((LANES,), F32)
    for off, vec in pairs:
        row = row.at[off:off + vec.shape[0]].set(vec.astype(F32))
    return row


def kernel(x, norm_mix, w_in, conv_qkv, dn_a_log, dn_dt_bias, dn_norm, fox_bias, conv_dw, conv_dw_b,
           conv_ln_g, conv_ln_b, w_a, w_b, w_c, w_out, norm_ffn, router_group_w, router_group_b,
           router_expert_w, router_expert_b, expert_w1, expert_w3, expert_w2, norm_final):
    b, t, d = x.shape
    n = b * t
    depth = w_in.shape[0]
    qk_dn = H_DN * DK_DN
    in_sizes = (qk_dn, qk_dn, H_DN * DV_DN, H_DN * DV_DN, H_DN, H_DN,
                H_FOX * D_FOX, H_FOX * D_FOX, H_FOX * D_FOX, H_FOX, 2 * C_CONV, d, d, d)
    splits = np.cumsum(in_sizes)[:-1].tolist()
    tm_final, rb = 512, 512

    x_src = x.reshape(n, d)
    w1_all = expert_w1.reshape(depth * N_EXPERTS, d, D_EXPERT)
    w3_all = expert_w3.reshape(depth * N_EXPERTS, d, D_EXPERT)
    w2_all = expert_w2.reshape(depth * N_EXPERTS, D_EXPERT, d)
    for l in range(depth):
        (qa, ka, va, za, ba, aa, qb, kb, vb, fb, uc, ga, gb, gc) = jnp.split(w_in[l], splits, axis=1)
        w_main = jnp.concatenate([qa, ka, va, za, qb * (D_FOX ** -0.5 * LOG2E), kb, vb, uc, ga, gb, gc],
                                 axis=1).astype(BF16)
        w_small = jnp.concatenate([ba, aa, fb, jnp.zeros((d, LANES - 3 * H_DN), F32)], axis=1).astype(BF16)
        vec = jnp.stack([conv_dw_b[l], conv_ln_g[l], conv_ln_b[l]] + [jnp.zeros((C_CONV,), F32)] * 5)
        outs = _in_proj(x_src, norm_mix[l][None, :], w_main, w_small, conv_dw[l], vec, seq_len=t)
        x2 = outs[0] if l > 0 else x_src
        dn_qkv, za_p, fox_qkv, zc, gates, small = outs[-6:]

        par = jnp.stack([_lane_row([(LANE_G, -jnp.exp(dn_a_log[l]))]),
                         _lane_row([(LANE_G, dn_dt_bias[l]), (LANE_F, fox_bias[l])]),
                         _lane_row([(0, dn_norm[l])])] + [jnp.zeros((LANES,), F32)] * 5)
        oa, qaug, kaug = _delta_rule(dn_qkv.reshape(b, t, -1), za_p.reshape(b, t, -1),
                                     small.reshape(b, t, LANES), conv_qkv[l], par)
        ob = _fox_attention(fox_qkv.reshape(b, t, -1), qaug, kaug)

        w_r = jnp.concatenate([router_group_w[l], jnp.zeros((d, LANE_EXP - N_GROUPS), F32),
                               router_expert_w[l], jnp.zeros((d, LANES - LANE_EXP - N_EXPERTS), F32)], axis=1)
        b_r = _lane_row([(LANE_GRP, router_group_b[l]), (LANE_EXP, router_expert_b[l])])[None, :]
        xn, hp, route, meta, counts = _merge(
            oa.reshape(n, -1), ob.reshape(n, -1), zc.reshape(n, -1), gates, x2,
            w_a[l].astype(BF16), w_b[l].astype(BF16), w_c[l].astype(BF16), w_out[l].astype(BF16),
            norm_ffn[l][None, :], w_r.T.astype(BF16), b_r.reshape(LANES, 1))

        x_src = (_moe_rows(hp, meta, counts, w1_all, w3_all, w2_all, layer=l, rb=rb), xn, route)
    return _final_combine(*x_src, norm_final[None, :], tm=tm_final).reshape(b, t, d)
```

```python
import functools

import jax
import jax.numpy as jnp
import numpy as np
from jax import lax
from jax.experimental import pallas as pl
from jax.experimental.pallas import tpu as pltpu
from jax.experimental.pallas import tpu_sc as plsc

F32 = jnp.float32
BF16 = jnp.bfloat16
U32 = jnp.uint32
I32 = jnp.int32

EPS = 1e-6
LOG2E = 1.4426950408889634
LANES = 128
SUBLANES = 8
SC_CORES, SC_SUBCORES = 2, 16
H_DN, DK_DN, DV_DN = 4, 128, 128
SHORT_CONV = 4
CHUNK = 64
H_FOX, D_FOX = 4, 128
C_CONV = 512
CONV_WIDTH = 31
N_GROUPS, EXPERTS_PER_GROUP = 4, 8
N_EXPERTS = N_GROUPS * EXPERTS_PER_GROUP
D_EXPERT = 256

W_DN = 3 * H_DN * DK_DN
W_FOX = 3 * H_FOX * D_FOX
VMEM_LIMIT = 56 * 1024 * 1024

LANE_BETA, LANE_G, LANE_F = 0, 4, 8
LANE_GRP, LANE_EXP = 0, 32
R_GATE0, R_GATE1, R_EID0, R_EID1, R_RANK0, R_RANK1 = 0, 1, 2, 3, 4, 5

NN = (((1,), (0,)), ((), ()))
NT = (((1,), (1,)), ((), ()))
TN = (((0,), (0,)), ((), ()))


def _mm(a, b, dims=NN):
    return lax.dot_general(a.astype(BF16), b.astype(BF16), dims, preferred_element_type=F32)


def _sigmoid(x):
    return 0.5 * jnp.tanh(0.5 * x) + 0.5


def _silu(x):
    half = 0.5 * x
    return half * jnp.tanh(half) + half


def _params(*sem):
    return pltpu.CompilerParams(dimension_semantics=sem, vmem_limit_bytes=VMEM_LIMIT)


def _in_proj_body(*refs, col_chunk, combine, tiles_per_seq, hrows, rchunk):
    if combine:
        r0_ref, r1_ref, xn_ref, route_ref, g_ref, w_ref, ws_ref, dw_ref, vec_ref, x_out_ref = refs[:10]
        x = _combine_rows(r0_ref, r1_ref, xn_ref, route_ref)
        x_out_ref[...] = x
        rest = refs[10:]
    else:
        x_ref, g_ref, w_ref, ws_ref, dw_ref, vec_ref = refs[:6]
        x = x_ref[...]
        rest = refs[6:]
    dn_ref, za_ref, fox_ref, zc_ref, gates_ref, small_ref, z_ref, zs_ref, y_ref = rest
    tm = x.shape[0]
    h = x * lax.rsqrt(jnp.mean(x * x, axis=-1, keepdims=True) + EPS) * g_ref[...]
    hb = h.astype(BF16)

    def project(col, width, hold=None):
        lhs = hb
        if hold is not None:
            zero = jnp.minimum(jnp.abs(hold), 0.0).astype(BF16)
            head = jnp.concatenate([hb[0:rchunk, 0:LANES] + zero, hb[0:rchunk, LANES:]], axis=1)
            lhs = jnp.concatenate([head, hb[rchunk:]], axis=0)
        return jnp.dot(lhs, w_ref[:, col:col + width], preferred_element_type=F32)

    starts, col = {}, 0
    for name, width in (("dn", dn_ref.shape[-1]), ("za", za_ref.shape[-1]), ("fox", fox_ref.shape[-1]),
                        ("conv", 2 * C_CONV), ("gates", gates_ref.shape[-1])):
        starts[name], col = col, col + width

    first = pl.program_id(0) % tiles_per_seq == 0
    col = starts["conv"]

    @pl.when(first)
    def _():
        z_ref[0:hrows, :] = jnp.zeros((hrows, C_CONV), F32)

    @pl.when(jnp.logical_not(first))
    def _():
        z_ref[0:hrows, :] = z_ref[tm:tm + hrows, :]

    z_ref[hrows:hrows + tm, :] = project(col, C_CONV) * _sigmoid(project(col + C_CONV, C_CONV))
    sub = SUBLANES
    span = tm + hrows - sub
    for phase in range(1, sub):
        zs_ref[phase - 1, 0:span, :] = z_ref[phase:phase + span, :]

    def conv_unit(lg, r0, after):
        lanes = slice(lg * LANES, (lg + 1) * LANES)
        acc = vec_ref[0:1, lanes] + jnp.minimum(jnp.abs(after[0:rchunk, 0:LANES]), 0.0)
        for k in range(CONV_WIDTH):
            off = hrows - (CONV_WIDTH - 1) + k
            phase, base = off % sub, off - off % sub + r0
            if phase == 0:
                tap = z_ref[base:base + rchunk, lanes]
            else:
                tap = zs_ref[phase - 1, base:base + rchunk, lanes]
            acc = acc + dw_ref[k:k + 1, lanes] * tap
        y_ref[r0:r0 + rchunk, lanes] = acc
        return acc

    units = [(lg, r0) for lg in range(C_CONV // LANES) for r0 in range(0, tm, rchunk)]
    chunks = [(ref, starts[name], c) for name, ref in (("dn", dn_ref), ("za", za_ref), ("fox", fox_ref),
                                                       ("gates", gates_ref))
              for c in range(0, ref.shape[-1], col_chunk)]
    share = [[] for _ in chunks]
    for u, unit in enumerate(units):
        share[u * len(chunks) // len(units)].append(unit)
    last = []
    for idx, ((ref, start, c), mine) in enumerate(zip(chunks, share)):
        res = project(start + c, col_chunk, hold=last[idx - 2] if idx >= 2 else None)
        ref[:, c:c + col_chunk] = res.astype(ref.dtype)
        tile = None
        for lg, r0 in mine:
            tile = conv_unit(lg, r0, res)
        last.append(tile)
    small_ref[...] = jnp.dot(hb, ws_ref[...], preferred_element_type=F32)

    y = y_ref[...]
    mu = jnp.mean(y, axis=-1, keepdims=True)
    yc = y - mu
    var = jnp.mean(yc * yc, axis=-1, keepdims=True)
    yn = yc * lax.rsqrt(var + EPS) * vec_ref[1:2, :] + vec_ref[2:3, :]
    zc_ref[...] = _silu(yn).astype(zc_ref.dtype)


def _in_proj(x_src, g, w, ws, dw, vec, *, seq_len, tm=256, col_chunk=512, hrows=32, rchunk=32):
    combine = isinstance(x_src, tuple)
    n, d = (x_src[1] if combine else x_src).shape
    nt = n // tm
    widths = (W_DN, H_DN * DV_DN, W_FOX, C_CONV, 3 * d)
    assert w.shape == (d, sum(widths) + C_CONV) and n % tm == 0 and seq_len % tm == 0
    assert hrows >= CONV_WIDTH - 1 and hrows % SUBLANES == 0 and tm % rchunk == 0
    row = lambda i: (i, 0)
    fixed = lambda i: (0, 0)
    out_shape = [jax.ShapeDtypeStruct((n, wd), BF16) for wd in widths] + [jax.ShapeDtypeStruct((n, LANES), F32)]
    out_specs = [pl.BlockSpec((tm, wd), row) for wd in widths] + [pl.BlockSpec((tm, LANES), row)]
    if combine:
        rows2, xn, route = x_src
        args = (rows2, rows2, xn, route)
        in_specs = [pl.BlockSpec((tm, d // 2), row), pl.BlockSpec((tm, d // 2), lambda i: (i + nt, 0)),
                    pl.BlockSpec((tm, d), row), pl.BlockSpec((tm, LANES), row)]
        out_shape = [jax.ShapeDtypeStruct((n, d), F32)] + out_shape
        out_specs = [pl.BlockSpec((tm, d), row)] + out_specs
    else:
        args = (x_src,)
        in_specs = [pl.BlockSpec((tm, d), row)]
    return pl.pallas_call(
        functools.partial(_in_proj_body, col_chunk=col_chunk, combine=combine, tiles_per_seq=seq_len // tm,
                          hrows=hrows, rchunk=rchunk),
        grid=(nt,),
        in_specs=in_specs + [pl.BlockSpec((1, d), fixed),
                             pl.BlockSpec(w.shape, fixed, pipeline_mode=pl.Buffered(1)),
                             pl.BlockSpec(ws.shape, fixed, pipeline_mode=pl.Buffered(1)),
                             pl.BlockSpec(dw.shape, fixed), pl.BlockSpec(vec.shape, fixed)],
        out_specs=out_specs,
        out_shape=out_shape,
        scratch_shapes=[pltpu.VMEM((tm + hrows, C_CONV), F32),
                        pltpu.VMEM((SUBLANES - 1, tm + hrows, C_CONV), F32),
                        pltpu.VMEM((tm, C_CONV), F32)],
        compiler_params=_params("arbitrary"),
        name="in_proj",
    )(*args, g, w, ws, dw, vec)


def _softplus_parts(z):
    t = jnp.log1p(jnp.exp(-jnp.abs(z)))
    return jnp.maximum(z, 0.0) + t, -(jnp.maximum(-z, 0.0) + t)


def _delta_body(qkv_ref, za_ref, sm_ref, cw_ref, par_ref, shifts_ref, tri_ref, place_ref,
                oa_ref, qaug_ref, kaug_ref, xs_ref, s_ref, carry_ref, *, ts):
    j = pl.program_id(1)
    halo = SUBLANES
    pack = 2 * SUBLANES

    @pl.when(j == 0)
    def _():
        xs_ref[0:halo, :] = jnp.zeros((halo, W_DN), F32)
        s_ref[...] = jnp.zeros_like(s_ref)
        carry_ref[...] = jnp.zeros_like(carry_ref)

    @pl.when(j > 0)
    def _():
        xs_ref[0:halo, :] = xs_ref[2 * halo:3 * halo, :]

    xb = qkv_ref[0]
    xs_ref[halo:2 * halo, :] = qkv_ref[0, 0:pack, :].astype(F32)[0:halo]
    xs_ref[2 * halo:3 * halo, :] = qkv_ref[0, ts - pack:ts, :].astype(F32)[pack - halo:pack]

    stacked = jnp.dot(shifts_ref[...], xb, preferred_element_type=F32)
    shifted = [stacked[(s - 1) * ts:s * ts] for s in range(1, SHORT_CONV)]
    head_row = lax.broadcasted_iota(I32, (halo, LANES), 0)

    def conv_silu(lane0):
        lanes = slice(lane0, lane0 + LANES)
        acc = cw_ref[SHORT_CONV - 1:SHORT_CONV, lanes] * xb[:, lanes].astype(F32)
        for s in range(1, SHORT_CONV):
            acc = acc + cw_ref[SHORT_CONV - 1 - s:SHORT_CONV - s, lanes] * shifted[s - 1][:, lanes]
        head = acc[0:halo]
        for s in range(1, SHORT_CONV):
            prev = jnp.where(head_row < s, xs_ref[halo - s:2 * halo - s, lanes], 0.0)
            head = head + cw_ref[SHORT_CONV - 1 - s:SHORT_CONV - s, lanes] * prev
        return _silu(jnp.concatenate([head, acc[halo:]], axis=0))

    def l2n(a):
        return a * lax.rsqrt(jnp.sum(a * a, axis=-1, keepdims=True) + EPS)

    sm = sm_ref[0]
    lane = lax.broadcasted_iota(I32, sm.shape, 1)
    sp, logsig = _softplus_parts(sm + par_ref[1:2, :])
    vals = jnp.where(lane < LANE_G, _sigmoid(sm),
                     jnp.where(lane < LANE_F, par_ref[0:1, :] * sp,
                               jnp.where(lane < LANE_F + H_FOX, logsig, 0.0)))
    log_chunk = CHUNK.bit_length() - 1

    hi = vals.astype(BF16)
    rem = vals - hi.astype(F32)
    mid = rem.astype(BF16)
    lo = (rem - mid.astype(F32)).astype(BF16)
    pieces = jnp.concatenate([hi, mid, lo], axis=-1)

    def cumsum(tri):
        y = jnp.dot(tri, pieces, preferred_element_type=F32)
        return (y[:, :LANES] + y[:, LANES:2 * LANES]) + y[:, 2 * LANES:]

    ccum = cumsum(tri_ref[0]) + carry_ref[...]
    gcum = cumsum(tri_ref[1])
    carry_ref[...] = ccum[ts - 1:ts, :]
    gcum_t = gcum.T

    cl = ccum * LOG2E
    c_hi = cl.astype(BF16)
    c_rem = cl - c_hi.astype(F32)
    c_mid = c_rem.astype(BF16)
    c_lo = (c_rem - c_mid.astype(F32)).astype(BF16)
    c_pieces = jnp.concatenate([c_hi, c_mid, c_lo], axis=-1)
    slot = lane & 7
    in_heads = lane < 8 * H_FOX
    ones_q = jnp.where(in_heads & (slot >= 3) & (slot < 6), 1.0, 0.0)
    ones_k = jnp.where(in_heads & (slot < 3), 1.0, 0.0)
    qaug_ref[0] = (jnp.dot(c_pieces, place_ref[0], preferred_element_type=F32) + ones_q).astype(BF16)
    kaug_ref[0] = (jnp.dot(c_pieces, place_ref[1], preferred_element_type=F32) + ones_k).astype(BF16)

    scale = DK_DN ** -0.5
    dn_norm = par_ref[2:3, :]

    pw = 2 * CHUNK
    prow = lax.broadcasted_iota(I32, (pw, pw), 0)
    pcol = lax.broadcasted_iota(I32, (pw, pw), 1)
    same = (prow >> log_chunk) == (pcol >> log_chunk)
    causal_p = (prow >= pcol) & same
    strict_p = (prow > pcol) & same
    levels = []
    s = 1
    while s < CHUNK:
        levels.append(((prow >> s.bit_length()) == (pcol >> s.bit_length()))
                      & ((prow & s) != 0) & ((pcol & s) == 0))
        s *= 2

    heads = range(H_DN)
    pairs = range(ts // pw)
    q, k, v = [], [], []
    for h in heads:
        q.append(l2n(conv_silu(h * DK_DN)))
        k.append(l2n(conv_silu(H_DN * DK_DN + h * DK_DN)))
        v.append(conv_silu(2 * H_DN * DK_DN + h * DV_DN))

    ctx = []
    for h in heads:
        for p in pairs:
            pr = slice(p * pw, (p + 1) * pw)
            gcol = gcum[pr, LANE_G + h:LANE_G + h + 1]
            grow = gcum_t[LANE_G + h:LANE_G + h + 1, pr]
            beta = vals[pr, LANE_BETA + h:LANE_BETA + h + 1]
            decay = jnp.where(causal_p, jnp.exp(jnp.where(causal_p, gcol - grow, 0.0)), 0.0)
            egc = jnp.exp(gcol)
            kp = k[h][pr]
            kb = kp * beta
            ctx.append(dict(
                h=h, p=p, gcol=gcol, kp=kp,
                a=jnp.where(strict_p, _mm(kb, kp, NT) * decay, 0.0),
                qk=jnp.where(causal_p, _mm(q[h][pr] * scale, kp, NT) * decay, 0.0),
                rhs=jnp.concatenate([v[h][pr] * beta, kb * egc], axis=-1),
                qg=q[h][pr] * (scale * egc)))

    for c in ctx:
        c["n"] = -jnp.where(levels[0], c["a"], 0.0)
    for level in levels[1:]:
        for c in ctx:
            m = jnp.where(level, c["a"], 0.0)
            c["y"] = m + _mm(c["n"], m)
        for c in ctx:
            c["n"] = c["n"] - (c["y"] + _mm(c["y"], c["n"]))
    for c in ctx:
        c["sol"] = c["rhs"] + _mm(c["n"], c["rhs"])

    state = [s_ref[h] for h in heads]
    for p in pairs:
        group = [c for c in ctx if c["p"] == p]
        v_prev = [None] * H_DN
        for ch in range(2):
            rows = slice(ch * CHUNK, (ch + 1) * CHUNK)
            out_rows = slice(p * pw + ch * CHUNK, p * pw + (ch + 1) * CHUNK)
            on_state = [_mm(jnp.concatenate([c["sol"][rows, DV_DN:], c["qg"][rows]], axis=0), state[c["h"]])
                        for c in group]
            for c, ws_qs in zip(group, on_state):
                h = c["h"]
                vn = c["sol"][rows, :DV_DN] - ws_qs[:CHUNK]
                v_pair = jnp.concatenate([vn, jnp.zeros_like(vn)] if ch == 0 else [v_prev[h], vn], axis=0)
                o = ws_qs[CHUNK:] + _mm(c["qk"][rows], v_pair)
                glast = c["gcol"][(ch + 1) * CHUNK - 1:(ch + 1) * CHUNK]
                state[h] = (state[h] * jnp.exp(glast)
                            + _mm(c["kp"][rows] * jnp.exp(glast - c["gcol"][rows]), vn, TN))
                v_prev[h] = vn
                o = o * lax.rsqrt(jnp.mean(o * o, axis=-1, keepdims=True) + EPS) * dn_norm
                za = za_ref[0, out_rows, h * DV_DN:(h + 1) * DV_DN].astype(F32)
                oa_ref[0, out_rows, h * DV_DN:(h + 1) * DV_DN] = (o * _silu(za)).astype(oa_ref.dtype)
    for h in heads:
        s_ref[h] = state[h]


def _delta_constants(ts):
    r = np.arange(ts)
    lag = r[:, None] - r[None, :]
    shifts = np.concatenate([lag == s for s in range(1, SHORT_CONV)], axis=0)
    tri = np.stack([lag >= 0, (lag >= 0) & (r[:, None] // CHUNK == r[None, :] // CHUNK)])
    pr, pc = np.arange(3 * LANES)[:, None], np.arange(LANES)[None, :]
    src_lane, piece = pr % LANES, pr // LANES
    owned = (src_lane >= LANE_F) & (src_lane < LANE_F + H_FOX) & (pc // 8 == src_lane - LANE_F)
    place = np.stack([owned & (pc % 8 == piece), -1.0 * (owned & (pc % 8 == piece + 3))])
    return tuple(jnp.asarray(a, dtype=BF16) for a in (shifts, tri, place))


def _delta_rule(qkv, za, small, conv_w, par, *, ts=256):
    b, t, _ = qkv.shape
    assert t % ts == 0 and ts % (2 * CHUNK) == 0
    blk = lambda width: pl.BlockSpec((1, ts, width), lambda bi, j: (bi, j, 0))
    fixed = lambda bi, j: (0, 0)
    fixed3 = lambda bi, j: (0, 0, 0)
    shifts, tri, place = _delta_constants(ts)
    return pl.pallas_call(
        functools.partial(_delta_body, ts=ts),
        grid=(b, t // ts),
        in_specs=[blk(W_DN), blk(H_DN * DV_DN), blk(LANES),
                  pl.BlockSpec(conv_w.shape, fixed), pl.BlockSpec(par.shape, fixed),
                  pl.BlockSpec(shifts.shape, fixed), pl.BlockSpec(tri.shape, fixed3),
                  pl.BlockSpec(place.shape, fixed3)],
        out_specs=[blk(H_DN * DV_DN), blk(LANES), blk(LANES)],
        out_shape=[jax.ShapeDtypeStruct((b, t, H_DN * DV_DN), BF16),
                   jax.ShapeDtypeStruct((b, t, LANES), BF16),
                   jax.ShapeDtypeStruct((b, t, LANES), BF16)],
        scratch_shapes=[pltpu.VMEM((3 * SUBLANES, W_DN), F32),
                        pltpu.VMEM((H_DN, DK_DN, DV_DN), F32),
                        pltpu.VMEM((1, LANES), F32)],
        compiler_params=_params("parallel", "arbitrary"),
        name="delta_rule",
    )(qkv, za, small, conv_w, par, shifts, tri, place)


def _fox_body(q_ref, k_ref, v_ref, qa_ref, ka_ref, o_ref, m_ref, acc_ref, *, tq, tk):
    i = pl.program_id(1)
    m_ref[...] = jnp.full(m_ref.shape, -jnp.inf, F32)
    acc_ref[...] = jnp.zeros_like(acc_ref)
    head_lanes = [slice(h * D_FOX, (h + 1) * D_FOX) for h in range(H_FOX)]
    lane = lax.broadcasted_iota(I32, (tk, LANES), 1)
    own = [jnp.where((lane >> 3) == h, 1.0, 0.0).astype(BF16) for h in range(H_FOX)]
    ones = jnp.ones((tk, D_FOX), BF16)
    keep = lax.broadcasted_iota(I32, (tq, tk), 1) <= lax.broadcasted_iota(I32, (tq, tk), 0)

    def block(start, diag_offset):
        rows = slice(0 if diag_offset is None else diag_offset, tq)
        ka = ka_ref[0, pl.ds(start, tk), :]
        for h, lanes in enumerate(head_lanes):
            q_aug = jnp.concatenate([q_ref[0, rows, lanes], qa_ref[0, rows, :]], axis=1)
            k_aug = jnp.concatenate([k_ref[0, pl.ds(start, tk), lanes], ka * own[h]], axis=1)
            s = lax.dot_general(q_aug, k_aug, NT, preferred_element_type=F32)
            if diag_offset is not None:
                s = jnp.where(keep[:tq - diag_offset], s, -jnp.inf)
            m_prev = m_ref[h, rows]
            m_next = jnp.maximum(m_prev, jnp.max(s, axis=-1, keepdims=True))
            p = jnp.exp2(s - jnp.concatenate([m_next] * (tk // LANES), axis=1))
            alpha = jnp.exp2(m_prev - m_next)
            v_aug = jnp.concatenate([v_ref[0, pl.ds(start, tk), lanes], ones], axis=1)
            acc_ref[h, rows] = (jnp.concatenate([alpha, alpha], axis=1) * acc_ref[h, rows]
                                + jnp.dot(p.astype(BF16), v_aug, preferred_element_type=F32))
            m_ref[h, rows] = m_next

    def full_block(jb, carry):
        block(pl.multiple_of(jb * tk, tk), None)
        return carry

    lax.fori_loop(0, i * (tq // tk), full_block, 0)
    for d in range(tq // tk):
        block(pl.multiple_of(i * tq + d * tk, tk), d * tk)
    for h, lanes in enumerate(head_lanes):
        acc = acc_ref[h]
        o_ref[0, :, lanes] = (acc[:, :D_FOX] / acc[:, D_FOX:]).astype(o_ref.dtype)


def _fox_attention(qkv, qaug, kaug, *, tq=1024, tk=512):
    b, t, _ = qkv.shape
    hd = H_FOX * D_FOX
    assert t % tq == 0 and tq % tk == 0 and tk % LANES == 0
    return pl.pallas_call(
        functools.partial(_fox_body, tq=tq, tk=tk),
        grid=(b, t // tq),
        in_specs=[pl.BlockSpec((1, tq, hd), lambda bi, i: (bi, i, 0)),
                  pl.BlockSpec((1, t, hd), lambda bi, i: (bi, 0, 1)),
                  pl.BlockSpec((1, t, hd), lambda bi, i: (bi, 0, 2)),
                  pl.BlockSpec((1, tq, LANES), lambda bi, i: (bi, i, 0)),
                  pl.BlockSpec((1, t, LANES), lambda bi, i: (bi, 0, 0))],
        out_specs=pl.BlockSpec((1, tq, hd), lambda bi, i: (bi, i, 0)),
        out_shape=jax.ShapeDtypeStruct((b, t, hd), BF16),
        scratch_shapes=[pltpu.VMEM((H_FOX, tq, LANES), F32), pltpu.VMEM((H_FOX, tq, 2 * D_FOX), F32)],
        compiler_params=_params("parallel", "arbitrary"),
        name="fox_attention",
    )(qkv, qkv, qkv, qaug, kaug)


def _merge_body(oa_ref, ob_ref, zc_ref, gates_ref, x_ref, wa_ref, wb_ref, wc_ref, wo_ref, g_ref,
                wr_ref, br_ref, xn_ref, hp_ref, route_ref, meta_ref, cnt_ref, run_ref, *, tm):
    step = pl.program_id(0)
    d = x_ref.shape[-1]

    @pl.when(step == 0)
    def _():
        run_ref[...] = jnp.zeros_like(run_ref)

    merged = None
    for idx, (m_ref, w_ref) in enumerate(((oa_ref, wa_ref), (ob_ref, wb_ref), (zc_ref, wc_ref))):
        y = jnp.dot(m_ref[...], w_ref[...], preferred_element_type=F32)
        term = _sigmoid(gates_ref[:, idx * d:(idx + 1) * d].astype(F32)) * y
        merged = term if merged is None else merged + term
    xn = x_ref[...] + jnp.dot(merged.astype(BF16), wo_ref[...], preferred_element_type=F32)
    xn_ref[...] = xn
    h2 = xn * lax.rsqrt(jnp.mean(xn * xn, axis=-1, keepdims=True) + EPS) * g_ref[...]

    hp_ref[...] = _pack_bf16_pairs(h2)

    logits = lax.dot_general(wr_ref[...], h2.astype(BF16), NT, preferred_element_type=F32) + br_ref[...]
    big = jnp.int32(LANES)
    gl = logits[LANE_GRP:LANE_GRP + N_GROUPS]
    g_row = lax.broadcasted_iota(I32, gl.shape, 0)
    gmax = jnp.max(gl, axis=0, keepdims=True)
    grp = jnp.min(jnp.where(gl == gmax, g_row, big), axis=0, keepdims=True)
    p_grp = 1.0 / jnp.sum(jnp.exp(gl - gmax), axis=0, keepdims=True)
    el_all = logits[LANE_EXP:LANE_EXP + N_EXPERTS]
    e_row = lax.broadcasted_iota(I32, el_all.shape, 0)
    el = jnp.where((e_row >> (EXPERTS_PER_GROUP.bit_length() - 1)) == grp, el_all, -jnp.inf)
    v0 = jnp.max(el, axis=0, keepdims=True)
    i0 = jnp.min(jnp.where(el == v0, e_row, big), axis=0, keepdims=True)
    el1 = jnp.where(e_row == i0, -jnp.inf, el)
    v1 = jnp.max(el1, axis=0, keepdims=True)
    i1 = jnp.min(jnp.where(el1 == v1, e_row, big), axis=0, keepdims=True)
    e1 = jnp.exp(v1 - v0)
    gate0 = p_grp / (1.0 + e1)
    gate1 = p_grp * e1 / (1.0 + e1)

    hot0 = e_row == i0
    hot1 = e_row == i1
    onehot = jnp.where(hot0 | hot1, 1.0, 0.0)
    tiles = [onehot[:, t * LANES:(t + 1) * LANES] for t in range(tm // LANES)]
    ri = lax.broadcasted_iota(I32, (LANES, LANES), 0)
    cj = lax.broadcasted_iota(I32, (LANES, LANES), 1)
    in_tile = _mm(jnp.concatenate(tiles, axis=0), jnp.where(ri < cj, 1.0, 0.0))
    before = run_ref[...]
    prefix = []
    for t, tile in enumerate(tiles):
        prefix.append(in_tile[t * N_EXPERTS:(t + 1) * N_EXPERTS] + before)
        before = before + jnp.sum(tile, axis=1, keepdims=True)
    prefix = jnp.concatenate(prefix, axis=1)
    run_ref[...] = before
    cnt_ref[...] = jnp.broadcast_to(before, cnt_ref.shape)
    rank0 = jnp.sum(jnp.where(hot0, prefix, 0.0), axis=0, keepdims=True)
    rank1 = jnp.sum(jnp.where(hot1, prefix, 0.0), axis=0, keepdims=True)

    rows = {R_GATE0: gate0, R_GATE1: gate1, R_EID0: i0.astype(F32), R_EID1: i1.astype(F32),
            R_RANK0: rank0, R_RANK1: rank1}
    fields = jnp.concatenate([rows.get(r, jnp.zeros_like(gate0)) for r in range(SUBLANES)], axis=0)
    meta_ref[...] = fields.astype(I32)
    route_ref[...] = jnp.concatenate([fields, jnp.zeros((LANES - SUBLANES, tm), F32)], axis=0).T


def _merge(oa, ob, zc, gates, x2, wa, wb, wc, wo, g, wr, br, *, tm=1024):
    n, d = x2.shape
    assert n % tm == 0
    row = lambda i: (i, 0)
    fixed = lambda i: (0, 0)
    full = lambda a: pl.BlockSpec(a.shape, fixed)
    return pl.pallas_call(
        functools.partial(_merge_body, tm=tm),
        grid=(n // tm,),
        in_specs=[pl.BlockSpec((tm, oa.shape[1]), row), pl.BlockSpec((tm, ob.shape[1]), row),
                  pl.BlockSpec((tm, zc.shape[1]), row), pl.BlockSpec((tm, 3 * d), row),
                  pl.BlockSpec((tm, d), row),
                  full(wa), full(wb), full(wc), full(wo), full(g), full(wr), full(br)],
        out_specs=[pl.BlockSpec((tm, d), row), pl.BlockSpec((tm, d // 2), row),
                   pl.BlockSpec((tm, LANES), row), pl.BlockSpec((SUBLANES, tm), lambda i: (0, i)),
                   pl.BlockSpec((N_EXPERTS, LANES), fixed)],
        out_shape=[jax.ShapeDtypeStruct((n, d), F32), jax.ShapeDtypeStruct((n, d // 2), U32),
                   jax.ShapeDtypeStruct((n, LANES), F32), jax.ShapeDtypeStruct((SUBLANES, n), I32),
                   jax.ShapeDtypeStruct((N_EXPERTS, LANES), F32)],
        scratch_shapes=[pltpu.VMEM((N_EXPERTS, 1), F32)],
        compiler_params=_params("arbitrary"),
        name="merge_router",
    )(oa, ob, zc, gates, x2, wa, wb, wc, wo, g, wr, br)


def _sc_mesh():
    return plsc.VectorSubcoreMesh(core_axis_name="c", subcore_axis_name="s")


def _sc_worker_base(per_worker):
    return (lax.axis_index("s") * SC_CORES + lax.axis_index("c")) * per_worker


def _sc_row_scatter(src, idx, out_rows, *, chunk=128):
    n, d = src.shape
    per_worker = n // (SC_CORES * SC_SUBCORES)
    chunk = min(chunk, per_worker)
    assert idx.shape == (2 * n,) and n % (SC_CORES * SC_SUBCORES) == 0 and per_worker % chunk == 0

    @functools.partial(
        pl.kernel, mesh=_sc_mesh(), out_type=jax.ShapeDtypeStruct((out_rows, d), src.dtype),
        scratch_types=[pltpu.VMEM((chunk,), I32), pltpu.VMEM((chunk, d), src.dtype), pltpu.SemaphoreType.DMA])
    def scatter(src_hbm, idx_hbm, out_hbm, idx_v, rows_v, sem):
        base = _sc_worker_base(per_worker)

        @pl.loop(0, per_worker // chunk)
        def _(step):
            off = base + step * chunk
            pltpu.sync_copy(src_hbm.at[pl.ds(off, chunk)], rows_v)
            for slot in range(2):
                pltpu.sync_copy(idx_hbm.at[pl.ds(slot * n + off, chunk)], idx_v)
                pltpu.async_copy(rows_v, out_hbm.at[idx_v], sem).wait()

    return scatter(src, idx)


def _pack_bf16_pairs(x):
    half = x.shape[-1] // 2
    lo = pltpu.bitcast(x[:, :half].astype(BF16).astype(F32), U32) >> 16
    hi = pltpu.bitcast(x[:, half:].astype(BF16).astype(F32), U32) & jnp.uint32(0xFFFF0000)
    return lo | hi


def _unpack_bf16_pairs(xp):
    return pltpu.bitcast(xp << 16, F32), pltpu.bitcast(xp & jnp.uint32(0xFFFF0000), F32)


def _expert_body(be_ref, valid_ref, nu_ref, xs_ref, w1_ref, w3_ref, w2_ref, ys_ref, w1b_ref, w3b_ref, w2b_ref):
    step = pl.program_id(0)
    live = step < nu_ref[0]

    @pl.when(jnp.logical_not(live))
    def _():
        ys_ref[...] = jnp.zeros_like(ys_ref)

    @pl.when(live & ((step == 0) | (be_ref[step] != be_ref[jnp.maximum(step - 1, 0)])))
    def _():
        w1b_ref[...] = w1_ref[0].astype(BF16)
        w3b_ref[...] = w3_ref[0].astype(BF16)
        w2b_ref[...] = w2_ref[0].astype(BF16)

    @pl.when(live)
    def _():
        row = lax.broadcasted_iota(I32, xs_ref.shape, 0)
        xp = jnp.where(row < valid_ref[step], xs_ref[...], jnp.uint32(0))
        half = xp.shape[-1]
        lo, hi = (part.astype(BF16) for part in _unpack_bf16_pairs(xp))

        def up(w_ref):
            return (jnp.dot(lo, w_ref[:half, :], preferred_element_type=F32)
                    + jnp.dot(hi, w_ref[half:, :], preferred_element_type=F32))

        act = (_silu(up(w1b_ref)) * up(w3b_ref)).astype(BF16)
        ys_ref[...] = _pack_bf16_pairs(jnp.dot(act, w2b_ref[...], preferred_element_type=F32))


def _experts(blk_eid, blk_valid, n_used, xs, w1, w3, w2, *, rb):
    p, half = xs.shape
    d = 2 * half
    nb = p // rb
    used = lambda i, be, bv, nu: jnp.maximum(jnp.minimum(i, nu[0] - 1), 0)
    wmap = lambda i, be, bv, nu: (be[used(i, be, bv, nu)], 0, 0)
    grid_spec = pltpu.PrefetchScalarGridSpec(
        num_scalar_prefetch=3,
        grid=(nb,),
        in_specs=[pl.BlockSpec((rb, half), lambda i, be, bv, nu: (used(i, be, bv, nu), 0)),
                  pl.BlockSpec((1, d, D_EXPERT), wmap), pl.BlockSpec((1, d, D_EXPERT), wmap),
                  pl.BlockSpec((1, D_EXPERT, d), wmap)],
        out_specs=pl.BlockSpec((rb, half), lambda i, be, bv, nu: (i, 0)),
        scratch_shapes=[pltpu.VMEM((d, D_EXPERT), BF16), pltpu.VMEM((d, D_EXPERT), BF16),
                        pltpu.VMEM((D_EXPERT, d), BF16)],
    )
    return pl.pallas_call(
        _expert_body,
        grid_spec=grid_spec,
        out_shape=jax.ShapeDtypeStruct((p, half), U32),
        compiler_params=_params("arbitrary"),
        name="moe_experts",
    )(blk_eid, blk_valid, n_used, xs, w1, w3, w2)


def _sc_row_gather(table, idx, *, chunk=128):
    rows, d = idx.shape[0], table.shape[1]
    per_worker = rows // (SC_CORES * SC_SUBCORES)
    chunk = min(chunk, per_worker)
    assert rows % (SC_CORES * SC_SUBCORES) == 0 and per_worker % chunk == 0

    @functools.partial(
        pl.kernel, mesh=_sc_mesh(), out_type=jax.ShapeDtypeStruct((rows, d), table.dtype),
        scratch_types=[pltpu.VMEM((chunk,), I32), pltpu.VMEM((chunk, d), table.dtype), pltpu.SemaphoreType.DMA])
    def gather(table_hbm, idx_hbm, out_hbm, idx_v, rows_v, sem):
        base = _sc_worker_base(per_worker)

        @pl.loop(0, per_worker // chunk)
        def _(step):
            off = base + step * chunk
            pltpu.sync_copy(idx_hbm.at[pl.ds(off, chunk)], idx_v)
            pltpu.async_copy(table_hbm.at[idx_v], rows_v, sem).wait()
            pltpu.sync_copy(rows_v, out_hbm.at[pl.ds(off, chunk)])

    return gather(table, idx)


def _combine_rows(r0_ref, r1_ref, x_ref, route_ref):
    route = route_ref[...]
    half = r0_ref.shape[-1]
    lo0, hi0 = _unpack_bf16_pairs(r0_ref[...])
    lo1, hi1 = _unpack_bf16_pairs(r1_ref[...])
    g0, g1 = route[:, R_GATE0:R_GATE0 + 1], route[:, R_GATE1:R_GATE1 + 1]
    return jnp.concatenate([x_ref[:, :half] + g0 * lo0 + g1 * lo1,
                            x_ref[:, half:] + g0 * hi0 + g1 * hi1], axis=1)


def _final_body(r0_ref, r1_ref, x_ref, route_ref, g_ref, o_ref):
    out = _combine_rows(r0_ref, r1_ref, x_ref, route_ref)
    o_ref[...] = out * lax.rsqrt(jnp.mean(out * out, axis=-1, keepdims=True) + EPS) * g_ref[...]


def _final_combine(rows2, xn, route, g, *, tm):
    n, d = xn.shape
    nt = n // tm
    row = lambda i: (i, 0)
    return pl.pallas_call(
        _final_body,
        grid=(nt,),
        in_specs=[pl.BlockSpec((tm, d // 2), row), pl.BlockSpec((tm, d // 2), lambda i: (i + nt, 0)),
                  pl.BlockSpec((tm, d), row), pl.BlockSpec((tm, LANES), row),
                  pl.BlockSpec((1, d), lambda i: (0, 0))],
        out_specs=pl.BlockSpec((tm, d), row),
        out_shape=jax.ShapeDtypeStruct((n, d), F32),
        compiler_params=_params("parallel"),
        name="moe_combine",
    )(rows2, rows2, xn, route, g)


def _moe_rows(hp, meta, counts, w1, w3, w2, *, layer, rb):
    n = hp.shape[0]
    cnt = counts[:, 0].astype(I32)
    nblk = (cnt + rb - 1) // rb
    bend = jnp.cumsum(nblk)
    pstart = (bend - nblk) * rb
    nb = (2 * n) // rb + N_EXPERTS
    n_used = bend[-1:].astype(I32)
    blk = jnp.arange(nb, dtype=I32)
    blk_eid = jnp.minimum(jnp.sum(bend[None, :] <= blk[:, None], axis=1), N_EXPERTS - 1).astype(I32)
    experts = jnp.arange(N_EXPERTS, dtype=I32)
    mine = blk_eid[:, None] == experts[None, :]
    blk_valid = jnp.clip(jnp.sum(jnp.where(mine, (cnt + pstart)[None, :], 0), axis=1) - blk * rb, 0, rb)
    eid = meta[R_EID0:R_EID1 + 1]
    first = jnp.sum(jnp.where(eid[None] == experts[:, None, None], pstart[:, None, None], 0), axis=0)
    dest = (first + meta[R_RANK0:R_RANK1 + 1]).reshape(-1)

    xs = _sc_row_scatter(hp, dest, nb * rb)
    ys = _experts(blk_eid + layer * N_EXPERTS, blk_valid.astype(I32), n_used, xs, w1, w3, w2, rb=rb)
    return _sc_row_gather(ys, dest)


def _lane_row(pairs):
    row = jnp.zeros((LANES,), F32)
    for off, vec in pairs:
        row = row.at[off:off + vec.shape[0]].set(vec.astype(F32))
    return row


def kernel(x, norm_mix, w_in, conv_qkv, dn_a_log, dn_dt_bias, dn_norm, fox_bias, conv_dw, conv_dw_b,
           conv_ln_g, conv_ln_b, w_a, w_b, w_c, w_out, norm_ffn, router_group_w, router_group_b,
           router_expert_w, router_expert_b, expert_w1, expert_w3, expert_w2, norm_final):
    b, t, d = x.shape
    n = b * t
    depth = w_in.shape[0]
    qk_dn = H_DN * DK_DN
    in_sizes = (qk_dn, qk_dn, H_DN * DV_DN, H_DN * DV_DN, H_DN, H_DN,
                H_FOX * D_FOX, H_FOX * D_FOX, H_FOX * D_FOX, H_FOX, 2 * C_CONV, d, d, d)
    splits = np.cumsum(in_sizes)[:-1].tolist()
    tm_final, rb = 512, 512

    x_src = x.reshape(n, d)
    w1_all = expert_w1.reshape(depth * N_EXPERTS, d, D_EXPERT)
    w3_all = expert_w3.reshape(depth * N_EXPERTS, d, D_EXPERT)
    w2_all = expert_w2.reshape(depth * N_EXPERTS, D_EXPERT, d)
    for l in range(depth):
        (qa, ka, va, za, ba, aa, qb, kb, vb, fb, uc, ga, gb, gc) = jnp.split(w_in[l], splits, axis=1)
        w_main = jnp.concatenate([qa, ka, va, za, qb * (D_FOX ** -0.5 * LOG2E), kb, vb, uc, ga, gb, gc],
                                 axis=1).astype(BF16)
        w_small = jnp.concatenate([ba, aa, fb, jnp.zeros((d, LANES - 3 * H_DN), F32)], axis=1).astype(BF16)
        vec = jnp.stack([conv_dw_b[l], conv_ln_g[l], conv_ln_b[l]] + [jnp.zeros((C_CONV,), F32)] * 5)
        outs = _in_proj(x_src, norm_mix[l][None, :], w_main, w_small, conv_dw[l], vec, seq_len=t)
        x2 = outs[0] if l > 0 else x_src
        dn_qkv, za_p, fox_qkv, zc, gates, small = outs[-6:]

        par = jnp.stack([_lane_row([(LANE_G, -jnp.exp(dn_a_log[l]))]),
                         _lane_row([(LANE_G, dn_dt_bias[l]), (LANE_F, fox_bias[l])]),
                         _lane_row([(0, dn_norm[l])])] + [jnp.zeros((LANES,), F32)] * 5)
        oa, qaug, kaug = _delta_rule(dn_qkv.reshape(b, t, -1), za_p.reshape(b, t, -1),
                                     small.reshape(b, t, LANES), conv_qkv[l], par)
        ob = _fox_attention(fox_qkv.reshape(b, t, -1), qaug, kaug)

        w_r = jnp.concatenate([router_group_w[l], jnp.zeros((d, LANE_EXP - N_GROUPS), F32),
                               router_expert_w[l], jnp.zeros((d, LANES - LANE_EXP - N_EXPERTS), F32)], axis=1)
        b_r = _lane_row([(LANE_GRP, router_group_b[l]), (LANE_EXP, router_expert_b[l])])[None, :]
        xn, hp, route, meta, counts = _merge(
            oa.reshape(n, -1), ob.reshape(n, -1), zc.reshape(n, -1), gates, x2,
            w_a[l].astype(BF16), w_b[l].astype(BF16), w_c[l].astype(BF16), w_out[l].astype(BF16),
            norm_ffn[l][None, :], w_r.T.astype(BF16), b_r.reshape(LANES, 1))

        x_src = (_moe_rows(hp, meta, counts, w1_all, w3_all, w2_all, layer=l, rb=rb), xn, route)
    return _final_combine(*x_src, norm_final[None, :], tm=tm_final).reshape(b, t, d)
```

```python
import functools

import jax
import jax.numpy as jnp
import numpy as np
from jax import lax
from jax.experimental import pallas as pl
from jax.experimental.pallas import tpu as pltpu
from jax.experimental.pallas import tpu_sc as plsc

F32 = jnp.float32
BF16 = jnp.bfloat16
U32 = jnp.uint32
I32 = jnp.int32

EPS = 1e-6
LOG2E = 1.4426950408889634
LANES = 128
SUBLANES = 8
SC_CORES, SC_SUBCORES = 2, 16
H_DN, DK_DN, DV_DN = 4, 128, 128
SHORT_CONV = 4
CHUNK = 64
H_FOX, D_FOX = 4, 128
C_CONV = 512
CONV_WIDTH = 31
N_GROUPS, EXPERTS_PER_GROUP = 4, 8
N_EXPERTS = N_GROUPS * EXPERTS_PER_GROUP
D_EXPERT = 256

W_DN = 3 * H_DN * DK_DN
W_FOX = 3 * H_FOX * D_FOX
VMEM_LIMIT = 56 * 1024 * 1024

LANE_BETA, LANE_G, LANE_F = 0, 4, 8
LANE_GRP, LANE_EXP = 0, 32
R_GATE0, R_GATE1, R_EID0, R_EID1, R_RANK0, R_RANK1 = 0, 1, 2, 3, 4, 5

NN = (((1,), (0,)), ((), ()))
NT = (((1,), (1,)), ((), ()))
TN = (((0,), (0,)), ((), ()))


def _mm(a, b, dims=NN):
    return lax.dot_general(a.astype(BF16), b.astype(BF16), dims, preferred_element_type=F32)


def _sigmoid(x):
    return 0.5 * jnp.tanh(0.5 * x) + 0.5


def _silu(x):
    half = 0.5 * x
    return half * jnp.tanh(half) + half


def _params(*sem):
    return pltpu.CompilerParams(dimension_semantics=sem, vmem_limit_bytes=VMEM_LIMIT)


def _in_proj_body(*refs, col_chunk, combine, tiles_per_seq, hrows, rchunk):
    if combine:
        r0_ref, r1_ref, xn_ref, route_ref, g_ref, w_ref, ws_ref, dw_ref, vec_ref, x_out_ref = refs[:10]
        x = _combine_rows(r0_ref, r1_ref, xn_ref, route_ref)
        x_out_ref[...] = x
        rest = refs[10:]
    else:
        x_ref, g_ref, w_ref, ws_ref, dw_ref, vec_ref = refs[:6]
        x = x_ref[...]
        rest = refs[6:]
    dn_ref, za_ref, fox_ref, zc_ref, gates_ref, small_ref, z_ref, zs_ref, y_ref = rest
    tm = x.shape[0]
    h = x * lax.rsqrt(jnp.mean(x * x, axis=-1, keepdims=True) + EPS) * g_ref[...]
    hb = h.astype(BF16)

    def project(col, width, hold=None):
        lhs = hb
        if hold is not None:
            zero = jnp.minimum(jnp.abs(hold), 0.0).astype(BF16)
            head = jnp.concatenate([hb[0:rchunk, 0:LANES] + zero, hb[0:rchunk, LANES:]], axis=1)
            lhs = jnp.concatenate([head, hb[rchunk:]], axis=0)
        return jnp.dot(lhs, w_ref[:, col:col + width], preferred_element_type=F32)

    starts, col = {}, 0
    for name, width in (("dn", dn_ref.shape[-1]), ("za", za_ref.shape[-1]), ("fox", fox_ref.shape[-1]),
                        ("conv", 2 * C_CONV), ("gates", gates_ref.shape[-1])):
        starts[name], col = col, col + width

    first = pl.program_id(0) % tiles_per_seq == 0
    col = starts["conv"]

    @pl.when(first)
    def _():
        z_ref[0:hrows, :] = jnp.zeros((hrows, C_CONV), F32)

    @pl.when(jnp.logical_not(first))
    def _():
        z_ref[0:hrows, :] = z_ref[tm:tm + hrows, :]

    z_ref[hrows:hrows + tm, :] = project(col, C_CONV) * _sigmoid(project(col + C_CONV, C_CONV))
    sub = SUBLANES
    span = tm + hrows - sub
    for phase in range(1, sub):
        zs_ref[phase - 1, 0:span, :] = z_ref[phase:phase + span, :]

    def conv_unit(lg, r0, after):
        lanes = slice(lg * LANES, (lg + 1) * LANES)
        acc = vec_ref[0:1, lanes] + jnp.minimum(jnp.abs(after[0:rchunk, 0:LANES]), 0.0)
        for k in range(CONV_WIDTH):
            off = hrows - (CONV_WIDTH - 1) + k
            phase, base = off % sub, off - off % sub + r0
            if phase == 0:
                tap = z_ref[base:base + rchunk, lanes]
            else:
                tap = zs_ref[phase - 1, base:base + rchunk, lanes]
            acc = acc + dw_ref[k:k + 1, lanes] * tap
        y_ref[r0:r0 + rchunk, lanes] = acc
        return acc

    units = [(lg, r0) for lg in range(C_CONV // LANES) for r0 in range(0, tm, rchunk)]
    chunks = [(ref, starts[name], c) for name, ref in (("dn", dn_ref), ("za", za_ref), ("fox", fox_ref),
                                                       ("gates", gates_ref))
              for c in range(0, ref.shape[-1], col_chunk)]
    share = [[] for _ in chunks]
    for u, unit in enumerate(units):
        share[u * len(chunks) // len(units)].append(unit)
    last = []
    for idx, ((ref, start, c), mine) in enumerate(zip(chunks, share)):
        res = project(start + c, col_chunk, hold=last[idx - 2] if idx >= 2 else None)
        ref[:, c:c + col_chunk] = res.astype(ref.dtype)
        tile = None
        for lg, r0 in mine:
            tile = conv_unit(lg, r0, res)
        last.append(tile)
    small_ref[...] = jnp.dot(hb, ws_ref[...], preferred_element_type=F32)

    y = y_ref[...]
    mu = jnp.mean(y, axis=-1, keepdims=True)
    yc = y - mu
    var = jnp.mean(yc * yc, axis=-1, keepdims=True)
    yn = yc * lax.rsqrt(var + EPS) * vec_ref[1:2, :] + vec_ref[2:3, :]
    zc_ref[...] = _silu(yn).astype(zc_ref.dtype)


def _in_proj(x_src, g, w, ws, dw, vec, *, seq_len, tm=256, col_chunk=512, hrows=32, rchunk=32):
    combine = isinstance(x_src, tuple)
    n, d = (x_src[1] if combine else x_src).shape
    nt = n // tm
    widths = (W_DN, H_DN * DV_DN, W_FOX, C_CONV, 3 * d)
    assert w.shape == (d, sum(widths) + C_CONV) and n % tm == 0 and seq_len % tm == 0
    assert hrows >= CONV_WIDTH - 1 and hrows % SUBLANES == 0 and tm % rchunk == 0
    row = lambda i: (i, 0)
    fixed = lambda i: (0, 0)
    out_shape = [jax.ShapeDtypeStruct((n, wd), BF16) for wd in widths] + [jax.ShapeDtypeStruct((n, LANES), F32)]
    out_specs = [pl.BlockSpec((tm, wd), row) for wd in widths] + [pl.BlockSpec((tm, LANES), row)]
    if combine:
        rows2, xn, route = x_src
        args = (rows2, rows2, xn, route)
        in_specs = [pl.BlockSpec((tm, d // 2), row), pl.BlockSpec((tm, d // 2), lambda i: (i + nt, 0)),
                    pl.BlockSpec((tm, d), row), pl.BlockSpec((tm, LANES), row)]
        out_shape = [jax.ShapeDtypeStruct((n, d), F32)] + out_shape
        out_specs = [pl.BlockSpec((tm, d), row)] + out_specs
    else:
        args = (x_src,)
        in_specs = [pl.BlockSpec((tm, d), row)]
    return pl.pallas_call(
        functools.partial(_in_proj_body, col_chunk=col_chunk, combine=combine, tiles_per_seq=seq_len // tm,
                          hrows=hrows, rchunk=rchunk),
        grid=(nt,),
        in_specs=in_specs + [pl.BlockSpec((1, d), fixed),
                             pl.BlockSpec(w.shape, fixed, pipeline_mode=pl.Buffered(1)),
                             pl.BlockSpec(ws.shape, fixed, pipeline_mode=pl.Buffered(1)),
                             pl.BlockSpec(dw.shape, fixed), pl.BlockSpec(vec.shape, fixed)],
        out_specs=out_specs,
        out_shape=out_shape,
        scratch_shapes=[pltpu.VMEM((tm + hrows, C_CONV), F32),
                        pltpu.VMEM((SUBLANES - 1, tm + hrows, C_CONV), F32),
                        pltpu.VMEM((tm, C_CONV), F32)],
        compiler_params=_params("arbitrary"),
        name="in_proj",
    )(*args, g, w, ws, dw, vec)


def _softplus_parts(z):
    t = jnp.log1p(jnp.exp(-jnp.abs(z)))
    return jnp.maximum(z, 0.0) + t, -(jnp.maximum(-z, 0.0) + t)


def _delta_body(qkv_ref, za_ref, sm_ref, cw_ref, par_ref, shifts_ref, tri_ref, place_ref,
                oa_ref, qaug_ref, kaug_ref, xs_ref, s_ref, carry_ref, *, ts):
    j = pl.program_id(1)
    halo = SUBLANES
    pack = 2 * SUBLANES

    @pl.when(j == 0)
    def _():
        xs_ref[0:halo, :] = jnp.zeros((halo, W_DN), F32)
        s_ref[...] = jnp.zeros_like(s_ref)
        carry_ref[...] = jnp.zeros_like(carry_ref)

    @pl.when(j > 0)
    def _():
        xs_ref[0:halo, :] = xs_ref[2 * halo:3 * halo, :]

    xb = qkv_ref[0]
    xs_ref[halo:2 * halo, :] = qkv_ref[0, 0:pack, :].astype(F32)[0:halo]
    xs_ref[2 * halo:3 * halo, :] = qkv_ref[0, ts - pack:ts, :].astype(F32)[pack - halo:pack]

    stacked = jnp.dot(shifts_ref[...], xb, preferred_element_type=F32)
    shifted = [stacked[(s - 1) * ts:s * ts] for s in range(1, SHORT_CONV)]
    head_row = lax.broadcasted_iota(I32, (halo, LANES), 0)

    def conv_silu(lane0):
        lanes = slice(lane0, lane0 + LANES)
        acc = cw_ref[SHORT_CONV - 1:SHORT_CONV, lanes] * xb[:, lanes].astype(F32)
        for s in range(1, SHORT_CONV):
            acc = acc + cw_ref[SHORT_CONV - 1 - s:SHORT_CONV - s, lanes] * shifted[s - 1][:, lanes]
        head = acc[0:halo]
        for s in range(1, SHORT_CONV):
            prev = jnp.where(head_row < s, xs_ref[halo - s:2 * halo - s, lanes], 0.0)
            head = head + cw_ref[SHORT_CONV - 1 - s:SHORT_CONV - s, lanes] * prev
        return _silu(jnp.concatenate([head, acc[halo:]], axis=0))

    def l2n(a):
        return a * lax.rsqrt(jnp.sum(a * a, axis=-1, keepdims=True) + EPS)

    sm = sm_ref[0]
    lane = lax.broadcasted_iota(I32, sm.shape, 1)
    sp, logsig = _softplus_parts(sm + par_ref[1:2, :])
    vals = jnp.where(lane < LANE_G, _sigmoid(sm),
                     jnp.where(lane < LANE_F, par_ref[0:1, :] * sp,
                               jnp.where(lane < LANE_F + H_FOX, logsig, 0.0)))
    log_chunk = CHUNK.bit_length() - 1

    hi = vals.astype(BF16)
    rem = vals - hi.astype(F32)
    mid = rem.astype(BF16)
    lo = (rem - mid.astype(F32)).astype(BF16)
    pieces = jnp.concatenate([hi, mid, lo], axis=-1)

    def cumsum(tri):
        y = jnp.dot(tri, pieces, preferred_element_type=F32)
        return (y[:, :LANES] + y[:, LANES:2 * LANES]) + y[:, 2 * LANES:]

    ccum = cumsum(tri_ref[0]) + carry_ref[...]
    gcum = cumsum(tri_ref[1])
    carry_ref[...] = ccum[ts - 1:ts, :]
    gcum_t = gcum.T

    cl = ccum * LOG2E
    c_hi = cl.astype(BF16)
    c_rem = cl - c_hi.astype(F32)
    c_mid = c_rem.astype(BF16)
    c_lo = (c_rem - c_mid.astype(F32)).astype(BF16)
    c_pieces = jnp.concatenate([c_hi, c_mid, c_lo], axis=-1)
    slot = lane & 7
    in_heads = lane < 8 * H_FOX
    ones_q = jnp.where(in_heads & (slot >= 3) & (slot < 6), 1.0, 0.0)
    ones_k = jnp.where(in_heads & (slot < 3), 1.0, 0.0)
    qaug_ref[0] = (jnp.dot(c_pieces, place_ref[0], preferred_element_type=F32) + ones_q).astype(BF16)
    kaug_ref[0] = (jnp.dot(c_pieces, place_ref[1], preferred_element_type=F32) + ones_k).astype(BF16)

    scale = DK_DN ** -0.5
    dn_norm = par_ref[2:3, :]

    pw = 2 * CHUNK
    prow = lax.broadcasted_iota(I32, (pw, pw), 0)
    pcol = lax.broadcasted_iota(I32, (pw, pw), 1)
    same = (prow >> log_chunk) == (pcol >> log_chunk)
    causal_p = (prow >= pcol) & same
    strict_p = (prow > pcol) & same
    levels = []
    s = 1
    while s < CHUNK:
        levels.append(((prow >> s.bit_length()) == (pcol >> s.bit_length()))
                      & ((prow & s) != 0) & ((pcol & s) == 0))
        s *= 2

    heads = range(H_DN)
    pairs = range(ts // pw)
    q, k, v = [], [], []
    for h in heads:
        q.append(l2n(conv_silu(h * DK_DN)))
        k.append(l2n(conv_silu(H_DN * DK_DN + h * DK_DN)))
        v.append(conv_silu(2 * H_DN * DK_DN + h * DV_DN))

    ctx = []
    for h in heads:
        for p in pairs:
            pr = slice(p * pw, (p + 1) * pw)
            gcol = gcum[pr, LANE_G + h:LANE_G + h + 1]
            grow = gcum_t[LANE_G + h:LANE_G + h + 1, pr]
            beta = vals[pr, LANE_BETA + h:LANE_BETA + h + 1]
            decay = jnp.where(causal_p, jnp.exp(jnp.where(causal_p, gcol - grow, 0.0)), 0.0)
            egc = jnp.exp(gcol)
            kp = k[h][pr]
            kb = kp * beta
            ctx.append(dict(
                h=h, p=p, gcol=gcol, kp=kp,
                a=jnp.where(strict_p, _mm(kb, kp, NT) * decay, 0.0),
                qk=jnp.where(causal_p, _mm(q[h][pr] * scale, kp, NT) * decay, 0.0),
                rhs=jnp.concatenate([v[h][pr] * beta, kb * egc], axis=-1),
                qg=q[h][pr] * (scale * egc)))

    for c in ctx:
        c["n"] = -jnp.where(levels[0], c["a"], 0.0)
    for level in levels[1:]:
        for c in ctx:
            m = jnp.where(level, c["a"], 0.0)
            c["y"] = m + _mm(c["n"], m)
        for c in ctx:
            c["n"] = c["n"] - (c["y"] + _mm(c["y"], c["n"]))
    for c in ctx:
        c["sol"] = c["rhs"] + _mm(c["n"], c["rhs"])

    state = [s_ref[h] for h in heads]
    for p in pairs:
        group = [c for c in ctx if c["p"] == p]
        v_prev = [None] * H_DN
        for ch in range(2):
            rows = slice(ch * CHUNK, (ch + 1) * CHUNK)
            out_rows = slice(p * pw + ch * CHUNK, p * pw + (ch + 1) * CHUNK)
            on_state = [_mm(jnp.concatenate([c["sol"][rows, DV_DN:], c["qg"][rows]], axis=0), state[c["h"]])
                        for c in group]
            for c, ws_qs in zip(group, on_state):
                h = c["h"]
                vn = c["sol"][rows, :DV_DN] - ws_qs[:CHUNK]
                v_pair = jnp.concatenate([vn, jnp.zeros_like(vn)] if ch == 0 else [v_prev[h], vn], axis=0)
                o = ws_qs[CHUNK:] + _mm(c["qk"][rows], v_pair)
                glast = c["gcol"][(ch + 1) * CHUNK - 1:(ch + 1) * CHUNK]
                state[h] = (state[h] * jnp.exp(glast)
                            + _mm(c["kp"][rows] * jnp.exp(glast - c["gcol"][rows]), vn, TN))
                v_prev[h] = vn
                o = o * lax.rsqrt(jnp.mean(o * o, axis=-1, keepdims=True) + EPS) * dn_norm
                za = za_ref[0, out_rows, h * DV_DN:(h + 1) * DV_DN].astype(F32)
                oa_ref[0, out_rows, h * DV_DN:(h + 1) * DV_DN] = (o * _silu(za)).astype(oa_ref.dtype)
    for h in heads:
        s_ref[h] = state[h]


def _delta_constants(ts):
    r = np.arange(ts)
    lag = r[:, None] - r[None, :]
    shifts = np.concatenate([lag == s for s in range(1, SHORT_CONV)], axis=0)
    tri = np.stack([lag >= 0, (lag >= 0) & (r[:, None] // CHUNK == r[None, :] // CHUNK)])
    pr, pc = np.arange(3 * LANES)[:, None], np.arange(LANES)[None, :]
    src_lane, piece = pr % LANES, pr // LANES
    owned = (src_lane >= LANE_F) & (src_lane < LANE_F + H_FOX) & (pc // 8 == src_lane - LANE_F)
    place = np.stack([owned & (pc % 8 == piece), -1.0 * (owned & (pc % 8 == piece + 3))])
    return tuple(jnp.asarray(a, dtype=BF16) for a in (shifts, tri, place))


def _delta_rule(qkv, za, small, conv_w, par, *, ts=256):
    b, t, _ = qkv.shape
    assert t % ts == 0 and ts % (2 * CHUNK) == 0
    blk = lambda width: pl.BlockSpec((1, ts, width), lambda bi, j: (bi, j, 0))
    fixed = lambda bi, j: (0, 0)
    fixed3 = lambda bi, j: (0, 0, 0)
    shifts, tri, place = _delta_constants(ts)
    return pl.pallas_call(
        functools.partial(_delta_body, ts=ts),
        grid=(b, t // ts),
        in_specs=[blk(W_DN), blk(H_DN * DV_DN), blk(LANES),
                  pl.BlockSpec(conv_w.shape, fixed), pl.BlockSpec(par.shape, fixed),
                  pl.BlockSpec(shifts.shape, fixed), pl.BlockSpec(tri.shape, fixed3),
                  pl.BlockSpec(place.shape, fixed3)],
        out_specs=[blk(H_DN * DV_DN), blk(LANES), blk(LANES)],
        out_shape=[jax.ShapeDtypeStruct((b, t, H_DN * DV_DN), BF16),
                   jax.ShapeDtypeStruct((b, t, LANES), BF16),
                   jax.ShapeDtypeStruct((b, t, LANES), BF16)],
        scratch_shapes=[pltpu.VMEM((3 * SUBLANES, W_DN), F32),
                        pltpu.VMEM((H_DN, DK_DN, DV_DN), F32),
                        pltpu.VMEM((1, LANES), F32)],
        compiler_params=_params("parallel", "arbitrary"),
        name="delta_rule",
    )(qkv, za, small, conv_w, par, shifts, tri, place)


def _fox_body(q_ref, k_ref, v_ref, qa_ref, ka_ref, o_ref, m_ref, acc_ref, *, tq, tk):
    i = pl.program_id(1)
    m_ref[...] = jnp.full(m_ref.shape, -jnp.inf, F32)
    acc_ref[...] = jnp.zeros_like(acc_ref)
    head_lanes = [slice(h * D_FOX, (h + 1) * D_FOX) for h in range(H_FOX)]
    lane = lax.broadcasted_iota(I32, (tk, LANES), 1)
    own = [jnp.where((lane >> 3) == h, 1.0, 0.0).astype(BF16) for h in range(H_FOX)]
    ones = jnp.ones((tk, D_FOX), BF16)
    keep = lax.broadcasted_iota(I32, (tq, tk), 1) <= lax.broadcasted_iota(I32, (tq, tk), 0)

    def block(start, diag_offset):
        rows = slice(0 if diag_offset is None else diag_offset, tq)
        ka = ka_ref[0, pl.ds(start, tk), :]
        for h, lanes in enumerate(head_lanes):
            q_aug = jnp.concatenate([q_ref[0, rows, lanes], qa_ref[0, rows, :]], axis=1)
            k_aug = jnp.concatenate([k_ref[0, pl.ds(start, tk), lanes], ka * own[h]], axis=1)
            s = lax.dot_general(q_aug, k_aug, NT, preferred_element_type=F32)
            if diag_offset is not None:
                s = jnp.where(keep[:tq - diag_offset], s, -jnp.inf)
            m_prev = m_ref[h, rows]
            m_next = jnp.maximum(m_prev, jnp.max(s, axis=-1, keepdims=True))
            p = jnp.exp2(s - jnp.concatenate([m_next] * (tk // LANES), axis=1))
            alpha = jnp.exp2(m_prev - m_next)
            v_aug = jnp.concatenate([v_ref[0, pl.ds(start, tk), lanes], ones], axis=1)
            acc_ref[h, rows] = (jnp.concatenate([alpha, alpha], axis=1) * acc_ref[h, rows]
                                + jnp.dot(p.astype(BF16), v_aug, preferred_element_type=F32))
            m_ref[h, rows] = m_next

    def full_block(jb, carry):
        block(pl.multiple_of(jb * tk, tk), None)
        return carry

    lax.fori_loop(0, i * (tq // tk), full_block, 0)
    for d in range(tq // tk):
        block(pl.multiple_of(i * tq + d * tk, tk), d * tk)
    for h, lanes in enumerate(head_lanes):
        acc = acc_ref[h]
        o_ref[0, :, lanes] = (acc[:, :D_FOX] / acc[:, D_FOX:]).astype(o_ref.dtype)


def _fox_attention(qkv, qaug, kaug, *, tq=1024, tk=512):
    b, t, _ = qkv.shape
    hd = H_FOX * D_FOX
    assert t % tq == 0 and tq % tk == 0 and tk % LANES == 0
    return pl.pallas_call(
        functools.partial(_fox_body, tq=tq, tk=tk),
        grid=(b, t // tq),
        in_specs=[pl.BlockSpec((1, tq, hd), lambda bi, i: (bi, i, 0)),
                  pl.BlockSpec((1, t, hd), lambda bi, i: (bi, 0, 1)),
                  pl.BlockSpec((1, t, hd), lambda bi, i: (bi, 0, 2)),
                  pl.BlockSpec((1, tq, LANES), lambda bi, i: (bi, i, 0)),
                  pl.BlockSpec((1, t, LANES), lambda bi, i: (bi, 0, 0))],
        out_specs=pl.BlockSpec((1, tq, hd), lambda bi, i: (bi, i, 0)),
        out_shape=jax.ShapeDtypeStruct((b, t, hd), BF16),
        scratch_shapes=[pltpu.VMEM((H_FOX, tq, LANES), F32), pltpu.VMEM((H_FOX, tq, 2 * D_FOX), F32)],
        compiler_params=_params("parallel", "arbitrary"),
        name="fox_attention",
    )(qkv, qkv, qkv, qaug, kaug)


def _merge_body(oa_ref, ob_ref, zc_ref, gates_ref, x_ref, wa_ref, wb_ref, wc_ref, wo_ref, g_ref,
                wr_ref, br_ref, xn_ref, hp_ref, route_ref, meta_ref, cnt_ref, run_ref, *, tm):
    step = pl.program_id(0)
    d = x_ref.shape[-1]

    @pl.when(step == 0)
    def _():
        run_ref[...] = jnp.zeros_like(run_ref)

    merged = None
    for idx, (m_ref, w_ref) in enumerate(((oa_ref, wa_ref), (ob_ref, wb_ref), (zc_ref, wc_ref))):
        y = jnp.dot(m_ref[...], w_ref[...], preferred_element_type=F32)
        term = _sigmoid(gates_ref[:, idx * d:(idx + 1) * d].astype(F32)) * y
        merged = term if merged is None else merged + term
    xn = x_ref[...] + jnp.dot(merged.astype(BF16), wo_ref[...], preferred_element_type=F32)
    xn_ref[...] = xn
    h2 = xn * lax.rsqrt(jnp.mean(xn * xn, axis=-1, keepdims=True) + EPS) * g_ref[...]

    hp_ref[...] = _pack_bf16_pairs(h2)

    logits = lax.dot_general(wr_ref[...], h2.astype(BF16), NT, preferred_element_type=F32) + br_ref[...]
    big = jnp.int32(LANES)
    gl = logits[LANE_GRP:LANE_GRP + N_GROUPS]
    g_row = lax.broadcasted_iota(I32, gl.shape, 0)
    gmax = jnp.max(gl, axis=0, keepdims=True)
    grp = jnp.min(jnp.where(gl == gmax, g_row, big), axis=0, keepdims=True)
    p_grp = 1.0 / jnp.sum(jnp.exp(gl - gmax), axis=0, keepdims=True)
    el_all = logits[LANE_EXP:LANE_EXP + N_EXPERTS]
    e_row = lax.broadcasted_iota(I32, el_all.shape, 0)
    el = jnp.where((e_row >> (EXPERTS_PER_GROUP.bit_length() - 1)) == grp, el_all, -jnp.inf)
    v0 = jnp.max(el, axis=0, keepdims=True)
    i0 = jnp.min(jnp.where(el == v0, e_row, big), axis=0, keepdims=True)
    el1 = jnp.where(e_row == i0, -jnp.inf, el)
    v1 = jnp.max(el1, axis=0, keepdims=True)
    i1 = jnp.min(jnp.where(el1 == v1, e_row, big), axis=0, keepdims=True)
    e1 = jnp.exp(v1 - v0)
    gate0 = p_grp / (1.0 + e1)
    gate1 = p_grp * e1 / (1.0 + e1)

    hot0 = e_row == i0
    hot1 = e_row == i1
    onehot = jnp.where(hot0 | hot1, 1.0, 0.0)
    tiles = [onehot[:, t * LANES:(t + 1) * LANES] for t in range(tm // LANES)]
    ri = lax.broadcasted_iota(I32, (LANES, LANES), 0)
    cj = lax.broadcasted_iota(I32, (LANES, LANES), 1)
    in_tile = _mm(jnp.concatenate(tiles, axis=0), jnp.where(ri < cj, 1.0, 0.0))
    before = run_ref[...]
    prefix = []
    for t, tile in enumerate(tiles):
        prefix.append(in_tile[t * N_EXPERTS:(t + 1) * N_EXPERTS] + before)
        before = before + jnp.sum(tile, axis=1, keepdims=True)
    prefix = jnp.concatenate(prefix, axis=1)
    run_ref[...] = before
    cnt_ref[...] = jnp.broadcast_to(before, cnt_ref.shape)
    rank0 = jnp.sum(jnp.where(hot0, prefix, 0.0), axis=0, keepdims=True)
    rank1 = jnp.sum(jnp.where(hot1, prefix, 0.0), axis=0, keepdims=True)

    rows = {R_GATE0: gate0, R_GATE1: gate1, R_EID0: i0.astype(F32), R_EID1: i1.astype(F32),
            R_RANK0: rank0, R_RANK1: rank1}
    fields = jnp.concatenate([rows.get(r, jnp.zeros_like(gate0)) for r in range(SUBLANES)], axis=0)
    meta_ref[...] = fields.astype(I32)
    route_ref[...] = jnp.concatenate([fields, jnp.zeros((LANES - SUBLANES, tm), F32)], axis=0).T


def _merge(oa, ob, zc, gates, x2, wa, wb, wc, wo, g, wr, br, *, tm=1024):
    n, d = x2.shape
    assert n % tm == 0
    row = lambda i: (i, 0)
    fixed = lambda i: (0, 0)
    full = lambda a: pl.BlockSpec(a.shape, fixed)
    return pl.pallas_call(
        functools.partial(_merge_body, tm=tm),
        grid=(n // tm,),
        in_specs=[pl.BlockSpec((tm, oa.shape[1]), row), pl.BlockSpec((tm, ob.shape[1]), row),
                  pl.BlockSpec((tm, zc.shape[1]), row), pl.BlockSpec((tm, 3 * d), row),
                  pl.BlockSpec((tm, d), row),
                  full(wa), full(wb), full(wc), full(wo), full(g), full(wr), full(br)],
        out_specs=[pl.BlockSpec((tm, d), row), pl.BlockSpec((tm, d // 2), row),
                   pl.BlockSpec((tm, LANES), row), pl.BlockSpec((SUBLANES, tm), lambda i: (0, i)),
                   pl.BlockSpec((N_EXPERTS, LANES), fixed)],
        out_shape=[jax.ShapeDtypeStruct((n, d), F32), jax.ShapeDtypeStruct((n, d // 2), U32),
                   jax.ShapeDtypeStruct((n, LANES), F32), jax.ShapeDtypeStruct((SUBLANES, n), I32),
                   jax.ShapeDtypeStruct((N_EXPERTS, LANES), F32)],
        scratch_shapes=[pltpu.VMEM((N_EXPERTS, 1), F32)],
        compiler_params=_params("arbitrary"),
        name="merge_router",
    )(oa, ob, zc, gates, x2, wa, wb, wc, wo, g, wr, br)


def _sc_mesh():
    return plsc.VectorSubcoreMesh(core_axis_name="c", subcore_axis_name="s")


def _sc_worker_base(per_worker):
    return (lax.axis_index("s") * SC_CORES + lax.axis_index("c")) * per_worker


def _sc_row_scatter(src, idx, out_rows, *, chunk=128):
    n, d = src.shape
    per_worker = n // (SC_CORES * SC_SUBCORES)
    chunk = min(chunk, per_worker)
    assert idx.shape == (2 * n,) and n % (SC_CORES * SC_SUBCORES) == 0 and per_worker % chunk == 0

    @functools.partial(
        pl.kernel, mesh=_sc_mesh(), out_type=jax.ShapeDtypeStruct((out_rows, d), src.dtype),
        scratch_types=[pltpu.VMEM((chunk,), I32), pltpu.VMEM((chunk, d), src.dtype), pltpu.SemaphoreType.DMA])
    def scatter(src_hbm, idx_hbm, out_hbm, idx_v, rows_v, sem):
        base = _sc_worker_base(per_worker)

        @pl.loop(0, per_worker // chunk)
        def _(step):
            off = base + step * chunk
            pltpu.sync_copy(src_hbm.at[pl.ds(off, chunk)], rows_v)
            for slot in range(2):
                pltpu.sync_copy(idx_hbm.at[pl.ds(slot * n + off, chunk)], idx_v)
                pltpu.async_copy(rows_v, out_hbm.at[idx_v], sem).wait()

    return scatter(src, idx)


def _pack_bf16_pairs(x):
    half = x.shape[-1] // 2
    lo = pltpu.bitcast(x[:, :half].astype(BF16).astype(F32), U32) >> 16
    hi = pltpu.bitcast(x[:, half:].astype(BF16).astype(F32), U32) & jnp.uint32(0xFFFF0000)
    return lo | hi


def _unpack_bf16_pairs(xp):
    return pltpu.bitcast(xp << 16, F32), pltpu.bitcast(xp & jnp.uint32(0xFFFF0000), F32)


def _expert_body(be_ref, valid_ref, nu_ref, xs_ref, w1_ref, w3_ref, w2_ref, ys_ref):
    del be_ref
    step = pl.program_id(0)
    live = step < nu_ref[0]

    @pl.when(jnp.logical_not(live))
    def _():
        ys_ref[...] = jnp.zeros_like(ys_ref)

    @pl.when(live)
    def _():
        row = lax.broadcasted_iota(I32, xs_ref.shape, 0)
        xp = jnp.where(row < valid_ref[step], xs_ref[...], jnp.uint32(0))
        half = xp.shape[-1]
        lo, hi = (part.astype(BF16) for part in _unpack_bf16_pairs(xp))

        def up(w_ref):
            return (jnp.dot(lo, w_ref[0, :half, :].astype(BF16), preferred_element_type=F32)
                    + jnp.dot(hi, w_ref[0, half:, :].astype(BF16), preferred_element_type=F32))

        act = (_silu(up(w1_ref)) * up(w3_ref)).astype(BF16)
        ys_ref[...] = _pack_bf16_pairs(jnp.dot(act, w2_ref[0].astype(BF16), preferred_element_type=F32))


def _experts(blk_eid, blk_valid, n_used, xs, w1, w3, w2, *, rb):
    p, half = xs.shape
    d = 2 * half
    nb = p // rb
    used = lambda i, be, bv, nu: jnp.maximum(jnp.minimum(i, nu[0] - 1), 0)
    wmap = lambda i, be, bv, nu: (be[used(i, be, bv, nu)], 0, 0)
    grid_spec = pltpu.PrefetchScalarGridSpec(
        num_scalar_prefetch=3,
        grid=(nb,),
        in_specs=[pl.BlockSpec((rb, half), lambda i, be, bv, nu: (used(i, be, bv, nu), 0)),
                  pl.BlockSpec((1, d, D_EXPERT), wmap), pl.BlockSpec((1, d, D_EXPERT), wmap),
                  pl.BlockSpec((1, D_EXPERT, d), wmap)],
        out_specs=pl.BlockSpec((rb, half), lambda i, be, bv, nu: (i, 0)),
    )
    return pl.pallas_call(
        _expert_body,
        grid_spec=grid_spec,
        out_shape=jax.ShapeDtypeStruct((p, half), U32),
        compiler_params=_params("arbitrary"),
        name="moe_experts",
    )(blk_eid, blk_valid, n_used, xs, w1, w3, w2)


def _sc_row_gather(table, idx, *, chunk=128):
    rows, d = idx.shape[0], table.shape[1]
    per_worker = rows // (SC_CORES * SC_SUBCORES)
    chunk = min(chunk, per_worker)
    assert rows % (SC_CORES * SC_SUBCORES) == 0 and per_worker % chunk == 0

    @functools.partial(
        pl.kernel, mesh=_sc_mesh(), out_type=jax.ShapeDtypeStruct((rows, d), table.dtype),
        scratch_types=[pltpu.VMEM((chunk,), I32), pltpu.VMEM((chunk, d), table.dtype), pltpu.SemaphoreType.DMA])
    def gather(table_hbm, idx_hbm, out_hbm, idx_v, rows_v, sem):
        base = _sc_worker_base(per_worker)

        @pl.loop(0, per_worker // chunk)
        def _(step):
            off = base + step * chunk
            pltpu.sync_copy(idx_hbm.at[pl.ds(off, chunk)], idx_v)
            pltpu.async_copy(table_hbm.at[idx_v], rows_v, sem).wait()
            pltpu.sync_copy(rows_v, out_hbm.at[pl.ds(off, chunk)])

    return gather(table, idx)


def _combine_rows(r0_ref, r1_ref, x_ref, route_ref):
    route = route_ref[...]
    half = r0_ref.shape[-1]
    lo0, hi0 = _unpack_bf16_pairs(r0_ref[...])
    lo1, hi1 = _unpack_bf16_pairs(r1_ref[...])
    g0, g1 = route[:, R_GATE0:R_GATE0 + 1], route[:, R_GATE1:R_GATE1 + 1]
    return jnp.concatenate([x_ref[:, :half] + g0 * lo0 + g1 * lo1,
                            x_ref[:, half:] + g0 * hi0 + g1 * hi1], axis=1)


def _final_body(r0_ref, r1_ref, x_ref, route_ref, g_ref, o_ref):
    out = _combine_rows(r0_ref, r1_ref, x_ref, route_ref)
    o_ref[...] = out * lax.rsqrt(jnp.mean(out * out, axis=-1, keepdims=True) + EPS) * g_ref[...]


def _final_combine(rows2, xn, route, g, *, tm):
    n, d = xn.shape
    nt = n // tm
    row = lambda i: (i, 0)
    return pl.pallas_call(
        _final_body,
        grid=(nt,),
        in_specs=[pl.BlockSpec((tm, d // 2), row), pl.BlockSpec((tm, d // 2), lambda i: (i + nt, 0)),
                  pl.BlockSpec((tm, d), row), pl.BlockSpec((tm, LANES), row),
                  pl.BlockSpec((1, d), lambda i: (0, 0))],
        out_specs=pl.BlockSpec((tm, d), row),
        out_shape=jax.ShapeDtypeStruct((n, d), F32),
        compiler_params=_params("parallel"),
        name="moe_combine",
    )(rows2, rows2, xn, route, g)


def _moe_rows(hp, meta, counts, w1, w3, w2, *, layer, rb):
    n = hp.shape[0]
    cnt = counts[:, 0].astype(I32)
    nblk = (cnt + rb - 1) // rb
    bend = jnp.cumsum(nblk)
    pstart = (bend - nblk) * rb
    nb = (2 * n) // rb + N_EXPERTS
    n_used = bend[-1:].astype(I32)
    blk = jnp.arange(nb, dtype=I32)
    blk_eid = jnp.minimum(jnp.sum(bend[None, :] <= blk[:, None], axis=1), N_EXPERTS - 1).astype(I32)
    experts = jnp.arange(N_EXPERTS, dtype=I32)
    mine = blk_eid[:, None] == experts[None, :]
    blk_valid = jnp.clip(jnp.sum(jnp.where(mine, (cnt + pstart)[None, :], 0), axis=1) - blk * rb, 0, rb)
    eid = meta[R_EID0:R_EID1 + 1]
    first = jnp.sum(jnp.where(eid[None] == experts[:, None, None], pstart[:, None, None], 0), axis=0)
    dest = (first + meta[R_RANK0:R_RANK1 + 1]).reshape(-1)

    xs = _sc_row_scatter(hp, dest, nb * rb)
    ys = _experts(blk_eid + layer * N_EXPERTS, blk_valid.astype(I32), n_used, xs, w1, w3, w2, rb=rb)
    return _sc_row_gather(ys, dest)


def _lane_row(pairs):
    row = jnp.zeros((LANES,), F32)
    for off, vec in pairs:
        row = row.at[off:off + vec.shape[0]].set(vec.astype(F32))
    return row


def kernel(x, norm_mix, w_in, conv_qkv, dn_a_log, dn_dt_bias, dn_norm, fox_bias, conv_dw, conv_dw_b,
           conv_ln_g, conv_ln_b, w_a, w_b, w_c, w_out, norm_ffn, router_group_w, router_group_b,
           router_expert_w, router_expert_b, expert_w1, expert_w3, expert_w2, norm_final):
    b, t, d = x.shape
    n = b * t
    depth = w_in.shape[0]
    qk_dn = H_DN * DK_DN
    in_sizes = (qk_dn, qk_dn, H_DN * DV_DN, H_DN * DV_DN, H_DN, H_DN,
                H_FOX * D_FOX, H_FOX * D_FOX, H_FOX * D_FOX, H_FOX, 2 * C_CONV, d, d, d)
    splits = np.cumsum(in_sizes)[:-1].tolist()
    tm_final, rb = 512, 512

    x_src = x.reshape(n, d)
    w1_all = expert_w1.reshape(depth * N_EXPERTS, d, D_EXPERT)
    w3_all = expert_w3.reshape(depth * N_EXPERTS, d, D_EXPERT)
    w2_all = expert_w2.reshape(depth * N_EXPERTS, D_EXPERT, d)
    for l in range(depth):
        (qa, ka, va, za, ba, aa, qb, kb, vb, fb, uc, ga, gb, gc) = jnp.split(w_in[l], splits, axis=1)
        w_main = jnp.concatenate([qa, ka, va, za, qb * (D_FOX ** -0.5 * LOG2E), kb, vb, uc, ga, gb, gc],
                                 axis=1).astype(BF16)
        w_small = jnp.concatenate([ba, aa, fb, jnp.zeros((d, LANES - 3 * H_DN), F32)], axis=1).astype(BF16)
        vec = jnp.stack([conv_dw_b[l], conv_ln_g[l], conv_ln_b[l]] + [jnp.zeros((C_CONV,), F32)] * 5)
        outs = _in_proj(x_src, norm_mix[l][None, :], w_main, w_small, conv_dw[l], vec, seq_len=t)
        x2 = outs[0] if l > 0 else x_src
        dn_qkv, za_p, fox_qkv, zc, gates, small = outs[-6:]

        par = jnp.stack([_lane_row([(LANE_G, -jnp.exp(dn_a_log[l]))]),
                         _lane_row([(LANE_G, dn_dt_bias[l]), (LANE_F, fox_bias[l])]),
                         _lane_row([(0, dn_norm[l])])] + [jnp.zeros((LANES,), F32)] * 5)
        oa, qaug, kaug = _delta_rule(dn_qkv.reshape(b, t, -1), za_p.reshape(b, t, -1),
                                     small.reshape(b, t, LANES), conv_qkv[l], par)
        ob = _fox_attention(fox_qkv.reshape(b, t, -1), qaug, kaug)

        w_r = jnp.concatenate([router_group_w[l], jnp.zeros((d, LANE_EXP - N_GROUPS), F32),
                               router_expert_w[l], jnp.zeros((d, LANES - LANE_EXP - N_EXPERTS), F32)], axis=1)
        b_r = _lane_row([(LANE_GRP, router_group_b[l]), (LANE_EXP, router_expert_b[l])])[None, :]
        xn, hp, route, meta, counts = _merge(
            oa.reshape(n, -1), ob.reshape(n, -1), zc.reshape(n, -1), gates, x2,
            w_a[l].astype(BF16), w_b[l].astype(BF16), w_c[l].astype(BF16), w_out[l].astype(BF16),
            norm_ffn[l][None, :], w_r.T.astype(BF16), b_r.reshape(LANES, 1))

        x_src = (_moe_rows(hp, meta, counts, w1_all, w3_all, w2_all, layer=l, rb=rb), xn, route)
    return _final_combine(*x_src, norm_final[None, :], tm=tm_final).reshape(b, t, d)
```

```python
import functools

import jax
import jax.numpy as jnp
from jax import lax
from jax.experimental import pallas as pl
from jax.experimental.pallas import tpu as pltpu
from jax.experimental.pallas import tpu_sc as plsc

F32 = jnp.float32
BF16 = jnp.bfloat16
U32 = jnp.uint32
I32 = jnp.int32

EPS = 1e-6
LOG2E = 1.4426950408889634
LANES = 128
SUBLANES = 8
SC_CORES, SC_SUBCORES = 2, 16
H_DN, DK_DN, DV_DN = 4, 128, 128
SHORT_CONV = 4
CHUNK = 64
H_FOX, D_FOX = 4, 128
C_CONV = 512
CONV_WIDTH = 31
N_GROUPS, EXPERTS_PER_GROUP = 4, 8
N_EXPERTS = N_GROUPS * EXPERTS_PER_GROUP
D_EXPERT = 256

W_DN = 3 * H_DN * DK_DN
W_FOX = 3 * H_FOX * D_FOX
VMEM_LIMIT = 56 * 1024 * 1024

LANE_BETA, LANE_G, LANE_F = 0, 4, 8
LANE_GRP, LANE_EXP = 0, 32
R_GATE0, R_GATE1, R_EID0, R_EID1, R_RANK0, R_RANK1 = 0, 1, 2, 3, 4, 5

NN = (((1,), (0,)), ((), ()))
NT = (((1,), (1,)), ((), ()))
TN = (((0,), (0,)), ((), ()))


def _mm(a, b, dims=NN):
    return lax.dot_general(a.astype(BF16), b.astype(BF16), dims, preferred_element_type=F32)


def _sigmoid(x):
    return 0.5 * jnp.tanh(0.5 * x) + 0.5


def _silu(x):
    half = 0.5 * x
    return half * jnp.tanh(half) + half


def _params(*sem):
    return pltpu.CompilerParams(dimension_semantics=sem, vmem_limit_bytes=VMEM_LIMIT)


def _in_proj_body(*refs, col_chunk, combine, tiles_per_seq, hrows, rchunk):
    if combine:
        r0_ref, r1_ref, xn_ref, route_ref, g_ref, w_ref, ws_ref, dw_ref, vec_ref, x_out_ref = refs[:10]
        x = _combine_rows(r0_ref, r1_ref, xn_ref, route_ref)
        x_out_ref[...] = x
        rest = refs[10:]
    else:
        x_ref, g_ref, w_ref, ws_ref, dw_ref, vec_ref = refs[:6]
        x = x_ref[...]
        rest = refs[6:]
    dn_ref, za_ref, fox_ref, zc_ref, gates_ref, small_ref, z_ref, zs_ref, y_ref = rest
    tm = x.shape[0]
    h = x * lax.rsqrt(jnp.mean(x * x, axis=-1, keepdims=True) + EPS) * g_ref[...]
    hb = h.astype(BF16)

    def project(col, width, hold=None):
        lhs = hb
        if hold is not None:
            zero = jnp.minimum(jnp.abs(hold), 0.0).astype(BF16)
            head = jnp.concatenate([hb[0:rchunk, 0:LANES] + zero, hb[0:rchunk, LANES:]], axis=1)
            lhs = jnp.concatenate([head, hb[rchunk:]], axis=0)
        return jnp.dot(lhs, w_ref[:, col:col + width], preferred_element_type=F32)

    starts, col = {}, 0
    for name, width in (("dn", dn_ref.shape[-1]), ("za", za_ref.shape[-1]), ("fox", fox_ref.shape[-1]),
                        ("conv", 2 * C_CONV), ("gates", gates_ref.shape[-1])):
        starts[name], col = col, col + width

    first = pl.program_id(0) % tiles_per_seq == 0
    col = starts["conv"]

    @pl.when(first)
    def _():
        z_ref[0:hrows, :] = jnp.zeros((hrows, C_CONV), F32)

    @pl.when(jnp.logical_not(first))
    def _():
        z_ref[0:hrows, :] = z_ref[tm:tm + hrows, :]

    z_ref[hrows:hrows + tm, :] = project(col, C_CONV) * _sigmoid(project(col + C_CONV, C_CONV))
    sub = SUBLANES
    span = tm + hrows - sub
    for phase in range(1, sub):
        zs_ref[phase - 1, 0:span, :] = z_ref[phase:phase + span, :]

    def conv_unit(lg, r0, after):
        lanes = slice(lg * LANES, (lg + 1) * LANES)
        acc = vec_ref[0:1, lanes] + jnp.minimum(jnp.abs(after[0:rchunk, 0:LANES]), 0.0)
        for k in range(CONV_WIDTH):
            off = hrows - (CONV_WIDTH - 1) + k
            phase, base = off % sub, off - off % sub + r0
            if phase == 0:
                tap = z_ref[base:base + rchunk, lanes]
            else:
                tap = zs_ref[phase - 1, base:base + rchunk, lanes]
            acc = acc + dw_ref[k:k + 1, lanes] * tap
        y_ref[r0:r0 + rchunk, lanes] = acc
        return acc

    units = [(lg, r0) for lg in range(C_CONV // LANES) for r0 in range(0, tm, rchunk)]
    chunks = [(ref, starts[name], c) for name, ref in (("dn", dn_ref), ("za", za_ref), ("fox", fox_ref),
                                                       ("gates", gates_ref))
              for c in range(0, ref.shape[-1], col_chunk)]
    share = [[] for _ in chunks]
    for u, unit in enumerate(units):
        share[u * len(chunks) // len(units)].append(unit)
    last = []
    for idx, ((ref, start, c), mine) in enumerate(zip(chunks, share)):
        res = project(start + c, col_chunk, hold=last[idx - 2] if idx >= 2 else None)
        ref[:, c:c + col_chunk] = res.astype(ref.dtype)
        tile = None
        for lg, r0 in mine:
            tile = conv_unit(lg, r0, res)
        last.append(tile)
    small_ref[...] = jnp.dot(hb, ws_ref[...], preferred_element_type=F32)

    y = y_ref[...]
    mu = jnp.mean(y, axis=-1, keepdims=True)
    yc = y - mu
    var = jnp.mean(yc * yc, axis=-1, keepdims=True)
    yn = yc * lax.rsqrt(var + EPS) * vec_ref[1:2, :] + vec_ref[2:3, :]
    zc_ref[...] = _silu(yn).astype(zc_ref.dtype)


def _in_proj(x_src, g, w, ws, dw, vec, *, seq_len, tm=256, col_chunk=512, hrows=32, rchunk=32):
    combine = isinstance(x_src, tuple)
    n, d = (x_src[1] if combine else x_src).shape
    nt = n // tm
    widths = (W_DN, H_DN * DV_DN, W_FOX, C_CONV, 3 * d)
    assert w.shape == (d, sum(widths) + C_CONV) and n % tm == 0 and seq_len % tm == 0
    assert hrows >= CONV_WIDTH - 1 and hrows % SUBLANES == 0 and tm % rchunk == 0
    row = lambda i: (i, 0)
    fixed = lambda i: (0, 0)
    out_shape = [jax.ShapeDtypeStruct((n, wd), BF16) for wd in widths] + [jax.ShapeDtypeStruct((n, LANES), F32)]
    out_specs = [pl.BlockSpec((tm, wd), row) for wd in widths] + [pl.BlockSpec((tm, LANES), row)]
    if combine:
        rows2, xn, route = x_src
        args = (rows2, rows2, xn, route)
        in_specs = [pl.BlockSpec((tm, d // 2), row), pl.BlockSpec((tm, d // 2), lambda i: (i + nt, 0)),
                    pl.BlockSpec((tm, d), row), pl.BlockSpec((tm, LANES), row)]
        out_shape = [jax.ShapeDtypeStruct((n, d), F32)] + out_shape
        out_specs = [pl.BlockSpec((tm, d), row)] + out_specs
    else:
        args = (x_src,)
        in_specs = [pl.BlockSpec((tm, d), row)]
    return pl.pallas_call(
        functools.partial(_in_proj_body, col_chunk=col_chunk, combine=combine, tiles_per_seq=seq_len // tm,
                          hrows=hrows, rchunk=rchunk),
        grid=(nt,),
        in_specs=in_specs + [pl.BlockSpec((1, d), fixed),
                             pl.BlockSpec(w.shape, fixed, pipeline_mode=pl.Buffered(1)),
                             pl.BlockSpec(ws.shape, fixed, pipeline_mode=pl.Buffered(1)),
                             pl.BlockSpec(dw.shape, fixed), pl.BlockSpec(vec.shape, fixed)],
        out_specs=out_specs,
        out_shape=out_shape,
        scratch_shapes=[pltpu.VMEM((tm + hrows, C_CONV), F32),
                        pltpu.VMEM((SUBLANES - 1, tm + hrows, C_CONV), F32),
                        pltpu.VMEM((tm, C_CONV), F32)],
        compiler_params=_params("arbitrary"),
        name="in_proj",
    )(*args, g, w, ws, dw, vec)


def _softplus_parts(z):
    t = jnp.log1p(jnp.exp(-jnp.abs(z)))
    return jnp.maximum(z, 0.0) + t, -(jnp.maximum(-z, 0.0) + t)


def _delta_body(qkv_ref, za_ref, sm_ref, cw_ref, par_ref, oa_ref, qaug_ref, kaug_ref,
                xs_ref, s_ref, carry_ref, *, ts):
    j = pl.program_id(1)
    halo = SUBLANES
    pack = 2 * SUBLANES

    @pl.when(j == 0)
    def _():
        xs_ref[0:halo, :] = jnp.zeros((halo, W_DN), F32)
        s_ref[...] = jnp.zeros_like(s_ref)
        carry_ref[...] = jnp.zeros_like(carry_ref)

    @pl.when(j > 0)
    def _():
        xs_ref[0:halo, :] = xs_ref[2 * halo:3 * halo, :]

    xb = qkv_ref[0]
    xs_ref[halo:2 * halo, :] = qkv_ref[0, 0:pack, :].astype(F32)[0:halo]
    xs_ref[2 * halo:3 * halo, :] = qkv_ref[0, ts - pack:ts, :].astype(F32)[pack - halo:pack]

    lag = lax.broadcasted_iota(I32, (ts, ts), 0) - lax.broadcasted_iota(I32, (ts, ts), 1)
    shifts = jnp.concatenate([jnp.where(lag == s, 1.0, 0.0).astype(BF16) for s in range(1, SHORT_CONV)], axis=0)
    stacked = jnp.dot(shifts, xb, preferred_element_type=F32)
    shifted = [stacked[(s - 1) * ts:s * ts] for s in range(1, SHORT_CONV)]
    head_row = lax.broadcasted_iota(I32, (halo, LANES), 0)

    def conv_silu(lane0):
        lanes = slice(lane0, lane0 + LANES)
        acc = cw_ref[SHORT_CONV - 1:SHORT_CONV, lanes] * xb[:, lanes].astype(F32)
        for s in range(1, SHORT_CONV):
            acc = acc + cw_ref[SHORT_CONV - 1 - s:SHORT_CONV - s, lanes] * shifted[s - 1][:, lanes]
        head = acc[0:halo]
        for s in range(1, SHORT_CONV):
            prev = jnp.where(head_row < s, xs_ref[halo - s:2 * halo - s, lanes], 0.0)
            head = head + cw_ref[SHORT_CONV - 1 - s:SHORT_CONV - s, lanes] * prev
        return _silu(jnp.concatenate([head, acc[halo:]], axis=0))

    def l2n(a):
        return a * lax.rsqrt(jnp.sum(a * a, axis=-1, keepdims=True) + EPS)

    sm = sm_ref[0]
    lane = lax.broadcasted_iota(I32, sm.shape, 1)
    sp, logsig = _softplus_parts(sm + par_ref[1:2, :])
    vals = jnp.where(lane < LANE_G, _sigmoid(sm),
                     jnp.where(lane < LANE_F, par_ref[0:1, :] * sp,
                               jnp.where(lane < LANE_F + H_FOX, logsig, 0.0)))
    row = lax.broadcasted_iota(I32, (ts, ts), 0)
    colm = lax.broadcasted_iota(I32, (ts, ts), 1)
    log_chunk = CHUNK.bit_length() - 1
    causal = (row >= colm) & ((row >> log_chunk) == (colm >> log_chunk))

    hi = vals.astype(BF16)
    rem = vals - hi.astype(F32)
    mid = rem.astype(BF16)
    lo = (rem - mid.astype(F32)).astype(BF16)
    pieces = jnp.concatenate([hi, mid, lo], axis=-1)

    def cumsum(mask):
        y = jnp.dot(jnp.where(mask, 1.0, 0.0).astype(BF16), pieces, preferred_element_type=F32)
        return (y[:, :LANES] + y[:, LANES:2 * LANES]) + y[:, 2 * LANES:]

    ccum = cumsum(row >= colm) + carry_ref[...]
    gcum = cumsum(causal)
    carry_ref[...] = ccum[ts - 1:ts, :]
    gcum_t = gcum.T

    cl = ccum * LOG2E
    c_hi = cl.astype(BF16)
    c_rem = cl - c_hi.astype(F32)
    c_mid = c_rem.astype(BF16)
    c_lo = (c_rem - c_mid.astype(F32)).astype(BF16)
    c_pieces = jnp.concatenate([c_hi, c_mid, c_lo], axis=-1)
    pr = lax.broadcasted_iota(I32, (3 * LANES, LANES), 0)
    pc = lax.broadcasted_iota(I32, (3 * LANES, LANES), 1)
    src_lane, piece = pr & (LANES - 1), pr >> (LANES.bit_length() - 1)
    owned = (src_lane >= LANE_F) & (src_lane < LANE_F + H_FOX) & ((pc >> 3) == src_lane - LANE_F)
    place_q = jnp.where(owned & ((pc & 7) == piece), 1.0, 0.0).astype(BF16)
    place_k = jnp.where(owned & ((pc & 7) == piece + 3), -1.0, 0.0).astype(BF16)
    slot = lane & 7
    in_heads = lane < 8 * H_FOX
    ones_q = jnp.where(in_heads & (slot >= 3) & (slot < 6), 1.0, 0.0)
    ones_k = jnp.where(in_heads & (slot < 3), 1.0, 0.0)
    qaug_ref[0] = (jnp.dot(c_pieces, place_q, preferred_element_type=F32) + ones_q).astype(BF16)
    kaug_ref[0] = (jnp.dot(c_pieces, place_k, preferred_element_type=F32) + ones_k).astype(BF16)

    scale = DK_DN ** -0.5
    dn_norm = par_ref[2:3, :]

    pw = 2 * CHUNK
    prow = lax.broadcasted_iota(I32, (pw, pw), 0)
    pcol = lax.broadcasted_iota(I32, (pw, pw), 1)
    same = (prow >> log_chunk) == (pcol >> log_chunk)
    causal_p = (prow >= pcol) & same
    strict_p = (prow > pcol) & same
    levels = []
    s = 1
    while s < CHUNK:
        levels.append(((prow >> s.bit_length()) == (pcol >> s.bit_length()))
                      & ((prow & s) != 0) & ((pcol & s) == 0))
        s *= 2

    heads = range(H_DN)
    pairs = range(ts // pw)
    q, k, v = [], [], []
    for h in heads:
        q.append(l2n(conv_silu(h * DK_DN)))
        k.append(l2n(conv_silu(H_DN * DK_DN + h * DK_DN)))
        v.append(conv_silu(2 * H_DN * DK_DN + h * DV_DN))

    ctx = []
    for h in heads:
        for p in pairs:
            pr = slice(p * pw, (p + 1) * pw)
            gcol = gcum[pr, LANE_G + h:LANE_G + h + 1]
            grow = gcum_t[LANE_G + h:LANE_G + h + 1, pr]
            beta = vals[pr, LANE_BETA + h:LANE_BETA + h + 1]
            decay = jnp.where(causal_p, jnp.exp(jnp.where(causal_p, gcol - grow, 0.0)), 0.0)
            egc = jnp.exp(gcol)
            kp = k[h][pr]
            kb = kp * beta
            ctx.append(dict(
                h=h, p=p, gcol=gcol, kp=kp,
                a=jnp.where(strict_p, _mm(kb, kp, NT) * decay, 0.0),
                qk=jnp.where(causal_p, _mm(q[h][pr] * scale, kp, NT) * decay, 0.0),
                rhs=jnp.concatenate([v[h][pr] * beta, kb * egc], axis=-1),
                qg=q[h][pr] * (scale * egc)))

    for c in ctx:
        c["n"] = -jnp.where(levels[0], c["a"], 0.0)
    for level in levels[1:]:
        for c in ctx:
            m = jnp.where(level, c["a"], 0.0)
            c["y"] = m + _mm(c["n"], m)
        for c in ctx:
            c["n"] = c["n"] - (c["y"] + _mm(c["y"], c["n"]))
    for c in ctx:
        c["sol"] = c["rhs"] + _mm(c["n"], c["rhs"])

    state = [s_ref[h] for h in heads]
    for p in pairs:
        group = [c for c in ctx if c["p"] == p]
        v_prev = [None] * H_DN
        for ch in range(2):
            rows = slice(ch * CHUNK, (ch + 1) * CHUNK)
            out_rows = slice(p * pw + ch * CHUNK, p * pw + (ch + 1) * CHUNK)
            on_state = [_mm(jnp.concatenate([c["sol"][rows, DV_DN:], c["qg"][rows]], axis=0), state[c["h"]])
                        for c in group]
            for c, ws_qs in zip(group, on_state):
                h = c["h"]
                vn = c["sol"][rows, :DV_DN] - ws_qs[:CHUNK]
                v_pair = jnp.concatenate([vn, jnp.zeros_like(vn)] if ch == 0 else [v_prev[h], vn], axis=0)
                o = ws_qs[CHUNK:] + _mm(c["qk"][rows], v_pair)
                glast = c["gcol"][(ch + 1) * CHUNK - 1:(ch + 1) * CHUNK]
                state[h] = (state[h] * jnp.exp(glast)
                            + _mm(c["kp"][rows] * jnp.exp(glast - c["gcol"][rows]), vn, TN))
                v_prev[h] = vn
                o = o * lax.rsqrt(jnp.mean(o * o, axis=-1, keepdims=True) + EPS) * dn_norm
                za = za_ref[0, out_rows, h * DV_DN:(h + 1) * DV_DN].astype(F32)
                oa_ref[0, out_rows, h * DV_DN:(h + 1) * DV_DN] = (o * _silu(za)).astype(oa_ref.dtype)
    for h in heads:
        s_ref[h] = state[h]


def _delta_rule(qkv, za, small, conv_w, par, *, ts=256):
    b, t, _ = qkv.shape
    assert t % ts == 0 and ts % (2 * CHUNK) == 0
    blk = lambda width: pl.BlockSpec((1, ts, width), lambda bi, j: (bi, j, 0))
    fixed = lambda bi, j: (0, 0)
    return pl.pallas_call(
        functools.partial(_delta_body, ts=ts),
        grid=(b, t // ts),
        in_specs=[blk(W_DN), blk(H_DN * DV_DN), blk(LANES),
                  pl.BlockSpec(conv_w.shape, fixed), pl.BlockSpec(par.shape, fixed)],
        out_specs=[blk(H_DN * DV_DN), blk(LANES), blk(LANES)],
        out_shape=[jax.ShapeDtypeStruct((b, t, H_DN * DV_DN), BF16),
                   jax.ShapeDtypeStruct((b, t, LANES), BF16),
                   jax.ShapeDtypeStruct((b, t, LANES), BF16)],
        scratch_shapes=[pltpu.VMEM((3 * SUBLANES, W_DN), F32),
                        pltpu.VMEM((H_DN, DK_DN, DV_DN), F32),
                        pltpu.VMEM((1, LANES), F32)],
        compiler_params=_params("parallel", "arbitrary"),
        name="delta_rule",
    )(qkv, za, small, conv_w, par)


def _fox_body(q_ref, k_ref, v_ref, qa_ref, ka_ref, o_ref, m_ref, acc_ref, *, tq, tk):
    i = pl.program_id(1)
    m_ref[...] = jnp.full(m_ref.shape, -jnp.inf, F32)
    acc_ref[...] = jnp.zeros_like(acc_ref)
    head_lanes = [slice(h * D_FOX, (h + 1) * D_FOX) for h in range(H_FOX)]
    lane = lax.broadcasted_iota(I32, (tk, LANES), 1)
    own = [jnp.where((lane >> 3) == h, 1.0, 0.0).astype(BF16) for h in range(H_FOX)]
    ones = jnp.ones((tk, D_FOX), BF16)
    keep = lax.broadcasted_iota(I32, (tq, tk), 1) <= lax.broadcasted_iota(I32, (tq, tk), 0)

    def block(start, diag_offset):
        rows = slice(0 if diag_offset is None else diag_offset, tq)
        ka = ka_ref[0, pl.ds(start, tk), :]
        for h, lanes in enumerate(head_lanes):
            q_aug = jnp.concatenate([q_ref[0, rows, lanes], qa_ref[0, rows, :]], axis=1)
            k_aug = jnp.concatenate([k_ref[0, pl.ds(start, tk), lanes], ka * own[h]], axis=1)
            s = lax.dot_general(q_aug, k_aug, NT, preferred_element_type=F32)
            if diag_offset is not None:
                s = jnp.where(keep[:tq - diag_offset], s, -jnp.inf)
            m_prev = m_ref[h, rows]
            m_next = jnp.maximum(m_prev, jnp.max(s, axis=-1, keepdims=True))
            p = jnp.exp2(s - jnp.concatenate([m_next] * (tk // LANES), axis=1))
            alpha = jnp.exp2(m_prev - m_next)
            v_aug = jnp.concatenate([v_ref[0, pl.ds(start, tk), lanes], ones], axis=1)
            acc_ref[h, rows] = (jnp.concatenate([alpha, alpha], axis=1) * acc_ref[h, rows]
                                + jnp.dot(p.astype(BF16), v_aug, preferred_element_type=F32))
            m_ref[h, rows] = m_next

    def full_block(jb, carry):
        block(pl.multiple_of(jb * tk, tk), None)
        return carry

    lax.fori_loop(0, i * (tq // tk), full_block, 0)
    for d in range(tq // tk):
        block(pl.multiple_of(i * tq + d * tk, tk), d * tk)
    for h, lanes in enumerate(head_lanes):
        acc = acc_ref[h]
        o_ref[0, :, lanes] = (acc[:, :D_FOX] / acc[:, D_FOX:]).astype(o_ref.dtype)


def _fox_attention(qkv, qaug, kaug, *, tq=1024, tk=512):
    b, t, _ = qkv.shape
    hd = H_FOX * D_FOX
    assert t % tq == 0 and tq % tk == 0 and tk % LANES == 0
    return pl.pallas_call(
        functools.partial(_fox_body, tq=tq, tk=tk),
        grid=(b, t // tq),
        in_specs=[pl.BlockSpec((1, tq, hd), lambda bi, i: (bi, i, 0)),
                  pl.BlockSpec((1, t, hd), lambda bi, i: (bi, 0, 1)),
                  pl.BlockSpec((1, t, hd), lambda bi, i: (bi, 0, 2)),
                  pl.BlockSpec((1, tq, LANES), lambda bi, i: (bi, i, 0)),
                  pl.BlockSpec((1, t, LANES), lambda bi, i: (bi, 0, 0))],
        out_specs=pl.BlockSpec((1, tq, hd), lambda bi, i: (bi, i, 0)),
        out_shape=jax.ShapeDtypeStruct((b, t, hd), BF16),
        scratch_shapes=[pltpu.VMEM((H_FOX, tq, LANES), F32), pltpu.VMEM((H_FOX, tq, 2 * D_FOX), F32)],
        compiler_params=_params("parallel", "arbitrary"),
        name="fox_attention",
    )(qkv, qkv, qkv, qaug, kaug)


def _merge_body(oa_ref, ob_ref, zc_ref, gates_ref, x_ref, wa_ref, wb_ref, wc_ref, wo_ref, g_ref,
                wr_ref, br_ref, xn_ref, hp_ref, route_ref, meta_ref, cnt_ref, run_ref, *, tm):
    step = pl.program_id(0)
    d = x_ref.shape[-1]

    @pl.when(step == 0)
    def _():
        run_ref[...] = jnp.zeros_like(run_ref)

    merged = None
    for idx, (m_ref, w_ref) in enumerate(((oa_ref, wa_ref), (ob_ref, wb_ref), (zc_ref, wc_ref))):
        y = jnp.dot(m_ref[...], w_ref[...], preferred_element_type=F32)
        term = _sigmoid(gates_ref[:, idx * d:(idx + 1) * d].astype(F32)) * y
        merged = term if merged is None else merged + term
    xn = x_ref[...] + jnp.dot(merged.astype(BF16), wo_ref[...], preferred_element_type=F32)
    xn_ref[...] = xn
    h2 = xn * lax.rsqrt(jnp.mean(xn * xn, axis=-1, keepdims=True) + EPS) * g_ref[...]

    hp_ref[...] = _pack_bf16_pairs(h2)

    logits = lax.dot_general(wr_ref[...], h2.astype(BF16), NT, preferred_element_type=F32) + br_ref[...]
    big = jnp.int32(LANES)
    gl = logits[LANE_GRP:LANE_GRP + N_GROUPS]
    g_row = lax.broadcasted_iota(I32, gl.shape, 0)
    gmax = jnp.max(gl, axis=0, keepdims=True)
    grp = jnp.min(jnp.where(gl == gmax, g_row, big), axis=0, keepdims=True)
    p_grp = 1.0 / jnp.sum(jnp.exp(gl - gmax), axis=0, keepdims=True)
    el_all = logits[LANE_EXP:LANE_EXP + N_EXPERTS]
    e_row = lax.broadcasted_iota(I32, el_all.shape, 0)
    el = jnp.where((e_row >> (EXPERTS_PER_GROUP.bit_length() - 1)) == grp, el_all, -jnp.inf)
    v0 = jnp.max(el, axis=0, keepdims=True)
    i0 = jnp.min(jnp.where(el == v0, e_row, big), axis=0, keepdims=True)
    el1 = jnp.where(e_row == i0, -jnp.inf, el)
    v1 = jnp.max(el1, axis=0, keepdims=True)
    i1 = jnp.min(jnp.where(el1 == v1, e_row, big), axis=0, keepdims=True)
    e1 = jnp.exp(v1 - v0)
    gate0 = p_grp / (1.0 + e1)
    gate1 = p_grp * e1 / (1.0 + e1)

    hot0 = e_row == i0
    hot1 = e_row == i1
    onehot = jnp.where(hot0 | hot1, 1.0, 0.0)
    tiles = [onehot[:, t * LANES:(t + 1) * LANES] for t in range(tm // LANES)]
    ri = lax.broadcasted_iota(I32, (LANES, LANES), 0)
    cj = lax.broadcasted_iota(I32, (LANES, LANES), 1)
    in_tile = _mm(jnp.concatenate(tiles, axis=0), jnp.where(ri < cj, 1.0, 0.0))
    before = run_ref[...]
    prefix = []
    for t, tile in enumerate(tiles):
        prefix.append(in_tile[t * N_EXPERTS:(t + 1) * N_EXPERTS] + before)
        before = before + jnp.sum(tile, axis=1, keepdims=True)
    prefix = jnp.concatenate(prefix, axis=1)
    run_ref[...] = before
    cnt_ref[...] = jnp.broadcast_to(before, cnt_ref.shape)
    rank0 = jnp.sum(jnp.where(hot0, prefix, 0.0), axis=0, keepdims=True)
    rank1 = jnp.sum(jnp.where(hot1, prefix, 0.0), axis=0, keepdims=True)

    rows = {R_GATE0: gate0, R_GATE1: gate1, R_EID0: i0.astype(F32), R_EID1: i1.astype(F32),
            R_RANK0: rank0, R_RANK1: rank1}
    fields = jnp.concatenate([rows.get(r, jnp.zeros_like(gate0)) for r in range(SUBLANES)], axis=0)
    meta_ref[...] = fields.astype(I32)
    route_ref[...] = jnp.concatenate([fields, jnp.zeros((LANES - SUBLANES, tm), F32)], axis=0).T


def _merge(oa, ob, zc, gates, x2, wa, wb, wc, wo, g, wr, br, *, tm=1024):
    n, d = x2.shape
    assert n % tm == 0
    row = lambda i: (i, 0)
    fixed = lambda i: (0, 0)
    full = lambda a: pl.BlockSpec(a.shape, fixed)
    return pl.pallas_call(
        functools.partial(_merge_body, tm=tm),
        grid=(n // tm,),
        in_specs=[pl.BlockSpec((tm, oa.shape[1]), row), pl.BlockSpec((tm, ob.shape[1]), row),
                  pl.BlockSpec((tm, zc.shape[1]), row), pl.BlockSpec((tm, 3 * d), row),
                  pl.BlockSpec((tm, d), row),
                  full(wa), full(wb), full(wc), full(wo), full(g), full(wr), full(br)],
        out_specs=[pl.BlockSpec((tm, d), row), pl.BlockSpec((tm, d // 2), row),
                   pl.BlockSpec((tm, LANES), row), pl.BlockSpec((SUBLANES, tm), lambda i: (0, i)),
                   pl.BlockSpec((N_EXPERTS, LANES), fixed)],
        out_shape=[jax.ShapeDtypeStruct((n, d), F32), jax.ShapeDtypeStruct((n, d // 2), U32),
                   jax.ShapeDtypeStruct((n, LANES), F32), jax.ShapeDtypeStruct((SUBLANES, n), I32),
                   jax.ShapeDtypeStruct((N_EXPERTS, LANES), F32)],
        scratch_shapes=[pltpu.VMEM((N_EXPERTS, 1), F32)],
        compiler_params=_params("arbitrary"),
        name="merge_router",
    )(oa, ob, zc, gates, x2, wa, wb, wc, wo, g, wr, br)


def _sc_mesh():
    return plsc.VectorSubcoreMesh(core_axis_name="c", subcore_axis_name="s")


def _sc_worker_base(per_worker):
    return (lax.axis_index("s") * SC_CORES + lax.axis_index("c")) * per_worker


def _sc_row_scatter(src, idx, out_rows, *, chunk=128):
    n, d = src.shape
    per_worker = n // (SC_CORES * SC_SUBCORES)
    chunk = min(chunk, per_worker)
    assert idx.shape == (2 * n,) and n % (SC_CORES * SC_SUBCORES) == 0 and per_worker % chunk == 0

    @functools.partial(
        pl.kernel, mesh=_sc_mesh(), out_type=jax.ShapeDtypeStruct((out_rows, d), src.dtype),
        scratch_types=[pltpu.VMEM((chunk,), I32), pltpu.VMEM((chunk, d), src.dtype), pltpu.SemaphoreType.DMA])
    def scatter(src_hbm, idx_hbm, out_hbm, idx_v, rows_v, sem):
        base = _sc_worker_base(per_worker)

        @pl.loop(0, per_worker // chunk)
        def _(step):
            off = base + step * chunk
            pltpu.sync_copy(src_hbm.at[pl.ds(off, chunk)], rows_v)
            for slot in range(2):
                pltpu.sync_copy(idx_hbm.at[pl.ds(slot * n + off, chunk)], idx_v)
                pltpu.async_copy(rows_v, out_hbm.at[idx_v], sem).wait()

    return scatter(src, idx)


def _pack_bf16_pairs(x):
    half = x.shape[-1] // 2
    lo = pltpu.bitcast(x[:, :half].astype(BF16).astype(F32), U32) >> 16
    hi = pltpu.bitcast(x[:, half:].astype(BF16).astype(F32), U32) & jnp.uint32(0xFFFF0000)
    return lo | hi


def _unpack_bf16_pairs(xp):
    return pltpu.bitcast(xp << 16, F32), pltpu.bitcast(xp & jnp.uint32(0xFFFF0000), F32)


def _expert_body(be_ref, valid_ref, nu_ref, xs_ref, w1_ref, w3_ref, w2_ref, ys_ref):
    del be_ref
    step = pl.program_id(0)
    live = step < nu_ref[0]

    @pl.when(jnp.logical_not(live))
    def _():
        ys_ref[...] = jnp.zeros_like(ys_ref)

    @pl.when(live)
    def _():
        row = lax.broadcasted_iota(I32, xs_ref.shape, 0)
        xp = jnp.where(row < valid_ref[step], xs_ref[...], jnp.uint32(0))
        half = xp.shape[-1]
        lo, hi = (part.astype(BF16) for part in _unpack_bf16_pairs(xp))

        def up(w_ref):
            return (jnp.dot(lo, w_ref[0, :half, :].astype(BF16), preferred_element_type=F32)
                    + jnp.dot(hi, w_ref[0, half:, :].astype(BF16), preferred_element_type=F32))

        act = (_silu(up(w1_ref)) * up(w3_ref)).astype(BF16)
        ys_ref[...] = _pack_bf16_pairs(jnp.dot(act, w2_ref[0].astype(BF16), preferred_element_type=F32))


def _experts(blk_eid, blk_valid, n_used, xs, w1, w3, w2, *, rb):
    p, half = xs.shape
    d = 2 * half
    nb = p // rb
    used = lambda i, be, bv, nu: jnp.maximum(jnp.minimum(i, nu[0] - 1), 0)
    wmap = lambda i, be, bv, nu: (be[used(i, be, bv, nu)], 0, 0)
    grid_spec = pltpu.PrefetchScalarGridSpec(
        num_scalar_prefetch=3,
        grid=(nb,),
        in_specs=[pl.BlockSpec((rb, half), lambda i, be, bv, nu: (used(i, be, bv, nu), 0)),
                  pl.BlockSpec((1, d, D_EXPERT), wmap), pl.BlockSpec((1, d, D_EXPERT), wmap),
                  pl.BlockSpec((1, D_EXPERT, d), wmap)],
        out_specs=pl.BlockSpec((rb, half), lambda i, be, bv, nu: (i, 0)),
    )
    return pl.pallas_call(
        _expert_body,
        grid_spec=grid_spec,
        out_shape=jax.ShapeDtypeStruct((p, half), U32),
        compiler_params=_params("arbitrary"),
        name="moe_experts",
    )(blk_eid, blk_valid, n_used, xs, w1, w3, w2)


def _sc_row_gather(table, idx, *, chunk=128):
    rows, d = idx.shape[0], table.shape[1]
    per_worker = rows // (SC_CORES * SC_SUBCORES)
    chunk = min(chunk, per_worker)
    assert rows % (SC_CORES * SC_SUBCORES) == 0 and per_worker % chunk == 0

    @functools.partial(
        pl.kernel, mesh=_sc_mesh(), out_type=jax.ShapeDtypeStruct((rows, d), table.dtype),
        scratch_types=[pltpu.VMEM((chunk,), I32), pltpu.VMEM((chunk, d), table.dtype), pltpu.SemaphoreType.DMA])
    def gather(table_hbm, idx_hbm, out_hbm, idx_v, rows_v, sem):
        base = _sc_worker_base(per_worker)

        @pl.loop(0, per_worker // chunk)
        def _(step):
            off = base + step * chunk
            pltpu.sync_copy(idx_hbm.at[pl.ds(off, chunk)], idx_v)
            pltpu.async_copy(table_hbm.at[idx_v], rows_v, sem).wait()
            pltpu.sync_copy(rows_v, out_hbm.at[pl.ds(off, chunk)])

    return gather(table, idx)


def _combine_rows(r0_ref, r1_ref, x_ref, route_ref):
    route = route_ref[...]
    half = r0_ref.shape[-1]
    lo0, hi0 = _unpack_bf16_pairs(r0_ref[...])
    lo1, hi1 = _unpack_bf16_pairs(r1_ref[...])
    g0, g1 = route[:, R_GATE0:R_GATE0 + 1], route[:, R_GATE1:R_GATE1 + 1]
    return jnp.concatenate([x_ref[:, :half] + g0 * lo0 + g1 * lo1,
                            x_ref[:, half:] + g0 * hi0 + g1 * hi1], axis=1)


def _final_body(r0_ref, r1_ref, x_ref, route_ref, g_ref, o_ref):
    out = _combine_rows(r0_ref, r1_ref, x_ref, route_ref)
    o_ref[...] = out * lax.rsqrt(jnp.mean(out * out, axis=-1, keepdims=True) + EPS) * g_ref[...]


def _final_combine(rows2, xn, route, g, *, tm):
    n, d = xn.shape
    nt = n // tm
    row = lambda i: (i, 0)
    return pl.pallas_call(
        _final_body,
        grid=(nt,),
        in_specs=[pl.BlockSpec((tm, d // 2), row), pl.BlockSpec((tm, d // 2), lambda i: (i + nt, 0)),
                  pl.BlockSpec((tm, d), row), pl.BlockSpec((tm, LANES), row),
                  pl.BlockSpec((1, d), lambda i: (0, 0))],
        out_specs=pl.BlockSpec((tm, d), row),
        out_shape=jax.ShapeDtypeStruct((n, d), F32),
        compiler_params=_params("parallel"),
        name="moe_combine",
    )(rows2, rows2, xn, route, g)


def _moe_rows(hp, meta, counts, w1, w3, w2, *, layer, rb):
    n = hp.shape[0]
    cnt = counts[:, 0].astype(I32)
    nblk = (cnt + rb - 1) // rb
    bend = jnp.cumsum(nblk)
    pstart = (bend - nblk) * rb
    nb = (2 * n) // rb + N_EXPERTS
    n_used = bend[-1:].astype(I32)
    blk = jnp.arange(nb, dtype=I32)
    blk_eid = jnp.minimum(jnp.sum(bend[None, :] <= blk[:, None], axis=1), N_EXPERTS - 1).astype(I32)
    experts = jnp.arange(N_EXPERTS, dtype=I32)
    mine = blk_eid[:, None] == experts[None, :]
    blk_valid = jnp.clip(jnp.sum(jnp.where(mine, (cnt + pstart)[None, :], 0), axis=1) - blk * rb, 0, rb)
    eid = meta[R_EID0:R_EID1 + 1]
    first = jnp.sum(jnp.where(eid[None] == experts[:, None, None], pstart[:, None, None], 0), axis=0)
    dest = (first + meta[R_RANK0:R_RANK1 + 1]).reshape(-1)

    xs = _sc_row_scatter(hp, dest, nb * rb)
    ys = _experts(blk_eid + layer * N_EXPERTS, blk_valid.astype(I32), n_used, xs, w1, w3, w2, rb=rb)
    return _sc_row_gather(ys, dest)


def _regroup_body(w_ref, wm_ref, ws_ref, *, bounds, q_scale):
    small0, fox0, fb0, conv0, end = bounds
    w = w_ref[0]
    fox_w = D_FOX * H_FOX
    main = jnp.concatenate([w[:, :small0], w[:, fox0:fox0 + fox_w] * q_scale, w[:, fox0 + fox_w:fb0],
                            w[:, conv0:end]], axis=1)
    narrow = jnp.concatenate([w[:, small0:fox0], w[:, fb0:conv0]], axis=1)
    wm_ref[...] = main.astype(BF16)
    ws_ref[...] = jnp.concatenate(
        [narrow, jnp.zeros((narrow.shape[0], LANES - narrow.shape[1]), F32)], axis=1).astype(BF16)


def _regroup_w_in(w_in, layer, *, q_scale, tr=128):
    _, d, d_in = w_in.shape
    small0 = 2 * H_DN * DK_DN + 2 * H_DN * DV_DN
    fox0 = small0 + 2 * H_DN
    fb0 = fox0 + W_FOX
    conv0 = fb0 + H_FOX
    width = d_in - (conv0 - fb0) - (fox0 - small0)
    assert d % tr == 0 and width % LANES == 0
    return pl.pallas_call(
        functools.partial(_regroup_body, bounds=(small0, fox0, fb0, conv0, d_in), q_scale=q_scale),
        grid=(d // tr,),
        in_specs=[pl.BlockSpec((1, tr, d_in), lambda i: (layer, i, 0))],
        out_specs=[pl.BlockSpec((tr, width), lambda i: (i, 0)), pl.BlockSpec((tr, LANES), lambda i: (i, 0))],
        out_shape=[jax.ShapeDtypeStruct((d, width), BF16), jax.ShapeDtypeStruct((d, LANES), BF16)],
        compiler_params=_params("parallel"),
        name="regroup_w_in",
    )(w_in)


def _lane_row(pairs):
    row = jnp.zeros((LANES,), F32)
    for off, vec in pairs:
        row = row.at[off:off + vec.shape[0]].set(vec.astype(F32))
    return row


def kernel(x, norm_mix, w_in, conv_qkv, dn_a_log, dn_dt_bias, dn_norm, fox_bias, conv_dw, conv_dw_b,
           conv_ln_g, conv_ln_b, w_a, w_b, w_c, w_out, norm_ffn, router_group_w, router_group_b,
           router_expert_w, router_expert_b, expert_w1, expert_w3, expert_w2, norm_final):
    b, t, d = x.shape
    n = b * t
    depth = w_in.shape[0]
    tm_final, rb = 512, 512

    x_src = x.reshape(n, d)
    w1_all = expert_w1.reshape(depth * N_EXPERTS, d, D_EXPERT)
    w3_all = expert_w3.reshape(depth * N_EXPERTS, d, D_EXPERT)
    w2_all = expert_w2.reshape(depth * N_EXPERTS, D_EXPERT, d)
    for l in range(depth):
        w_main, w_small = _regroup_w_in(w_in, l, q_scale=D_FOX ** -0.5 * LOG2E)
        vec =jnp.stack([conv_dw_b[l], conv_ln_g[l], conv_ln_b[l]] + [jnp.zeros((C_CONV,), F32)] * 5)
        outs = _in_proj(x_src, norm_mix[l][None, :], w_main, w_small, conv_dw[l], vec, seq_len=t)
        x2 = outs[0] if l > 0 else x_src
        dn_qkv, za_p, fox_qkv, zc, gates, small = outs[-6:]

        par = jnp.stack([_lane_row([(LANE_G, -jnp.exp(dn_a_log[l]))]),
                         _lane_row([(LANE_G, dn_dt_bias[l]), (LANE_F, fox_bias[l])]),
                         _lane_row([(0, dn_norm[l])])] + [jnp.zeros((LANES,), F32)] * 5)
        oa, qaug, kaug = _delta_rule(dn_qkv.reshape(b, t, -1), za_p.reshape(b, t, -1),
                                     small.reshape(b, t, LANES), conv_qkv[l], par)
        ob = _fox_attention(fox_qkv.reshape(b, t, -1), qaug, kaug)

        w_r = jnp.concatenate([router_group_w[l], jnp.zeros((d, LANE_EXP - N_GROUPS), F32),
                               router_expert_w[l], jnp.zeros((d, LANES - LANE_EXP - N_EXPERTS), F32)], axis=1)
        b_r = _lane_row([(LANE_GRP, router_group_b[l]), (LANE_EXP, router_expert_b[l])])[None, :]
        xn, hp, route, meta, counts = _merge(
            oa.reshape(n, -1), ob.reshape(n, -1), zc.reshape(n, -1), gates, x2,
            w_a[l].astype(BF16), w_b[l].astype(BF16), w_c[l].astype(BF16), w_out[l].astype(BF16),
            norm_ffn[l][None, :], w_r.T.astype(BF16), b_r.reshape(LANES, 1))

        x_src = (_moe_rows(hp, meta, counts, w1_all, w3_all, w2_all, layer=l, rb=rb), xn, route)
    return _final_combine(*x_src, norm_final[None, :], tm=tm_final).reshape(b, t, d)
```

```python
import functools

import jax
import jax.numpy as jnp
from jax import lax
from jax.experimental import pallas as pl
from jax.experimental.pallas import tpu as pltpu
from jax.experimental.pallas import tpu_sc as plsc

F32 = jnp.float32
BF16 = jnp.bfloat16
U32 = jnp.uint32
I32 = jnp.int32

EPS = 1e-6
LOG2E = 1.4426950408889634
LANES = 128
SUBLANES = 8
SC_CORES, SC_SUBCORES = 2, 16
H_DN, DK_DN, DV_DN = 4, 128, 128
SHORT_CONV = 4
CHUNK = 64
H_FOX, D_FOX = 4, 128
C_CONV = 512
CONV_WIDTH = 31
N_GROUPS, EXPERTS_PER_GROUP = 4, 8
N_EXPERTS = N_GROUPS * EXPERTS_PER_GROUP
D_EXPERT = 256

W_DN = 3 * H_DN * DK_DN
W_FOX = 3 * H_FOX * D_FOX
VMEM_LIMIT = 56 * 1024 * 1024

LANE_BETA, LANE_G, LANE_F = 0, 4, 8
LANE_GRP, LANE_EXP = 0, 32
R_GATE0, R_GATE1, R_EID0, R_EID1, R_RANK0, R_RANK1 = 0, 1, 2, 3, 4, 5

NN = (((1,), (0,)), ((), ()))
NT = (((1,), (1,)), ((), ()))
TN = (((0,), (0,)), ((), ()))


def _mm(a, b, dims=NN):
    return lax.dot_general(a.astype(BF16), b.astype(BF16), dims, preferred_element_type=F32)


def _sigmoid(x):
    return 0.5 * jnp.tanh(0.5 * x) + 0.5


def _silu(x):
    half = 0.5 * x
    return half * jnp.tanh(half) + half


def _params(*sem):
    return pltpu.CompilerParams(dimension_semantics=sem, vmem_limit_bytes=VMEM_LIMIT)


def _in_proj_body(*refs, col_chunk, combine, tiles_per_seq, hrows, rchunk):
    if combine:
        r0_ref, r1_ref, xn_ref, route_ref, g_ref, w_ref, ws_ref, dw_ref, vec_ref, x_out_ref = refs[:10]
        x = _combine_rows(r0_ref, r1_ref, xn_ref, route_ref)
        x_out_ref[...] = x
        rest = refs[10:]
    else:
        x_ref, g_ref, w_ref, ws_ref, dw_ref, vec_ref = refs[:6]
        x = x_ref[...]
        rest = refs[6:]
    dn_ref, za_ref, fox_ref, zc_ref, gates_ref, small_ref, z_ref, zs_ref, y_ref = rest
    tm = x.shape[0]
    h = x * lax.rsqrt(jnp.mean(x * x, axis=-1, keepdims=True) + EPS) * g_ref[...]
    hb = h.astype(BF16)

    def project(col, width, hold=None):
        lhs = hb
        if hold is not None:
            zero = jnp.minimum(jnp.abs(hold), 0.0).astype(BF16)
            head = jnp.concatenate([hb[0:rchunk, 0:LANES] + zero, hb[0:rchunk, LANES:]], axis=1)
            lhs = jnp.concatenate([head, hb[rchunk:]], axis=0)
        return jnp.dot(lhs, w_ref[:, col:col + width], preferred_element_type=F32)

    starts, col = {}, 0
    for name, width in (("dn", dn_ref.shape[-1]), ("za", za_ref.shape[-1]), ("fox", fox_ref.shape[-1]),
                        ("conv", 2 * C_CONV), ("gates", gates_ref.shape[-1])):
        starts[name], col = col, col + width

    first = pl.program_id(0) % tiles_per_seq == 0
    col = starts["conv"]

    @pl.when(first)
    def _():
        z_ref[0:hrows, :] = jnp.zeros((hrows, C_CONV), F32)

    @pl.when(jnp.logical_not(first))
    def _():
        z_ref[0:hrows, :] = z_ref[tm:tm + hrows, :]

    z_ref[hrows:hrows + tm, :] = project(col, C_CONV) * _sigmoid(project(col + C_CONV, C_CONV))
    sub = SUBLANES
    span = tm + hrows - sub
    for phase in range(1, sub):
        zs_ref[phase - 1, 0:span, :] = z_ref[phase:phase + span, :]

    def conv_unit(lg, r0, after):
        lanes = slice(lg * LANES, (lg + 1) * LANES)
        acc = vec_ref[0:1, lanes] + jnp.minimum(jnp.abs(after[0:rchunk, 0:LANES]), 0.0)
        for k in range(CONV_WIDTH):
            off = hrows - (CONV_WIDTH - 1) + k
            phase, base = off % sub, off - off % sub + r0
            if phase == 0:
                tap = z_ref[base:base + rchunk, lanes]
            else:
                tap = zs_ref[phase - 1, base:base + rchunk, lanes]
            acc = acc + dw_ref[k:k + 1, lanes] * tap
        y_ref[r0:r0 + rchunk, lanes] = acc
        return acc

    units = [(lg, r0) for lg in range(C_CONV // LANES) for r0 in range(0, tm, rchunk)]
    chunks = [(ref, starts[name], c) for name, ref in (("dn", dn_ref), ("za", za_ref), ("fox", fox_ref),
                                                       ("gates", gates_ref))
              for c in range(0, ref.shape[-1], col_chunk)]
    share = [[] for _ in chunks]
    for u, unit in enumerate(units):
        share[u * len(chunks) // len(units)].append(unit)
    last = []
    for idx, ((ref, start, c), mine) in enumerate(zip(chunks, share)):
        res = project(start + c, col_chunk, hold=last[idx - 2] if idx >= 2 else None)
        ref[:, c:c + col_chunk] = res.astype(ref.dtype)
        tile = None
        for lg, r0 in mine:
            tile = conv_unit(lg, r0, res)
        last.append(tile)
    small_ref[...] = jnp.dot(hb, ws_ref[...], preferred_element_type=F32)

    y = y_ref[...]
    mu = jnp.mean(y, axis=-1, keepdims=True)
    yc = y - mu
    var = jnp.mean(yc * yc, axis=-1, keepdims=True)
    yn = yc * lax.rsqrt(var + EPS) * vec_ref[1:2, :] + vec_ref[2:3, :]
    zc_ref[...] = _silu(yn).astype(zc_ref.dtype)


def _in_proj(x_src, g, w, ws, dw, vec, *, seq_len, tm=512, col_chunk=512, hrows=32, rchunk=32):
    combine = isinstance(x_src, tuple)
    n, d = (x_src[1] if combine else x_src).shape
    nt = n // tm
    widths = (W_DN, H_DN * DV_DN, W_FOX, C_CONV, 3 * d)
    assert w.shape == (d, sum(widths) + C_CONV) and n % tm == 0 and seq_len % tm == 0
    assert hrows >= CONV_WIDTH - 1 and hrows % SUBLANES == 0 and tm % rchunk == 0
    row = lambda i: (i, 0)
    fixed = lambda i: (0, 0)
    out_shape = [jax.ShapeDtypeStruct((n, wd), BF16) for wd in widths] + [jax.ShapeDtypeStruct((n, LANES), F32)]
    out_specs = [pl.BlockSpec((tm, wd), row) for wd in widths] + [pl.BlockSpec((tm, LANES), row)]
    if combine:
        rows2, xn, route = x_src
        args = (rows2, rows2, xn, route)
        in_specs = [pl.BlockSpec((tm, d // 2), row), pl.BlockSpec((tm, d // 2), lambda i: (i + nt, 0)),
                    pl.BlockSpec((tm, d), row), pl.BlockSpec((tm, LANES), row)]
        out_shape = [jax.ShapeDtypeStruct((n, d), F32)] + out_shape
        out_specs = [pl.BlockSpec((tm, d), row)] + out_specs
    else:
        args = (x_src,)
        in_specs = [pl.BlockSpec((tm, d), row)]
    return pl.pallas_call(
        functools.partial(_in_proj_body, col_chunk=col_chunk, combine=combine, tiles_per_seq=seq_len // tm,
                          hrows=hrows, rchunk=rchunk),
        grid=(nt,),
        in_specs=in_specs + [pl.BlockSpec((1, d), fixed),
                             pl.BlockSpec(w.shape, fixed, pipeline_mode=pl.Buffered(1)),
                             pl.BlockSpec(ws.shape, fixed, pipeline_mode=pl.Buffered(1)),
                             pl.BlockSpec(dw.shape, fixed), pl.BlockSpec(vec.shape, fixed)],
        out_specs=out_specs,
        out_shape=out_shape,
        scratch_shapes=[pltpu.VMEM((tm + hrows, C_CONV), F32),
                        pltpu.VMEM((SUBLANES - 1, tm + hrows, C_CONV), F32),
                        pltpu.VMEM((tm, C_CONV), F32)],
        compiler_params=_params("arbitrary"),
        name="in_proj",
    )(*args, g, w, ws, dw, vec)


def _softplus_parts(z):
    t = jnp.log1p(jnp.exp(-jnp.abs(z)))
    return jnp.maximum(z, 0.0) + t, -(jnp.maximum(-z, 0.0) + t)


def _delta_body(qkv_ref, za_ref, sm_ref, cw_ref, par_ref, oa_ref, qaug_ref, kaug_ref,
                xs_ref, s_ref, carry_ref, *, ts):
    j = pl.program_id(1)
    halo = SUBLANES
    pack = 2 * SUBLANES

    @pl.when(j == 0)
    def _():
        xs_ref[0:halo, :] = jnp.zeros((halo, W_DN), F32)
        s_ref[...] = jnp.zeros_like(s_ref)
        carry_ref[...] = jnp.zeros_like(carry_ref)

    @pl.when(j > 0)
    def _():
        xs_ref[0:halo, :] = xs_ref[2 * halo:3 * halo, :]

    xb = qkv_ref[0]
    xs_ref[halo:2 * halo, :] = qkv_ref[0, 0:pack, :].astype(F32)[0:halo]
    xs_ref[2 * halo:3 * halo, :] = qkv_ref[0, ts - pack:ts, :].astype(F32)[pack - halo:pack]

    lag = lax.broadcasted_iota(I32, (ts, ts), 0) - lax.broadcasted_iota(I32, (ts, ts), 1)
    shifts = jnp.concatenate([jnp.where(lag == s, 1.0, 0.0).astype(BF16) for s in range(1, SHORT_CONV)], axis=0)
    stacked = jnp.dot(shifts, xb, preferred_element_type=F32)
    shifted = [stacked[(s - 1) * ts:s * ts] for s in range(1, SHORT_CONV)]
    head_row = lax.broadcasted_iota(I32, (halo, LANES), 0)

    def conv_silu(lane0):
        lanes = slice(lane0, lane0 + LANES)
        acc = cw_ref[SHORT_CONV - 1:SHORT_CONV, lanes] * xb[:, lanes].astype(F32)
        for s in range(1, SHORT_CONV):
            acc = acc + cw_ref[SHORT_CONV - 1 - s:SHORT_CONV - s, lanes] * shifted[s - 1][:, lanes]
        head = acc[0:halo]
        for s in range(1, SHORT_CONV):
            prev = jnp.where(head_row < s, xs_ref[halo - s:2 * halo - s, lanes], 0.0)
            head = head + cw_ref[SHORT_CONV - 1 - s:SHORT_CONV - s, lanes] * prev
        return _silu(jnp.concatenate([head, acc[halo:]], axis=0))

    def l2n(a):
        return a * lax.rsqrt(jnp.sum(a * a, axis=-1, keepdims=True) + EPS)

    sm = sm_ref[0]
    lane = lax.broadcasted_iota(I32, sm.shape, 1)
    sp, logsig = _softplus_parts(sm + par_ref[1:2, :])
    vals = jnp.where(lane < LANE_G, _sigmoid(sm),
                     jnp.where(lane < LANE_F, par_ref[0:1, :] * sp,
                               jnp.where(lane < LANE_F + H_FOX, logsig, 0.0)))
    row = lax.broadcasted_iota(I32, (ts, ts), 0)
    colm = lax.broadcasted_iota(I32, (ts, ts), 1)
    log_chunk = CHUNK.bit_length() - 1
    causal = (row >= colm) & ((row >> log_chunk) == (colm >> log_chunk))

    hi = vals.astype(BF16)
    rem = vals - hi.astype(F32)
    mid = rem.astype(BF16)
    lo = (rem - mid.astype(F32)).astype(BF16)
    pieces = jnp.concatenate([hi, mid, lo], axis=-1)

    def cumsum(mask):
        y = jnp.dot(jnp.where(mask, 1.0, 0.0).astype(BF16), pieces, preferred_element_type=F32)
        return (y[:, :LANES] + y[:, LANES:2 * LANES]) + y[:, 2 * LANES:]

    ccum = cumsum(row >= colm) + carry_ref[...]
    gcum = cumsum(causal)
    carry_ref[...] = ccum[ts - 1:ts, :]
    gcum_t = gcum.T

    cl = ccum * LOG2E
    c_hi = cl.astype(BF16)
    c_rem = cl - c_hi.astype(F32)
    c_mid = c_rem.astype(BF16)
    c_lo = (c_rem - c_mid.astype(F32)).astype(BF16)
    c_pieces = jnp.concatenate([c_hi, c_mid, c_lo], axis=-1)
    pr = lax.broadcasted_iota(I32, (3 * LANES, LANES), 0)
    pc = lax.broadcasted_iota(I32, (3 * LANES, LANES), 1)
    src_lane, piece = pr & (LANES - 1), pr >> (LANES.bit_length() - 1)
    owned = (src_lane >= LANE_F) & (src_lane < LANE_F + H_FOX) & ((pc >> 3) == src_lane - LANE_F)
    place_q = jnp.where(owned & ((pc & 7) == piece), 1.0, 0.0).astype(BF16)
    place_k = jnp.where(owned & ((pc & 7) == piece + 3), -1.0, 0.0).astype(BF16)
    slot = lane & 7
    in_heads = lane < 8 * H_FOX
    ones_q = jnp.where(in_heads & (slot >= 3) & (slot < 6), 1.0, 0.0)
    ones_k = jnp.where(in_heads & (slot < 3), 1.0, 0.0)
    qaug_ref[0] = (jnp.dot(c_pieces, place_q, preferred_element_type=F32) + ones_q).astype(BF16)
    kaug_ref[0] = (jnp.dot(c_pieces, place_k, preferred_element_type=F32) + ones_k).astype(BF16)

    scale = DK_DN ** -0.5
    dn_norm = par_ref[2:3, :]

    pw = 2 * CHUNK
    prow = lax.broadcasted_iota(I32, (pw, pw), 0)
    pcol = lax.broadcasted_iota(I32, (pw, pw), 1)
    same = (prow >> log_chunk) == (pcol >> log_chunk)
    causal_p = (prow >= pcol) & same
    strict_p = (prow > pcol) & same
    levels = []
    s = 1
    while s < CHUNK:
        levels.append(((prow >> s.bit_length()) == (pcol >> s.bit_length()))
                      & ((prow & s) != 0) & ((pcol & s) == 0))
        s *= 2

    heads = range(H_DN)
    pairs = range(ts // pw)
    q, k, v = [], [], []
    for h in heads:
        q.append(l2n(conv_silu(h * DK_DN)))
        k.append(l2n(conv_silu(H_DN * DK_DN + h * DK_DN)))
        v.append(conv_silu(2 * H_DN * DK_DN + h * DV_DN))

    ctx = []
    for h in heads:
        for p in pairs:
            pr = slice(p * pw, (p + 1) * pw)
            gcol = gcum[pr, LANE_G + h:LANE_G + h + 1]
            grow = gcum_t[LANE_G + h:LANE_G + h + 1, pr]
            beta = vals[pr, LANE_BETA + h:LANE_BETA + h + 1]
            decay = jnp.where(causal_p, jnp.exp(jnp.where(causal_p, gcol - grow, 0.0)), 0.0)
            egc = jnp.exp(gcol)
            kp = k[h][pr]
            kb = kp * beta
            ctx.append(dict(
                h=h, p=p, gcol=gcol, kp=kp,
                a=jnp.where(strict_p, _mm(kb, kp, NT) * decay, 0.0),
                qk=jnp.where(causal_p, _mm(q[h][pr] * scale, kp, NT) * decay, 0.0),
                rhs=jnp.concatenate([v[h][pr] * beta, kb * egc], axis=-1),
                qg=q[h][pr] * (scale * egc)))

    for c in ctx:
        c["n"] = -jnp.where(levels[0], c["a"], 0.0)
    for level in levels[1:]:
        for c in ctx:
            m = jnp.where(level, c["a"], 0.0)
            c["y"] = m + _mm(c["n"], m)
        for c in ctx:
            c["n"] = c["n"] - (c["y"] + _mm(c["y"], c["n"]))
    for c in ctx:
        c["sol"] = c["rhs"] + _mm(c["n"], c["rhs"])

    state = [s_ref[h] for h in heads]
    for p in pairs:
        group = [c for c in ctx if c["p"] == p]
        v_prev = [None] * H_DN
        for ch in range(2):
            rows = slice(ch * CHUNK, (ch + 1) * CHUNK)
            out_rows = slice(p * pw + ch * CHUNK, p * pw + (ch + 1) * CHUNK)
            on_state = [_mm(jnp.concatenate([c["sol"][rows, DV_DN:], c["qg"][rows]], axis=0), state[c["h"]])
                        for c in group]
            for c, ws_qs in zip(group, on_state):
                h = c["h"]
                vn = c["sol"][rows, :DV_DN] - ws_qs[:CHUNK]
                v_pair = jnp.concatenate([vn, jnp.zeros_like(vn)] if ch == 0 else [v_prev[h], vn], axis=0)
                o = ws_qs[CHUNK:] + _mm(c["qk"][rows], v_pair)
                glast = c["gcol"][(ch + 1) * CHUNK - 1:(ch + 1) * CHUNK]
                state[h] = (state[h] * jnp.exp(glast)
                            + _mm(c["kp"][rows] * jnp.exp(glast - c["gcol"][rows]), vn, TN))
                v_prev[h] = vn
                o = o * lax.rsqrt(jnp.mean(o * o, axis=-1, keepdims=True) + EPS) * dn_norm
                za = za_ref[0, out_rows, h * DV_DN:(h + 1) * DV_DN].astype(F32)
                oa_ref[0, out_rows, h * DV_DN:(h + 1) * DV_DN] = (o * _silu(za)).astype(oa_ref.dtype)
    for h in heads:
        s_ref[h] = state[h]


def _delta_rule(qkv, za, small, conv_w, par, *, ts=256):
    b, t, _ = qkv.shape
    assert t % ts == 0 and ts % (2 * CHUNK) == 0
    blk = lambda width: pl.BlockSpec((1, ts, width), lambda bi, j: (bi, j, 0))
    fixed = lambda bi, j: (0, 0)
    return pl.pallas_call(
        functools.partial(_delta_body, ts=ts),
        grid=(b, t // ts),
        in_specs=[blk(W_DN), blk(H_DN * DV_DN), blk(LANES),
                  pl.BlockSpec(conv_w.shape, fixed), pl.BlockSpec(par.shape, fixed)],
        out_specs=[blk(H_DN * DV_DN), blk(LANES), blk(LANES)],
        out_shape=[jax.ShapeDtypeStruct((b, t, H_DN * DV_DN), BF16),
                   jax.ShapeDtypeStruct((b, t, LANES), BF16),
                   jax.ShapeDtypeStruct((b, t, LANES), BF16)],
        scratch_shapes=[pltpu.VMEM((3 * SUBLANES, W_DN), F32),
                        pltpu.VMEM((H_DN, DK_DN, DV_DN), F32),
                        pltpu.VMEM((1, LANES), F32)],
        compiler_params=_params("parallel", "arbitrary"),
        name="delta_rule",
    )(qkv, za, small, conv_w, par)


def _fox_body(q_ref, k_ref, v_ref, qa_ref, ka_ref, o_ref, m_ref, acc_ref, *, tq, tk):
    i = pl.program_id(1)
    m_ref[...] = jnp.full(m_ref.shape, -jnp.inf, F32)
    acc_ref[...] = jnp.zeros_like(acc_ref)
    head_lanes = [slice(h * D_FOX, (h + 1) * D_FOX) for h in range(H_FOX)]
    lane = lax.broadcasted_iota(I32, (tk, LANES), 1)
    own = [jnp.where((lane >> 3) == h, 1.0, 0.0).astype(BF16) for h in range(H_FOX)]
    ones = jnp.ones((tk, D_FOX), BF16)
    keep = lax.broadcasted_iota(I32, (tq, tk), 1) <= lax.broadcasted_iota(I32, (tq, tk), 0)

    def block(start, diag_offset):
        rows = slice(0 if diag_offset is None else diag_offset, tq)
        ka = ka_ref[0, pl.ds(start, tk), :]
        for h, lanes in enumerate(head_lanes):
            q_aug = jnp.concatenate([q_ref[0, rows, lanes], qa_ref[0, rows, :]], axis=1)
            k_aug = jnp.concatenate([k_ref[0, pl.ds(start, tk), lanes], ka * own[h]], axis=1)
            s = lax.dot_general(q_aug, k_aug, NT, preferred_element_type=F32)
            if diag_offset is not None:
                s = jnp.where(keep[:tq - diag_offset], s, -jnp.inf)
            m_prev = m_ref[h, rows]
            m_next = jnp.maximum(m_prev, jnp.max(s, axis=-1, keepdims=True))
            p = jnp.exp2(s - jnp.concatenate([m_next] * (tk // LANES), axis=1))
            alpha = jnp.exp2(m_prev - m_next)
            v_aug = jnp.concatenate([v_ref[0, pl.ds(start, tk), lanes], ones], axis=1)
            acc_ref[h, rows] = (jnp.concatenate([alpha, alpha], axis=1) * acc_ref[h, rows]
                                + jnp.dot(p.astype(BF16), v_aug, preferred_element_type=F32))
            m_ref[h, rows] = m_next

    def full_block(jb, carry):
        block(pl.multiple_of(jb * tk, tk), None)
        return carry

    lax.fori_loop(0, i * (tq // tk), full_block, 0)
    for d in range(tq // tk):
        block(pl.multiple_of(i * tq + d * tk, tk), d * tk)
    for h, lanes in enumerate(head_lanes):
        acc = acc_ref[h]
        o_ref[0, :, lanes] = (acc[:, :D_FOX] / acc[:, D_FOX:]).astype(o_ref.dtype)


def _fox_attention(qkv, qaug, kaug, *, tq=1024, tk=512):
    b, t, _ = qkv.shape
    hd = H_FOX * D_FOX
    assert t % tq == 0 and tq % tk == 0 and tk % LANES == 0
    return pl.pallas_call(
        functools.partial(_fox_body, tq=tq, tk=tk),
        grid=(b, t // tq),
        in_specs=[pl.BlockSpec((1, tq, hd), lambda bi, i: (bi, i, 0)),
                  pl.BlockSpec((1, t, hd), lambda bi, i: (bi, 0, 1)),
                  pl.BlockSpec((1, t, hd), lambda bi, i: (bi, 0, 2)),
                  pl.BlockSpec((1, tq, LANES), lambda bi, i: (bi, i, 0)),
                  pl.BlockSpec((1, t, LANES), lambda bi, i: (bi, 0, 0))],
        out_specs=pl.BlockSpec((1, tq, hd), lambda bi, i: (bi, i, 0)),
        out_shape=jax.ShapeDtypeStruct((b, t, hd), BF16),
        scratch_shapes=[pltpu.VMEM((H_FOX, tq, LANES), F32), pltpu.VMEM((H_FOX, tq, 2 * D_FOX), F32)],
        compiler_params=_params("parallel", "arbitrary"),
        name="fox_attention",
    )(qkv, qkv, qkv, qaug, kaug)


def _merge_body(oa_ref, ob_ref, zc_ref, gates_ref, x_ref, wa_ref, wb_ref, wc_ref, wo_ref, g_ref,
                wr_ref, br_ref, xn_ref, hp_ref, route_ref, meta_ref, cnt_ref, run_ref, *, tm):
    step = pl.program_id(0)
    d = x_ref.shape[-1]

    @pl.when(step == 0)
    def _():
        run_ref[...] = jnp.zeros_like(run_ref)

    merged = None
    for idx, (m_ref, w_ref) in enumerate(((oa_ref, wa_ref), (ob_ref, wb_ref), (zc_ref, wc_ref))):
        y = jnp.dot(m_ref[...], w_ref[...], preferred_element_type=F32)
        term = _sigmoid(gates_ref[:, idx * d:(idx + 1) * d].astype(F32)) * y
        merged = term if merged is None else merged + term
    xn = x_ref[...] + jnp.dot(merged.astype(BF16), wo_ref[...], preferred_element_type=F32)
    xn_ref[...] = xn
    h2 = xn * lax.rsqrt(jnp.mean(xn * xn, axis=-1, keepdims=True) + EPS) * g_ref[...]

    hp_ref[...] = _pack_bf16_pairs(h2)

    logits = lax.dot_general(wr_ref[...], h2.astype(BF16), NT, preferred_element_type=F32) + br_ref[...]
    big = jnp.int32(LANES)
    gl = logits[LANE_GRP:LANE_GRP + N_GROUPS]
    g_row = lax.broadcasted_iota(I32, gl.shape, 0)
    gmax = jnp.max(gl, axis=0, keepdims=True)
    grp = jnp.min(jnp.where(gl == gmax, g_row, big), axis=0, keepdims=True)
    p_grp = 1.0 / jnp.sum(jnp.exp(gl - gmax), axis=0, keepdims=True)
    el_all = logits[LANE_EXP:LANE_EXP + N_EXPERTS]
    e_row = lax.broadcasted_iota(I32, el_all.shape, 0)
    el = jnp.where((e_row >> (EXPERTS_PER_GROUP.bit_length() - 1)) == grp, el_all, -jnp.inf)
    v0 = jnp.max(el, axis=0, keepdims=True)
    i0 = jnp.min(jnp.where(el == v0, e_row, big), axis=0, keepdims=True)
    el1 = jnp.where(e_row == i0, -jnp.inf, el)
    v1 = jnp.max(el1, axis=0, keepdims=True)
    i1 = jnp.min(jnp.where(el1 == v1, e_row, big), axis=0, keepdims=True)
    e1 = jnp.exp(v1 - v0)
    gate0 = p_grp / (1.0 + e1)
    gate1 = p_grp * e1 / (1.0 + e1)

    hot0 = e_row == i0
    hot1 = e_row == i1
    onehot = jnp.where(hot0 | hot1, 1.0, 0.0)
    tiles = [onehot[:, t * LANES:(t + 1) * LANES] for t in range(tm // LANES)]
    ri = lax.broadcasted_iota(I32, (LANES, LANES), 0)
    cj = lax.broadcasted_iota(I32, (LANES, LANES), 1)
    in_tile = _mm(jnp.concatenate(tiles, axis=0), jnp.where(ri < cj, 1.0, 0.0))
    before = run_ref[...]
    prefix = []
    for t, tile in enumerate(tiles):
        prefix.append(in_tile[t * N_EXPERTS:(t + 1) * N_EXPERTS] + before)
        before = before + jnp.sum(tile, axis=1, keepdims=True)
    prefix = jnp.concatenate(prefix, axis=1)
    run_ref[...] = before
    cnt_ref[...] = jnp.broadcast_to(before, cnt_ref.shape)
    rank0 = jnp.sum(jnp.where(hot0, prefix, 0.0), axis=0, keepdims=True)
    rank1 = jnp.sum(jnp.where(hot1, prefix, 0.0), axis=0, keepdims=True)

    rows = {R_GATE0: gate0, R_GATE1: gate1, R_EID0: i0.astype(F32), R_EID1: i1.astype(F32),
            R_RANK0: rank0, R_RANK1: rank1}
    fields = jnp.concatenate([rows.get(r, jnp.zeros_like(gate0)) for r in range(SUBLANES)], axis=0)
    meta_ref[...] = fields.astype(I32)
    route_ref[...] = jnp.concatenate([fields, jnp.zeros((LANES - SUBLANES, tm), F32)], axis=0).T


def _merge(oa, ob, zc, gates, x2, wa, wb, wc, wo, g, wr, br, *, tm=1024):
    n, d = x2.shape
    assert n % tm == 0
    row = lambda i: (i, 0)
    fixed = lambda i: (0, 0)
    full = lambda a: pl.BlockSpec(a.shape, fixed)
    return pl.pallas_call(
        functools.partial(_merge_body, tm=tm),
        grid=(n // tm,),
        in_specs=[pl.BlockSpec((tm, oa.shape[1]), row), pl.BlockSpec((tm, ob.shape[1]), row),
                  pl.BlockSpec((tm, zc.shape[1]), row), pl.BlockSpec((tm, 3 * d), row),
                  pl.BlockSpec((tm, d), row),
                  full(wa), full(wb), full(wc), full(wo), full(g), full(wr), full(br)],
        out_specs=[pl.BlockSpec((tm, d), row), pl.BlockSpec((tm, d // 2), row),
                   pl.BlockSpec((tm, LANES), row), pl.BlockSpec((SUBLANES, tm), lambda i: (0, i)),
                   pl.BlockSpec((N_EXPERTS, LANES), fixed)],
        out_shape=[jax.ShapeDtypeStruct((n, d), F32), jax.ShapeDtypeStruct((n, d // 2), U32),
                   jax.ShapeDtypeStruct((n, LANES), F32), jax.ShapeDtypeStruct((SUBLANES, n), I32),
                   jax.ShapeDtypeStruct((N_EXPERTS, LANES), F32)],
        scratch_shapes=[pltpu.VMEM((N_EXPERTS, 1), F32)],
        compiler_params=_params("arbitrary"),
        name="merge_router",
    )(oa, ob, zc, gates, x2, wa, wb, wc, wo, g, wr, br)


def _sc_mesh():
    return plsc.VectorSubcoreMesh(core_axis_name="c", subcore_axis_name="s")


def _sc_worker_base(per_worker):
    return (lax.axis_index("s") * SC_CORES + lax.axis_index("c")) * per_worker


def _sc_row_scatter(src, idx, out_rows, *, chunk=128):
    n, d = src.shape
    per_worker = n // (SC_CORES * SC_SUBCORES)
    chunk = min(chunk, per_worker)
    assert idx.shape == (2 * n,) and n % (SC_CORES * SC_SUBCORES) == 0 and per_worker % chunk == 0

    @functools.partial(
        pl.kernel, mesh=_sc_mesh(), out_type=jax.ShapeDtypeStruct((out_rows, d), src.dtype),
        scratch_types=[pltpu.VMEM((chunk,), I32), pltpu.VMEM((chunk, d), src.dtype), pltpu.SemaphoreType.DMA])
    def scatter(src_hbm, idx_hbm, out_hbm, idx_v, rows_v, sem):
        base = _sc_worker_base(per_worker)

        @pl.loop(0, per_worker // chunk)
        def _(step):
            off = base + step * chunk
            pltpu.sync_copy(src_hbm.at[pl.ds(off, chunk)], rows_v)
            for slot in range(2):
                pltpu.sync_copy(idx_hbm.at[pl.ds(slot * n + off, chunk)], idx_v)
                pltpu.async_copy(rows_v, out_hbm.at[idx_v], sem).wait()

    return scatter(src, idx)


def _pack_bf16_pairs(x):
    half = x.shape[-1] // 2
    lo = pltpu.bitcast(x[:, :half].astype(BF16).astype(F32), U32) >> 16
    hi = pltpu.bitcast(x[:, half:].astype(BF16).astype(F32), U32) & jnp.uint32(0xFFFF0000)
    return lo | hi


def _unpack_bf16_pairs(xp):
    return pltpu.bitcast(xp << 16, F32), pltpu.bitcast(xp & jnp.uint32(0xFFFF0000), F32)


def _expert_body(be_ref, valid_ref, nu_ref, xs_ref, w1_ref, w3_ref, w2_ref, ys_ref):
    del be_ref
    step = pl.program_id(0)
    live = step < nu_ref[0]

    @pl.when(jnp.logical_not(live))
    def _():
        ys_ref[...] = jnp.zeros_like(ys_ref)

    @pl.when(live)
    def _():
        row = lax.broadcasted_iota(I32, xs_ref.shape, 0)
        xp = jnp.where(row < valid_ref[step], xs_ref[...], jnp.uint32(0))
        half = xp.shape[-1]
        lo, hi = (part.astype(BF16) for part in _unpack_bf16_pairs(xp))

        def up(w_ref):
            return (jnp.dot(lo, w_ref[0, :half, :].astype(BF16), preferred_element_type=F32)
                    + jnp.dot(hi, w_ref[0, half:, :].astype(BF16), preferred_element_type=F32))

        act = (_silu(up(w1_ref)) * up(w3_ref)).astype(BF16)
        ys_ref[...] = _pack_bf16_pairs(jnp.dot(act, w2_ref[0].astype(BF16), preferred_element_type=F32))


def _experts(blk_eid, blk_valid, n_used, xs, w1, w3, w2, *, rb):
    p, half = xs.shape
    d = 2 * half
    nb = p // rb
    used = lambda i, be, bv, nu: jnp.maximum(jnp.minimum(i, nu[0] - 1), 0)
    wmap = lambda i, be, bv, nu: (be[used(i, be, bv, nu)], 0, 0)
    grid_spec = pltpu.PrefetchScalarGridSpec(
        num_scalar_prefetch=3,
        grid=(nb,),
        in_specs=[pl.BlockSpec((rb, half), lambda i, be, bv, nu: (used(i, be, bv, nu), 0)),
                  pl.BlockSpec((1, d, D_EXPERT), wmap), pl.BlockSpec((1, d, D_EXPERT), wmap),
                  pl.BlockSpec((1, D_EXPERT, d), wmap)],
        out_specs=pl.BlockSpec((rb, half), lambda i, be, bv, nu: (i, 0)),
    )
    return pl.pallas_call(
        _expert_body,
        grid_spec=grid_spec,
        out_shape=jax.ShapeDtypeStruct((p, half), U32),
        compiler_params=_params("arbitrary"),
        name="moe_experts",
    )(blk_eid, blk_valid, n_used, xs, w1, w3, w2)


def _sc_row_gather(table, idx, *, chunk=128):
    rows, d = idx.shape[0], table.shape[1]
    per_worker = rows // (SC_CORES * SC_SUBCORES)
    chunk = min(chunk, per_worker)
    assert rows % (SC_CORES * SC_SUBCORES) == 0 and per_worker % chunk == 0

    @functools.partial(
        pl.kernel, mesh=_sc_mesh(), out_type=jax.ShapeDtypeStruct((rows, d), table.dtype),
        scratch_types=[pltpu.VMEM((chunk,), I32), pltpu.VMEM((chunk, d), table.dtype), pltpu.SemaphoreType.DMA])
    def gather(table_hbm, idx_hbm, out_hbm, idx_v, rows_v, sem):
        base = _sc_worker_base(per_worker)

        @pl.loop(0, per_worker // chunk)
        def _(step):
            off = base + step * chunk
            pltpu.sync_copy(idx_hbm.at[pl.ds(off, chunk)], idx_v)
            pltpu.async_copy(table_hbm.at[idx_v], rows_v, sem).wait()
            pltpu.sync_copy(rows_v, out_hbm.at[pl.ds(off, chunk)])

    return gather(table, idx)


def _combine_rows(r0_ref, r1_ref, x_ref, route_ref):
    route = route_ref[...]
    half = r0_ref.shape[-1]
    lo0, hi0 = _unpack_bf16_pairs(r0_ref[...])
    lo1, hi1 = _unpack_bf16_pairs(r1_ref[...])
    g0, g1 = route[:, R_GATE0:R_GATE0 + 1], route[:, R_GATE1:R_GATE1 + 1]
    return jnp.concatenate([x_ref[:, :half] + g0 * lo0 + g1 * lo1,
                            x_ref[:, half:] + g0 * hi0 + g1 * hi1], axis=1)


def _final_body(r0_ref, r1_ref, x_ref, route_ref, g_ref, o_ref):
    out = _combine_rows(r0_ref, r1_ref, x_ref, route_ref)
    o_ref[...] = out * lax.rsqrt(jnp.mean(out * out, axis=-1, keepdims=True) + EPS) * g_ref[...]


def _final_combine(rows2, xn, route, g, *, tm):
    n, d = xn.shape
    nt = n // tm
    row = lambda i: (i, 0)
    return pl.pallas_call(
        _final_body,
        grid=(nt,),
        in_specs=[pl.BlockSpec((tm, d // 2), row), pl.BlockSpec((tm, d // 2), lambda i: (i + nt, 0)),
                  pl.BlockSpec((tm, d), row), pl.BlockSpec((tm, LANES), row),
                  pl.BlockSpec((1, d), lambda i: (0, 0))],
        out_specs=pl.BlockSpec((tm, d), row),
        out_shape=jax.ShapeDtypeStruct((n, d), F32),
        compiler_params=_params("parallel"),
        name="moe_combine",
    )(rows2, rows2, xn, route, g)


def _moe_rows(hp, meta, counts, w1, w3, w2, *, layer, rb):
    n = hp.shape[0]
    cnt = counts[:, 0].astype(I32)
    nblk = (cnt + rb - 1) // rb
    bend = jnp.cumsum(nblk)
    pstart = (bend - nblk) * rb
    nb = (2 * n) // rb + N_EXPERTS
    n_used = bend[-1:].astype(I32)
    blk = jnp.arange(nb, dtype=I32)
    blk_eid = jnp.minimum(jnp.sum(bend[None, :] <= blk[:, None], axis=1), N_EXPERTS - 1).astype(I32)
    experts = jnp.arange(N_EXPERTS, dtype=I32)
    mine = blk_eid[:, None] == experts[None, :]
    blk_valid = jnp.clip(jnp.sum(jnp.where(mine, (cnt + pstart)[None, :], 0), axis=1) - blk * rb, 0, rb)
    eid = meta[R_EID0:R_EID1 + 1]
    first = jnp.sum(jnp.where(eid[None] == experts[:, None, None], pstart[:, None, None], 0), axis=0)
    dest = (first + meta[R_RANK0:R_RANK1 + 1]).reshape(-1)

    xs = _sc_row_scatter(hp, dest, nb * rb)
    ys = _experts(blk_eid + layer * N_EXPERTS, blk_valid.astype(I32), n_used, xs, w1, w3, w2, rb=rb)
    return _sc_row_gather(ys, dest)


def _regroup_body(w_ref, wm_ref, ws_ref, *, bounds, q_scale):
    small0, fox0, fb0, conv0, end = bounds
    w = w_ref[0]
    fox_w = D_FOX * H_FOX
    main = jnp.concatenate([w[:, :small0], w[:, fox0:fox0 + fox_w] * q_scale, w[:, fox0 + fox_w:fb0],
                            w[:, conv0:end]], axis=1)
    narrow = jnp.concatenate([w[:, small0:fox0], w[:, fb0:conv0]], axis=1)
    wm_ref[...] = main.astype(BF16)
    ws_ref[...] = jnp.concatenate(
        [narrow, jnp.zeros((narrow.shape[0], LANES - narrow.shape[1]), F32)], axis=1).astype(BF16)


def _regroup_w_in(w_in, layer, *, q_scale, tr=128):
    _, d, d_in = w_in.shape
    small0 = 2 * H_DN * DK_DN + 2 * H_DN * DV_DN
    fox0 = small0 + 2 * H_DN
    fb0 = fox0 + W_FOX
    conv0 = fb0 + H_FOX
    width = d_in - (conv0 - fb0) - (fox0 - small0)
    assert d % tr == 0 and width % LANES == 0
    return pl.pallas_call(
        functools.partial(_regroup_body, bounds=(small0, fox0, fb0, conv0, d_in), q_scale=q_scale),
        grid=(d // tr,),
        in_specs=[pl.BlockSpec((1, tr, d_in), lambda i: (layer, i, 0))],
        out_specs=[pl.BlockSpec((tr, width), lambda i: (i, 0)), pl.BlockSpec((tr, LANES), lambda i: (i, 0))],
        out_shape=[jax.ShapeDtypeStruct((d, width), BF16), jax.ShapeDtypeStruct((d, LANES), BF16)],
        compiler_params=_params("parallel"),
        name="regroup_w_in",
    )(w_in)


def _regroup_t_body(a_ref, b_ref, c_ref, *out_refs, edges, q_scale):
    j = pl.program_id(0)
    a_end, q_end, b_end = edges

    def emit(src_ref, scale):
        for l, out_ref in enumerate(out_refs):
            block = src_ref[:, l, :].T
            out_ref[...] = (block if scale is None else block * scale).astype(out_ref.dtype)

    pl.when(j < a_end)(lambda: emit(a_ref, None))
    pl.when((j >= a_end) & (j < q_end))(lambda: emit(b_ref, q_scale))
    pl.when((j >= q_end) & (j < b_end))(lambda: emit(b_ref, None))
    pl.when(j >= b_end)(lambda: emit(c_ref, None))


def _regroup_w_in_t(w_in, *, q_scale, tc=LANES):
    depth, d, d_in = w_in.shape
    small0 = 2 * H_DN * DK_DN + 2 * H_DN * DV_DN
    fox0 = small0 + 2 * H_DN
    fb0 = fox0 + W_FOX
    conv0 = fb0 + H_FOX
    wt = jnp.transpose(w_in, (2, 0, 1))
    views = (wt[:small0], wt[fox0:fb0], wt[conv0:])
    blocks = [v.shape[0] // tc for v in views]
    assert all(v.shape[0] % tc == 0 for v in views)
    a_end, b_end = blocks[0], blocks[0] + blocks[1]
    q_end = a_end + (H_FOX * D_FOX) // tc
    width = sum(v.shape[0] for v in views)
    clamp = lambda j, lo, nblk: jnp.clip(j - lo, 0, nblk - 1)
    mains = pl.pallas_call(
        functools.partial(_regroup_t_body, edges=(a_end, q_end, b_end), q_scale=q_scale),
        grid=(width // tc,),
        in_specs=[pl.BlockSpec((tc, depth, d), lambda j: (clamp(j, 0, blocks[0]), 0, 0)),
                  pl.BlockSpec((tc, depth, d), lambda j: (clamp(j, a_end, blocks[1]), 0, 0)),
                  pl.BlockSpec((tc, depth, d), lambda j: (clamp(j, b_end, blocks[2]), 0, 0))],
        out_specs=[pl.BlockSpec((d, tc), lambda j: (0, j)) for _ in range(depth)],
        out_shape=[jax.ShapeDtypeStruct((d, width), BF16) for _ in range(depth)],
        compiler_params=_params("arbitrary"),
        name="regroup_w_in",
    )(*views)
    narrow = jnp.concatenate([wt[small0:fox0], wt[fb0:conv0]], axis=0)
    narrow = jnp.pad(jnp.transpose(narrow, (1, 2, 0)), ((0, 0), (0, 0), (0, LANES - narrow.shape[0])))
    return mains, narrow.astype(BF16)


def _lane_row(pairs):
    row = jnp.zeros((LANES,), F32)
    for off, vec in pairs:
        row = row.at[off:off + vec.shape[0]].set(vec.astype(F32))
    return row


def kernel(x, norm_mix, w_in, conv_qkv, dn_a_log, dn_dt_bias, dn_norm, fox_bias, conv_dw, conv_dw_b,
           conv_ln_g, conv_ln_b, w_a, w_b, w_c, w_out, norm_ffn, router_group_w, router_group_b,
           router_expert_w, router_expert_b, expert_w1, expert_w3, expert_w2, norm_final):
    b, t, d = x.shape
    n = b * t
    depth = w_in.shape[0]
    tm_final, rb = 512, 512

    x_src = x.reshape(n, d)
    w1_all = expert_w1.reshape(depth * N_EXPERTS, d, D_EXPERT)
    w3_all = expert_w3.reshape(depth * N_EXPERTS, d, D_EXPERT)
    w2_all = expert_w2.reshape(depth * N_EXPERTS, D_EXPERT, d)
    w_mains, w_smalls = _regroup_w_in_t(w_in, q_scale=D_FOX ** -0.5 * LOG2E)
    for l in range(depth):
        w_main, w_small = w_mains[l], w_smalls[l]
        vec = jnp.stack([conv_dw_b[l], conv_ln_g[l], conv_ln_b[l]] + [jnp.zeros((C_CONV,), F32)] * 5)
        outs = _in_proj(x_src, norm_mix[l][None, :], w_main, w_small, conv_dw[l], vec, seq_len=t)
        x2 = outs[0] if l > 0 else x_src
        dn_qkv, za_p, fox_qkv, zc, gates, small = outs[-6:]

        par = jnp.stack([_lane_row([(LANE_G, -jnp.exp(dn_a_log[l]))]),
                         _lane_row([(LANE_G, dn_dt_bias[l]), (LANE_F, fox_bias[l])]),
                         _lane_row([(0, dn_norm[l])])] + [jnp.zeros((LANES,), F32)] * 5)
        oa, qaug, kaug = _delta_rule(dn_qkv.reshape(b, t, -1), za_p.reshape(b, t, -1),
                                     small.reshape(b, t, LANES), conv_qkv[l], par)
        ob = _fox_attention(fox_qkv.reshape(b, t, -1), qaug, kaug)

        w_r = jnp.concatenate([router_group_w[l], jnp.zeros((d, LANE_EXP - N_GROUPS), F32),
                               router_expert_w[l], jnp.zeros((d, LANES - LANE_EXP - N_EXPERTS), F32)], axis=1)
        b_r = _lane_row([(LANE_GRP, router_group_b[l]), (LANE_EXP, router_expert_b[l])])[None, :]
        xn, hp, route, meta, counts = _merge(
            oa.reshape(n, -1), ob.reshape(n, -1), zc.reshape(n, -1), gates, x2,
            w_a[l].astype(BF16), w_b[l].astype(BF16), w_c[l].astype(BF16), w_out[l].astype(BF16),
            norm_ffn[l][None, :], w_r.T.astype(BF16), b_r.reshape(LANES, 1))

        x_src = (_moe_rows(hp, meta, counts, w1_all, w3_all, w2_all, layer=l, rb=rb), xn, route)
    return _final_combine(*x_src, norm_final[None, :], tm=tm_final).reshape(b, t, d)
```

```python
import functools

import jax
import jax.numpy as jnp
from jax import lax
from jax.experimental import pallas as pl
from jax.experimental.pallas import tpu as pltpu
from jax.experimental.pallas import tpu_sc as plsc

F32 = jnp.float32
BF16 = jnp.bfloat16
U32 = jnp.uint32
I32 = jnp.int32

EPS = 1e-6
LOG2E = 1.4426950408889634
LANES = 128
SUBLANES = 8
SC_CORES, SC_SUBCORES = 2, 16
H_DN, DK_DN, DV_DN = 4, 128, 128
SHORT_CONV = 4
CHUNK = 64
H_FOX, D_FOX = 4, 128
C_CONV = 512
CONV_WIDTH = 31
N_GROUPS, EXPERTS_PER_GROUP = 4, 8
N_EXPERTS = N_GROUPS * EXPERTS_PER_GROUP
D_EXPERT = 256

W_DN = 3 * H_DN * DK_DN
W_FOX = 3 * H_FOX * D_FOX
VMEM_LIMIT = 56 * 1024 * 1024

LANE_BETA, LANE_G, LANE_F = 0, 4, 8
LANE_GRP, LANE_EXP = 0, 32
R_GATE0, R_GATE1, R_EID0, R_EID1, R_RANK0, R_RANK1 = 0, 1, 2, 3, 4, 5

NN = (((1,), (0,)), ((), ()))
NT = (((1,), (1,)), ((), ()))
TN = (((0,), (0,)), ((), ()))


def _mm(a, b, dims=NN):
    return lax.dot_general(a.astype(BF16), b.astype(BF16), dims, preferred_element_type=F32)


def _sigmoid(x):
    return 0.5 * jnp.tanh(0.5 * x) + 0.5


def _silu(x):
    half = 0.5 * x
    return half * jnp.tanh(half) + half


def _params(*sem):
    return pltpu.CompilerParams(dimension_semantics=sem, vmem_limit_bytes=VMEM_LIMIT)


def _in_proj_body(*refs, col_chunk, combine, tiles_per_seq, hrows, rchunk):
    if combine:
        r0_ref, r1_ref, xn_ref, route_ref, g_ref, w_ref, ws_ref, dw_ref, vec_ref, x_out_ref = refs[:10]
        x = _combine_rows(r0_ref, r1_ref, xn_ref, route_ref)
        x_out_ref[...] = x
        rest = refs[10:]
    else:
        x_ref, g_ref, w_ref, ws_ref, dw_ref, vec_ref = refs[:6]
        x = x_ref[...]
        rest = refs[6:]
    dn_ref, za_ref, fox_ref, zc_ref, gates_ref, small_ref, z_ref, zs_ref, y_ref = rest
    tm = x.shape[0]
    h = x * lax.rsqrt(jnp.mean(x * x, axis=-1, keepdims=True) + EPS) * g_ref[...]
    hb = h.astype(BF16)

    def project(col, width, hold=None):
        lhs = hb
        if hold is not None:
            zero = jnp.minimum(jnp.abs(hold), 0.0).astype(BF16)
            head = jnp.concatenate([hb[0:rchunk, 0:LANES] + zero, hb[0:rchunk, LANES:]], axis=1)
            lhs = jnp.concatenate([head, hb[rchunk:]], axis=0)
        return jnp.dot(lhs, w_ref[:, col:col + width], preferred_element_type=F32)

    starts, col = {}, 0
    for name, width in (("dn", dn_ref.shape[-1]), ("za", za_ref.shape[-1]), ("fox", fox_ref.shape[-1]),
                        ("conv", 2 * C_CONV), ("gates", gates_ref.shape[-1])):
        starts[name], col = col, col + width

    first = pl.program_id(0) % tiles_per_seq == 0
    col = starts["conv"]

    @pl.when(first)
    def _():
        z_ref[0:hrows, :] = jnp.zeros((hrows, C_CONV), F32)

    @pl.when(jnp.logical_not(first))
    def _():
        z_ref[0:hrows, :] = z_ref[tm:tm + hrows, :]

    z_ref[hrows:hrows + tm, :] = project(col, C_CONV) * _sigmoid(project(col + C_CONV, C_CONV))
    sub = SUBLANES
    span = tm + hrows - sub
    for phase in range(1, sub):
        zs_ref[phase - 1, 0:span, :] = z_ref[phase:phase + span, :]

    def conv_unit(lg, r0, after):
        lanes = slice(lg * LANES, (lg + 1) * LANES)
        acc = vec_ref[0:1, lanes] + jnp.minimum(jnp.abs(after[0:rchunk, 0:LANES]), 0.0)
        for k in range(CONV_WIDTH):
            off = hrows - (CONV_WIDTH - 1) + k
            phase, base = off % sub, off - off % sub + r0
            if phase == 0:
                tap = z_ref[base:base + rchunk, lanes]
            else:
                tap = zs_ref[phase - 1, base:base + rchunk, lanes]
            acc = acc + dw_ref[k:k + 1, lanes] * tap
        y_ref[r0:r0 + rchunk, lanes] = acc
        return acc

    units = [(lg, r0) for lg in range(C_CONV // LANES) for r0 in range(0, tm, rchunk)]
    chunks = [(ref, starts[name], c) for name, ref in (("dn", dn_ref), ("za", za_ref), ("fox", fox_ref),
                                                       ("gates", gates_ref))
              for c in range(0, ref.shape[-1], col_chunk)]
    share = [[] for _ in chunks]
    for u, unit in enumerate(units):
        share[u * len(chunks) // len(units)].append(unit)
    last = []
    for idx, ((ref, start, c), mine) in enumerate(zip(chunks, share)):
        res = project(start + c, col_chunk, hold=last[idx - 2] if idx >= 2 else None)
        ref[:, c:c + col_chunk] = res.astype(ref.dtype)
        tile = None
        for lg, r0 in mine:
            tile = conv_unit(lg, r0, res)
        last.append(tile)
    small_ref[...] = jnp.dot(hb, ws_ref[...], preferred_element_type=F32)

    y = y_ref[...]
    mu = jnp.mean(y, axis=-1, keepdims=True)
    yc = y - mu
    var = jnp.mean(yc * yc, axis=-1, keepdims=True)
    yn = yc * lax.rsqrt(var + EPS) * vec_ref[1:2, :] + vec_ref[2:3, :]
    zc_ref[...] = _silu(yn).astype(zc_ref.dtype)


def _in_proj(x_src, g, w, ws, dw, vec, *, seq_len, tm=512, col_chunk=512, hrows=32, rchunk=32):
    combine = isinstance(x_src, tuple)
    n, d = (x_src[1] if combine else x_src).shape
    nt = n // tm
    widths = (W_DN, H_DN * DV_DN, W_FOX, C_CONV, 3 * d)
    assert w.shape == (d, sum(widths) + C_CONV) and n % tm == 0 and seq_len % tm == 0
    assert hrows >= CONV_WIDTH - 1 and hrows % SUBLANES == 0 and tm % rchunk == 0
    row = lambda i: (i, 0)
    fixed = lambda i: (0, 0)
    out_shape = [jax.ShapeDtypeStruct((n, wd), BF16) for wd in widths] + [jax.ShapeDtypeStruct((n, LANES), F32)]
    out_specs = [pl.BlockSpec((tm, wd), row) for wd in widths] + [pl.BlockSpec((tm, LANES), row)]
    if combine:
        rows2, xn, route = x_src
        args = (rows2, rows2, xn, route)
        in_specs = [pl.BlockSpec((tm, d // 2), row), pl.BlockSpec((tm, d // 2), lambda i: (i + nt, 0)),
                    pl.BlockSpec((tm, d), row), pl.BlockSpec((tm, LANES), row)]
        out_shape = [jax.ShapeDtypeStruct((n, d), F32)] + out_shape
        out_specs = [pl.BlockSpec((tm, d), row)] + out_specs
    else:
        args = (x_src,)
        in_specs = [pl.BlockSpec((tm, d), row)]
    return pl.pallas_call(
        functools.partial(_in_proj_body, col_chunk=col_chunk, combine=combine, tiles_per_seq=seq_len // tm,
                          hrows=hrows, rchunk=rchunk),
        grid=(nt,),
        in_specs=in_specs + [pl.BlockSpec((1, d), fixed),
                             pl.BlockSpec(w.shape, fixed, pipeline_mode=pl.Buffered(1)),
                             pl.BlockSpec(ws.shape, fixed, pipeline_mode=pl.Buffered(1)),
                             pl.BlockSpec(dw.shape, fixed), pl.BlockSpec(vec.shape, fixed)],
        out_specs=out_specs,
        out_shape=out_shape,
        scratch_shapes=[pltpu.VMEM((tm + hrows, C_CONV), F32),
                        pltpu.VMEM((SUBLANES - 1, tm + hrows, C_CONV), F32),
                        pltpu.VMEM((tm, C_CONV), F32)],
        compiler_params=_params("arbitrary"),
        name="in_proj",
    )(*args, g, w, ws, dw, vec)


def _softplus_parts(z):
    t = jnp.log1p(jnp.exp(-jnp.abs(z)))
    return jnp.maximum(z, 0.0) + t, -(jnp.maximum(-z, 0.0) + t)


def _delta_body(qkv_ref, za_ref, sm_ref, cw_ref, par_ref, oa_ref, qaug_ref, kaug_ref,
                xs_ref, s_ref, carry_ref, *, ts):
    j = pl.program_id(1)
    halo = SUBLANES
    pack = 2 * SUBLANES

    @pl.when(j == 0)
    def _():
        xs_ref[0:halo, :] = jnp.zeros((halo, W_DN), F32)
        s_ref[...] = jnp.zeros_like(s_ref)
        carry_ref[...] = jnp.zeros_like(carry_ref)

    @pl.when(j > 0)
    def _():
        xs_ref[0:halo, :] = xs_ref[2 * halo:3 * halo, :]

    xb = qkv_ref[0]
    xs_ref[halo:2 * halo, :] = qkv_ref[0, 0:pack, :].astype(F32)[0:halo]
    xs_ref[2 * halo:3 * halo, :] = qkv_ref[0, ts - pack:ts, :].astype(F32)[pack - halo:pack]

    lag = lax.broadcasted_iota(I32, (ts, ts), 0) - lax.broadcasted_iota(I32, (ts, ts), 1)
    shifts = jnp.concatenate([jnp.where(lag == s, 1.0, 0.0).astype(BF16) for s in range(1, SHORT_CONV)], axis=0)
    stacked = jnp.dot(shifts, xb, preferred_element_type=F32)
    shifted = [stacked[(s - 1) * ts:s * ts] for s in range(1, SHORT_CONV)]
    head_row = lax.broadcasted_iota(I32, (halo, LANES), 0)

    def conv_silu(lane0):
        lanes = slice(lane0, lane0 + LANES)
        acc = cw_ref[SHORT_CONV - 1:SHORT_CONV, lanes] * xb[:, lanes].astype(F32)
        for s in range(1, SHORT_CONV):
            acc = acc + cw_ref[SHORT_CONV - 1 - s:SHORT_CONV - s, lanes] * shifted[s - 1][:, lanes]
        head = acc[0:halo]
        for s in range(1, SHORT_CONV):
            prev = jnp.where(head_row < s, xs_ref[halo - s:2 * halo - s, lanes], 0.0)
            head = head + cw_ref[SHORT_CONV - 1 - s:SHORT_CONV - s, lanes] * prev
        return _silu(jnp.concatenate([head, acc[halo:]], axis=0))

    def l2n(a):
        return a * lax.rsqrt(jnp.sum(a * a, axis=-1, keepdims=True) + EPS)

    sm = sm_ref[0]
    lane = lax.broadcasted_iota(I32, sm.shape, 1)
    sp, logsig = _softplus_parts(sm + par_ref[1:2, :])
    vals = jnp.where(lane < LANE_G, _sigmoid(sm),
                     jnp.where(lane < LANE_F, par_ref[0:1, :] * sp,
                               jnp.where(lane < LANE_F + H_FOX, logsig, 0.0)))
    row = lax.broadcasted_iota(I32, (ts, ts), 0)
    colm = lax.broadcasted_iota(I32, (ts, ts), 1)
    log_chunk = CHUNK.bit_length() - 1
    causal = (row >= colm) & ((row >> log_chunk) == (colm >> log_chunk))

    hi = vals.astype(BF16)
    rem = vals - hi.astype(F32)
    mid = rem.astype(BF16)
    lo = (rem - mid.astype(F32)).astype(BF16)
    pieces = jnp.concatenate([hi, mid, lo], axis=-1)

    def cumsum(mask):
        y = jnp.dot(jnp.where(mask, 1.0, 0.0).astype(BF16), pieces, preferred_element_type=F32)
        return (y[:, :LANES] + y[:, LANES:2 * LANES]) + y[:, 2 * LANES:]

    ccum = cumsum(row >= colm) + carry_ref[...]
    gcum = cumsum(causal)
    carry_ref[...] = ccum[ts - 1:ts, :]
    gcum_t = gcum.T

    cl = ccum * LOG2E
    c_hi = cl.astype(BF16)
    c_rem = cl - c_hi.astype(F32)
    c_mid = c_rem.astype(BF16)
    c_lo = (c_rem - c_mid.astype(F32)).astype(BF16)
    c_pieces = jnp.concatenate([c_hi, c_mid, c_lo], axis=-1)
    pr = lax.broadcasted_iota(I32, (3 * LANES, LANES), 0)
    pc = lax.broadcasted_iota(I32, (3 * LANES, LANES), 1)
    src_lane, piece = pr & (LANES - 1), pr >> (LANES.bit_length() - 1)
    owned = (src_lane >= LANE_F) & (src_lane < LANE_F + H_FOX) & ((pc >> 3) == src_lane - LANE_F)
    place_q = jnp.where(owned & ((pc & 7) == piece), 1.0, 0.0).astype(BF16)
    place_k = jnp.where(owned & ((pc & 7) == piece + 3), -1.0, 0.0).astype(BF16)
    slot = lane & 7
    in_heads = lane < 8 * H_FOX
    ones_q = jnp.where(in_heads & (slot >= 3) & (slot < 6), 1.0, 0.0)
    ones_k = jnp.where(in_heads & (slot < 3), 1.0, 0.0)
    qaug_ref[0] = (jnp.dot(c_pieces, place_q, preferred_element_type=F32) + ones_q).astype(BF16)
    kaug_ref[0] = (jnp.dot(c_pieces, place_k, preferred_element_type=F32) + ones_k).astype(BF16)

    scale = DK_DN ** -0.5
    dn_norm = par_ref[2:3, :]

    pw = 2 * CHUNK
    prow = lax.broadcasted_iota(I32, (pw, pw), 0)
    pcol = lax.broadcasted_iota(I32, (pw, pw), 1)
    same = (prow >> log_chunk) == (pcol >> log_chunk)
    causal_p = (prow >= pcol) & same
    strict_p = (prow > pcol) & same
    levels = []
    s = 1
    while s < CHUNK:
        levels.append(((prow >> s.bit_length()) == (pcol >> s.bit_length()))
                      & ((prow & s) != 0) & ((pcol & s) == 0))
        s *= 2

    heads = range(H_DN)
    pairs = range(ts // pw)
    q, k, v = [], [], []
    for h in heads:
        q.append(l2n(conv_silu(h * DK_DN)))
        k.append(l2n(conv_silu(H_DN * DK_DN + h * DK_DN)))
        v.append(conv_silu(2 * H_DN * DK_DN + h * DV_DN))

    ctx = []
    for h in heads:
        for p in pairs:
            pr = slice(p * pw, (p + 1) * pw)
            gcol = gcum[pr, LANE_G + h:LANE_G + h + 1]
            grow = gcum_t[LANE_G + h:LANE_G + h + 1, pr]
            beta = vals[pr, LANE_BETA + h:LANE_BETA + h + 1]
            decay = jnp.where(causal_p, jnp.exp(jnp.where(causal_p, gcol - grow, 0.0)), 0.0)
            egc = jnp.exp(gcol)
            kp = k[h][pr]
            kb = kp * beta
            ctx.append(dict(
                h=h, p=p, gcol=gcol, kp=kp,
                a=jnp.where(strict_p, _mm(kb, kp, NT) * decay, 0.0),
                qk=jnp.where(causal_p, _mm(q[h][pr] * scale, kp, NT) * decay, 0.0),
                rhs=jnp.concatenate([v[h][pr] * beta, kb * egc], axis=-1),
                qg=q[h][pr] * (scale * egc)))

    for c in ctx:
        c["n"] = -jnp.where(levels[0], c["a"], 0.0)
    for level in levels[1:]:
        for c in ctx:
            m = jnp.where(level, c["a"], 0.0)
            c["y"] = m + _mm(c["n"], m)
        for c in ctx:
            c["n"] = c["n"] - (c["y"] + _mm(c["y"], c["n"]))
    for c in ctx:
        c["sol"] = c["rhs"] + _mm(c["n"], c["rhs"])

    state = [s_ref[h] for h in heads]
    for p in pairs:
        group = [c for c in ctx if c["p"] == p]
        v_prev = [None] * H_DN
        for ch in range(2):
            rows = slice(ch * CHUNK, (ch + 1) * CHUNK)
            out_rows = slice(p * pw + ch * CHUNK, p * pw + (ch + 1) * CHUNK)
            on_state = [_mm(jnp.concatenate([c["sol"][rows, DV_DN:], c["qg"][rows]], axis=0), state[c["h"]])
                        for c in group]
            for c, ws_qs in zip(group, on_state):
                h = c["h"]
                vn = c["sol"][rows, :DV_DN] - ws_qs[:CHUNK]
                v_pair = jnp.concatenate([vn, jnp.zeros_like(vn)] if ch == 0 else [v_prev[h], vn], axis=0)
                o = ws_qs[CHUNK:] + _mm(c["qk"][rows], v_pair)
                glast = c["gcol"][(ch + 1) * CHUNK - 1:(ch + 1) * CHUNK]
                state[h] = (state[h] * jnp.exp(glast)
                            + _mm(c["kp"][rows] * jnp.exp(glast - c["gcol"][rows]), vn, TN))
                v_prev[h] = vn
                o = o * lax.rsqrt(jnp.mean(o * o, axis=-1, keepdims=True) + EPS) * dn_norm
                za = za_ref[0, out_rows, h * DV_DN:(h + 1) * DV_DN].astype(F32)
                oa_ref[0, out_rows, h * DV_DN:(h + 1) * DV_DN] = (o * _silu(za)).astype(oa_ref.dtype)
    for h in heads:
        s_ref[h] = state[h]


def _delta_rule(qkv, za, small, conv_w, par, *, ts=256):
    b, t, _ = qkv.shape
    assert t % ts == 0 and ts % (2 * CHUNK) == 0
    blk = lambda width: pl.BlockSpec((1, ts, width), lambda bi, j: (bi, j, 0))
    fixed = lambda bi, j: (0, 0)
    return pl.pallas_call(
        functools.partial(_delta_body, ts=ts),
        grid=(b, t // ts),
        in_specs=[blk(W_DN), blk(H_DN * DV_DN), blk(LANES),
                  pl.BlockSpec(conv_w.shape, fixed), pl.BlockSpec(par.shape, fixed)],
        out_specs=[blk(H_DN * DV_DN), blk(LANES), blk(LANES)],
        out_shape=[jax.ShapeDtypeStruct((b, t, H_DN * DV_DN), BF16),
                   jax.ShapeDtypeStruct((b, t, LANES), BF16),
                   jax.ShapeDtypeStruct((b, t, LANES), BF16)],
        scratch_shapes=[pltpu.VMEM((3 * SUBLANES, W_DN), F32),
                        pltpu.VMEM((H_DN, DK_DN, DV_DN), F32),
                        pltpu.VMEM((1, LANES), F32)],
        compiler_params=_params("parallel", "arbitrary"),
        name="delta_rule",
    )(qkv, za, small, conv_w, par)


def _fox_body(q_ref, k_ref, v_ref, qa_ref, ka_ref, o_ref, m_ref, acc_ref, *, tq, tk):
    i = pl.program_id(1)
    m_ref[...] = jnp.full(m_ref.shape, -jnp.inf, F32)
    acc_ref[...] = jnp.zeros_like(acc_ref)
    head_lanes = [slice(h * D_FOX, (h + 1) * D_FOX) for h in range(H_FOX)]
    lane = lax.broadcasted_iota(I32, (tk, LANES), 1)
    own = [jnp.where((lane >> 3) == h, 1.0, 0.0).astype(BF16) for h in range(H_FOX)]
    ones = jnp.ones((tk, D_FOX), BF16)
    keep = lax.broadcasted_iota(I32, (tq, tk), 1) <= lax.broadcasted_iota(I32, (tq, tk), 0)

    def block(start, diag_offset):
        rows = slice(0 if diag_offset is None else diag_offset, tq)
        ka = ka_ref[0, pl.ds(start, tk), :]
        for h, lanes in enumerate(head_lanes):
            q_aug = jnp.concatenate([q_ref[0, rows, lanes], qa_ref[0, rows, :]], axis=1)
            k_aug = jnp.concatenate([k_ref[0, pl.ds(start, tk), lanes], ka * own[h]], axis=1)
            s = lax.dot_general(q_aug, k_aug, NT, preferred_element_type=F32)
            if diag_offset is not None:
                s = jnp.where(keep[:tq - diag_offset], s, -jnp.inf)
            m_prev = m_ref[h, rows]
            m_next = jnp.maximum(m_prev, jnp.max(s, axis=-1, keepdims=True))
            p = jnp.exp2(s - jnp.concatenate([m_next] * (tk // LANES), axis=1))
            alpha = jnp.exp2(m_prev - m_next)
            v_aug = jnp.concatenate([v_ref[0, pl.ds(start, tk), lanes], ones], axis=1)
            acc_ref[h, rows] = (jnp.concatenate([alpha, alpha], axis=1) * acc_ref[h, rows]
                                + jnp.dot(p.astype(BF16), v_aug, preferred_element_type=F32))
            m_ref[h, rows] = m_next

    def full_block(jb, carry):
        block(pl.multiple_of(jb * tk, tk), None)
        return carry

    lax.fori_loop(0, i * (tq // tk), full_block, 0)
    for d in range(tq // tk):
        block(pl.multiple_of(i * tq + d * tk, tk), d * tk)
    for h, lanes in enumerate(head_lanes):
        acc = acc_ref[h]
        o_ref[0, :, lanes] = (acc[:, :D_FOX] / acc[:, D_FOX:]).astype(o_ref.dtype)


def _fox_attention(qkv, qaug, kaug, *, tq=1024, tk=512):
    b, t, _ = qkv.shape
    hd = H_FOX * D_FOX
    assert t % tq == 0 and tq % tk == 0 and tk % LANES == 0
    return pl.pallas_call(
        functools.partial(_fox_body, tq=tq, tk=tk),
        grid=(b, t // tq),
        in_specs=[pl.BlockSpec((1, tq, hd), lambda bi, i: (bi, i, 0)),
                  pl.BlockSpec((1, t, hd), lambda bi, i: (bi, 0, 1)),
                  pl.BlockSpec((1, t, hd), lambda bi, i: (bi, 0, 2)),
                  pl.BlockSpec((1, tq, LANES), lambda bi, i: (bi, i, 0)),
                  pl.BlockSpec((1, t, LANES), lambda bi, i: (bi, 0, 0))],
        out_specs=pl.BlockSpec((1, tq, hd), lambda bi, i: (bi, i, 0)),
        out_shape=jax.ShapeDtypeStruct((b, t, hd), BF16),
        scratch_shapes=[pltpu.VMEM((H_FOX, tq, LANES), F32), pltpu.VMEM((H_FOX, tq, 2 * D_FOX), F32)],
        compiler_params=_params("parallel", "arbitrary"),
        name="fox_attention",
    )(qkv, qkv, qkv, qaug, kaug)


def _merge_body(oa_ref, ob_ref, zc_ref, gates_ref, x_ref, wa_ref, wb_ref, wc_ref, wo_ref, g_ref,
                wr_ref, br_ref, xn_ref, hp_ref, route_ref, meta_ref, cnt_ref, run_ref, *, tm):
    step = pl.program_id(0)
    d = x_ref.shape[-1]

    @pl.when(step == 0)
    def _():
        run_ref[...] = jnp.zeros_like(run_ref)

    merged = None
    for idx, (m_ref, w_ref) in enumerate(((oa_ref, wa_ref), (ob_ref, wb_ref), (zc_ref, wc_ref))):
        y = jnp.dot(m_ref[...], w_ref[...], preferred_element_type=F32)
        term = _sigmoid(gates_ref[:, idx * d:(idx + 1) * d].astype(F32)) * y
        merged = term if merged is None else merged + term
    xn = x_ref[...] + jnp.dot(merged.astype(BF16), wo_ref[...], preferred_element_type=F32)
    xn_ref[...] = xn
    h2 = xn * lax.rsqrt(jnp.mean(xn * xn, axis=-1, keepdims=True) + EPS) * g_ref[...]

    hp_ref[...] = _pack_bf16_pairs(h2)

    logits = lax.dot_general(wr_ref[...], h2.astype(BF16), NT, preferred_element_type=F32) + br_ref[...]
    big = jnp.int32(LANES)
    gl = logits[LANE_GRP:LANE_GRP + N_GROUPS]
    g_row = lax.broadcasted_iota(I32, gl.shape, 0)
    gmax = jnp.max(gl, axis=0, keepdims=True)
    grp = jnp.min(jnp.where(gl == gmax, g_row, big), axis=0, keepdims=True)
    p_grp = 1.0 / jnp.sum(jnp.exp(gl - gmax), axis=0, keepdims=True)
    el_all = logits[LANE_EXP:LANE_EXP + N_EXPERTS]
    e_row = lax.broadcasted_iota(I32, el_all.shape, 0)
    el = jnp.where((e_row >> (EXPERTS_PER_GROUP.bit_length() - 1)) == grp, el_all, -jnp.inf)
    v0 = jnp.max(el, axis=0, keepdims=True)
    i0 = jnp.min(jnp.where(el == v0, e_row, big), axis=0, keepdims=True)
    el1 = jnp.where(e_row == i0, -jnp.inf, el)
    v1 = jnp.max(el1, axis=0, keepdims=True)
    i1 = jnp.min(jnp.where(el1 == v1, e_row, big), axis=0, keepdims=True)
    e1 = jnp.exp(v1 - v0)
    gate0 = p_grp / (1.0 + e1)
    gate1 = p_grp * e1 / (1.0 + e1)

    hot0 = e_row == i0
    hot1 = e_row == i1
    onehot = jnp.where(hot0 | hot1, 1.0, 0.0)
    tiles = [onehot[:, t * LANES:(t + 1) * LANES] for t in range(tm // LANES)]
    ri = lax.broadcasted_iota(I32, (LANES, LANES), 0)
    cj = lax.broadcasted_iota(I32, (LANES, LANES), 1)
    in_tile = _mm(jnp.concatenate(tiles, axis=0), jnp.where(ri < cj, 1.0, 0.0))
    before = run_ref[...]
    prefix = []
    for t, tile in enumerate(tiles):
        prefix.append(in_tile[t * N_EXPERTS:(t + 1) * N_EXPERTS] + before)
        before = before + jnp.sum(tile, axis=1, keepdims=True)
    prefix = jnp.concatenate(prefix, axis=1)
    run_ref[...] = before
    cnt_ref[...] = jnp.broadcast_to(before, cnt_ref.shape)
    rank0 = jnp.sum(jnp.where(hot0, prefix, 0.0), axis=0, keepdims=True)
    rank1 = jnp.sum(jnp.where(hot1, prefix, 0.0), axis=0, keepdims=True)

    rows = {R_GATE0: gate0, R_GATE1: gate1, R_EID0: i0.astype(F32), R_EID1: i1.astype(F32),
            R_RANK0: rank0, R_RANK1: rank1}
    fields = jnp.concatenate([rows.get(r, jnp.zeros_like(gate0)) for r in range(SUBLANES)], axis=0)
    meta_ref[...] = fields.astype(I32)
    route_ref[...] = jnp.concatenate([fields, jnp.zeros((LANES - SUBLANES, tm), F32)], axis=0).T


def _merge(oa, ob, zc, gates, x2, wa, wb, wc, wo, g, wr, br, *, tm=1024):
    n, d = x2.shape
    assert n % tm == 0
    row = lambda i: (i, 0)
    fixed = lambda i: (0, 0)
    full = lambda a: pl.BlockSpec(a.shape, fixed)
    return pl.pallas_call(
        functools.partial(_merge_body, tm=tm),
        grid=(n // tm,),
        in_specs=[pl.BlockSpec((tm, oa.shape[1]), row), pl.BlockSpec((tm, ob.shape[1]), row),
                  pl.BlockSpec((tm, zc.shape[1]), row), pl.BlockSpec((tm, 3 * d), row),
                  pl.BlockSpec((tm, d), row),
                  full(wa), full(wb), full(wc), full(wo), full(g), full(wr), full(br)],
        out_specs=[pl.BlockSpec((tm, d), row), pl.BlockSpec((tm, d // 2), row),
                   pl.BlockSpec((tm, LANES), row), pl.BlockSpec((SUBLANES, tm), lambda i: (0, i)),
                   pl.BlockSpec((N_EXPERTS, LANES), fixed)],
        out_shape=[jax.ShapeDtypeStruct((n, d), F32), jax.ShapeDtypeStruct((n, d // 2), U32),
                   jax.ShapeDtypeStruct((n, LANES), F32), jax.ShapeDtypeStruct((SUBLANES, n), I32),
                   jax.ShapeDtypeStruct((N_EXPERTS, LANES), F32)],
        scratch_shapes=[pltpu.VMEM((N_EXPERTS, 1), F32)],
        compiler_params=_params("arbitrary"),
        name="merge_router",
    )(oa, ob, zc, gates, x2, wa, wb, wc, wo, g, wr, br)


def _sc_mesh():
    return plsc.VectorSubcoreMesh(core_axis_name="c", subcore_axis_name="s")


def _sc_worker_base(per_worker):
    return (lax.axis_index("s") * SC_CORES + lax.axis_index("c")) * per_worker


def _sc_row_scatter(src, idx, out_rows, *, chunk=128):
    n, d = src.shape
    per_worker = n // (SC_CORES * SC_SUBCORES)
    chunk = min(chunk, per_worker)
    assert idx.shape == (2 * n,) and n % (SC_CORES * SC_SUBCORES) == 0 and per_worker % chunk == 0

    @functools.partial(
        pl.kernel, mesh=_sc_mesh(), out_type=jax.ShapeDtypeStruct((out_rows, d), src.dtype),
        scratch_types=[pltpu.VMEM((chunk,), I32), pltpu.VMEM((chunk, d), src.dtype), pltpu.SemaphoreType.DMA])
    def scatter(src_hbm, idx_hbm, out_hbm, idx_v, rows_v, sem):
        base = _sc_worker_base(per_worker)

        @pl.loop(0, per_worker // chunk)
        def _(step):
            off = base + step * chunk
            pltpu.sync_copy(src_hbm.at[pl.ds(off, chunk)], rows_v)
            for slot in range(2):
                pltpu.sync_copy(idx_hbm.at[pl.ds(slot * n + off, chunk)], idx_v)
                pltpu.async_copy(rows_v, out_hbm.at[idx_v], sem).wait()

    return scatter(src, idx)


def _pack_bf16_pairs(x):
    half = x.shape[-1] // 2
    lo = pltpu.bitcast(x[:, :half].astype(BF16).astype(F32), U32) >> 16
    hi = pltpu.bitcast(x[:, half:].astype(BF16).astype(F32), U32) & jnp.uint32(0xFFFF0000)
    return lo | hi


def _unpack_bf16_pairs(xp):
    return pltpu.bitcast(xp << 16, F32), pltpu.bitcast(xp & jnp.uint32(0xFFFF0000), F32)


def _expert_body(be_ref, valid_ref, nu_ref, xs_ref, w1_ref, w3_ref, w2_ref, ys_ref):
    del be_ref
    step = pl.program_id(0)
    live = step < nu_ref[0]

    @pl.when(jnp.logical_not(live))
    def _():
        ys_ref[...] = jnp.zeros_like(ys_ref)

    @pl.when(live)
    def _():
        row = lax.broadcasted_iota(I32, xs_ref.shape, 0)
        xp = jnp.where(row < valid_ref[step], xs_ref[...], jnp.uint32(0))
        half = xp.shape[-1]
        lo, hi = (part.astype(BF16) for part in _unpack_bf16_pairs(xp))

        def up(w_ref):
            return (jnp.dot(lo, w_ref[0, :half, :].astype(BF16), preferred_element_type=F32)
                    + jnp.dot(hi, w_ref[0, half:, :].astype(BF16), preferred_element_type=F32))

        act = (_silu(up(w1_ref)) * up(w3_ref)).astype(BF16)
        ys_ref[...] = _pack_bf16_pairs(jnp.dot(act, w2_ref[0].astype(BF16), preferred_element_type=F32))


def _experts(blk_eid, blk_valid, n_used, xs, w1, w3, w2, *, rb):
    p, half = xs.shape
    d = 2 * half
    nb = p // rb
    used = lambda i, be, bv, nu: jnp.maximum(jnp.minimum(i, nu[0] - 1), 0)
    wmap = lambda i, be, bv, nu: (be[used(i, be, bv, nu)], 0, 0)
    grid_spec = pltpu.PrefetchScalarGridSpec(
        num_scalar_prefetch=3,
        grid=(nb,),
        in_specs=[pl.BlockSpec((rb, half), lambda i, be, bv, nu: (used(i, be, bv, nu), 0)),
                  pl.BlockSpec((1, d, D_EXPERT), wmap), pl.BlockSpec((1, d, D_EXPERT), wmap),
                  pl.BlockSpec((1, D_EXPERT, d), wmap)],
        out_specs=pl.BlockSpec((rb, half), lambda i, be, bv, nu: (i, 0)),
    )
    return pl.pallas_call(
        _expert_body,
        grid_spec=grid_spec,
        out_shape=jax.ShapeDtypeStruct((p, half), U32),
        compiler_params=_params("arbitrary"),
        name="moe_experts",
    )(blk_eid, blk_valid, n_used, xs, w1, w3, w2)


def _sc_row_gather(table, idx, *, chunk=128):
    rows, d = idx.shape[0], table.shape[1]
    per_worker = rows // (SC_CORES * SC_SUBCORES)
    chunk = min(chunk, per_worker)
    assert rows % (SC_CORES * SC_SUBCORES) == 0 and per_worker % chunk == 0

    @functools.partial(
        pl.kernel, mesh=_sc_mesh(), out_type=jax.ShapeDtypeStruct((rows, d), table.dtype),
        scratch_types=[pltpu.VMEM((chunk,), I32), pltpu.VMEM((chunk, d), table.dtype), pltpu.SemaphoreType.DMA])
    def gather(table_hbm, idx_hbm, out_hbm, idx_v, rows_v, sem):
        base = _sc_worker_base(per_worker)

        @pl.loop(0, per_worker // chunk)
        def _(step):
            off = base + step * chunk
            pltpu.sync_copy(idx_hbm.at[pl.ds(off, chunk)], idx_v)
            pltpu.async_copy(table_hbm.at[idx_v], rows_v, sem).wait()
            pltpu.sync_copy(rows_v, out_hbm.at[pl.ds(off, chunk)])

    return gather(table, idx)


def _combine_rows(r0_ref, r1_ref, x_ref, route_ref):
    route = route_ref[...]
    half = r0_ref.shape[-1]
    lo0, hi0 = _unpack_bf16_pairs(r0_ref[...])
    lo1, hi1 = _unpack_bf16_pairs(r1_ref[...])
    g0, g1 = route[:, R_GATE0:R_GATE0 + 1], route[:, R_GATE1:R_GATE1 + 1]
    return jnp.concatenate([x_ref[:, :half] + g0 * lo0 + g1 * lo1,
                            x_ref[:, half:] + g0 * hi0 + g1 * hi1], axis=1)


def _final_body(r0_ref, r1_ref, x_ref, route_ref, g_ref, o_ref):
    out = _combine_rows(r0_ref, r1_ref, x_ref, route_ref)
    o_ref[...] = out * lax.rsqrt(jnp.mean(out * out, axis=-1, keepdims=True) + EPS) * g_ref[...]


def _final_combine(rows2, xn, route, g, *, tm):
    n, d = xn.shape
    nt = n // tm
    row = lambda i: (i, 0)
    return pl.pallas_call(
        _final_body,
        grid=(nt,),
        in_specs=[pl.BlockSpec((tm, d // 2), row), pl.BlockSpec((tm, d // 2), lambda i: (i + nt, 0)),
                  pl.BlockSpec((tm, d), row), pl.BlockSpec((tm, LANES), row),
                  pl.BlockSpec((1, d), lambda i: (0, 0))],
        out_specs=pl.BlockSpec((tm, d), row),
        out_shape=jax.ShapeDtypeStruct((n, d), F32),
        compiler_params=_params("parallel"),
        name="moe_combine",
    )(rows2, rows2, xn, route, g)


def _moe_rows(hp, meta, counts, w1, w3, w2, *, layer, rb):
    n = hp.shape[0]
    cnt = counts[:, 0].astype(I32)
    nblk = (cnt + rb - 1) // rb
    bend = jnp.cumsum(nblk)
    pstart = (bend - nblk) * rb
    nb = (2 * n) // rb + N_EXPERTS
    n_used = bend[-1:].astype(I32)
    blk = jnp.arange(nb, dtype=I32)
    blk_eid = jnp.minimum(jnp.sum(bend[None, :] <= blk[:, None], axis=1), N_EXPERTS - 1).astype(I32)
    experts = jnp.arange(N_EXPERTS, dtype=I32)
    mine = blk_eid[:, None] == experts[None, :]
    blk_valid = jnp.clip(jnp.sum(jnp.where(mine, (cnt + pstart)[None, :], 0), axis=1) - blk * rb, 0, rb)
    eid = meta[R_EID0:R_EID1 + 1]
    first = jnp.sum(jnp.where(eid[None] == experts[:, None, None], pstart[:, None, None], 0), axis=0)
    dest = (first + meta[R_RANK0:R_RANK1 + 1]).reshape(-1)

    xs = _sc_row_scatter(hp, dest, nb * rb)
    ys = _experts(blk_eid + layer * N_EXPERTS, blk_valid.astype(I32), n_used, xs, w1, w3, w2, rb=rb)
    return _sc_row_gather(ys, dest)


def _regroup_body(w_ref, wm_ref, ws_ref, *, bounds, q_scale):
    small0, fox0, fb0, conv0, end = bounds
    w = w_ref[0]
    fox_w = D_FOX * H_FOX
    main = jnp.concatenate([w[:, :small0], w[:, fox0:fox0 + fox_w] * q_scale, w[:, fox0 + fox_w:fb0],
                            w[:, conv0:end]], axis=1)
    narrow = jnp.concatenate([w[:, small0:fox0], w[:, fb0:conv0]], axis=1)
    wm_ref[...] = main.astype(BF16)
    ws_ref[...] = jnp.concatenate(
        [narrow, jnp.zeros((narrow.shape[0], LANES - narrow.shape[1]), F32)], axis=1).astype(BF16)


def _regroup_w_in(w_in, layer, *, q_scale, tr=128):
    _, d, d_in = w_in.shape
    small0 = 2 * H_DN * DK_DN + 2 * H_DN * DV_DN
    fox0 = small0 + 2 * H_DN
    fb0 = fox0 + W_FOX
    conv0 = fb0 + H_FOX
    width = d_in - (conv0 - fb0) - (fox0 - small0)
    assert d % tr == 0 and width % LANES == 0
    return pl.pallas_call(
        functools.partial(_regroup_body, bounds=(small0, fox0, fb0, conv0, d_in), q_scale=q_scale),
        grid=(d // tr,),
        in_specs=[pl.BlockSpec((1, tr, d_in), lambda i: (layer, i, 0))],
        out_specs=[pl.BlockSpec((tr, width), lambda i: (i, 0)), pl.BlockSpec((tr, LANES), lambda i: (i, 0))],
        out_shape=[jax.ShapeDtypeStruct((d, width), BF16), jax.ShapeDtypeStruct((d, LANES), BF16)],
        compiler_params=_params("parallel"),
        name="regroup_w_in",
    )(w_in)


def _regroup_t_body(a_ref, b_ref, c_ref, n1_ref, n2_ref, *out_refs, edges, q_scale):
    j = pl.program_id(0)
    a_end, q_end, b_end = edges
    depth = len(out_refs) // 2
    small_refs, out_refs = out_refs[depth:], out_refs[:depth]

    @pl.when(j == 0)
    def _():
        for l, ws_ref in enumerate(small_refs):
            cols = jnp.concatenate([n1_ref[:, l, :], n2_ref[:, l, :]], axis=0)
            pad = jnp.zeros((LANES - cols.shape[0], cols.shape[1]), F32)
            ws_ref[...] = jnp.concatenate([cols, pad], axis=0).T.astype(ws_ref.dtype)

    def emit(src_ref, scale):
        for l, out_ref in enumerate(out_refs):
            block = src_ref[:, l, :].T
            out_ref[...] = (block if scale is None else block * scale).astype(out_ref.dtype)

    pl.when(j < a_end)(lambda: emit(a_ref, None))
    pl.when((j >= a_end) & (j < q_end))(lambda: emit(b_ref, q_scale))
    pl.when((j >= q_end) & (j < b_end))(lambda: emit(b_ref, None))
    pl.when(j >= b_end)(lambda: emit(c_ref, None))


def _regroup_w_in_t(w_in, *, q_scale, tc=LANES):
    depth, d, d_in = w_in.shape
    small0 = 2 * H_DN * DK_DN + 2 * H_DN * DV_DN
    fox0 = small0 + 2 * H_DN
    fb0 = fox0 + W_FOX
    conv0 = fb0 + H_FOX
    wt = jnp.transpose(w_in, (2, 0, 1))
    views = (wt[:small0], wt[fox0:fb0], wt[conv0:])
    blocks = [v.shape[0] // tc for v in views]
    assert all(v.shape[0] % tc == 0 for v in views)
    a_end, b_end = blocks[0], blocks[0] + blocks[1]
    q_end = a_end + (H_FOX * D_FOX) // tc
    width = sum(v.shape[0] for v in views)
    clamp = lambda j, lo, nblk: jnp.clip(j - lo, 0, nblk - 1)
    narrow = (wt[small0:fox0], wt[fb0:conv0])
    whole = lambda a: pl.BlockSpec(a.shape, lambda j: (0, 0, 0))
    outs = pl.pallas_call(
        functools.partial(_regroup_t_body, edges=(a_end, q_end, b_end), q_scale=q_scale),
        grid=(width // tc,),
        in_specs=[pl.BlockSpec((tc, depth, d), lambda j: (clamp(j, 0, blocks[0]), 0, 0)),
                  pl.BlockSpec((tc, depth, d), lambda j: (clamp(j, a_end, blocks[1]), 0, 0)),
                  pl.BlockSpec((tc, depth, d), lambda j: (clamp(j, b_end, blocks[2]), 0, 0)),
                  whole(narrow[0]), whole(narrow[1])],
        out_specs=([pl.BlockSpec((d, tc), lambda j: (0, j)) for _ in range(depth)]
                   + [pl.BlockSpec((d, LANES), lambda j: (0, 0)) for _ in range(depth)]),
        out_shape=([jax.ShapeDtypeStruct((d, width), BF16) for _ in range(depth)]
                   + [jax.ShapeDtypeStruct((d, LANES), BF16) for _ in range(depth)]),
        compiler_params=_params("arbitrary"),
        name="regroup_w_in",
    )(*views, *narrow)
    return outs[:depth], outs[depth:]


def _lane_row(pairs):
    row = jnp.zeros((LANES,), F32)
    for off, vec in pairs:
        row = row.at[off:off + vec.shape[0]].set(vec.astype(F32))
    return row


def kernel(x, norm_mix, w_in, conv_qkv, dn_a_log, dn_dt_bias, dn_norm, fox_bias, conv_dw, conv_dw_b,
           conv_ln_g, conv_ln_b, w_a, w_b, w_c, w_out, norm_ffn, router_group_w, router_group_b,
           router_expert_w, router_expert_b, expert_w1, expert_w3, expert_w2, norm_final):
    b, t, d = x.shape
    n = b * t
    depth = w_in.shape[0]
    tm_final, rb = 512, 512

    x_src = x.reshape(n, d)
    w1_all = expert_w1.reshape(depth * N_EXPERTS, d, D_EXPERT)
    w3_all = expert_w3.reshape(depth * N_EXPERTS, d, D_EXPERT)
    w2_all = expert_w2.reshape(depth * N_EXPERTS, D_EXPERT, d)
    w_mains, w_smalls = _regroup_w_in_t(w_in, q_scale=D_FOX ** -0.5 * LOG2E)
    for l in range(depth):
        w_main, w_small = w_mains[l], w_smalls[l]
        vec = jnp.stack([conv_dw_b[l], conv_ln_g[l], conv_ln_b[l]] + [jnp.zeros((C_CONV,), F32)] * 5)
        outs = _in_proj(x_src, norm_mix[l][None, :], w_main, w_small, conv_dw[l], vec, seq_len=t)
        x2 = outs[0] if l > 0 else x_src
        dn_qkv, za_p, fox_qkv, zc, gates, small = outs[-6:]

        par = jnp.stack([_lane_row([(LANE_G, -jnp.exp(dn_a_log[l]))]),
                         _lane_row([(LANE_G, dn_dt_bias[l]), (LANE_F, fox_bias[l])]),
                         _lane_row([(0, dn_norm[l])])] + [jnp.zeros((LANES,), F32)] * 5)
        oa, qaug, kaug = _delta_rule(dn_qkv.reshape(b, t, -1), za_p.reshape(b, t, -1),
                                     small.reshape(b, t, LANES), conv_qkv[l], par)
        ob = _fox_attention(fox_qkv.reshape(b, t, -1), qaug, kaug)

        w_r = jnp.concatenate([router_group_w[l], jnp.zeros((d, LANE_EXP - N_GROUPS), F32),
                               router_expert_w[l], jnp.zeros((d, LANES - LANE_EXP - N_EXPERTS), F32)], axis=1)
        b_r = _lane_row([(LANE_GRP, router_group_b[l]), (LANE_EXP, router_expert_b[l])])[None, :]
        xn, hp, route, meta, counts = _merge(
            oa.reshape(n, -1), ob.reshape(n, -1), zc.reshape(n, -1), gates, x2,
            w_a[l].astype(BF16), w_b[l].astype(BF16), w_c[l].astype(BF16), w_out[l].astype(BF16),
            norm_ffn[l][None, :], w_r.T.astype(BF16), b_r.reshape(LANES, 1))

        x_src = (_moe_rows(hp, meta, counts, w1_all, w3_all, w2_all, layer=l, rb=rb), xn, route)
    return _final_combine(*x_src, norm_final[None, :], tm=tm_final).reshape(b, t, d)
```

```python
import functools

import jax
import jax.numpy as jnp
from jax import lax
from jax.experimental import pallas as pl
from jax.experimental.pallas import tpu as pltpu
from jax.experimental.pallas import tpu_sc as plsc

F32 = jnp.float32
BF16 = jnp.bfloat16
U32 = jnp.uint32
I32 = jnp.int32

EPS = 1e-6
LOG2E = 1.4426950408889634
LANES = 128
SUBLANES = 8
SC_CORES, SC_SUBCORES = 2, 16
H_DN, DK_DN, DV_DN = 4, 128, 128
SHORT_CONV = 4
CHUNK = 64
H_FOX, D_FOX = 4, 128
C_CONV = 512
CONV_WIDTH = 31
N_GROUPS, EXPERTS_PER_GROUP = 4, 8
N_EXPERTS = N_GROUPS * EXPERTS_PER_GROUP
D_EXPERT = 256

W_DN = 3 * H_DN * DK_DN
W_FOX = 3 * H_FOX * D_FOX
VMEM_LIMIT = 56 * 1024 * 1024

LANE_BETA, LANE_G, LANE_F = 0, 4, 8
LANE_GRP, LANE_EXP = 0, 32
R_GATE0, R_GATE1, R_EID0, R_EID1, R_RANK0, R_RANK1 = 0, 1, 2, 3, 4, 5

NN = (((1,), (0,)), ((), ()))
NT = (((1,), (1,)), ((), ()))
TN = (((0,), (0,)), ((), ()))


def _mm(a, b, dims=NN):
    return lax.dot_general(a.astype(BF16), b.astype(BF16), dims, preferred_element_type=F32)


def _sigmoid(x):
    return 0.5 * jnp.tanh(0.5 * x) + 0.5


def _silu(x):
    half = 0.5 * x
    return half * jnp.tanh(half) + half


def _params(*sem):
    return pltpu.CompilerParams(dimension_semantics=sem, vmem_limit_bytes=VMEM_LIMIT)


def _in_proj_body(*refs, col_chunk, combine, tiles_per_seq, hrows, rchunk):
    if combine:
        r0_ref, r1_ref, xn_ref, route_ref, g_ref, w_ref, ws_ref, dw_ref, vec_ref, x_out_ref = refs[:10]
        x = _combine_rows(r0_ref, r1_ref, xn_ref, route_ref)
        x_out_ref[...] = x
        rest = refs[10:]
    else:
        x_ref, g_ref, w_ref, ws_ref, dw_ref, vec_ref = refs[:6]
        x = x_ref[...]
        rest = refs[6:]
    dn_ref, za_ref, fox_ref, zc_ref, gates_ref, small_ref, z_ref, zs_ref, y_ref = rest
    tm = x.shape[0]
    h = x * lax.rsqrt(jnp.mean(x * x, axis=-1, keepdims=True) + EPS) * g_ref[...]
    hb = h.astype(BF16)

    def project(col, width, hold=None):
        lhs = hb
        if hold is not None:
            zero = jnp.minimum(jnp.abs(hold), 0.0).astype(BF16)
            head = jnp.concatenate([hb[0:rchunk, 0:LANES] + zero, hb[0:rchunk, LANES:]], axis=1)
            lhs = jnp.concatenate([head, hb[rchunk:]], axis=0)
        return jnp.dot(lhs, w_ref[:, col:col + width], preferred_element_type=F32)

    starts, col = {}, 0
    for name, width in (("dn", dn_ref.shape[-1]), ("za", za_ref.shape[-1]), ("fox", fox_ref.shape[-1]),
                        ("conv", 2 * C_CONV), ("gates", gates_ref.shape[-1])):
        starts[name], col = col, col + width

    first = pl.program_id(0) % tiles_per_seq == 0
    col = starts["conv"]

    @pl.when(first)
    def _():
        z_ref[0:hrows, :] = jnp.zeros((hrows, C_CONV), F32)

    @pl.when(jnp.logical_not(first))
    def _():
        z_ref[0:hrows, :] = z_ref[tm:tm + hrows, :]

    z_ref[hrows:hrows + tm, :] = project(col, C_CONV) * _sigmoid(project(col + C_CONV, C_CONV))
    sub = SUBLANES
    span = tm + hrows - sub
    for phase in range(1, sub):
        zs_ref[phase - 1, 0:span, :] = z_ref[phase:phase + span, :]

    def conv_unit(lg, r0, after):
        lanes = slice(lg * LANES, (lg + 1) * LANES)
        acc = vec_ref[0:1, lanes] + jnp.minimum(jnp.abs(after[0:rchunk, 0:LANES]), 0.0)
        for k in range(CONV_WIDTH):
            off = hrows - (CONV_WIDTH - 1) + k
            phase, base = off % sub, off - off % sub + r0
            if phase == 0:
                tap = z_ref[base:base + rchunk, lanes]
            else:
                tap = zs_ref[phase - 1, base:base + rchunk, lanes]
            acc = acc + dw_ref[k:k + 1, lanes] * tap
        y_ref[r0:r0 + rchunk, lanes] = acc
        return acc

    units = [(lg, r0) for lg in range(C_CONV // LANES) for r0 in range(0, tm, rchunk)]
    chunks = [(ref, starts[name], c) for name, ref in (("dn", dn_ref), ("za", za_ref), ("fox", fox_ref),
                                                       ("gates", gates_ref))
              for c in range(0, ref.shape[-1], col_chunk)]
    share = [[] for _ in chunks]
    for u, unit in enumerate(units):
        share[u * len(chunks) // len(units)].append(unit)
    last = []
    for idx, ((ref, start, c), mine) in enumerate(zip(chunks, share)):
        res = project(start + c, col_chunk, hold=last[idx - 2] if idx >= 2 else None)
        ref[:, c:c + col_chunk] = res.astype(ref.dtype)
        tile = None
        for lg, r0 in mine:
            tile = conv_unit(lg, r0, res)
        last.append(tile)
    small_ref[...] = jnp.dot(hb, ws_ref[...], preferred_element_type=F32)

    y = y_ref[...]
    mu = jnp.mean(y, axis=-1, keepdims=True)
    yc = y - mu
    var = jnp.mean(yc * yc, axis=-1, keepdims=True)
    yn = yc * lax.rsqrt(var + EPS) * vec_ref[1:2, :] + vec_ref[2:3, :]
    zc_ref[...] = _silu(yn).astype(zc_ref.dtype)


def _in_proj(x_src, g, w, ws, dw, vec, *, seq_len, tm=512, col_chunk=512, hrows=32, rchunk=32):
    combine = isinstance(x_src, tuple)
    n, d = (x_src[1] if combine else x_src).shape
    nt = n // tm
    widths = (W_DN, H_DN * DV_DN, W_FOX, C_CONV, 3 * d)
    assert w.shape == (d, sum(widths) + C_CONV) and n % tm == 0 and seq_len % tm == 0
    assert hrows >= CONV_WIDTH - 1 and hrows % SUBLANES == 0 and tm % rchunk == 0
    row = lambda i: (i, 0)
    fixed = lambda i: (0, 0)
    out_shape = [jax.ShapeDtypeStruct((n, wd), BF16) for wd in widths] + [jax.ShapeDtypeStruct((n, LANES), F32)]
    out_specs = [pl.BlockSpec((tm, wd), row) for wd in widths] + [pl.BlockSpec((tm, LANES), row)]
    if combine:
        rows2, xn, route = x_src
        args = (rows2, rows2, xn, route)
        in_specs = [pl.BlockSpec((tm, d // 2), row), pl.BlockSpec((tm, d // 2), lambda i: (i + nt, 0)),
                    pl.BlockSpec((tm, d), row), pl.BlockSpec((tm, LANES), row)]
        out_shape = [jax.ShapeDtypeStruct((n, d), F32)] + out_shape
        out_specs = [pl.BlockSpec((tm, d), row)] + out_specs
    else:
        args = (x_src,)
        in_specs = [pl.BlockSpec((tm, d), row)]
    return pl.pallas_call(
        functools.partial(_in_proj_body, col_chunk=col_chunk, combine=combine, tiles_per_seq=seq_len // tm,
                          hrows=hrows, rchunk=rchunk),
        grid=(nt,),
        in_specs=in_specs + [pl.BlockSpec((1, d), fixed),
                             pl.BlockSpec(w.shape, fixed, pipeline_mode=pl.Buffered(1)),
                             pl.BlockSpec(ws.shape, fixed, pipeline_mode=pl.Buffered(1)),
                             pl.BlockSpec(dw.shape, fixed), pl.BlockSpec(vec.shape, fixed)],
        out_specs=out_specs,
        out_shape=out_shape,
        scratch_shapes=[pltpu.VMEM((tm + hrows, C_CONV), F32),
                        pltpu.VMEM((SUBLANES - 1, tm + hrows, C_CONV), F32),
                        pltpu.VMEM((tm, C_CONV), F32)],
        compiler_params=_params("arbitrary"),
        name="in_proj",
    )(*args, g, w, ws, dw, vec)


def _softplus_parts(z):
    t = jnp.log1p(jnp.exp(-jnp.abs(z)))
    return jnp.maximum(z, 0.0) + t, -(jnp.maximum(-z, 0.0) + t)


def _delta_body(qkv_ref, za_ref, sm_ref, cw_ref, par_ref, oa_ref, qaug_ref, kaug_ref,
                xs_ref, s_ref, carry_ref, *, ts):
    j = pl.program_id(1)
    halo = SUBLANES
    pack = 2 * SUBLANES

    @pl.when(j == 0)
    def _():
        xs_ref[0:halo, :] = jnp.zeros((halo, W_DN), F32)
        s_ref[...] = jnp.zeros_like(s_ref)
        carry_ref[...] = jnp.zeros_like(carry_ref)

    @pl.when(j > 0)
    def _():
        xs_ref[0:halo, :] = xs_ref[2 * halo:3 * halo, :]

    xb = qkv_ref[0]
    xs_ref[halo:2 * halo, :] = qkv_ref[0, 0:pack, :].astype(F32)[0:halo]
    xs_ref[2 * halo:3 * halo, :] = qkv_ref[0, ts - pack:ts, :].astype(F32)[pack - halo:pack]

    lag = lax.broadcasted_iota(I32, (ts, ts), 0) - lax.broadcasted_iota(I32, (ts, ts), 1)
    shifts = jnp.concatenate([jnp.where(lag == s, 1.0, 0.0).astype(BF16) for s in range(1, SHORT_CONV)], axis=0)
    stacked = jnp.dot(shifts, xb, preferred_element_type=F32)
    shifted = [stacked[(s - 1) * ts:s * ts] for s in range(1, SHORT_CONV)]
    head_row = lax.broadcasted_iota(I32, (halo, LANES), 0)

    def conv_silu(lane0):
        lanes = slice(lane0, lane0 + LANES)
        acc = cw_ref[SHORT_CONV - 1:SHORT_CONV, lanes] * xb[:, lanes].astype(F32)
        for s in range(1, SHORT_CONV):
            acc = acc + cw_ref[SHORT_CONV - 1 - s:SHORT_CONV - s, lanes] * shifted[s - 1][:, lanes]
        head = acc[0:halo]
        for s in range(1, SHORT_CONV):
            prev = jnp.where(head_row < s, xs_ref[halo - s:2 * halo - s, lanes], 0.0)
            head = head + cw_ref[SHORT_CONV - 1 - s:SHORT_CONV - s, lanes] * prev
        return _silu(jnp.concatenate([head, acc[halo:]], axis=0))

    def l2n(a):
        return a * lax.rsqrt(jnp.sum(a * a, axis=-1, keepdims=True) + EPS)

    sm = sm_ref[0]
    lane = lax.broadcasted_iota(I32, sm.shape, 1)
    sp, logsig = _softplus_parts(sm + par_ref[1:2, :])
    vals = jnp.where(lane < LANE_G, _sigmoid(sm),
                     jnp.where(lane < LANE_F, par_ref[0:1, :] * sp,
                               jnp.where(lane < LANE_F + H_FOX, logsig, 0.0)))
    row = lax.broadcasted_iota(I32, (ts, ts), 0)
    colm = lax.broadcasted_iota(I32, (ts, ts), 1)
    log_chunk = CHUNK.bit_length() - 1
    causal = (row >= colm) & ((row >> log_chunk) == (colm >> log_chunk))

    hi = vals.astype(BF16)
    rem = vals - hi.astype(F32)
    mid = rem.astype(BF16)
    lo = (rem - mid.astype(F32)).astype(BF16)
    pieces = jnp.concatenate([hi, mid, lo], axis=-1)

    def cumsum(mask):
        y = jnp.dot(jnp.where(mask, 1.0, 0.0).astype(BF16), pieces, preferred_element_type=F32)
        return (y[:, :LANES] + y[:, LANES:2 * LANES]) + y[:, 2 * LANES:]

    ccum = cumsum(row >= colm) + carry_ref[...]
    gcum = cumsum(causal)
    carry_ref[...] = ccum[ts - 1:ts, :]
    gcum_t = gcum.T

    cl = ccum * LOG2E
    c_hi = cl.astype(BF16)
    c_rem = cl - c_hi.astype(F32)
    c_mid = c_rem.astype(BF16)
    c_lo = (c_rem - c_mid.astype(F32)).astype(BF16)
    c_pieces = jnp.concatenate([c_hi, c_mid, c_lo], axis=-1)
    pr = lax.broadcasted_iota(I32, (3 * LANES, LANES), 0)
    pc = lax.broadcasted_iota(I32, (3 * LANES, LANES), 1)
    src_lane, piece = pr & (LANES - 1), pr >> (LANES.bit_length() - 1)
    owned = (src_lane >= LANE_F) & (src_lane < LANE_F + H_FOX) & ((pc >> 3) == src_lane - LANE_F)
    place_q = jnp.where(owned & ((pc & 7) == piece), 1.0, 0.0).astype(BF16)
    place_k = jnp.where(owned & ((pc & 7) == piece + 3), -1.0, 0.0).astype(BF16)
    slot = lane & 7
    in_heads = lane < 8 * H_FOX
    ones_q = jnp.where(in_heads & (slot >= 3) & (slot < 6), 1.0, 0.0)
    ones_k = jnp.where(in_heads & (slot < 3), 1.0, 0.0)
    qaug_ref[0] = (jnp.dot(c_pieces, place_q, preferred_element_type=F32) + ones_q).astype(BF16)
    kaug_ref[0] = (jnp.dot(c_pieces, place_k, preferred_element_type=F32) + ones_k).astype(BF16)

    scale = DK_DN ** -0.5
    dn_norm = par_ref[2:3, :]

    pw = 2 * CHUNK
    prow = lax.broadcasted_iota(I32, (pw, pw), 0)
    pcol = lax.broadcasted_iota(I32, (pw, pw), 1)
    same = (prow >> log_chunk) == (pcol >> log_chunk)
    causal_p = (prow >= pcol) & same
    strict_p = (prow > pcol) & same
    levels = []
    s = 1
    while s < CHUNK:
        levels.append(((prow >> s.bit_length()) == (pcol >> s.bit_length()))
                      & ((prow & s) != 0) & ((pcol & s) == 0))
        s *= 2

    heads = range(H_DN)
    pairs = range(ts // pw)
    q, k, v = [], [], []
    for h in heads:
        q.append(l2n(conv_silu(h * DK_DN)))
        k.append(l2n(conv_silu(H_DN * DK_DN + h * DK_DN)))
        v.append(conv_silu(2 * H_DN * DK_DN + h * DV_DN))

    ctx = []
    for h in heads:
        for p in pairs:
            pr = slice(p * pw, (p + 1) * pw)
            gcol = gcum[pr, LANE_G + h:LANE_G + h + 1]
            grow = gcum_t[LANE_G + h:LANE_G + h + 1, pr]
            beta = vals[pr, LANE_BETA + h:LANE_BETA + h + 1]
            decay = jnp.where(causal_p, jnp.exp(jnp.where(causal_p, gcol - grow, 0.0)), 0.0)
            egc = jnp.exp(gcol)
            kp = k[h][pr]
            kb = kp * beta
            ctx.append(dict(
                h=h, p=p, gcol=gcol, kp=kp,
                a=jnp.where(strict_p, _mm(kb, kp, NT) * decay, 0.0),
                qk=jnp.where(causal_p, _mm(q[h][pr] * scale, kp, NT) * decay, 0.0),
                rhs=jnp.concatenate([v[h][pr] * beta, kb * egc], axis=-1),
                qg=q[h][pr] * (scale * egc)))

    for c in ctx:
        c["n"] = -jnp.where(levels[0], c["a"], 0.0)
    for level in levels[1:]:
        for c in ctx:
            m = jnp.where(level, c["a"], 0.0)
            c["y"] = m + _mm(c["n"], m)
        for c in ctx:
            c["n"] = c["n"] - (c["y"] + _mm(c["y"], c["n"]))
    for c in ctx:
        c["sol"] = c["rhs"] + _mm(c["n"], c["rhs"])

    state = [s_ref[h] for h in heads]
    for p in pairs:
        group = [c for c in ctx if c["p"] == p]
        v_prev = [None] * H_DN
        for ch in range(2):
            rows = slice(ch * CHUNK, (ch + 1) * CHUNK)
            out_rows = slice(p * pw + ch * CHUNK, p * pw + (ch + 1) * CHUNK)
            on_state = [_mm(jnp.concatenate([c["sol"][rows, DV_DN:], c["qg"][rows]], axis=0), state[c["h"]])
                        for c in group]
            for c, ws_qs in zip(group, on_state):
                h = c["h"]
                vn = c["sol"][rows, :DV_DN] - ws_qs[:CHUNK]
                v_pair = jnp.concatenate([vn, jnp.zeros_like(vn)] if ch == 0 else [v_prev[h], vn], axis=0)
                o = ws_qs[CHUNK:] + _mm(c["qk"][rows], v_pair)
                glast = c["gcol"][(ch + 1) * CHUNK - 1:(ch + 1) * CHUNK]
                state[h] = (state[h] * jnp.exp(glast)
                            + _mm(c["kp"][rows] * jnp.exp(glast - c["gcol"][rows]), vn, TN))
                v_prev[h] = vn
                o = o * lax.rsqrt(jnp.mean(o * o, axis=-1, keepdims=True) + EPS) * dn_norm
                za = za_ref[0, out_rows, h * DV_DN:(h + 1) * DV_DN].astype(F32)
                oa_ref[0, out_rows, h * DV_DN:(h + 1) * DV_DN] = (o * _silu(za)).astype(oa_ref.dtype)
    for h in heads:
        s_ref[h] = state[h]


def _delta_rule(qkv, za, small, conv_w, par, *, ts=256):
    b, t, _ = qkv.shape
    assert t % ts == 0 and ts % (2 * CHUNK) == 0
    blk = lambda width: pl.BlockSpec((1, ts, width), lambda bi, j: (bi, j, 0))
    fixed = lambda bi, j: (0, 0)
    return pl.pallas_call(
        functools.partial(_delta_body, ts=ts),
        grid=(b, t // ts),
        in_specs=[blk(W_DN), blk(H_DN * DV_DN), blk(LANES),
                  pl.BlockSpec(conv_w.shape, fixed), pl.BlockSpec(par.shape, fixed)],
        out_specs=[blk(H_DN * DV_DN), blk(LANES), blk(LANES)],
        out_shape=[jax.ShapeDtypeStruct((b, t, H_DN * DV_DN), BF16),
                   jax.ShapeDtypeStruct((b, t, LANES), BF16),
                   jax.ShapeDtypeStruct((b, t, LANES), BF16)],
        scratch_shapes=[pltpu.VMEM((3 * SUBLANES, W_DN), F32),
                        pltpu.VMEM((H_DN, DK_DN, DV_DN), F32),
                        pltpu.VMEM((1, LANES), F32)],
        compiler_params=_params("parallel", "arbitrary"),
        name="delta_rule",
    )(qkv, za, small, conv_w, par)


def _fox_body(q_ref, k_ref, v_ref, qa_ref, ka_ref, o_ref, m_ref, acc_ref, *, tq, tk):
    i = pl.program_id(1)
    m_ref[...] = jnp.full(m_ref.shape, -jnp.inf, F32)
    acc_ref[...] = jnp.zeros_like(acc_ref)
    head_lanes = [slice(h * D_FOX, (h + 1) * D_FOX) for h in range(H_FOX)]
    lane = lax.broadcasted_iota(I32, (tk, LANES), 1)
    own = [jnp.where((lane >> 3) == h, 1.0, 0.0).astype(BF16) for h in range(H_FOX)]
    ones = jnp.ones((tk, D_FOX), BF16)
    keep = lax.broadcasted_iota(I32, (tq, tk), 1) <= lax.broadcasted_iota(I32, (tq, tk), 0)

    def block(start, diag_offset):
        rows = slice(0 if diag_offset is None else diag_offset, tq)
        ka = ka_ref[0, pl.ds(start, tk), :]
        for h, lanes in enumerate(head_lanes):
            q_aug = jnp.concatenate([q_ref[0, rows, lanes], qa_ref[0, rows, :]], axis=1)
            k_aug = jnp.concatenate([k_ref[0, pl.ds(start, tk), lanes], ka * own[h]], axis=1)
            s = lax.dot_general(q_aug, k_aug, NT, preferred_element_type=F32)
            if diag_offset is not None:
                s = jnp.where(keep[:tq - diag_offset], s, -jnp.inf)
            m_prev = m_ref[h, rows]
            m_next = jnp.maximum(m_prev, jnp.max(s, axis=-1, keepdims=True))
            p = jnp.exp2(s - jnp.concatenate([m_next] * (tk // LANES), axis=1))
            alpha = jnp.exp2(m_prev - m_next)
            v_aug = jnp.concatenate([v_ref[0, pl.ds(start, tk), lanes], ones], axis=1)
            acc_ref[h, rows] = (jnp.concatenate([alpha, alpha], axis=1) * acc_ref[h, rows]
                                + jnp.dot(p.astype(BF16), v_aug, preferred_element_type=F32))
            m_ref[h, rows] = m_next

    def full_block(jb, carry):
        block(pl.multiple_of(jb * tk, tk), None)
        return carry

    lax.fori_loop(0, i * (tq // tk), full_block, 0)
    for d in range(tq // tk):
        block(pl.multiple_of(i * tq + d * tk, tk), d * tk)
    for h, lanes in enumerate(head_lanes):
        acc = acc_ref[h]
        o_ref[0, :, lanes] = (acc[:, :D_FOX] / acc[:, D_FOX:]).astype(o_ref.dtype)


def _fox_attention(qkv, qaug, kaug, *, tq=1024, tk=512):
    b, t, _ = qkv.shape
    hd = H_FOX * D_FOX
    assert t % tq == 0 and tq % tk == 0 and tk % LANES == 0
    return pl.pallas_call(
        functools.partial(_fox_body, tq=tq, tk=tk),
        grid=(b, t // tq),
        in_specs=[pl.BlockSpec((1, tq, hd), lambda bi, i: (bi, i, 0)),
                  pl.BlockSpec((1, t, hd), lambda bi, i: (bi, 0, 1)),
                  pl.BlockSpec((1, t, hd), lambda bi, i: (bi, 0, 2)),
                  pl.BlockSpec((1, tq, LANES), lambda bi, i: (bi, i, 0)),
                  pl.BlockSpec((1, t, LANES), lambda bi, i: (bi, 0, 0))],
        out_specs=pl.BlockSpec((1, tq, hd), lambda bi, i: (bi, i, 0)),
        out_shape=jax.ShapeDtypeStruct((b, t, hd), BF16),
        scratch_shapes=[pltpu.VMEM((H_FOX, tq, LANES), F32), pltpu.VMEM((H_FOX, tq, 2 * D_FOX), F32)],
        compiler_params=_params("parallel", "arbitrary"),
        name="fox_attention",
    )(qkv, qkv, qkv, qaug, kaug)


def _merge_body(oa_ref, ob_ref, zc_ref, gates_ref, x_ref, wa_ref, wb_ref, wc_ref, wo_ref, g_ref,
                wr_ref, br_ref, xn_ref, hp_ref, route_ref, meta_ref, cnt_ref, run_ref, *, tm):
    step = pl.program_id(0)
    d = x_ref.shape[-1]

    @pl.when(step == 0)
    def _():
        run_ref[...] = jnp.zeros_like(run_ref)

    merged = None
    for idx, (m_ref, w_ref) in enumerate(((oa_ref, wa_ref), (ob_ref, wb_ref), (zc_ref, wc_ref))):
        y = jnp.dot(m_ref[...], w_ref[...], preferred_element_type=F32)
        term = _sigmoid(gates_ref[:, idx * d:(idx + 1) * d].astype(F32)) * y
        merged = term if merged is None else merged + term
    xn = x_ref[...] + jnp.dot(merged.astype(BF16), wo_ref[...], preferred_element_type=F32)
    xn_ref[...] = xn
    h2 = xn * lax.rsqrt(jnp.mean(xn * xn, axis=-1, keepdims=True) + EPS) * g_ref[...]

    hp_ref[...] = _pack_bf16_pairs(h2)

    logits = lax.dot_general(wr_ref[...], h2.astype(BF16), NT, preferred_element_type=F32) + br_ref[...]
    big = jnp.int32(LANES)
    gl = logits[LANE_GRP:LANE_GRP + N_GROUPS]
    g_row = lax.broadcasted_iota(I32, gl.shape, 0)
    gmax = jnp.max(gl, axis=0, keepdims=True)
    grp = jnp.min(jnp.where(gl == gmax, g_row, big), axis=0, keepdims=True)
    p_grp = 1.0 / jnp.sum(jnp.exp(gl - gmax), axis=0, keepdims=True)
    el_all = logits[LANE_EXP:LANE_EXP + N_EXPERTS]
    e_row = lax.broadcasted_iota(I32, el_all.shape, 0)
    el = jnp.where((e_row >> (EXPERTS_PER_GROUP.bit_length() - 1)) == grp, el_all, -jnp.inf)
    v0 = jnp.max(el, axis=0, keepdims=True)
    i0 = jnp.min(jnp.where(el == v0, e_row, big), axis=0, keepdims=True)
    el1 = jnp.where(e_row == i0, -jnp.inf, el)
    v1 = jnp.max(el1, axis=0, keepdims=True)
    i1 = jnp.min(jnp.where(el1 == v1, e_row, big), axis=0, keepdims=True)
    e1 = jnp.exp(v1 - v0)
    gate0 = p_grp / (1.0 + e1)
    gate1 = p_grp * e1 / (1.0 + e1)

    hot0 = e_row == i0
    hot1 = e_row == i1
    onehot = jnp.where(hot0 | hot1, 1.0, 0.0)
    tiles = [onehot[:, t * LANES:(t + 1) * LANES] for t in range(tm // LANES)]
    ri = lax.broadcasted_iota(I32, (LANES, LANES), 0)
    cj = lax.broadcasted_iota(I32, (LANES, LANES), 1)
    in_tile = _mm(jnp.concatenate(tiles, axis=0), jnp.where(ri < cj, 1.0, 0.0))
    before = run_ref[...]
    prefix = []
    for t, tile in enumerate(tiles):
        prefix.append(in_tile[t * N_EXPERTS:(t + 1) * N_EXPERTS] + before)
        before = before + jnp.sum(tile, axis=1, keepdims=True)
    prefix = jnp.concatenate(prefix, axis=1)
    run_ref[...] = before
    cnt_ref[...] = jnp.broadcast_to(before, cnt_ref.shape)
    rank0 = jnp.sum(jnp.where(hot0, prefix, 0.0), axis=0, keepdims=True)
    rank1 = jnp.sum(jnp.where(hot1, prefix, 0.0), axis=0, keepdims=True)

    rows = {R_GATE0: gate0, R_GATE1: gate1, R_EID0: i0.astype(F32), R_EID1: i1.astype(F32),
            R_RANK0: rank0, R_RANK1: rank1}
    fields = jnp.concatenate([rows.get(r, jnp.zeros_like(gate0)) for r in range(SUBLANES)], axis=0)
    meta_ref[...] = fields.astype(I32)
    route_ref[...] = jnp.concatenate([fields, jnp.zeros((LANES - SUBLANES, tm), F32)], axis=0).T


def _merge(oa, ob, zc, gates, x2, wa, wb, wc, wo, g, wr, br, *, tm=1024):
    n, d = x2.shape
    assert n % tm == 0
    row = lambda i: (i, 0)
    fixed = lambda i: (0, 0)
    full = lambda a: pl.BlockSpec(a.shape, fixed)
    return pl.pallas_call(
        functools.partial(_merge_body, tm=tm),
        grid=(n // tm,),
        in_specs=[pl.BlockSpec((tm, oa.shape[1]), row), pl.BlockSpec((tm, ob.shape[1]), row),
                  pl.BlockSpec((tm, zc.shape[1]), row), pl.BlockSpec((tm, 3 * d), row),
                  pl.BlockSpec((tm, d), row),
                  full(wa), full(wb), full(wc), full(wo), full(g), full(wr), full(br)],
        out_specs=[pl.BlockSpec((tm, d), row), pl.BlockSpec((tm, d // 2), row),
                   pl.BlockSpec((tm, LANES), row), pl.BlockSpec((SUBLANES, tm), lambda i: (0, i)),
                   pl.BlockSpec((N_EXPERTS, LANES), fixed)],
        out_shape=[jax.ShapeDtypeStruct((n, d), F32), jax.ShapeDtypeStruct((n, d // 2), U32),
                   jax.ShapeDtypeStruct((n, LANES), F32), jax.ShapeDtypeStruct((SUBLANES, n), I32),
                   jax.ShapeDtypeStruct((N_EXPERTS, LANES), F32)],
        scratch_shapes=[pltpu.VMEM((N_EXPERTS, 1), F32)],
        compiler_params=_params("arbitrary"),
        name="merge_router",
    )(oa, ob, zc, gates, x2, wa, wb, wc, wo, g, wr, br)


def _sc_mesh():
    return plsc.VectorSubcoreMesh(core_axis_name="c", subcore_axis_name="s")


def _sc_worker_base(per_worker):
    return (lax.axis_index("s") * SC_CORES + lax.axis_index("c")) * per_worker


def _sc_row_scatter(src, idx, out_rows, *, chunk=128):
    n, d = src.shape
    per_worker = n // (SC_CORES * SC_SUBCORES)
    chunk = min(chunk, per_worker)
    assert idx.shape == (2 * n,) and n % (SC_CORES * SC_SUBCORES) == 0 and per_worker % chunk == 0

    @functools.partial(
        pl.kernel, mesh=_sc_mesh(), out_type=jax.ShapeDtypeStruct((out_rows, d), src.dtype),
        scratch_types=[pltpu.VMEM((chunk,), I32), pltpu.VMEM((chunk, d), src.dtype), pltpu.SemaphoreType.DMA])
    def scatter(src_hbm, idx_hbm, out_hbm, idx_v, rows_v, sem):
        base = _sc_worker_base(per_worker)

        @pl.loop(0, per_worker // chunk)
        def _(step):
            off = base + step * chunk
            pltpu.sync_copy(src_hbm.at[pl.ds(off, chunk)], rows_v)
            for slot in range(2):
                pltpu.sync_copy(idx_hbm.at[pl.ds(slot * n + off, chunk)], idx_v)
                pltpu.async_copy(rows_v, out_hbm.at[idx_v], sem).wait()

    return scatter(src, idx)


def _pack_bf16_pairs(x):
    half = x.shape[-1] // 2
    lo = pltpu.bitcast(x[:, :half].astype(BF16).astype(F32), U32) >> 16
    hi = pltpu.bitcast(x[:, half:].astype(BF16).astype(F32), U32) & jnp.uint32(0xFFFF0000)
    return lo | hi


def _unpack_bf16_pairs(xp):
    return pltpu.bitcast(xp << 16, F32), pltpu.bitcast(xp & jnp.uint32(0xFFFF0000), F32)


def _expert_body(be_ref, valid_ref, nu_ref, xs_ref, w1_ref, w3_ref, w2_ref, ys_ref):
    del be_ref
    step = pl.program_id(0)
    live = step < nu_ref[0]

    @pl.when(jnp.logical_not(live))
    def _():
        ys_ref[...] = jnp.zeros_like(ys_ref)

    @pl.when(live)
    def _():
        row = lax.broadcasted_iota(I32, xs_ref.shape, 0)
        xp = jnp.where(row < valid_ref[step], xs_ref[...], jnp.uint32(0))
        half = xp.shape[-1]
        lo, hi = (part.astype(BF16) for part in _unpack_bf16_pairs(xp))

        def up(w_ref):
            return (jnp.dot(lo, w_ref[0, :half, :].astype(BF16), preferred_element_type=F32)
                    + jnp.dot(hi, w_ref[0, half:, :].astype(BF16), preferred_element_type=F32))

        act = (_silu(up(w1_ref)) * up(w3_ref)).astype(BF16)
        ys_ref[...] = _pack_bf16_pairs(jnp.dot(act, w2_ref[0].astype(BF16), preferred_element_type=F32))


def _experts(blk_eid, blk_valid, n_used, xs, w1, w3, w2, *, rb):
    p, half = xs.shape
    d = 2 * half
    nb = p // rb
    used = lambda i, be, bv, nu: jnp.maximum(jnp.minimum(i, nu[0] - 1), 0)
    wmap = lambda i, be, bv, nu: (be[used(i, be, bv, nu)], 0, 0)
    grid_spec = pltpu.PrefetchScalarGridSpec(
        num_scalar_prefetch=3,
        grid=(nb,),
        in_specs=[pl.BlockSpec((rb, half), lambda i, be, bv, nu: (used(i, be, bv, nu), 0)),
                  pl.BlockSpec((1, d, D_EXPERT), wmap), pl.BlockSpec((1, d, D_EXPERT), wmap),
                  pl.BlockSpec((1, D_EXPERT, d), wmap)],
        out_specs=pl.BlockSpec((rb, half), lambda i, be, bv, nu: (i, 0)),
    )
    return pl.pallas_call(
        _expert_body,
        grid_spec=grid_spec,
        out_shape=jax.ShapeDtypeStruct((p, half), U32),
        compiler_params=_params("arbitrary"),
        name="moe_experts",
    )(blk_eid, blk_valid, n_used, xs, w1, w3, w2)


def _sc_row_gather(table, idx, *, chunk=128):
    rows, d = idx.shape[0], table.shape[1]
    per_worker = rows // (SC_CORES * SC_SUBCORES)
    chunk = min(chunk, per_worker)
    assert rows % (SC_CORES * SC_SUBCORES) == 0 and per_worker % chunk == 0

    @functools.partial(
        pl.kernel, mesh=_sc_mesh(), out_type=jax.ShapeDtypeStruct((rows, d), table.dtype),
        scratch_types=[pltpu.VMEM((chunk,), I32), pltpu.VMEM((chunk, d), table.dtype), pltpu.SemaphoreType.DMA])
    def gather(table_hbm, idx_hbm, out_hbm, idx_v, rows_v, sem):
        base = _sc_worker_base(per_worker)

        @pl.loop(0, per_worker // chunk)
        def _(step):
            off = base + step * chunk
            pltpu.sync_copy(idx_hbm.at[pl.ds(off, chunk)], idx_v)
            pltpu.async_copy(table_hbm.at[idx_v], rows_v, sem).wait()
            pltpu.sync_copy(rows_v, out_hbm.at[pl.ds(off, chunk)])

    return gather(table, idx)


def _combine_rows(r0_ref, r1_ref, x_ref, route_ref):
    route = route_ref[...]
    half = r0_ref.shape[-1]
    lo0, hi0 = _unpack_bf16_pairs(r0_ref[...])
    lo1, hi1 = _unpack_bf16_pairs(r1_ref[...])
    g0, g1 = route[:, R_GATE0:R_GATE0 + 1], route[:, R_GATE1:R_GATE1 + 1]
    return jnp.concatenate([x_ref[:, :half] + g0 * lo0 + g1 * lo1,
                            x_ref[:, half:] + g0 * hi0 + g1 * hi1], axis=1)


def _final_body(r0_ref, r1_ref, x_ref, route_ref, g_ref, o_ref):
    out = _combine_rows(r0_ref, r1_ref, x_ref, route_ref)
    o_ref[...] = out * lax.rsqrt(jnp.mean(out * out, axis=-1, keepdims=True) + EPS) * g_ref[...]


def _final_combine(rows2, xn, route, g, *, tm):
    n, d = xn.shape
    nt = n // tm
    row = lambda i: (i, 0)
    return pl.pallas_call(
        _final_body,
        grid=(nt,),
        in_specs=[pl.BlockSpec((tm, d // 2), row), pl.BlockSpec((tm, d // 2), lambda i: (i + nt, 0)),
                  pl.BlockSpec((tm, d), row), pl.BlockSpec((tm, LANES), row),
                  pl.BlockSpec((1, d), lambda i: (0, 0))],
        out_specs=pl.BlockSpec((tm, d), row),
        out_shape=jax.ShapeDtypeStruct((n, d), F32),
        compiler_params=_params("parallel"),
        name="moe_combine",
    )(rows2, rows2, xn, route, g)


def _moe_rows(hp, meta, counts, w1, w3, w2, *, layer, rb):
    n = hp.shape[0]
    cnt = counts[:, 0].astype(I32)
    nblk = (cnt + rb - 1) // rb
    bend = jnp.cumsum(nblk)
    pstart = (bend - nblk) * rb
    nb = (2 * n) // rb + N_EXPERTS
    n_used = bend[-1:].astype(I32)
    blk = jnp.arange(nb, dtype=I32)
    blk_eid = jnp.minimum(jnp.sum(bend[None, :] <= blk[:, None], axis=1), N_EXPERTS - 1).astype(I32)
    experts = jnp.arange(N_EXPERTS, dtype=I32)
    mine = blk_eid[:, None] == experts[None, :]
    blk_valid = jnp.clip(jnp.sum(jnp.where(mine, (cnt + pstart)[None, :], 0), axis=1) - blk * rb, 0, rb)
    eid = meta[R_EID0:R_EID1 + 1]
    first = jnp.sum(jnp.where(eid[None] == experts[:, None, None], pstart[:, None, None], 0), axis=0)
    dest = (first + meta[R_RANK0:R_RANK1 + 1]).reshape(-1)

    xs = _sc_row_scatter(hp, dest, nb * rb)
    ys = _experts(blk_eid + layer * N_EXPERTS, blk_valid.astype(I32), n_used, xs, w1, w3, w2, rb=rb)
    return _sc_row_gather(ys, dest)


def _regroup_body(w_ref, wm_ref, ws_ref, *, bounds, q_scale):
    small0, fox0, fb0, conv0, end = bounds
    w = w_ref[0]
    fox_w = D_FOX * H_FOX
    main = jnp.concatenate([w[:, :small0], w[:, fox0:fox0 + fox_w] * q_scale, w[:, fox0 + fox_w:fb0],
                            w[:, conv0:end]], axis=1)
    narrow = jnp.concatenate([w[:, small0:fox0], w[:, fb0:conv0]], axis=1)
    wm_ref[...] = main.astype(BF16)
    ws_ref[...] = jnp.concatenate(
        [narrow, jnp.zeros((narrow.shape[0], LANES - narrow.shape[1]), F32)], axis=1).astype(BF16)


def _regroup_w_in(w_in, layer, *, q_scale, tr=128):
    _, d, d_in = w_in.shape
    small0 = 2 * H_DN * DK_DN + 2 * H_DN * DV_DN
    fox0 = small0 + 2 * H_DN
    fb0 = fox0 + W_FOX
    conv0 = fb0 + H_FOX
    width = d_in - (conv0 - fb0) - (fox0 - small0)
    assert d % tr == 0 and width % LANES == 0
    return pl.pallas_call(
        functools.partial(_regroup_body, bounds=(small0, fox0, fb0, conv0, d_in), q_scale=q_scale),
        grid=(d // tr,),
        in_specs=[pl.BlockSpec((1, tr, d_in), lambda i: (layer, i, 0))],
        out_specs=[pl.BlockSpec((tr, width), lambda i: (i, 0)), pl.BlockSpec((tr, LANES), lambda i: (i, 0))],
        out_shape=[jax.ShapeDtypeStruct((d, width), BF16), jax.ShapeDtypeStruct((d, LANES), BF16)],
        compiler_params=_params("parallel"),
        name="regroup_w_in",
    )(w_in)


def _regroup_t_body(wt_ref, n1_ref, n2_ref, *rest, depth, bounds, tc, q_scale):
    out_refs, small_refs, (buf_ref, sem) = rest[:depth], rest[depth:2 * depth], rest[2 * depth:]
    j = pl.program_id(0)
    fox0, conv0, a_end, q_end, b_end = bounds

    def fetch(block, slot):
        start = jnp.where(block < a_end, block * tc,
                          jnp.where(block < b_end, fox0 + (block - a_end) * tc, conv0 + (block - b_end) * tc))
        return pltpu.make_async_copy(wt_ref.at[pl.ds(start, tc)], buf_ref.at[slot], sem.at[slot])

    @pl.when(j == 0)
    def _():
        fetch(j, 0).start()
        for l, ws_ref in enumerate(small_refs):
            cols = jnp.concatenate([n1_ref[:, l, :], n2_ref[:, l, :]], axis=0)
            pad = jnp.zeros((LANES - cols.shape[0], cols.shape[1]), F32)
            ws_ref[...] = jnp.concatenate([cols, pad], axis=0).T.astype(ws_ref.dtype)

    @pl.when(j + 1 < pl.num_programs(0))
    def _():
        fetch(j + 1, (j + 1) % 2).start()

    slot = j % 2
    fetch(j, slot).wait()
    scale = jnp.where((j >= a_end) & (j < q_end), q_scale, 1.0)
    for l, out_ref in enumerate(out_refs):
        out_ref[...] = (buf_ref[slot, :, l, :].T * scale).astype(out_ref.dtype)


def _regroup_w_in_t(w_in, *, q_scale, tc=LANES):
    depth, d, d_in = w_in.shape
    small0 = 2 * H_DN * DK_DN + 2 * H_DN * DV_DN
    fox0 = small0 + 2 * H_DN
    fb0 = fox0 + W_FOX
    conv0 = fb0 + H_FOX
    wt = jnp.transpose(w_in, (2, 0, 1))
    spans = (small0, fb0 - fox0, d_in - conv0)
    assert all(span % tc == 0 for span in spans)
    a_end = spans[0] // tc
    b_end = a_end + spans[1] // tc
    q_end = a_end + (H_FOX * D_FOX) // tc
    width = sum(spans)
    narrow = (wt[small0:fox0], wt[fb0:conv0])
    whole = lambda a: pl.BlockSpec(a.shape, lambda j: (0, 0, 0))
    outs = pl.pallas_call(
        functools.partial(_regroup_t_body, depth=depth, bounds=(fox0, conv0, a_end, q_end, b_end), tc=tc,
                          q_scale=q_scale),
        grid=(width // tc,),
        in_specs=[pl.BlockSpec(memory_space=pl.ANY), whole(narrow[0]), whole(narrow[1])],
        out_specs=([pl.BlockSpec((d, tc), lambda j: (0, j)) for _ in range(depth)]
                   + [pl.BlockSpec((d, LANES), lambda j: (0, 0)) for _ in range(depth)]),
        out_shape=([jax.ShapeDtypeStruct((d, width), BF16) for _ in range(depth)]
                   + [jax.ShapeDtypeStruct((d, LANES), BF16) for _ in range(depth)]),
        scratch_shapes=[pltpu.VMEM((2, tc, depth, d), F32), pltpu.SemaphoreType.DMA((2,))],
        compiler_params=_params("arbitrary"),
        name="regroup_w_in",
    )(wt, *narrow)
    return outs[:depth], outs[depth:]


def _lane_row(pairs):
    row = jnp.zeros((LANES,), F32)
    for off, vec in pairs:
        row = row.at[off:off + vec.shape[0]].set(vec.astype(F32))
    return row


def kernel(x, norm_mix, w_in, conv_qkv, dn_a_log, dn_dt_bias, dn_norm, fox_bias, conv_dw, conv_dw_b,
           conv_ln_g, conv_ln_b, w_a, w_b, w_c, w_out, norm_ffn, router_group_w, router_group_b,
           router_expert_w, router_expert_b, expert_w1, expert_w3, expert_w2, norm_final):
    b, t, d = x.shape
    n = b * t
    depth = w_in.shape[0]
    tm_final, rb = 512, 512

    x_src = x.reshape(n, d)
    w1_all = expert_w1.reshape(depth * N_EXPERTS, d, D_EXPERT)
    w3_all = expert_w3.reshape(depth * N_EXPERTS, d, D_EXPERT)
    w2_all = expert_w2.reshape(depth * N_EXPERTS, D_EXPERT, d)
    w_mains, w_smalls = _regroup_w_in_t(w_in, q_scale=D_FOX ** -0.5 * LOG2E)
    for l in range(depth):
        w_main, w_small = w_mains[l], w_smalls[l]
        vec = jnp.stack([conv_dw_b[l], conv_ln_g[l], conv_ln_b[l]] + [jnp.zeros((C_CONV,), F32)] * 5)
        outs = _in_proj(x_src, norm_mix[l][None, :], w_main, w_small, conv_dw[l], vec, seq_len=t)
        x2 = outs[0] if l > 0 else x_src
        dn_qkv, za_p, fox_qkv, zc, gates, small = outs[-6:]

        par = jnp.stack([_lane_row([(LANE_G, -jnp.exp(dn_a_log[l]))]),
                         _lane_row([(LANE_G, dn_dt_bias[l]), (LANE_F, fox_bias[l])]),
                         _lane_row([(0, dn_norm[l])])] + [jnp.zeros((LANES,), F32)] * 5)
        oa, qaug, kaug = _delta_rule(dn_qkv.reshape(b, t, -1), za_p.reshape(b, t, -1),
                                     small.reshape(b, t, LANES), conv_qkv[l], par)
        ob = _fox_attention(fox_qkv.reshape(b, t, -1), qaug, kaug)

        w_r = jnp.concatenate([router_group_w[l], jnp.zeros((d, LANE_EXP - N_GROUPS), F32),
                               router_expert_w[l], jnp.zeros((d, LANES - LANE_EXP - N_EXPERTS), F32)], axis=1)
        b_r = _lane_row([(LANE_GRP, router_group_b[l]), (LANE_EXP, router_expert_b[l])])[None, :]
        xn, hp, route, meta, counts = _merge(
            oa.reshape(n, -1), ob.reshape(n, -1), zc.reshape(n, -1), gates, x2,
            w_a[l].astype(BF16), w_b[l].astype(BF16), w_c[l].astype(BF16), w_out[l].astype(BF16),
            norm_ffn[l][None, :], w_r.T.astype(BF16), b_r.reshape(LANES, 1))

        x_src = (_moe_rows(hp, meta, counts, w1_all, w3_all, w2_all, layer=l, rb=rb), xn, route)
    return _final_combine(*x_src, norm_final[None, :], tm=tm_final).reshape(b, t, d)
```

```python
import functools

import jax
import jax.numpy as jnp
from jax import lax
from jax.experimental import pallas as pl
from jax.experimental.pallas import tpu as pltpu
from jax.experimental.pallas import tpu_sc as plsc

F32 = jnp.float32
BF16 = jnp.bfloat16
U32 = jnp.uint32
I32 = jnp.int32

EPS = 1e-6
LOG2E = 1.4426950408889634
LANES = 128
SUBLANES = 8
SC_CORES, SC_SUBCORES = 2, 16
H_DN, DK_DN, DV_DN = 4, 128, 128
SHORT_CONV = 4
CHUNK = 64
H_FOX, D_FOX = 4, 128
C_CONV = 512
CONV_WIDTH = 31
N_GROUPS, EXPERTS_PER_GROUP = 4, 8
N_EXPERTS = N_GROUPS * EXPERTS_PER_GROUP
D_EXPERT = 256

W_DN = 3 * H_DN * DK_DN
W_FOX = 3 * H_FOX * D_FOX
VMEM_LIMIT = 56 * 1024 * 1024

LANE_BETA, LANE_G, LANE_F = 0, 4, 8
LANE_GRP, LANE_EXP = 0, 32
R_GATE0, R_GATE1, R_EID0, R_EID1, R_RANK0, R_RANK1 = 0, 1, 2, 3, 4, 5

NN = (((1,), (0,)), ((), ()))
NT = (((1,), (1,)), ((), ()))
TN = (((0,), (0,)), ((), ()))


def _mm(a, b, dims=NN):
    return lax.dot_general(a.astype(BF16), b.astype(BF16), dims, preferred_element_type=F32)


def _sigmoid(x):
    return 0.5 * jnp.tanh(0.5 * x) + 0.5


def _silu(x):
    half = 0.5 * x
    return half * jnp.tanh(half) + half


def _params(*sem):
    return pltpu.CompilerParams(dimension_semantics=sem, vmem_limit_bytes=VMEM_LIMIT)


def _in_proj_body(*refs, col_chunk, combine, tiles_per_seq, hrows, rchunk):
    if combine:
        r0_ref, r1_ref, xn_ref, route_ref, g_ref, w_ref, ws_ref, dw_ref, vec_ref, x_out_ref = refs[:10]
        x = _combine_rows(r0_ref, r1_ref, xn_ref, route_ref)
        x_out_ref[...] = x
        rest = refs[10:]
    else:
        x_ref, g_ref, w_ref, ws_ref, dw_ref, vec_ref = refs[:6]
        x = x_ref[...]
        rest = refs[6:]
    dn_ref, za_ref, fox_ref, zc_ref, gates_ref, small_ref, z_ref, zs_ref, y_ref = rest
    tm = x.shape[0]
    h = x * lax.rsqrt(jnp.mean(x * x, axis=-1, keepdims=True) + EPS) * g_ref[...]
    hb = h.astype(BF16)

    def project(col, width, hold=None):
        lhs = hb
        if hold is not None:
            zero = jnp.minimum(jnp.abs(hold), 0.0).astype(BF16)
            head = jnp.concatenate([hb[0:rchunk, 0:LANES] + zero, hb[0:rchunk, LANES:]], axis=1)
            lhs = jnp.concatenate([head, hb[rchunk:]], axis=0)
        return jnp.dot(lhs, w_ref[:, col:col + width], preferred_element_type=F32)

    starts, col = {}, 0
    for name, width in (("dn", dn_ref.shape[-1]), ("za", za_ref.shape[-1]), ("fox", fox_ref.shape[-1]),
                        ("conv", 2 * C_CONV), ("gates", gates_ref.shape[-1])):
        starts[name], col = col, col + width

    first = pl.program_id(0) % tiles_per_seq == 0
    col = starts["conv"]

    @pl.when(first)
    def _():
        z_ref[0:hrows, :] = jnp.zeros((hrows, C_CONV), F32)

    @pl.when(jnp.logical_not(first))
    def _():
        z_ref[0:hrows, :] = z_ref[tm:tm + hrows, :]

    z_ref[hrows:hrows + tm, :] = project(col, C_CONV) * _sigmoid(project(col + C_CONV, C_CONV))
    sub = SUBLANES
    span = tm + hrows - sub
    for phase in range(1, sub):
        zs_ref[phase - 1, 0:span, :] = z_ref[phase:phase + span, :]

    def conv_unit(lg, r0, after):
        lanes = slice(lg * LANES, (lg + 1) * LANES)
        acc = vec_ref[0:1, lanes] + jnp.minimum(jnp.abs(after[0:rchunk, 0:LANES]), 0.0)
        for k in range(CONV_WIDTH):
            off = hrows - (CONV_WIDTH - 1) + k
            phase, base = off % sub, off - off % sub + r0
            if phase == 0:
                tap = z_ref[base:base + rchunk, lanes]
            else:
                tap = zs_ref[phase - 1, base:base + rchunk, lanes]
            acc = acc + dw_ref[k:k + 1, lanes] * tap
        y_ref[r0:r0 + rchunk, lanes] = acc
        return acc

    units = [(lg, r0) for lg in range(C_CONV // LANES) for r0 in range(0, tm, rchunk)]
    chunks = [(ref, starts[name], c) for name, ref in (("dn", dn_ref), ("za", za_ref), ("fox", fox_ref),
                                                       ("gates", gates_ref))
              for c in range(0, ref.shape[-1], col_chunk)]
    share = [[] for _ in chunks]
    for u, unit in enumerate(units):
        share[u * len(chunks) // len(units)].append(unit)
    last = []
    for idx, ((ref, start, c), mine) in enumerate(zip(chunks, share)):
        res = project(start + c, col_chunk, hold=last[idx - 2] if idx >= 2 else None)
        ref[:, c:c + col_chunk] = res.astype(ref.dtype)
        tile = None
        for lg, r0 in mine:
            tile = conv_unit(lg, r0, res)
        last.append(tile)
    small_ref[...] = jnp.dot(hb, ws_ref[...], preferred_element_type=F32)

    y = y_ref[...]
    mu = jnp.mean(y, axis=-1, keepdims=True)
    yc = y - mu
    var = jnp.mean(yc * yc, axis=-1, keepdims=True)
    yn = yc * lax.rsqrt(var + EPS) * vec_ref[1:2, :] + vec_ref[2:3, :]
    zc_ref[...] = _silu(yn).astype(zc_ref.dtype)


def _in_proj(x_src, g, w, ws, dw, vec, *, seq_len, tm=512, col_chunk=512, hrows=32, rchunk=32):
    combine = isinstance(x_src, tuple)
    n, d = (x_src[1] if combine else x_src).shape
    nt = n // tm
    widths = (W_DN, H_DN * DV_DN, W_FOX, C_CONV, 3 * d)
    assert w.shape == (d, sum(widths) + C_CONV) and n % tm == 0 and seq_len % tm == 0
    assert hrows >= CONV_WIDTH - 1 and hrows % SUBLANES == 0 and tm % rchunk == 0
    row = lambda i: (i, 0)
    fixed = lambda i: (0, 0)
    out_shape = [jax.ShapeDtypeStruct((n, wd), BF16) for wd in widths] + [jax.ShapeDtypeStruct((n, LANES), F32)]
    out_specs = [pl.BlockSpec((tm, wd), row) for wd in widths] + [pl.BlockSpec((tm, LANES), row)]
    if combine:
        rows2, xn, route = x_src
        args = (rows2, rows2, xn, route)
        in_specs = [pl.BlockSpec((tm, d // 2), row), pl.BlockSpec((tm, d // 2), lambda i: (i + nt, 0)),
                    pl.BlockSpec((tm, d), row), pl.BlockSpec((tm, LANES), row)]
        out_shape = [jax.ShapeDtypeStruct((n, d), F32)] + out_shape
        out_specs = [pl.BlockSpec((tm, d), row)] + out_specs
    else:
        args = (x_src,)
        in_specs = [pl.BlockSpec((tm, d), row)]
    return pl.pallas_call(
        functools.partial(_in_proj_body, col_chunk=col_chunk, combine=combine, tiles_per_seq=seq_len // tm,
                          hrows=hrows, rchunk=rchunk),
        grid=(nt,),
        in_specs=in_specs + [pl.BlockSpec((1, d), fixed),
                             pl.BlockSpec(w.shape, fixed, pipeline_mode=pl.Buffered(1)),
                             pl.BlockSpec(ws.shape, fixed, pipeline_mode=pl.Buffered(1)),
                             pl.BlockSpec(dw.shape, fixed), pl.BlockSpec(vec.shape, fixed)],
        out_specs=out_specs,
        out_shape=out_shape,
        scratch_shapes=[pltpu.VMEM((tm + hrows, C_CONV), F32),
                        pltpu.VMEM((SUBLANES - 1, tm + hrows, C_CONV), F32),
                        pltpu.VMEM((tm, C_CONV), F32)],
        compiler_params=_params("arbitrary"),
        name="in_proj",
    )(*args, g, w, ws, dw, vec)


def _softplus_parts(z):
    t = jnp.log1p(jnp.exp(-jnp.abs(z)))
    return jnp.maximum(z, 0.0) + t, -(jnp.maximum(-z, 0.0) + t)


def _delta_body(qkv_ref, za_ref, sm_ref, cw_ref, par_ref, oa_ref, qaug_ref, kaug_ref,
                xs_ref, s_ref, carry_ref, *, ts):
    j = pl.program_id(1)
    halo = SUBLANES
    pack = 2 * SUBLANES

    @pl.when(j == 0)
    def _():
        xs_ref[0:halo, :] = jnp.zeros((halo, W_DN), F32)
        s_ref[...] = jnp.zeros_like(s_ref)
        carry_ref[...] = jnp.zeros_like(carry_ref)

    @pl.when(j > 0)
    def _():
        xs_ref[0:halo, :] = xs_ref[2 * halo:3 * halo, :]

    xb = qkv_ref[0]
    xs_ref[halo:2 * halo, :] = qkv_ref[0, 0:pack, :].astype(F32)[0:halo]
    xs_ref[2 * halo:3 * halo, :] = qkv_ref[0, ts - pack:ts, :].astype(F32)[pack - halo:pack]

    lag = lax.broadcasted_iota(I32, (ts, ts), 0) - lax.broadcasted_iota(I32, (ts, ts), 1)
    shifts = jnp.concatenate([jnp.where(lag == s, 1.0, 0.0).astype(BF16) for s in range(1, SHORT_CONV)], axis=0)
    stacked = jnp.dot(shifts, xb, preferred_element_type=F32)
    shifted = [stacked[(s - 1) * ts:s * ts] for s in range(1, SHORT_CONV)]
    head_row = lax.broadcasted_iota(I32, (halo, LANES), 0)

    def conv_silu(lane0):
        lanes = slice(lane0, lane0 + LANES)
        acc = cw_ref[SHORT_CONV - 1:SHORT_CONV, lanes] * xb[:, lanes].astype(F32)
        for s in range(1, SHORT_CONV):
            acc = acc + cw_ref[SHORT_CONV - 1 - s:SHORT_CONV - s, lanes] * shifted[s - 1][:, lanes]
        head = acc[0:halo]
        for s in range(1, SHORT_CONV):
            prev = jnp.where(head_row < s, xs_ref[halo - s:2 * halo - s, lanes], 0.0)
            head = head + cw_ref[SHORT_CONV - 1 - s:SHORT_CONV - s, lanes] * prev
        return _silu(jnp.concatenate([head, acc[halo:]], axis=0))

    def l2n(a):
        return a * lax.rsqrt(jnp.sum(a * a, axis=-1, keepdims=True) + EPS)

    sm = sm_ref[0]
    lane = lax.broadcasted_iota(I32, sm.shape, 1)
    sp, logsig = _softplus_parts(sm + par_ref[1:2, :])
    vals = jnp.where(lane < LANE_G, _sigmoid(sm),
                     jnp.where(lane < LANE_F, par_ref[0:1, :] * sp,
                               jnp.where(lane < LANE_F + H_FOX, logsig, 0.0)))
    row = lax.broadcasted_iota(I32, (ts, ts), 0)
    colm = lax.broadcasted_iota(I32, (ts, ts), 1)
    log_chunk = CHUNK.bit_length() - 1
    causal = (row >= colm) & ((row >> log_chunk) == (colm >> log_chunk))

    hi = vals.astype(BF16)
    rem = vals - hi.astype(F32)
    mid = rem.astype(BF16)
    lo = (rem - mid.astype(F32)).astype(BF16)
    pieces = jnp.concatenate([hi, mid, lo], axis=-1)

    def cumsum(mask):
        y = jnp.dot(jnp.where(mask, 1.0, 0.0).astype(BF16), pieces, preferred_element_type=F32)
        return (y[:, :LANES] + y[:, LANES:2 * LANES]) + y[:, 2 * LANES:]

    ccum = cumsum(row >= colm) + carry_ref[...]
    gcum = cumsum(causal)
    carry_ref[...] = ccum[ts - 1:ts, :]
    gcum_t = gcum.T

    cl = ccum * LOG2E
    c_hi = cl.astype(BF16)
    c_rem = cl - c_hi.astype(F32)
    c_mid = c_rem.astype(BF16)
    c_lo = (c_rem - c_mid.astype(F32)).astype(BF16)
    c_pieces = jnp.concatenate([c_hi, c_mid, c_lo], axis=-1)
    pr = lax.broadcasted_iota(I32, (3 * LANES, LANES), 0)
    pc = lax.broadcasted_iota(I32, (3 * LANES, LANES), 1)
    src_lane, piece = pr & (LANES - 1), pr >> (LANES.bit_length() - 1)
    owned = (src_lane >= LANE_F) & (src_lane < LANE_F + H_FOX) & ((pc >> 3) == src_lane - LANE_F)
    place_q = jnp.where(owned & ((pc & 7) == piece), 1.0, 0.0).astype(BF16)
    place_k = jnp.where(owned & ((pc & 7) == piece + 3), -1.0, 0.0).astype(BF16)
    slot = lane & 7
    in_heads = lane < 8 * H_FOX
    ones_q = jnp.where(in_heads & (slot >= 3) & (slot < 6), 1.0, 0.0)
    ones_k = jnp.where(in_heads & (slot < 3), 1.0, 0.0)
    qaug_ref[0] = (jnp.dot(c_pieces, place_q, preferred_element_type=F32) + ones_q).astype(BF16)
    kaug_ref[0] = (jnp.dot(c_pieces, place_k, preferred_element_type=F32) + ones_k).astype(BF16)

    scale = DK_DN ** -0.5
    dn_norm = par_ref[2:3, :]

    pw = 2 * CHUNK
    prow = lax.broadcasted_iota(I32, (pw, pw), 0)
    pcol = lax.broadcasted_iota(I32, (pw, pw), 1)
    same = (prow >> log_chunk) == (pcol >> log_chunk)
    causal_p = (prow >= pcol) & same
    strict_p = (prow > pcol) & same
    levels = []
    s = 1
    while s < CHUNK:
        levels.append(((prow >> s.bit_length()) == (pcol >> s.bit_length()))
                      & ((prow & s) != 0) & ((pcol & s) == 0))
        s *= 2

    heads = range(H_DN)
    pairs = range(ts // pw)
    q, k, v = [], [], []
    for h in heads:
        q.append(l2n(conv_silu(h * DK_DN)))
        k.append(l2n(conv_silu(H_DN * DK_DN + h * DK_DN)))
        v.append(conv_silu(2 * H_DN * DK_DN + h * DV_DN))

    ctx = []
    for h in heads:
        for p in pairs:
            pr = slice(p * pw, (p + 1) * pw)
            gcol = gcum[pr, LANE_G + h:LANE_G + h + 1]
            grow = gcum_t[LANE_G + h:LANE_G + h + 1, pr]
            beta = vals[pr, LANE_BETA + h:LANE_BETA + h + 1]
            decay = jnp.where(causal_p, jnp.exp(jnp.where(causal_p, gcol - grow, 0.0)), 0.0)
            egc = jnp.exp(gcol)
            kp = k[h][pr]
            kb = kp * beta
            ctx.append(dict(
                h=h, p=p, gcol=gcol, kp=kp,
                a=jnp.where(strict_p, _mm(kb, kp, NT) * decay, 0.0),
                qk=jnp.where(causal_p, _mm(q[h][pr] * scale, kp, NT) * decay, 0.0),
                rhs=jnp.concatenate([v[h][pr] * beta, kb * egc], axis=-1),
                qg=q[h][pr] * (scale * egc)))

    for c in ctx:
        c["n"] = -jnp.where(levels[0], c["a"], 0.0)
    for level in levels[1:]:
        for c in ctx:
            m = jnp.where(level, c["a"], 0.0)
            c["y"] = m + _mm(c["n"], m)
        for c in ctx:
            c["n"] = c["n"] - (c["y"] + _mm(c["y"], c["n"]))
    for c in ctx:
        c["sol"] = c["rhs"] + _mm(c["n"], c["rhs"])

    state = [s_ref[h] for h in heads]
    for p in pairs:
        group = [c for c in ctx if c["p"] == p]
        v_prev = [None] * H_DN
        for ch in range(2):
            rows = slice(ch * CHUNK, (ch + 1) * CHUNK)
            out_rows = slice(p * pw + ch * CHUNK, p * pw + (ch + 1) * CHUNK)
            on_state = [_mm(jnp.concatenate([c["sol"][rows, DV_DN:], c["qg"][rows]], axis=0), state[c["h"]])
                        for c in group]
            for c, ws_qs in zip(group, on_state):
                h = c["h"]
                vn = c["sol"][rows, :DV_DN] - ws_qs[:CHUNK]
                v_pair = jnp.concatenate([vn, jnp.zeros_like(vn)] if ch == 0 else [v_prev[h], vn], axis=0)
                o = ws_qs[CHUNK:] + _mm(c["qk"][rows], v_pair)
                glast = c["gcol"][(ch + 1) * CHUNK - 1:(ch + 1) * CHUNK]
                state[h] = (state[h] * jnp.exp(glast)
                            + _mm(c["kp"][rows] * jnp.exp(glast - c["gcol"][rows]), vn, TN))
                v_prev[h] = vn
                o = o * lax.rsqrt(jnp.mean(o * o, axis=-1, keepdims=True) + EPS) * dn_norm
                za = za_ref[0, out_rows, h * DV_DN:(h + 1) * DV_DN].astype(F32)
                oa_ref[0, out_rows, h * DV_DN:(h + 1) * DV_DN] = (o * _silu(za)).astype(oa_ref.dtype)
    for h in heads:
        s_ref[h] = state[h]


def _delta_rule(qkv, za, small, conv_w, par, *, ts=256):
    b, t, _ = qkv.shape
    assert t % ts == 0 and ts % (2 * CHUNK) == 0
    blk = lambda width: pl.BlockSpec((1, ts, width), lambda bi, j: (bi, j, 0))
    fixed = lambda bi, j: (0, 0)
    return pl.pallas_call(
        functools.partial(_delta_body, ts=ts),
        grid=(b, t // ts),
        in_specs=[blk(W_DN), blk(H_DN * DV_DN), blk(LANES),
                  pl.BlockSpec(conv_w.shape, fixed), pl.BlockSpec(par.shape, fixed)],
        out_specs=[blk(H_DN * DV_DN), blk(LANES), blk(LANES)],
        out_shape=[jax.ShapeDtypeStruct((b, t, H_DN * DV_DN), BF16),
                   jax.ShapeDtypeStruct((b, t, LANES), BF16),
                   jax.ShapeDtypeStruct((b, t, LANES), BF16)],
        scratch_shapes=[pltpu.VMEM((3 * SUBLANES, W_DN), F32),
                        pltpu.VMEM((H_DN, DK_DN, DV_DN), F32),
                        pltpu.VMEM((1, LANES), F32)],
        compiler_params=_params("parallel", "arbitrary"),
        name="delta_rule",
    )(qkv, za, small, conv_w, par)


def _fox_body(q_ref, k_ref, v_ref, qa_ref, ka_ref, o_ref, m_ref, acc_ref, *, tq, tk):
    i = pl.program_id(1)
    m_ref[...] = jnp.full(m_ref.shape, -jnp.inf, F32)
    acc_ref[...] = jnp.zeros_like(acc_ref)
    head_lanes = [slice(h * D_FOX, (h + 1) * D_FOX) for h in range(H_FOX)]
    lane = lax.broadcasted_iota(I32, (tk, LANES), 1)
    own = [jnp.where((lane >> 3) == h, 1.0, 0.0).astype(BF16) for h in range(H_FOX)]
    ones = jnp.ones((tk, D_FOX), BF16)
    keep = lax.broadcasted_iota(I32, (tq, tk), 1) <= lax.broadcasted_iota(I32, (tq, tk), 0)

    def block(start, diag_offset):
        rows = slice(0 if diag_offset is None else diag_offset, tq)
        ka = ka_ref[0, pl.ds(start, tk), :]
        for h, lanes in enumerate(head_lanes):
            q_aug = jnp.concatenate([q_ref[0, rows, lanes], qa_ref[0, rows, :]], axis=1)
            k_aug = jnp.concatenate([k_ref[0, pl.ds(start, tk), lanes], ka * own[h]], axis=1)
            s = lax.dot_general(q_aug, k_aug, NT, preferred_element_type=F32)
            if diag_offset is not None:
                s = jnp.where(keep[:tq - diag_offset], s, -jnp.inf)
            m_prev = m_ref[h, rows]
            m_next = jnp.maximum(m_prev, jnp.max(s, axis=-1, keepdims=True))
            p = jnp.exp2(s - jnp.concatenate([m_next] * (tk // LANES), axis=1))
            alpha = jnp.exp2(m_prev - m_next)
            v_aug = jnp.concatenate([v_ref[0, pl.ds(start, tk), lanes], ones], axis=1)
            acc_ref[h, rows] = (jnp.concatenate([alpha, alpha], axis=1) * acc_ref[h, rows]
                                + jnp.dot(p.astype(BF16), v_aug, preferred_element_type=F32))
            m_ref[h, rows] = m_next

    def full_block(jb, carry):
        block(pl.multiple_of(jb * tk, tk), None)
        return carry

    lax.fori_loop(0, i * (tq // tk), full_block, 0)
    for d in range(tq // tk):
        block(pl.multiple_of(i * tq + d * tk, tk), d * tk)
    for h, lanes in enumerate(head_lanes):
        acc = acc_ref[h]
        o_ref[0, :, lanes] = (acc[:, :D_FOX] / acc[:, D_FOX:]).astype(o_ref.dtype)


def _fox_attention(qkv, qaug, kaug, *, tq=1024, tk=512):
    b, t, _ = qkv.shape
    hd = H_FOX * D_FOX
    assert t % tq == 0 and tq % tk == 0 and tk % LANES == 0
    return pl.pallas_call(
        functools.partial(_fox_body, tq=tq, tk=tk),
        grid=(b, t // tq),
        in_specs=[pl.BlockSpec((1, tq, hd), lambda bi, i: (bi, i, 0)),
                  pl.BlockSpec((1, t, hd), lambda bi, i: (bi, 0, 1)),
                  pl.BlockSpec((1, t, hd), lambda bi, i: (bi, 0, 2)),
                  pl.BlockSpec((1, tq, LANES), lambda bi, i: (bi, i, 0)),
                  pl.BlockSpec((1, t, LANES), lambda bi, i: (bi, 0, 0))],
        out_specs=pl.BlockSpec((1, tq, hd), lambda bi, i: (bi, i, 0)),
        out_shape=jax.ShapeDtypeStruct((b, t, hd), BF16),
        scratch_shapes=[pltpu.VMEM((H_FOX, tq, LANES), F32), pltpu.VMEM((H_FOX, tq, 2 * D_FOX), F32)],
        compiler_params=_params("parallel", "arbitrary"),
        name="fox_attention",
    )(qkv, qkv, qkv, qaug, kaug)


def _merge_body(oa_ref, ob_ref, zc_ref, gates_ref, x_ref, wa_ref, wb_ref, wc_ref, wo_ref, g_ref,
                wr_ref, br_ref, xn_ref, hp_ref, route_ref, meta_ref, cnt_ref, run_ref, *, tm):
    step = pl.program_id(0)
    d = x_ref.shape[-1]

    @pl.when(step == 0)
    def _():
        run_ref[...] = jnp.zeros_like(run_ref)

    merged = None
    for idx, (m_ref, w_ref) in enumerate(((oa_ref, wa_ref), (ob_ref, wb_ref), (zc_ref, wc_ref))):
        y = jnp.dot(m_ref[...], w_ref[...], preferred_element_type=F32)
        term = _sigmoid(gates_ref[:, idx * d:(idx + 1) * d].astype(F32)) * y
        merged = term if merged is None else merged + term
    xn = x_ref[...] + jnp.dot(merged.astype(BF16), wo_ref[...], preferred_element_type=F32)
    xn_ref[...] = xn
    h2 = xn * lax.rsqrt(jnp.mean(xn * xn, axis=-1, keepdims=True) + EPS) * g_ref[...]

    hp_ref[...] = _pack_bf16_pairs(h2)

    logits = lax.dot_general(wr_ref[...], h2.astype(BF16), NT, preferred_element_type=F32) + br_ref[...]
    big = jnp.int32(LANES)
    gl = logits[LANE_GRP:LANE_GRP + N_GROUPS]
    g_row = lax.broadcasted_iota(I32, gl.shape, 0)
    gmax = jnp.max(gl, axis=0, keepdims=True)
    grp = jnp.min(jnp.where(gl == gmax, g_row, big), axis=0, keepdims=True)
    p_grp = 1.0 / jnp.sum(jnp.exp(gl - gmax), axis=0, keepdims=True)
    el_all = logits[LANE_EXP:LANE_EXP + N_EXPERTS]
    e_row = lax.broadcasted_iota(I32, el_all.shape, 0)
    el = jnp.where((e_row >> (EXPERTS_PER_GROUP.bit_length() - 1)) == grp, el_all, -jnp.inf)
    v0 = jnp.max(el, axis=0, keepdims=True)
    i0 = jnp.min(jnp.where(el == v0, e_row, big), axis=0, keepdims=True)
    el1 = jnp.where(e_row == i0, -jnp.inf, el)
    v1 = jnp.max(el1, axis=0, keepdims=True)
    i1 = jnp.min(jnp.where(el1 == v1, e_row, big), axis=0, keepdims=True)
    e1 = jnp.exp(v1 - v0)
    gate0 = p_grp / (1.0 + e1)
    gate1 = p_grp * e1 / (1.0 + e1)

    hot0 = e_row == i0
    hot1 = e_row == i1
    onehot = jnp.where(hot0 | hot1, 1.0, 0.0)
    tiles = [onehot[:, t * LANES:(t + 1) * LANES] for t in range(tm // LANES)]
    ri = lax.broadcasted_iota(I32, (LANES, LANES), 0)
    cj = lax.broadcasted_iota(I32, (LANES, LANES), 1)
    in_tile = _mm(jnp.concatenate(tiles, axis=0), jnp.where(ri < cj, 1.0, 0.0))
    before = run_ref[...]
    prefix = []
    for t, tile in enumerate(tiles):
        prefix.append(in_tile[t * N_EXPERTS:(t + 1) * N_EXPERTS] + before)
        before = before + jnp.sum(tile, axis=1, keepdims=True)
    prefix = jnp.concatenate(prefix, axis=1)
    run_ref[...] = before
    cnt_ref[...] = jnp.broadcast_to(before, cnt_ref.shape)
    rank0 = jnp.sum(jnp.where(hot0, prefix, 0.0), axis=0, keepdims=True)
    rank1 = jnp.sum(jnp.where(hot1, prefix, 0.0), axis=0, keepdims=True)

    rows = {R_GATE0: gate0, R_GATE1: gate1, R_EID0: i0.astype(F32), R_EID1: i1.astype(F32),
            R_RANK0: rank0, R_RANK1: rank1}
    fields = jnp.concatenate([rows.get(r, jnp.zeros_like(gate0)) for r in range(SUBLANES)], axis=0)
    meta_ref[...] = fields.astype(I32)
    route_ref[...] = jnp.concatenate([fields, jnp.zeros((LANES - SUBLANES, tm), F32)], axis=0).T


def _merge(oa, ob, zc, gates, x2, wa, wb, wc, wo, g, wr, br, *, tm=1024):
    n, d = x2.shape
    assert n % tm == 0
    row = lambda i: (i, 0)
    fixed = lambda i: (0, 0)
    full = lambda a: pl.BlockSpec(a.shape, fixed)
    return pl.pallas_call(
        functools.partial(_merge_body, tm=tm),
        grid=(n // tm,),
        in_specs=[pl.BlockSpec((tm, oa.shape[1]), row), pl.BlockSpec((tm, ob.shape[1]), row),
                  pl.BlockSpec((tm, zc.shape[1]), row), pl.BlockSpec((tm, 3 * d), row),
                  pl.BlockSpec((tm, d), row),
                  full(wa), full(wb), full(wc), full(wo), full(g), full(wr), full(br)],
        out_specs=[pl.BlockSpec((tm, d), row), pl.BlockSpec((tm, d // 2), row),
                   pl.BlockSpec((tm, LANES), row), pl.BlockSpec((SUBLANES, tm), lambda i: (0, i)),
                   pl.BlockSpec((N_EXPERTS, LANES), fixed)],
        out_shape=[jax.ShapeDtypeStruct((n, d), F32), jax.ShapeDtypeStruct((n, d // 2), U32),
                   jax.ShapeDtypeStruct((n, LANES), F32), jax.ShapeDtypeStruct((SUBLANES, n), I32),
                   jax.ShapeDtypeStruct((N_EXPERTS, LANES), F32)],
        scratch_shapes=[pltpu.VMEM((N_EXPERTS, 1), F32)],
        compiler_params=_params("arbitrary"),
        name="merge_router",
    )(oa, ob, zc, gates, x2, wa, wb, wc, wo, g, wr, br)


def _sc_mesh():
    return plsc.VectorSubcoreMesh(core_axis_name="c", subcore_axis_name="s")


def _sc_worker_base(per_worker):
    return (lax.axis_index("s") * SC_CORES + lax.axis_index("c")) * per_worker


def _sc_row_scatter(src, idx, out_rows, *, chunk=128):
    n, d = src.shape
    per_worker = n // (SC_CORES * SC_SUBCORES)
    chunk = min(chunk, per_worker)
    assert idx.shape == (2 * n,) and n % (SC_CORES * SC_SUBCORES) == 0 and per_worker % chunk == 0

    @functools.partial(
        pl.kernel, mesh=_sc_mesh(), out_type=jax.ShapeDtypeStruct((out_rows, d), src.dtype),
        scratch_types=[pltpu.VMEM((chunk,), I32), pltpu.VMEM((chunk, d), src.dtype), pltpu.SemaphoreType.DMA])
    def scatter(src_hbm, idx_hbm, out_hbm, idx_v, rows_v, sem):
        base = _sc_worker_base(per_worker)

        @pl.loop(0, per_worker // chunk)
        def _(step):
            off = base + step * chunk
            pltpu.sync_copy(src_hbm.at[pl.ds(off, chunk)], rows_v)
            for slot in range(2):
                pltpu.sync_copy(idx_hbm.at[pl.ds(slot * n + off, chunk)], idx_v)
                pltpu.async_copy(rows_v, out_hbm.at[idx_v], sem).wait()

    return scatter(src, idx)


def _pack_bf16_pairs(x):
    half = x.shape[-1] // 2
    lo = pltpu.bitcast(x[:, :half].astype(BF16).astype(F32), U32) >> 16
    hi = pltpu.bitcast(x[:, half:].astype(BF16).astype(F32), U32) & jnp.uint32(0xFFFF0000)
    return lo | hi


def _unpack_bf16_pairs(xp):
    return pltpu.bitcast(xp << 16, F32), pltpu.bitcast(xp & jnp.uint32(0xFFFF0000), F32)


def _expert_body(be_ref, valid_ref, nu_ref, xs_ref, w1_ref, w3_ref, w2_ref, ys_ref):
    del be_ref
    step = pl.program_id(0)
    live = step < nu_ref[0]

    @pl.when(jnp.logical_not(live))
    def _():
        ys_ref[...] = jnp.zeros_like(ys_ref)

    @pl.when(live)
    def _():
        row = lax.broadcasted_iota(I32, xs_ref.shape, 0)
        xp = jnp.where(row < valid_ref[step], xs_ref[...], jnp.uint32(0))
        half = xp.shape[-1]
        lo, hi = (part.astype(BF16) for part in _unpack_bf16_pairs(xp))

        def up(w_ref):
            return (jnp.dot(lo, w_ref[0, :half, :].astype(BF16), preferred_element_type=F32)
                    + jnp.dot(hi, w_ref[0, half:, :].astype(BF16), preferred_element_type=F32))

        act = (_silu(up(w1_ref)) * up(w3_ref)).astype(BF16)
        ys_ref[...] = _pack_bf16_pairs(jnp.dot(act, w2_ref[0].astype(BF16), preferred_element_type=F32))


def _experts(blk_eid, blk_valid, n_used, xs, w1, w3, w2, *, rb):
    p, half = xs.shape
    d = 2 * half
    nb = p // rb
    used = lambda i, be, bv, nu: jnp.maximum(jnp.minimum(i, nu[0] - 1), 0)
    wmap = lambda i, be, bv, nu: (be[used(i, be, bv, nu)], 0, 0)
    grid_spec = pltpu.PrefetchScalarGridSpec(
        num_scalar_prefetch=3,
        grid=(nb,),
        in_specs=[pl.BlockSpec((rb, half), lambda i, be, bv, nu: (used(i, be, bv, nu), 0)),
                  pl.BlockSpec((1, d, D_EXPERT), wmap), pl.BlockSpec((1, d, D_EXPERT), wmap),
                  pl.BlockSpec((1, D_EXPERT, d), wmap)],
        out_specs=pl.BlockSpec((rb, half), lambda i, be, bv, nu: (i, 0)),
    )
    return pl.pallas_call(
        _expert_body,
        grid_spec=grid_spec,
        out_shape=jax.ShapeDtypeStruct((p, half), U32),
        compiler_params=_params("arbitrary"),
        name="moe_experts",
    )(blk_eid, blk_valid, n_used, xs, w1, w3, w2)


def _sc_row_gather(table, idx, *, chunk=128):
    rows, d = idx.shape[0], table.shape[1]
    per_worker = rows // (SC_CORES * SC_SUBCORES)
    chunk = min(chunk, per_worker)
    assert rows % (SC_CORES * SC_SUBCORES) == 0 and per_worker % chunk == 0

    @functools.partial(
        pl.kernel, mesh=_sc_mesh(), out_type=jax.ShapeDtypeStruct((rows, d), table.dtype),
        scratch_types=[pltpu.VMEM((chunk,), I32), pltpu.VMEM((chunk, d), table.dtype), pltpu.SemaphoreType.DMA])
    def gather(table_hbm, idx_hbm, out_hbm, idx_v, rows_v, sem):
        base = _sc_worker_base(per_worker)

        @pl.loop(0, per_worker // chunk)
        def _(step):
            off = base + step * chunk
            pltpu.sync_copy(idx_hbm.at[pl.ds(off, chunk)], idx_v)
            pltpu.async_copy(table_hbm.at[idx_v], rows_v, sem).wait()
            pltpu.sync_copy(rows_v, out_hbm.at[pl.ds(off, chunk)])

    return gather(table, idx)


def _combine_rows(r0_ref, r1_ref, x_ref, route_ref):
    route = route_ref[...]
    half = r0_ref.shape[-1]
    lo0, hi0 = _unpack_bf16_pairs(r0_ref[...])
    lo1, hi1 = _unpack_bf16_pairs(r1_ref[...])
    g0, g1 = route[:, R_GATE0:R_GATE0 + 1], route[:, R_GATE1:R_GATE1 + 1]
    return jnp.concatenate([x_ref[:, :half] + g0 * lo0 + g1 * lo1,
                            x_ref[:, half:] + g0 * hi0 + g1 * hi1], axis=1)


def _final_body(r0_ref, r1_ref, x_ref, route_ref, g_ref, o_ref):
    out = _combine_rows(r0_ref, r1_ref, x_ref, route_ref)
    o_ref[...] = out * lax.rsqrt(jnp.mean(out * out, axis=-1, keepdims=True) + EPS) * g_ref[...]


def _final_combine(rows2, xn, route, g, *, tm):
    n, d = xn.shape
    nt = n // tm
    row = lambda i: (i, 0)
    return pl.pallas_call(
        _final_body,
        grid=(nt,),
        in_specs=[pl.BlockSpec((tm, d // 2), row), pl.BlockSpec((tm, d // 2), lambda i: (i + nt, 0)),
                  pl.BlockSpec((tm, d), row), pl.BlockSpec((tm, LANES), row),
                  pl.BlockSpec((1, d), lambda i: (0, 0))],
        out_specs=pl.BlockSpec((tm, d), row),
        out_shape=jax.ShapeDtypeStruct((n, d), F32),
        compiler_params=_params("parallel"),
        name="moe_combine",
    )(rows2, rows2, xn, route, g)


def _moe_rows(hp, meta, counts, w1, w3, w2, *, layer, rb):
    n = hp.shape[0]
    cnt = counts[:, 0].astype(I32)
    nblk = (cnt + rb - 1) // rb
    bend = jnp.cumsum(nblk)
    pstart = (bend - nblk) * rb
    nb = (2 * n) // rb + N_EXPERTS
    n_used = bend[-1:].astype(I32)
    blk = jnp.arange(nb, dtype=I32)
    blk_eid = jnp.minimum(jnp.sum(bend[None, :] <= blk[:, None], axis=1), N_EXPERTS - 1).astype(I32)
    experts = jnp.arange(N_EXPERTS, dtype=I32)
    mine = blk_eid[:, None] == experts[None, :]
    blk_valid = jnp.clip(jnp.sum(jnp.where(mine, (cnt + pstart)[None, :], 0), axis=1) - blk * rb, 0, rb)
    eid = meta[R_EID0:R_EID1 + 1]
    first = jnp.sum(jnp.where(eid[None] == experts[:, None, None], pstart[:, None, None], 0), axis=0)
    dest = (first + meta[R_RANK0:R_RANK1 + 1]).reshape(-1)

    xs = _sc_row_scatter(hp, dest, nb * rb)
    ys = _experts(blk_eid + layer * N_EXPERTS, blk_valid.astype(I32), n_used, xs, w1, w3, w2, rb=rb)
    return _sc_row_gather(ys, dest)


def _regroup_t_body(wt_ref, n1_ref, n2_ref, *rest, depth, bounds, tc, q_scale):
    out_refs, small_refs, (buf_ref, sem) = rest[:depth], rest[depth:2 * depth], rest[2 * depth:]
    j = pl.program_id(0)
    fox0, conv0, a_end, q_end, b_end = bounds

    def fetch(block, slot):
        start = jnp.where(block < a_end, block * tc,
                          jnp.where(block < b_end, fox0 + (block - a_end) * tc, conv0 + (block - b_end) * tc))
        return pltpu.make_async_copy(wt_ref.at[pl.ds(start, tc)], buf_ref.at[slot], sem.at[slot])

    @pl.when(j == 0)
    def _():
        fetch(j, 0).start()
        for l, ws_ref in enumerate(small_refs):
            cols = jnp.concatenate([n1_ref[:, l, :], n2_ref[:, l, :]], axis=0)
            pad = jnp.zeros((LANES - cols.shape[0], cols.shape[1]), F32)
            ws_ref[...] = jnp.concatenate([cols, pad], axis=0).T.astype(ws_ref.dtype)

    @pl.when(j + 1 < pl.num_programs(0))
    def _():
        fetch(j + 1, (j + 1) % 2).start()

    slot = j % 2
    fetch(j, slot).wait()
    scale = jnp.where((j >= a_end) & (j < q_end), q_scale, 1.0)
    for l, out_ref in enumerate(out_refs):
        out_ref[...] = (buf_ref[slot, :, l, :].T * scale).astype(out_ref.dtype)


def _regroup_w_in_t(w_in, *, q_scale, tc=2 * LANES):
    depth, d, d_in = w_in.shape
    small0 = 2 * H_DN * DK_DN + 2 * H_DN * DV_DN
    fox0 = small0 + 2 * H_DN
    fb0 = fox0 + W_FOX
    conv0 = fb0 + H_FOX
    wt = jnp.transpose(w_in, (2, 0, 1))
    spans = (small0, fb0 - fox0, d_in - conv0)
    assert all(span % tc == 0 for span in spans)
    a_end = spans[0] // tc
    b_end = a_end + spans[1] // tc
    q_end = a_end + (H_FOX * D_FOX) // tc
    width = sum(spans)
    narrow = (wt[small0:fox0], wt[fb0:conv0])
    whole = lambda a: pl.BlockSpec(a.shape, lambda j: (0, 0, 0))
    outs = pl.pallas_call(
        functools.partial(_regroup_t_body, depth=depth, bounds=(fox0, conv0, a_end, q_end, b_end), tc=tc,
                          q_scale=q_scale),
        grid=(width // tc,),
        in_specs=[pl.BlockSpec(memory_space=pl.ANY), whole(narrow[0]), whole(narrow[1])],
        out_specs=([pl.BlockSpec((d, tc), lambda j: (0, j)) for _ in range(depth)]
                   + [pl.BlockSpec((d, LANES), lambda j: (0, 0)) for _ in range(depth)]),
        out_shape=([jax.ShapeDtypeStruct((d, width), BF16) for _ in range(depth)]
                   + [jax.ShapeDtypeStruct((d, LANES), BF16) for _ in range(depth)]),
        scratch_shapes=[pltpu.VMEM((2, tc, depth, d), F32), pltpu.SemaphoreType.DMA((2,))],
        compiler_params=_params("arbitrary"),
        name="regroup_w_in",
    )(wt, *narrow)
    return outs[:depth], outs[depth:]


def _lane_row(pairs):
    row = jnp.zeros((LANES,), F32)
    for off, vec in pairs:
        row = row.at[off:off + vec.shape[0]].set(vec.astype(F32))
    return row


def kernel(x, norm_mix, w_in, conv_qkv, dn_a_log, dn_dt_bias, dn_norm, fox_bias, conv_dw, conv_dw_b,
           conv_ln_g, conv_ln_b, w_a, w_b, w_c, w_out, norm_ffn, router_group_w, router_group_b,
           router_expert_w, router_expert_b, expert_w1, expert_w3, expert_w2, norm_final):
    b, t, d = x.shape
    n = b * t
    depth = w_in.shape[0]
    tm_final, rb = 512, 512

    x_src = x.reshape(n, d)
    w1_all = expert_w1.reshape(depth * N_EXPERTS, d, D_EXPERT)
    w3_all = expert_w3.reshape(depth * N_EXPERTS, d, D_EXPERT)
    w2_all = expert_w2.reshape(depth * N_EXPERTS, D_EXPERT, d)
    w_mains, w_smalls = _regroup_w_in_t(w_in, q_scale=D_FOX ** -0.5 * LOG2E)
    for l in range(depth):
        w_main, w_small = w_mains[l], w_smalls[l]
        vec = jnp.stack([conv_dw_b[l], conv_ln_g[l], conv_ln_b[l]] + [jnp.zeros((C_CONV,), F32)] * 5)
        outs = _in_proj(x_src, norm_mix[l][None, :], w_main, w_small, conv_dw[l], vec, seq_len=t)
        x2 = outs[0] if l > 0 else x_src
        dn_qkv, za_p, fox_qkv, zc, gates, small = outs[-6:]

        par = jnp.stack([_lane_row([(LANE_G, -jnp.exp(dn_a_log[l]))]),
                         _lane_row([(LANE_G, dn_dt_bias[l]), (LANE_F, fox_bias[l])]),
                         _lane_row([(0, dn_norm[l])])] + [jnp.zeros((LANES,), F32)] * 5)
        oa, qaug, kaug = _delta_rule(dn_qkv.reshape(b, t, -1), za_p.reshape(b, t, -1),
                                     small.reshape(b, t, LANES), conv_qkv[l], par)
        ob = _fox_attention(fox_qkv.reshape(b, t, -1), qaug, kaug)

        w_r = jnp.concatenate([router_group_w[l], jnp.zeros((d, LANE_EXP - N_GROUPS), F32),
                               router_expert_w[l], jnp.zeros((d, LANES - LANE_EXP - N_EXPERTS), F32)], axis=1)
        b_r = _lane_row([(LANE_GRP, router_group_b[l]), (LANE_EXP, router_expert_b[l])])[None, :]
        xn, hp, route, meta, counts = _merge(
            oa.reshape(n, -1), ob.reshape(n, -1), zc.reshape(n, -1), gates, x2,
            w_a[l].astype(BF16), w_b[l].astype(BF16), w_c[l].astype(BF16), w_out[l].astype(BF16),
            norm_ffn[l][None, :], w_r.T.astype(BF16), b_r.reshape(LANES, 1))

        x_src = (_moe_rows(hp, meta, counts, w1_all, w3_all, w2_all, layer=l, rb=rb), xn, route)
    return _final_combine(*x_src, norm_final[None, :], tm=tm_final).reshape(b, t, d)
```
